```python
import math
import jax, jax.numpy as jnp
from jax import lax
import numpy as np

D_MODEL = 1024
BATCH = 32
SEQ = 2048
DEPTH = 1
DEC_BATCH = 32
DEC_SEQ = 64
PAST_LEN = 1024

CHUNK = 64
EPS = 1e-6
DN_QK_HEADS = 4
DN_V_HEADS = 8
DN_HEAD_DIM = 64
DN_QK_DIM = DN_QK_HEADS * DN_HEAD_DIM
DN_V_DIM = DN_V_HEADS * DN_HEAD_DIM
DN_CONV_DIM = 2 * DN_QK_DIM + DN_V_DIM
CONV_W = 4
SWA_HEADS = 8
SWA_KV_HEADS = 2
SWA_HEAD_DIM = 64
SWA_GROUP = SWA_HEADS // SWA_KV_HEADS
SWA_Q_DIM = SWA_HEADS * SWA_HEAD_DIM
SWA_KV_DIM = SWA_KV_HEADS * SWA_HEAD_DIM
WINDOW = 128
WIN_CHUNKS = WINDOW // CHUNK
NUM_BUCKETS = 32
MAX_DISTANCE = 128
MIX_DIM = DN_V_DIM + SWA_Q_DIM
IN_DIM = DN_CONV_DIM + DN_V_DIM + 2 * DN_V_HEADS + SWA_Q_DIM + 2 * SWA_KV_DIM
N_EXPERTS = 32
TOP_K = 4
D_FF = 1024
SWIGLU_ALPHA = 1.702
SWIGLU_LIMIT = 7.0
MOE_BLOCK = 256

kernel_name = 'hybrid_deltanet_swa_moe_stream_step'


def rms_norm(x, w):
    xf = x.astype(jnp.float32)
    y = xf * lax.rsqrt(jnp.mean(xf * xf, axis=-1, keepdims=True) + EPS)
    return (y * w.astype(jnp.float32)).astype(x.dtype)


def l2norm(x):
    return x * lax.rsqrt(jnp.sum(x * x, axis=-1, keepdims=True) + EPS)


def causal_conv(u, prev, w):
    xc = jnp.concatenate([prev.astype(u.dtype), u], axis=1)
    t_len = u.shape[1]
    out = xc[:, 0:t_len] * w[0]
    for j in range(1, CONV_W):
        out = out + xc[:, j:j + t_len] * w[j]
    return jax.nn.silu(out), xc[:, -(CONV_W - 1):]


def gated_delta_chunked(q, k, v, g, beta, s0, chunk):
    bsz, t_len, n_h, dk = q.shape
    n_c = t_len // chunk

    def blocks(a):
        a = a.reshape((bsz, n_c, chunk) + a.shape[2:])
        return jnp.swapaxes(a, 2, 3)

    q, k, v, g, beta = blocks(q), blocks(k), blocks(v), blocks(g), blocks(beta)
    gc = jnp.cumsum(g, axis=-1)
    idx = jnp.arange(chunk)
    incl = idx[:, None] >= idx[None, :]
    strict = idx[:, None] > idx[None, :]
    decay = jnp.exp(jnp.where(incl, gc[..., :, None] - gc[..., None, :], -jnp.inf))
    kb = k * beta[..., None]
    m = jnp.where(strict, jnp.einsum('bnhid,bnhjd->bnhij', kb, k) * decay, 0.0)
    a = m + jnp.eye(chunk, dtype=m.dtype)
    rhs = jnp.concatenate([kb * jnp.exp(gc)[..., None], v * beta[..., None]], axis=-1)
    sol = lax.linalg.triangular_solve(a, rhs, left_side=True, lower=True, unit_diagonal=True)
    w_c, u_c = sol[..., :dk], sol[..., dk:]
    aqk = jnp.einsum('bnhid,bnhjd->bnhij', q, k) * decay
    qg = q * jnp.exp(gc)[..., None]
    glast = gc[..., -1]
    kd = k * jnp.exp(glast[..., None] - gc)[..., None]

    def step(s, xs):
        w_i, u_i, aqk_i, qg_i, kd_i, gl_i = xs
        vnew = u_i - jnp.einsum('bhcd,bhde->bhce', w_i, s)
        o = jnp.einsum('bhcd,bhde->bhce', qg_i, s) + jnp.einsum('bhij,bhje->bhie', aqk_i, vnew)
        s = s * jnp.exp(gl_i)[..., None, None] + jnp.einsum('bhcd,bhce->bhde', kd_i, vnew)
        return s, o

    xs = tuple(jnp.moveaxis(t, 1, 0) for t in (w_c, u_c, aqk, qg, kd, glast))
    s_final, o = lax.scan(step, s0, xs)
    o = jnp.swapaxes(jnp.moveaxis(o, 0, 1), 2, 3).reshape(bsz, t_len, n_h, v.shape[-1])
    return o, s_final


def t5_bucket(rel):
    half = NUM_BUCKETS // 2
    max_exact = half // 2
    ret = jnp.where(rel > 0, half, 0)
    n = jnp.abs(rel)
    nf = jnp.maximum(n, 1).astype(jnp.float32)
    large = max_exact + (jnp.log(nf / max_exact) / math.log(MAX_DISTANCE / max_exact) * (half - max_exact)).astype(jnp.int32)
    large = jnp.minimum(large, half - 1)
    return ret + jnp.where(n < max_exact, n, large)


def relative_bias(table, q_pos, k_pos):
    bucket = t5_bucket(k_pos[None, :] - q_pos[:, None])
    b = table.astype(jnp.float32)[bucket]
    return jnp.transpose(b, (2, 0, 1)).reshape(SWA_KV_HEADS, SWA_GROUP, q_pos.shape[0], k_pos.shape[0])


def sink_attention(q, k, v, bias, mask, sink):
    s = jnp.einsum('bnqhgd,bnkhd->bnhgqk', q, k).astype(jnp.float32) * (SWA_HEAD_DIM ** -0.5) + bias
    s = jnp.where(mask[None, :, None, None], s, -1e30)
    sk = sink.astype(jnp.float32)[None, None, :, :, None, None]
    mx = jnp.maximum(jnp.max(s, axis=-1, keepdims=True), sk)
    p = jnp.exp(s - mx)
    p = p / (jnp.sum(p, axis=-1, keepdims=True) + jnp.exp(sk - mx))
    return jnp.einsum('bnhgqk,bnkhd->bnqhgd', p.astype(v.dtype), v)


def hybrid_mixer(h, conv_prev, s_prev, k_prev, v_prev, swa_len,
                 w_in, b_in, conv_w, a_log, dt_bias, dn_norm_w, sinks, rel_bias, w_out, b_out):
    bsz, t_len, _ = h.shape
    proj = jnp.dot(h, w_in) + b_in
    o1 = DN_CONV_DIM
    o2 = o1 + DN_V_DIM
    o3 = o2 + DN_V_HEADS
    o4 = o3 + DN_V_HEADS
    o5 = o4 + SWA_Q_DIM
    o6 = o5 + SWA_KV_DIM
    u_conv, z, b_gate, a_gate, sq, sk, sv = jnp.split(proj, [o1, o2, o3, o4, o5, o6], axis=-1)

    cu, conv_new = causal_conv(u_conv, conv_prev, conv_w)
    dq, dk, dv = jnp.split(cu.astype(jnp.float32), [DN_QK_DIM, 2 * DN_QK_DIM], axis=-1)
    rep = DN_V_HEADS // DN_QK_HEADS
    dq = jnp.repeat(l2norm(dq.reshape(bsz, t_len, DN_QK_HEADS, DN_HEAD_DIM)), rep, axis=2) * (DN_HEAD_DIM ** -0.5)
    dk = jnp.repeat(l2norm(dk.reshape(bsz, t_len, DN_QK_HEADS, DN_HEAD_DIM)), rep, axis=2)
    dv = dv.reshape(bsz, t_len, DN_V_HEADS, DN_HEAD_DIM)
    beta = jax.nn.sigmoid(b_gate.astype(jnp.float32))
    g = -jnp.exp(a_log.astype(jnp.float32)) * jax.nn.softplus(a_gate.astype(jnp.float32) + dt_bias.astype(jnp.float32))
    blk = CHUNK if t_len % CHUNK == 0 else t_len
    o_dn, s_new = gated_delta_chunked(dq, dk, dv, g, beta, s_prev.astype(jnp.float32), blk)
    zg = jax.nn.silu(z.astype(jnp.float32).reshape(bsz, t_len, DN_V_HEADS, DN_HEAD_DIM))
    o_dn = (rms_norm(o_dn, dn_norm_w) * zg).reshape(bsz, t_len, DN_V_DIM).astype(h.dtype)

    q = sq.reshape(bsz, t_len, SWA_KV_HEADS, SWA_GROUP, SWA_HEAD_DIM)
    k = sk.reshape(bsz, t_len, SWA_KV_HEADS, SWA_HEAD_DIM)
    v = sv.reshape(bsz, t_len, SWA_KV_HEADS, SWA_HEAD_DIM)
    if k_prev is None:
        n_blk = t_len // CHUNK
        band = (WIN_CHUNKS + 1) * CHUNK
        pad = ((0, 0), (WIN_CHUNKS * CHUNK, 0), (0, 0), (0, 0))
        kp = jnp.pad(k, pad).reshape(bsz, n_blk + WIN_CHUNKS, CHUNK, SWA_KV_HEADS, SWA_HEAD_DIM)
        vp = jnp.pad(v, pad).reshape(bsz, n_blk + WIN_CHUNKS, CHUNK, SWA_KV_HEADS, SWA_HEAD_DIM)
        kb = jnp.concatenate([kp[:, j:j + n_blk] for j in range(WIN_CHUNKS + 1)], axis=2)
        vb = jnp.concatenate([vp[:, j:j + n_blk] for j in range(WIN_CHUNKS + 1)], axis=2)
        qb = q.reshape(bsz, n_blk, CHUNK, SWA_KV_HEADS, SWA_GROUP, SWA_HEAD_DIM)
        off = jnp.arange(band) // CHUNK - WIN_CHUNKS
        mask = (jnp.arange(n_blk)[:, None] + off[None, :] >= 0)[:, None, :]
        q_rel = jnp.arange(CHUNK)
        k_rel = jnp.arange(band) - WIN_CHUNKS * CHUNK
        k_state = k[:, t_len - swa_len:]
        v_state = v[:, t_len - swa_len:]
    else:
        cl = k_prev.shape[1]
        kall = jnp.concatenate([k_prev.astype(k.dtype), k], axis=1)
        vall = jnp.concatenate([v_prev.astype(v.dtype), v], axis=1)
        qb, kb, vb = q[:, None], kall[:, None], vall[:, None]
        mask = jnp.ones((1, 1, cl + t_len), dtype=bool)
        q_rel = jnp.arange(t_len)
        k_rel = jnp.concatenate([jnp.arange(cl) - cl, jnp.arange(t_len)])
        k_state = kall[:, t_len:]
        v_state = vall[:, t_len:]
    bias = relative_bias(rel_bias, q_rel, k_rel)
    o_swa = sink_attention(qb, kb, vb, bias, mask, sinks.reshape(SWA_KV_HEADS, SWA_GROUP))
    o_swa = o_swa.reshape(bsz, t_len, SWA_Q_DIM).astype(h.dtype)

    out = jnp.dot(jnp.concatenate([o_dn, o_swa], axis=-1), w_out) + b_out
    return out, conv_new, s_new.astype(h.dtype), k_state, v_state


def moe_ffn(h, w_router, b_router, w1, b1, w2, b2):
    bsz, t_len, d = h.shape
    hf = h.reshape(-1, d)
    n_tok = hf.shape[0]
    n_asg = n_tok * TOP_K
    logits = jnp.dot(hf, w_router).astype(jnp.float32) + b_router.astype(jnp.float32)
    top_v, top_i = lax.top_k(logits, TOP_K)
    gates = jax.nn.softmax(top_v, axis=-1)
    flat_e = top_i.reshape(-1)
    order = jnp.argsort(flat_e)
    e_sorted = flat_e[order]
    tok_sorted = order // TOP_K
    gate_sorted = gates.reshape(-1)[order]
    sizes = jnp.zeros((N_EXPERTS,), jnp.int32).at[flat_e].add(1)
    start = jnp.cumsum(sizes) - sizes
    padded = (sizes + MOE_BLOCK - 1) // MOE_BLOCK * MOE_BLOCK
    pend = jnp.cumsum(padded)
    pstart = pend - padded
    dest = pstart[e_sorted] + jnp.arange(n_asg, dtype=jnp.int32) - start[e_sorted]
    n_blocks = -(-n_asg // MOE_BLOCK) + N_EXPERTS
    n_rows = n_blocks * MOE_BLOCK
    tok_pad = jnp.full((n_rows,), n_tok, jnp.int32).at[dest].set(tok_sorted.astype(jnp.int32))
    gate_pad = jnp.zeros((n_rows,), jnp.float32).at[dest].set(gate_sorted)
    blk_expert = jnp.minimum(jnp.searchsorted(pend, jnp.arange(n_blocks) * MOE_BLOCK, side='right'), N_EXPERTS - 1)
    x_pad = jnp.concatenate([hf, jnp.zeros((1, d), hf.dtype)], axis=0)[tok_pad].reshape(n_blocks, MOE_BLOCK, d)

    def expert_block(args):
        xb, e = args
        up = jnp.dot(xb, w1[e]) + b1[e]
        glu, lin = jnp.split(up, 2, axis=-1)
        glu = jnp.minimum(glu, SWIGLU_LIMIT)
        lin = jnp.clip(lin, -SWIGLU_LIMIT, SWIGLU_LIMIT)
        act = glu * jax.nn.sigmoid(SWIGLU_ALPHA * glu) * (lin + 1.0)
        return jnp.dot(act, w2[e]) + b2[e]

    y = lax.map(expert_block, (x_pad, blk_expert)).reshape(n_rows, d)
    y = y * gate_pad[:, None].astype(y.dtype)
    out = jnp.zeros((n_tok + 1, d), y.dtype).at[tok_pad].add(y)[:n_tok]
    return out.reshape(bsz, t_len, d)


def setup_inputs(seed: int = 0) -> dict:
    key = jax.random.key(seed)
    ks = iter(jax.random.split(key, 40))

    def nrm(shape, std):
        return jax.random.normal(next(ks), shape, jnp.float32) * std

    swa_len = min(WINDOW, PAST_LEN)
    dt = jnp.exp(jax.random.uniform(next(ks), (DEPTH, DN_V_HEADS), jnp.float32, math.log(1e-3), math.log(1e-1)))
    dt_bias = dt + jnp.log(-jnp.expm1(-dt))
    a_log = jnp.log(jax.random.uniform(next(ks), (DEPTH, DN_V_HEADS), jnp.float32, 1.0, 16.0))
    inputs = {}
    inputs['x_prompt'] = nrm((BATCH, SEQ, D_MODEL), 1.0)
    inputs['x_sample'] = nrm((DEC_BATCH, DEC_SEQ, D_MODEL), 1.0)
    inputs['c_prompt'] = nrm((BATCH, D_MODEL), 1.0)
    inputs['c_sample'] = nrm((DEC_BATCH, D_MODEL), 1.0)
    inputs['state_conv'] = nrm((DEPTH, DEC_BATCH, CONV_W - 1, DN_CONV_DIM), 1.0)
    inputs['state_delta'] = nrm((DEPTH, DEC_BATCH, DN_V_HEADS, DN_HEAD_DIM, DN_HEAD_DIM), 0.3)
    inputs['cache_swa_k'] = nrm((DEPTH, DEC_BATCH, swa_len, SWA_KV_HEADS, SWA_HEAD_DIM), 1.0)
    inputs['cache_swa_v'] = nrm((DEPTH, DEC_BATCH, swa_len, SWA_KV_HEADS, SWA_HEAD_DIM), 1.0)
    inputs['w_ada'] = nrm((DEPTH, D_MODEL, 6 * D_MODEL), 0.5 * D_MODEL ** -0.5)
    inputs['b_ada'] = nrm((DEPTH, 6 * D_MODEL), 0.02)
    inputs['norm_mix'] = 1.0 + nrm((DEPTH, D_MODEL), 0.02)
    inputs['w_in'] = nrm((DEPTH, D_MODEL, IN_DIM), D_MODEL ** -0.5)
    inputs['b_in'] = nrm((DEPTH, IN_DIM), 0.02)
    inputs['conv_w'] = nrm((DEPTH, CONV_W, DN_CONV_DIM), CONV_W ** -0.5)
    inputs['a_log'] = a_log
    inputs['dt_bias'] = dt_bias
    inputs['dn_norm_w'] = 1.0 + nrm((DEPTH, DN_HEAD_DIM), 0.02)
    inputs['sinks'] = nrm((DEPTH, SWA_HEADS), 0.5)
    inputs['rel_bias'] = nrm((NUM_BUCKETS, SWA_HEADS), 0.5)
    inputs['w_out'] = nrm((DEPTH, MIX_DIM, D_MODEL), MIX_DIM ** -0.5)
    inputs['b_out'] = nrm((DEPTH, D_MODEL), 0.02)
    inputs['norm_ffn'] = 1.0 + nrm((DEPTH, D_MODEL), 0.02)
    inputs['w_router'] = nrm((DEPTH, D_MODEL, N_EXPERTS), D_MODEL ** -0.5)
    inputs['b_router'] = nrm((DEPTH, N_EXPERTS), 0.01)
    inputs['w1'] = nrm((DEPTH, N_EXPERTS, D_MODEL, 2 * D_FF), D_MODEL ** -0.5)
    inputs['b1'] = nrm((DEPTH, N_EXPERTS, 2 * D_FF), 0.02)
    inputs['w2'] = nrm((DEPTH, N_EXPERTS, D_FF, D_MODEL), D_FF ** -0.5)
    inputs['b2'] = nrm((DEPTH, N_EXPERTS, D_MODEL), 0.02)
    inputs['norm_final'] = 1.0 + nrm((D_MODEL,), 0.02)
    return inputs


def reference(x_prompt, x_sample, c_prompt, c_sample, state_conv, state_delta, cache_swa_k, cache_swa_v,
              w_ada, b_ada, norm_mix, w_in, b_in, conv_w, a_log, dt_bias, dn_norm_w, sinks, rel_bias,
              w_out, b_out, norm_ffn, w_router, b_router, w1, b1, w2, b2, norm_final):
    swa_len = cache_swa_k.shape[2]

    def block(x, c, l, conv_prev, s_prev, k_prev, v_prev):
        mod = jnp.dot(jax.nn.silu(c), w_ada[l]) + b_ada[l]
        sh1, sc1, g1, sh2, sc2, g2 = [m[:, None, :] for m in jnp.split(mod, 6, axis=-1)]
        h = rms_norm(x, norm_mix[l]) * (1.0 + sc1) + sh1
        mix, conv_new, s_new, k_new, v_new = hybrid_mixer(
            h, conv_prev, s_prev, k_prev, v_prev, swa_len,
            w_in[l], b_in[l], conv_w[l], a_log[l], dt_bias[l], dn_norm_w[l], sinks[l], rel_bias,
            w_out[l], b_out[l])
        x = x + g1 * mix
        h = rms_norm(x, norm_ffn[l]) * (1.0 + sc2) + sh2
        x = x + g2 * moe_ffn(h, w_router[l], b_router[l], w1[l], b1[l], w2[l], b2[l])
        return x, conv_new, s_new, k_new, v_new

    yp = x_prompt
    bp = x_prompt.shape[0]
    p_conv, p_delta, p_k, p_v = [], [], [], []
    for l in range(DEPTH):
        conv0 = jnp.zeros((bp, CONV_W - 1, DN_CONV_DIM), x_prompt.dtype)
        s0 = jnp.zeros((bp, DN_V_HEADS, DN_HEAD_DIM, DN_HEAD_DIM), jnp.float32)
        yp, cn, sn, kn, vn = block(yp, c_prompt, l, conv0, s0, None, None)
        p_conv.append(cn); p_delta.append(sn); p_k.append(kn); p_v.append(vn)
    yp = rms_norm(yp, norm_final)

    ys = x_sample
    s_conv, s_delta, s_k, s_v = [], [], [], []
    for l in range(DEPTH):
        ys, cn, sn, kn, vn = block(ys, c_sample, l, state_conv[l], state_delta[l], cache_swa_k[l], cache_swa_v[l])
        s_conv.append(cn); s_delta.append(sn); s_k.append(kn); s_v.append(vn)
    ys = rms_norm(ys, norm_final)

    return (yp, ys,
            jnp.stack(p_conv), jnp.stack(p_delta), jnp.stack(p_k), jnp.stack(p_v),
            jnp.stack(s_conv), jnp.stack(s_delta), jnp.stack(s_k), jnp.stack(s_v))
```

```python
import functools
import math

import jax
import jax.numpy as jnp
from jax import lax
from jax.experimental import pallas as pl
from jax.experimental.pallas import tpu as pltpu

F32 = jnp.float32
BF16 = jnp.bfloat16

D_MODEL = 1024
CHUNK = 64
EPS = 1e-6
DN_QK_HEADS = 4
DN_V_HEADS = 8
DN_HEAD_DIM = 64
DN_QK_DIM = DN_QK_HEADS * DN_HEAD_DIM
DN_V_DIM = DN_V_HEADS * DN_HEAD_DIM
DN_CONV_DIM = 2 * DN_QK_DIM + DN_V_DIM
CONV_W = 4
SWA_HEADS = 8
SWA_KV_HEADS = 2
SWA_HEAD_DIM = 64
SWA_GROUP = SWA_HEADS // SWA_KV_HEADS
SWA_Q_DIM = SWA_HEADS * SWA_HEAD_DIM
SWA_KV_DIM = SWA_KV_HEADS * SWA_HEAD_DIM
WINDOW = 128
WIN_CHUNKS = WINDOW // CHUNK
BAND = (WIN_CHUNKS + 1) * CHUNK
NUM_BUCKETS = 32
MAX_DISTANCE = 128
N_EXPERTS = 32
TOP_K = 4
D_FF = 1024
SWIGLU_ALPHA = 1.702
SWIGLU_LIMIT = 7.0

LANES = 128
GATE_LANES = LANES
PROJ_DIM = DN_CONV_DIM + DN_V_DIM + SWA_Q_DIM + 2 * SWA_KV_DIM + GATE_LANES
MOE_ROWS = 512
TOKEN_TILE = 512
VMEM_LIMIT = 56 * 1024 * 1024
NEG = -1e30


def _params(*sem):
    return pltpu.CompilerParams(dimension_semantics=sem, vmem_limit_bytes=VMEM_LIMIT)


def _split2(a):
    hi = a.astype(BF16)
    lo = (a - hi.astype(F32)).astype(BF16)
    return hi, lo


def _dot(a, b):
    return jnp.dot(a, b, preferred_element_type=F32)


def _dot_nt(a, b):
    return lax.dot_general(a, b, (((1,), (1,)), ((), ())), preferred_element_type=F32)


def _dot_x3(a, b):
    a1, a2 = _split2(a)
    b1, b2 = _split2(b)
    return _dot(a1, b1) + (_dot(a1, b2) + _dot(a2, b1))


def _dot_exact_lhs(l01, g):
    g1 = g.astype(BF16)
    r = g - g1.astype(F32)
    g2 = r.astype(BF16)
    g3 = (r - g2.astype(F32)).astype(BF16)
    return _dot(l01, g1) + (_dot(l01, g2) + _dot(l01, g3))


def _silu(x):
    return x * jax.nn.sigmoid(x)


def _rms(x):
    return x * lax.rsqrt(jnp.mean(x * x, axis=-1, keepdims=True) + EPS)


def _mod_kernel(c_ref, w_ref, b_ref, o_ref):
    o_ref[...] = _dot_x3(_silu(c_ref[...]), w_ref[...]) + b_ref[...]


def _modulation(c, w_ada, b_ada):
    nb = c.shape[0]
    n_out = w_ada.shape[1]
    tn = 1024
    return pl.pallas_call(
        _mod_kernel,
        grid=(n_out // tn,),
        in_specs=[pl.BlockSpec((nb, D_MODEL), lambda j: (0, 0)),
                  pl.BlockSpec((D_MODEL, tn), lambda j: (0, j)),
                  pl.BlockSpec((1, tn), lambda j: (0, j))],
        out_specs=pl.BlockSpec((nb, tn), lambda j: (0, j)),
        out_shape=jax.ShapeDtypeStruct((nb, n_out), F32),
        compiler_params=_params("arbitrary"),
        name="modulation",
    )(c, w_ada, b_ada.reshape(1, n_out))


def _inproj_kernel(x_ref, sc_ref, sh_ref, nw_ref, w_ref, b_ref,
                   u_ref, z_ref, sq_ref, sk_ref, sv_ref, ba_ref):
    h = _rms(x_ref[0]) * nw_ref[...]
    h = h * (1.0 + sc_ref[0]) + sh_ref[0]
    p = _dot(h.astype(BF16), w_ref[...]) + b_ref[...]
    o = 0
    for ref in (u_ref, z_ref, sq_ref, sk_ref, sv_ref, ba_ref):
        w = ref.shape[-1]
        ref[0] = p[:, o:o + w]
        o += w


def _inproj(x, sc, sh, norm_w, w_cat, b_cat, tm):
    bsz, t_len, _ = x.shape
    widths = (DN_CONV_DIM, DN_V_DIM, SWA_Q_DIM, SWA_KV_DIM, SWA_KV_DIM, GATE_LANES)
    tok = lambda b, i: (b, i, 0)
    per_b = lambda b, i: (b, 0, 0)
    fixed = lambda b, i: (0, 0)
    return pl.pallas_call(
        _inproj_kernel,
        grid=(bsz, t_len // tm),
        in_specs=[pl.BlockSpec((1, tm, D_MODEL), tok),
                  pl.BlockSpec((1, 1, D_MODEL), per_b),
                  pl.BlockSpec((1, 1, D_MODEL), per_b),
                  pl.BlockSpec((1, D_MODEL), fixed),
                  pl.BlockSpec((D_MODEL, PROJ_DIM), fixed),
                  pl.BlockSpec((1, PROJ_DIM), fixed)],
        out_specs=[pl.BlockSpec((1, tm, w), tok) for w in widths],
        out_shape=[jax.ShapeDtypeStruct((bsz, t_len, w), F32) for w in widths],
        compiler_params=_params("parallel", "arbitrary"),
        name="inproj",
    )(x, sc, sh, norm_w, w_cat, b_cat)


def _deltanet_kernel(u_ref, z_ref, ba_ref, cprev_ref, s0_ref, cw_ref, alog_ref, dtb_ref, nw_ref,
                     o_ref, sfin_ref, xc_scr, s_scr):
    n = pl.program_id(1)

    @pl.when(n == 0)
    def _():
        xc_scr[0:8, :] = cprev_ref[0]
        s_scr[...] = s0_ref[0]

    xc_scr[8:8 + CHUNK, :] = u_ref[0]
    conv = xc_scr[5:5 + CHUNK, :] * cw_ref[0:1, :]
    for j in range(1, CONV_W):
        conv = conv + xc_scr[5 + j:5 + j + CHUNK, :] * cw_ref[j:j + 1, :]
    tail = xc_scr[CHUNK:CHUNK + 8, :]
    xc_scr[0:8, :] = tail
    cu = _silu(conv)

    ba = ba_ref[0]
    beta_all = jax.nn.sigmoid(ba)
    sp = ba + dtb_ref[...]
    sp = jnp.maximum(sp, 0.0) + jnp.log1p(jnp.exp(-jnp.abs(sp)))
    g_all = -jnp.exp(alog_ref[...]) * sp

    row = lax.broadcasted_iota(jnp.int32, (CHUNK, CHUNK), 0)
    col = lax.broadcasted_iota(jnp.int32, (CHUNK, CHUNK), 1)
    incl = row >= col
    strict = row > col
    lower01 = jnp.where(incl, 1.0, 0.0).astype(BF16)
    eye = jnp.where(row == col, 1.0, 0.0).astype(F32)
    merge_masks = [(row // 2 == col // 2) & strict]
    blk = 2
    while blk < CHUNK:
        merge_masks.append((row // (2 * blk) == col // (2 * blk)) & (row // blk != col // blk) & strict)
        blk *= 2
    gc_all = _dot_exact_lhs(lower01, g_all)
    gc_t = gc_all.T

    scale = DN_HEAD_DIM ** -0.5
    rep = DN_V_HEADS // DN_QK_HEADS
    for hq in range(DN_QK_HEADS):
        q = cu[:, hq * DN_HEAD_DIM:(hq + 1) * DN_HEAD_DIM]
        k = cu[:, DN_QK_DIM + hq * DN_HEAD_DIM:DN_QK_DIM + (hq + 1) * DN_HEAD_DIM]
        q = q * lax.rsqrt(jnp.sum(q * q, axis=-1, keepdims=True) + EPS) * scale
        k = k * lax.rsqrt(jnp.sum(k * k, axis=-1, keepdims=True) + EPS)
        kk = _dot_nt(k, k)
        qk = _dot_nt(q, k)
        for r in range(rep):
            hv = hq * rep + r
            v = cu[:, 2 * DN_QK_DIM + hv * DN_HEAD_DIM:2 * DN_QK_DIM + (hv + 1) * DN_HEAD_DIM]
            beta = beta_all[:, hv:hv + 1]
            gcol = gc_all[:, 8 + hv:9 + hv]
            grow = gc_t[8 + hv:9 + hv, :]
            glast = gc_all[CHUNK - 1:CHUNK, 8 + hv:9 + hv]
            decay = jnp.exp(jnp.where(incl, gcol - grow, NEG))
            eg = jnp.exp(gcol)

            m = jnp.where(strict, (kk * beta) * decay, 0.0)
            t = eye - jnp.where(merge_masks[0], m, 0.0)
            for mask in merge_masks[1:]:
                t = t - _dot(t, _dot(jnp.where(mask, m, 0.0), t))
            t0 = t.astype(BF16)
            m_hi, m_lo = _split2(m)
            resid = eye - t0.astype(F32) - (_dot(m_hi, t0) + _dot(m_lo, t0))
            t1 = _dot(t0, resid.astype(BF16)).astype(BF16)

            s = s_scr[hv]
            rhs = (v * beta - _dot(k * (beta * eg), s)).astype(BF16)
            vnew = _dot(t0, rhs) + _dot(t1, rhs)
            o = _dot(q * eg, s) + _dot(qk * decay, vnew)
            kd = k * jnp.exp(glast - gcol)
            s_scr[hv] = s * jnp.exp(glast) + _dot(kd.T, vnew)

            zg = _silu(z_ref[0, :, hv * DN_HEAD_DIM:(hv + 1) * DN_HEAD_DIM])
            o_ref[0, :, hv * DN_HEAD_DIM:(hv + 1) * DN_HEAD_DIM] = _rms(o) * nw_ref[...] * zg

    @pl.when(n == pl.num_programs(1) - 1)
    def _():
        sfin_ref[0] = s_scr[...]


def _deltanet(u, z, ba, cprev8, s0, conv_w, alog_l, dtb_l, dn_norm_w):
    bsz, t_len, _ = u.shape
    tok = lambda b, n: (b, n, 0)
    per_b = lambda b, n: (b, 0, 0)
    per_b4 = lambda b, n: (b, 0, 0, 0)
    fixed = lambda b, n: (0, 0)
    return pl.pallas_call(
        _deltanet_kernel,
        grid=(bsz, t_len // CHUNK),
        in_specs=[pl.BlockSpec((1, CHUNK, DN_CONV_DIM), tok),
                  pl.BlockSpec((1, CHUNK, DN_V_DIM), tok),
                  pl.BlockSpec((1, CHUNK, GATE_LANES), tok),
                  pl.BlockSpec((1, 8, DN_CONV_DIM), per_b),
                  pl.BlockSpec((1, DN_V_HEADS, DN_HEAD_DIM, DN_HEAD_DIM), per_b4),
                  pl.BlockSpec((CONV_W, DN_CONV_DIM), fixed),
                  pl.BlockSpec((1, GATE_LANES), fixed),
                  pl.BlockSpec((1, GATE_LANES), fixed),
                  pl.BlockSpec((1, DN_HEAD_DIM), fixed)],
        out_specs=[pl.BlockSpec((1, CHUNK, DN_V_DIM), tok),
                   pl.BlockSpec((1, DN_V_HEADS, DN_HEAD_DIM, DN_HEAD_DIM), per_b4)],
        out_shape=[jax.ShapeDtypeStruct((bsz, t_len, DN_V_DIM), F32),
                   jax.ShapeDtypeStruct((bsz, DN_V_HEADS, DN_HEAD_DIM, DN_HEAD_DIM), F32)],
        scratch_shapes=[pltpu.VMEM((CHUNK + 8, DN_CONV_DIM), F32),
                        pltpu.VMEM((DN_V_HEADS, DN_HEAD_DIM, DN_HEAD_DIM), F32)],
        compiler_params=_params("parallel", "arbitrary"),
        name="deltanet",
    )(u, z, ba, cprev8, s0, conv_w, alog_l, dtb_l, dn_norm_w)


def _bias_kernel(bucket_ref, table_ref, o_ref):
    bucket = bucket_ref[...]
    for h in range(SWA_HEADS):
        acc = jnp.zeros(bucket.shape, F32)
        for b in range(NUM_BUCKETS):
            acc = jnp.where(bucket == b, table_ref[b, h], acc)
        o_ref[h] = acc


def _rel_bias(bucket, table):
    return pl.pallas_call(
        _bias_kernel,
        in_specs=[pl.BlockSpec(memory_space=pltpu.VMEM),
                  pl.BlockSpec(memory_space=pltpu.SMEM)],
        out_specs=pl.BlockSpec(memory_space=pltpu.VMEM),
        out_shape=jax.ShapeDtypeStruct((SWA_HEADS, CHUNK, BAND), F32),
        name="rel_bias",
    )(bucket, table)


def _swa_kernel(q_ref, k0_ref, k1_ref, k2_ref, v0_ref, v1_ref, v2_ref, bias_ref, sink_ref, o_ref, *, hist):
    n = pl.program_id(1)
    kb = jnp.concatenate([k0_ref[0], k1_ref[0], k2_ref[0]], axis=0)
    vb = jnp.concatenate([v0_ref[0], v1_ref[0], v2_ref[0]], axis=0)
    kchunk = lax.broadcasted_iota(jnp.int32, (1, BAND), 1) // CHUNK
    valid = (n + hist + kchunk - WIN_CHUNKS) >= 0
    for h in range(SWA_HEADS):
        kv = h // SWA_GROUP
        q = q_ref[0, :, h * SWA_HEAD_DIM:(h + 1) * SWA_HEAD_DIM]
        k = kb[:, kv * SWA_HEAD_DIM:(kv + 1) * SWA_HEAD_DIM]
        v = vb[:, kv * SWA_HEAD_DIM:(kv + 1) * SWA_HEAD_DIM]
        s = _dot_nt(q, k) * (SWA_HEAD_DIM ** -0.5) + bias_ref[h]
        s = jnp.where(valid, s, NEG)
        sink = sink_ref[h]
        mx = jnp.maximum(jnp.max(s, axis=-1, keepdims=True), sink)
        p = jnp.exp(s - mx)
        p = p / (jnp.sum(p, axis=-1, keepdims=True) + jnp.exp(sink - mx))
        o_ref[0, :, h * SWA_HEAD_DIM:(h + 1) * SWA_HEAD_DIM] = _dot(p, v)


def _swa(q, k_all, v_all, bias, sinks_l, hist):
    bsz, t_len, _ = q.shape
    tok = lambda b, n: (b, n, 0)

    def band(j):
        return lambda b, n: (b, jnp.maximum(n + hist + j - WIN_CHUNKS, 0), 0)

    kv_specs = [pl.BlockSpec((1, CHUNK, SWA_KV_DIM), band(j)) for j in range(WIN_CHUNKS + 1)]
    return pl.pallas_call(
        functools.partial(_swa_kernel, hist=hist),
        grid=(bsz, t_len // CHUNK),
        in_specs=[pl.BlockSpec((1, CHUNK, SWA_Q_DIM), tok)] + kv_specs + kv_specs
                 + [pl.BlockSpec((SWA_HEADS, CHUNK, BAND), lambda b, n: (0, 0, 0)),
                    pl.BlockSpec(memory_space=pltpu.SMEM)],
        out_specs=pl.BlockSpec((1, CHUNK, SWA_Q_DIM), tok),
        out_shape=jax.ShapeDtypeStruct((bsz, t_len, SWA_Q_DIM), F32),
        compiler_params=_params("parallel", "arbitrary"),
        name="swa",
    )(q, k_all, k_all, k_all, v_all, v_all, v_all, bias, sinks_l)


def _outproj_kernel(odn_ref, oswa_ref, x_ref, g1_ref, sc_ref, sh_ref, wo1_ref, wo2_ref, bo_ref,
                    nw_ref, wr_ref, br_ref, x1_ref, h2_ref, ti_ref, tg_ref):
    mix = _dot(odn_ref[0].astype(BF16), wo1_ref[...]) + _dot(oswa_ref[0].astype(BF16), wo2_ref[...])
    x1 = x_ref[0] + g1_ref[0] * (mix + bo_ref[...])
    x1_ref[0] = x1
    h2 = _rms(x1) * nw_ref[...]
    h2 = h2 * (1.0 + sc_ref[0]) + sh_ref[0]
    h2_ref[0] = h2.astype(BF16)

    logits = _dot_x3(h2, wr_ref[...]) + br_ref[...]
    lane = lax.broadcasted_iota(jnp.int32, logits.shape, 1)
    vals, idxs = [], []
    for _ in range(TOP_K):
        m = jnp.max(logits, axis=-1, keepdims=True)
        i = jnp.min(jnp.where(logits == m, lane, LANES), axis=-1, keepdims=True)
        vals.append(m)
        idxs.append(i)
        logits = jnp.where(lane == i, -jnp.inf, logits)
    es = [jnp.exp(v - vals[0]) for v in vals]
    den = es[0] + es[1] + es[2] + es[3]
    ti = jnp.zeros(lane.shape, jnp.int32)
    tg = jnp.zeros(lane.shape, F32)
    for kk in range(TOP_K):
        ti = jnp.where(lane == kk, idxs[kk], ti)
        tg = jnp.where(lane == kk, es[kk] / den, tg)
    ti_ref[0] = ti
    tg_ref[0] = tg


def _outproj(odn, oswa, x, g1, sc, sh, wo1, wo2, bo, norm_w, wr, br, tm):
    bsz, t_len, _ = x.shape
    tok = lambda b, i: (b, i, 0)
    per_b = lambda b, i: (b, 0, 0)
    fixed = lambda b, i: (0, 0)
    return pl.pallas_call(
        _outproj_kernel,
        grid=(bsz, t_len // tm),
        in_specs=[pl.BlockSpec((1, tm, DN_V_DIM), tok),
                  pl.BlockSpec((1, tm, SWA_Q_DIM), tok),
                  pl.BlockSpec((1, tm, D_MODEL), tok),
                  pl.BlockSpec((1, 1, D_MODEL), per_b),
                  pl.BlockSpec((1, 1, D_MODEL), per_b),
                  pl.BlockSpec((1, 1, D_MODEL), per_b),
                  pl.BlockSpec((DN_V_DIM, D_MODEL), fixed),
                  pl.BlockSpec((SWA_Q_DIM, D_MODEL), fixed),
                  pl.BlockSpec((1, D_MODEL), fixed),
                  pl.BlockSpec((1, D_MODEL), fixed),
                  pl.BlockSpec((D_MODEL, LANES), fixed),
                  pl.BlockSpec((1, LANES), fixed)],
        out_specs=[pl.BlockSpec((1, tm, D_MODEL), tok),
                   pl.BlockSpec((1, tm, D_MODEL), tok),
                   pl.BlockSpec((1, tm, LANES), tok),
                   pl.BlockSpec((1, tm, LANES), tok)],
        out_shape=[jax.ShapeDtypeStruct((bsz, t_len, D_MODEL), F32),
                   jax.ShapeDtypeStruct((bsz, t_len, D_MODEL), BF16),
                   jax.ShapeDtypeStruct((bsz, t_len, LANES), jnp.int32),
                   jax.ShapeDtypeStruct((bsz, t_len, LANES), F32)],
        compiler_params=_params("parallel", "arbitrary"),
        name="outproj_router",
    )(odn, oswa, x, g1, sc, sh, wo1, wo2, bo, norm_w, wr, br)


def _moe_kernel(be_ref, xs_ref, w1_ref, b1_ref, w2_ref, b2_ref, y_ref):
    del be_ref
    up = _dot(xs_ref[...], w1_ref[0]) + b1_ref[0]
    glu = jnp.minimum(up[:, :D_FF], SWIGLU_LIMIT)
    lin = jnp.clip(up[:, D_FF:], -SWIGLU_LIMIT, SWIGLU_LIMIT)
    act = glu * jax.nn.sigmoid(SWIGLU_ALPHA * glu) * (lin + 1.0)
    y_ref[...] = (_dot(act.astype(BF16), w2_ref[0]) + b2_ref[0]).astype(y_ref.dtype)


def _moe_experts(blk_expert, xs, w1, b1, w2, b2):
    n_rows = xs.shape[0]
    n_blocks = n_rows // MOE_ROWS
    grid_spec = pltpu.PrefetchScalarGridSpec(
        num_scalar_prefetch=1,
        grid=(n_blocks,),
        in_specs=[pl.BlockSpec((MOE_ROWS, D_MODEL), lambda i, be: (i, 0)),
                  pl.BlockSpec((1, D_MODEL, 2 * D_FF), lambda i, be: (be[i], 0, 0)),
                  pl.BlockSpec((1, 1, 2 * D_FF), lambda i, be: (be[i], 0, 0)),
                  pl.BlockSpec((1, D_FF, D_MODEL), lambda i, be: (be[i], 0, 0)),
                  pl.BlockSpec((1, 1, D_MODEL), lambda i, be: (be[i], 0, 0))],
        out_specs=pl.BlockSpec((MOE_ROWS, D_MODEL), lambda i, be: (i, 0)),
    )
    return pl.pallas_call(
        _moe_kernel,
        grid_spec=grid_spec,
        out_shape=jax.ShapeDtypeStruct((n_rows, D_MODEL), F32),
        compiler_params=_params("arbitrary"),
        name="moe_experts",
    )(blk_expert, xs, w1, b1, w2, b2)


def _final_kernel(x1_ref, yg_ref, tg_ref, g2_ref, nw_ref, y_ref):
    tg = tg_ref[0]
    moe = yg_ref[0, 0] * tg[:, 0:1]
    for kk in range(1, TOP_K):
        moe = moe + yg_ref[kk, 0] * tg[:, kk:kk + 1]
    x2 = x1_ref[0] + g2_ref[0] * moe
    y_ref[0] = _rms(x2) * nw_ref[...]


def _final(x1, yg, tg, g2, norm_w, tm):
    bsz, t_len, _ = x1.shape
    tok = lambda b, i: (b, i, 0)
    return pl.pallas_call(
        _final_kernel,
        grid=(bsz, t_len // tm),
        in_specs=[pl.BlockSpec((1, tm, D_MODEL), tok),
                  pl.BlockSpec((TOP_K, 1, tm, D_MODEL), lambda b, i: (0, b, i, 0)),
                  pl.BlockSpec((1, tm, LANES), tok),
                  pl.BlockSpec((1, 1, D_MODEL), lambda b, i: (b, 0, 0)),
                  pl.BlockSpec((1, D_MODEL), lambda b, i: (0, 0))],
        out_specs=pl.BlockSpec((1, tm, D_MODEL), tok),
        out_shape=jax.ShapeDtypeStruct((bsz, t_len, D_MODEL), F32),
        compiler_params=_params("parallel", "arbitrary"),
        name="combine_final",
    )(x1, yg, tg, g2, norm_w)


def _t5_bucket(rel):
    half = NUM_BUCKETS // 2
    max_exact = half // 2
    ret = jnp.where(rel > 0, half, 0)
    n = jnp.abs(rel)
    nf = jnp.maximum(n, 1).astype(jnp.float32)
    large = max_exact + (jnp.log(nf / max_exact) / math.log(MAX_DISTANCE / max_exact)
                         * (half - max_exact)).astype(jnp.int32)
    large = jnp.minimum(large, half - 1)
    return ret + jnp.where(n < max_exact, n, large)


def _route(top_i):
    n_tok = top_i.shape[0]
    n_asg = n_tok * TOP_K
    flat_e = top_i.reshape(-1)
    order = jnp.argsort(flat_e, stable=True)
    e_sorted = flat_e[order]
    sizes = jnp.zeros((N_EXPERTS,), jnp.int32).at[flat_e].add(1)
    start = jnp.cumsum(sizes) - sizes
    padded = (sizes + MOE_ROWS - 1) // MOE_ROWS * MOE_ROWS
    pend = jnp.cumsum(padded)
    pstart = pend - padded
    dest = pstart[e_sorted] + jnp.arange(n_asg, dtype=jnp.int32) - start[e_sorted]
    n_blocks = -(-n_asg // MOE_ROWS) + N_EXPERTS
    n_rows = n_blocks * MOE_ROWS
    tok_pad = jnp.zeros((n_rows,), jnp.int32).at[dest].set((order // TOP_K).astype(jnp.int32))
    pos = jnp.zeros((n_asg,), jnp.int32).at[order].set(dest)
    blk_expert = jnp.minimum(
        jnp.searchsorted(pend, jnp.arange(n_blocks, dtype=jnp.int32) * MOE_ROWS, side='right'),
        N_EXPERTS - 1).astype(jnp.int32)
    return tok_pad, pos.reshape(n_tok, TOP_K), blk_expert


def kernel(x_prompt, x_sample, c_prompt, c_sample, state_conv, state_delta, cache_swa_k, cache_swa_v,
           w_ada, b_ada, norm_mix, w_in, b_in, conv_w, a_log, dt_bias, dn_norm_w, sinks, rel_bias,
           w_out, b_out, norm_ffn, w_router, b_router, w1, b1, w2, b2, norm_final):
    depth = w_ada.shape[0]
    assert depth == 1, "the final norm is fused into the layer's combine step"
    bp, tp, _ = x_prompt.shape
    bs, ts, _ = x_sample.shape
    groups = [dict(x=x_prompt, c=c_prompt, hist=0), dict(x=x_sample, c=c_sample, hist=WIN_CHUNKS)]

    q_rel = jnp.arange(CHUNK)
    k_rel = jnp.arange(BAND) - WIN_CHUNKS * CHUNK
    bucket = _t5_bucket(k_rel[None, :] - q_rel[:, None]).astype(jnp.int32)
    bias = _rel_bias(bucket, rel_bias)

    nf_w = norm_final.reshape(1, D_MODEL)
    outs = {g: dict(conv=[], delta=[], k=[], v=[]) for g in range(2)}
    xs_cur = [x_prompt, x_sample]

    for l in range(depth):
        o1 = DN_CONV_DIM + DN_V_DIM
        o2 = o1 + 2 * DN_V_HEADS
        wl, bl = w_in[l], b_in[l]
        w_cat = jnp.concatenate(
            [wl[:, :o1], wl[:, o2:], wl[:, o1:o2], jnp.zeros((D_MODEL, GATE_LANES - 2 * DN_V_HEADS), F32)],
            axis=1).astype(BF16)
        b_cat = jnp.concatenate(
            [bl[:o1], bl[o2:], bl[o1:o2], jnp.zeros((GATE_LANES - 2 * DN_V_HEADS,), F32)]).reshape(1, PROJ_DIM)
        pad8 = jnp.zeros((DN_V_HEADS,), F32)
        padr = jnp.zeros((GATE_LANES - 2 * DN_V_HEADS,), F32)
        alog_l = jnp.concatenate([pad8, a_log[l], padr]).reshape(1, GATE_LANES)
        dtb_l = jnp.concatenate([pad8, dt_bias[l], padr]).reshape(1, GATE_LANES)
        wo1 = w_out[l][:DN_V_DIM].astype(BF16)
        wo2 = w_out[l][DN_V_DIM:].astype(BF16)
        wr = jnp.concatenate([w_router[l], jnp.zeros((D_MODEL, LANES - N_EXPERTS), F32)], axis=1)
        br = jnp.concatenate([b_router[l], jnp.full((LANES - N_EXPERTS,), NEG, F32)]).reshape(1, LANES)
        w1b = w1[l].astype(BF16)
        w2b = w2[l].astype(BF16)
        b1l = b1[l].reshape(N_EXPERTS, 1, 2 * D_FF)
        b2l = b2[l].reshape(N_EXPERTS, 1, D_MODEL)

        mod = _modulation(jnp.concatenate([c_prompt, c_sample], axis=0), w_ada[l], b_ada[l])
        stage = []
        for gi, grp in enumerate(groups):
            x = xs_cur[gi]
            bsz, t_len, _ = x.shape
            tm = min(TOKEN_TILE, t_len)
            m = mod[:bp] if gi == 0 else mod[bp:]
            sh1, sc1, g1, sh2, sc2, g2 = [a.reshape(bsz, 1, D_MODEL) for a in jnp.split(m, 6, axis=-1)]
            u, z, sq, sk, sv, ba = _inproj(x, sc1, sh1, norm_mix[l].reshape(1, D_MODEL), w_cat, b_cat, tm)
            if gi == 0:
                cprev = jnp.zeros((bsz, CONV_W - 1, DN_CONV_DIM), F32)
                s0 = jnp.zeros((bsz, DN_V_HEADS, DN_HEAD_DIM, DN_HEAD_DIM), F32)
                k_all, v_all = sk, sv
            else:
                cprev = state_conv[l]
                s0 = state_delta[l]
                k_all = jnp.concatenate([cache_swa_k[l].reshape(bsz, -1, SWA_KV_DIM), sk], axis=1)
                v_all = jnp.concatenate([cache_swa_v[l].reshape(bsz, -1, SWA_KV_DIM), sv], axis=1)
            swa_len = cache_swa_k.shape[2]
            cprev8 = jnp.concatenate([jnp.zeros((bsz, 8 - (CONV_W - 1), DN_CONV_DIM), F32), cprev], axis=1)
            o_dn, s_new = _deltanet(u, z, ba, cprev8, s0, conv_w[l], alog_l, dtb_l,
                                    dn_norm_w[l].reshape(1, DN_HEAD_DIM))
            o_swa = _swa(sq, k_all, v_all, bias, sinks[l], grp["hist"])
            x1, h2, ti, tg = _outproj(o_dn, o_swa, x, g1, sc2, sh2, wo1, wo2, b_out[l].reshape(1, D_MODEL),
                                      norm_ffn[l].reshape(1, D_MODEL), wr, br, tm)
            conv_new = jnp.concatenate([cprev, u], axis=1)[:, -(CONV_W - 1):]
            k_state = k_all[:, k_all.shape[1] - swa_len:].reshape(bsz, swa_len, SWA_KV_HEADS, SWA_HEAD_DIM)
            v_state = v_all[:, v_all.shape[1] - swa_len:].reshape(bsz, swa_len, SWA_KV_HEADS, SWA_HEAD_DIM)
            outs[gi]["conv"].append(conv_new)
            outs[gi]["delta"].append(s_new)
            outs[gi]["k"].append(k_state)
            outs[gi]["v"].append(v_state)
            stage.append(dict(x1=x1, h2=h2, ti=ti, tg=tg, g2=g2, tm=tm))

        h2_all = jnp.concatenate([s["h2"].reshape(-1, D_MODEL) for s in stage], axis=0)
        ti_all = jnp.concatenate([s["ti"].reshape(-1, LANES)[:, :TOP_K] for s in stage], axis=0)
        n_tok = h2_all.shape[0]
        tok_pad, pos, blk_expert = _route(ti_all)
        xs = jnp.take(h2_all, tok_pad, axis=0)
        y = _moe_experts(blk_expert, xs, w1b, b1l, w2b, b2l)
        yg = jnp.take(y, pos.T, axis=0)
        off = 0
        for gi, s in enumerate(stage):
            bsz, t_len, _ = s["x1"].shape
            ygg = yg[:, off:off + bsz * t_len].reshape(TOP_K, bsz, t_len, D_MODEL)
            off += bsz * t_len
            xs_cur[gi] = _final(s["x1"], ygg, s["tg"], s["g2"], nf_w, s["tm"])

    res = [xs_cur[0], xs_cur[1]]
    for gi in range(2):
        for name in ("conv", "delta", "k", "v"):
            res.append(jnp.stack(outs[gi][name]))
    return tuple(res)
```

```python
import functools
import math

import jax
import jax.numpy as jnp
from jax import lax
from jax.experimental import pallas as pl
from jax.experimental.pallas import tpu as pltpu

F32 = jnp.float32
BF16 = jnp.bfloat16

D_MODEL = 1024
CHUNK = 64
EPS = 1e-6
DN_QK_HEADS = 4
DN_V_HEADS = 8
DN_HEAD_DIM = 64
DN_QK_DIM = DN_QK_HEADS * DN_HEAD_DIM
DN_V_DIM = DN_V_HEADS * DN_HEAD_DIM
DN_CONV_DIM = 2 * DN_QK_DIM + DN_V_DIM
CONV_W = 4
SWA_HEADS = 8
SWA_KV_HEADS = 2
SWA_HEAD_DIM = 64
SWA_GROUP = SWA_HEADS // SWA_KV_HEADS
SWA_Q_DIM = SWA_HEADS * SWA_HEAD_DIM
SWA_KV_DIM = SWA_KV_HEADS * SWA_HEAD_DIM
WINDOW = 128
WIN_CHUNKS = WINDOW // CHUNK
BAND = (WIN_CHUNKS + 1) * CHUNK
NUM_BUCKETS = 32
MAX_DISTANCE = 128
N_EXPERTS = 32
TOP_K = 4
D_FF = 1024
SWIGLU_ALPHA = 1.702
SWIGLU_LIMIT = 7.0

LANES = 128
GATE_LANES = LANES
PROJ_DIM = DN_CONV_DIM + DN_V_DIM + SWA_Q_DIM + 2 * SWA_KV_DIM + GATE_LANES
MOE_ROWS = 512
TOKEN_TILE = 512
DN_SEQS_PER_STEP = 2
VMEM_LIMIT = 56 * 1024 * 1024
NEG = -1e30


def _params(*sem):
    return pltpu.CompilerParams(dimension_semantics=sem, vmem_limit_bytes=VMEM_LIMIT)


def _split2(a):
    hi = a.astype(BF16)
    lo = (a - hi.astype(F32)).astype(BF16)
    return hi, lo


def _dot(a, b):
    return jnp.dot(a, b, preferred_element_type=F32)


def _dot_nt(a, b):
    return lax.dot_general(a, b, (((1,), (1,)), ((), ())), preferred_element_type=F32)


def _dot_x3(a, b):
    a1, a2 = _split2(a)
    b1, b2 = _split2(b)
    return _dot(a1, b1) + (_dot(a1, b2) + _dot(a2, b1))


def _dot_exact_lhs(l01, g):
    g1 = g.astype(BF16)
    r = g - g1.astype(F32)
    g2 = r.astype(BF16)
    g3 = (r - g2.astype(F32)).astype(BF16)
    return _dot(l01, g1) + (_dot(l01, g2) + _dot(l01, g3))


def _silu(x):
    return x * jax.nn.sigmoid(x)


def _rms(x):
    return x * lax.rsqrt(jnp.mean(x * x, axis=-1, keepdims=True) + EPS)


def _mod_kernel(c_ref, w_ref, b_ref, o_ref):
    o_ref[...] = _dot_x3(_silu(c_ref[...]), w_ref[...]) + b_ref[...]


def _modulation(c, w_ada, b_ada):
    nb = c.shape[0]
    n_out = w_ada.shape[1]
    tn = 1024
    return pl.pallas_call(
        _mod_kernel,
        grid=(n_out // tn,),
        in_specs=[pl.BlockSpec((nb, D_MODEL), lambda j: (0, 0)),
                  pl.BlockSpec((D_MODEL, tn), lambda j: (0, j)),
                  pl.BlockSpec((1, tn), lambda j: (0, j))],
        out_specs=pl.BlockSpec((nb, tn), lambda j: (0, j)),
        out_shape=jax.ShapeDtypeStruct((nb, n_out), F32),
        compiler_params=_params("arbitrary"),
        name="modulation",
    )(c, w_ada, b_ada.reshape(1, n_out))


def _inproj_kernel(x_ref, sc_ref, sh_ref, nw_ref, w_ref, b_ref,
                   u_ref, z_ref, sq_ref, sk_ref, sv_ref, ba_ref):
    h = _rms(x_ref[0]) * nw_ref[...]
    h = h * (1.0 + sc_ref[0]) + sh_ref[0]
    p = _dot(h.astype(BF16), w_ref[...]) + b_ref[...]
    o = 0
    for ref in (u_ref, z_ref, sq_ref, sk_ref, sv_ref, ba_ref):
        w = ref.shape[-1]
        ref[0] = p[:, o:o + w]
        o += w


def _inproj(x, sc, sh, norm_w, w_cat, b_cat, tm):
    bsz, t_len, _ = x.shape
    widths = (DN_CONV_DIM, DN_V_DIM, SWA_Q_DIM, SWA_KV_DIM, SWA_KV_DIM, GATE_LANES)
    tok = lambda b, i: (b, i, 0)
    per_b = lambda b, i: (b, 0, 0)
    fixed = lambda b, i: (0, 0)
    return pl.pallas_call(
        _inproj_kernel,
        grid=(bsz, t_len // tm),
        in_specs=[pl.BlockSpec((1, tm, D_MODEL), tok),
                  pl.BlockSpec((1, 1, D_MODEL), per_b),
                  pl.BlockSpec((1, 1, D_MODEL), per_b),
                  pl.BlockSpec((1, D_MODEL), fixed),
                  pl.BlockSpec((D_MODEL, PROJ_DIM), fixed),
                  pl.BlockSpec((1, PROJ_DIM), fixed)],
        out_specs=[pl.BlockSpec((1, tm, w), tok) for w in widths],
        out_shape=[jax.ShapeDtypeStruct((bsz, t_len, w), F32) for w in widths],
        compiler_params=_params("parallel", "arbitrary"),
        name="inproj",
    )(x, sc, sh, norm_w, w_cat, b_cat)


def _bmm(a, b):
    return lax.dot_general(a.astype(BF16), b.astype(BF16), (((2,), (1,)), ((0,), (0,))),
                           preferred_element_type=F32)


def _bmm_nt(a, b):
    return lax.dot_general(a.astype(BF16), b.astype(BF16), (((2,), (2,)), ((0,), (0,))),
                           preferred_element_type=F32)


def _deltanet_kernel(u_ref, z_ref, ba_ref, cprev_ref, s0_ref, cw_ref, alog_ref, dtb_ref, nw_ref,
                     o_ref, sfin_ref, xc_scr, s_scr, *, nb):
    n = pl.program_id(1)
    hd = DN_HEAD_DIM
    rep = DN_V_HEADS // DN_QK_HEADS

    @pl.when(n == 0)
    def _():
        for i in range(nb):
            xc_scr[i, 0:8, :] = cprev_ref[i]
            s_scr[i * DN_V_HEADS:(i + 1) * DN_V_HEADS] = s0_ref[i]

    row = lax.broadcasted_iota(jnp.int32, (CHUNK, CHUNK), 0)
    col = lax.broadcasted_iota(jnp.int32, (CHUNK, CHUNK), 1)
    incl = row >= col
    strict = row > col
    lower01 = jnp.where(incl, 1.0, 0.0).astype(BF16)
    eye = jnp.where(row == col, 1.0, 0.0).astype(F32)
    merge_masks = [(row // 2 == col // 2) & strict]
    blk = 2
    while blk < CHUNK:
        merge_masks.append((row // (2 * blk) == col // (2 * blk)) & (row // blk != col // blk) & strict)
        blk *= 2

    qk_rows, k_items, q_items, v_items, z_items = [], [], [], [], []
    beta_items, gcol_items, grow_items = [], [], []
    for i in range(nb):
        xc_scr[i, 8:8 + CHUNK, :] = u_ref[i]
        conv = xc_scr[i, 5:5 + CHUNK, :] * cw_ref[0:1, :]
        for j in range(1, CONV_W):
            conv = conv + xc_scr[i, 5 + j:5 + j + CHUNK, :] * cw_ref[j:j + 1, :]
        tail = xc_scr[i, CHUNK:CHUNK + 8, :]
        xc_scr[i, 0:8, :] = tail
        cu = _silu(conv)

        ba = ba_ref[i]
        beta_all = jax.nn.sigmoid(ba)
        sp = ba + dtb_ref[...]
        sp = jnp.maximum(sp, 0.0) + jnp.log1p(jnp.exp(-jnp.abs(sp)))
        g_all = -jnp.exp(alog_ref[...]) * sp
        gc_all = _dot_exact_lhs(lower01, g_all)
        gc_t = gc_all.T

        for hq in range(DN_QK_HEADS):
            q = cu[:, hq * hd:(hq + 1) * hd]
            k = cu[:, DN_QK_DIM + hq * hd:DN_QK_DIM + (hq + 1) * hd]
            q = q * lax.rsqrt(jnp.sum(q * q, axis=-1, keepdims=True) + EPS) * (hd ** -0.5)
            k = k * lax.rsqrt(jnp.sum(k * k, axis=-1, keepdims=True) + EPS)
            qk_rows.append(jnp.concatenate([q, k], axis=0))
            for r in range(rep):
                hv = hq * rep + r
                q_items.append(q)
                k_items.append(k)
                v_items.append(cu[:, 2 * DN_QK_DIM + hv * hd:2 * DN_QK_DIM + (hv + 1) * hd])
                z_items.append(z_ref[i, :, hv * hd:(hv + 1) * hd])
                beta_items.append(beta_all[:, hv:hv + 1])
                gcol_items.append(gc_all[:, 8 + hv:9 + hv])
                grow_items.append(gc_t[8 + hv:9 + hv, :])

    q = jnp.stack(q_items)
    k = jnp.stack(k_items)
    v = jnp.stack(v_items)
    beta = jnp.stack(beta_items)
    gcol = jnp.stack(gcol_items)
    grow = jnp.stack(grow_items)
    glast = gcol[:, CHUNK - 1:CHUNK, :]

    qkk = _bmm_nt(jnp.stack(qk_rows), jnp.stack(k_items[::rep]))
    qkk = jnp.stack([qkk[j // rep] for j in range(nb * DN_V_HEADS)])
    qk, kk = qkk[:, :CHUNK], qkk[:, CHUNK:]

    decay = jnp.exp(jnp.where(incl, gcol - grow, NEG))
    eg = jnp.exp(gcol)

    m = jnp.where(strict, (kk * beta) * decay, 0.0)
    t = eye - jnp.where(merge_masks[0], m, 0.0)
    for mask in merge_masks[1:]:
        t = t - _bmm(t, _bmm(jnp.where(mask, m, 0.0), t))
    t0 = t.astype(BF16)
    m_hi, m_lo = _split2(m)
    mt0 = _bmm(jnp.concatenate([m_hi, m_lo], axis=2), jnp.concatenate([t0, t0], axis=1))
    resid = eye - t0.astype(F32) - mt0
    t1 = _bmm(t0, resid).astype(BF16)

    s = s_scr[...]
    ks = _bmm(jnp.concatenate([k * (beta * eg), q * eg], axis=1), s)
    rhs = (v * beta - ks[:, :CHUNK]).astype(BF16)
    vnew = _bmm(jnp.concatenate([t0, t1], axis=2), jnp.concatenate([rhs, rhs], axis=1))
    o = ks[:, CHUNK:] + _bmm(qk * decay, vnew)
    kd = k * jnp.exp(glast - gcol)
    s_scr[...] = s * jnp.exp(glast) + _bmm(jnp.swapaxes(kd, 1, 2), vnew)

    o = _rms(o) * nw_ref[...] * _silu(jnp.stack(z_items))
    for i in range(nb):
        for hv in range(DN_V_HEADS):
            o_ref[i, :, hv * hd:(hv + 1) * hd] = o[i * DN_V_HEADS + hv]

    @pl.when(n == pl.num_programs(1) - 1)
    def _():
        for i in range(nb):
            sfin_ref[i] = s_scr[i * DN_V_HEADS:(i + 1) * DN_V_HEADS]


def _deltanet(u, z, ba, cprev8, s0, conv_w, alog_l, dtb_l, dn_norm_w):
    bsz, t_len, _ = u.shape
    nb = DN_SEQS_PER_STEP
    tok = lambda b, n: (b, n, 0)
    per_b = lambda b, n: (b, 0, 0)
    per_b4 = lambda b, n: (b, 0, 0, 0)
    fixed = lambda b, n: (0, 0)
    return pl.pallas_call(
        functools.partial(_deltanet_kernel, nb=nb),
        grid=(bsz // nb, t_len // CHUNK),
        in_specs=[pl.BlockSpec((nb, CHUNK, DN_CONV_DIM), tok),
                  pl.BlockSpec((nb, CHUNK, DN_V_DIM), tok),
                  pl.BlockSpec((nb, CHUNK, GATE_LANES), tok),
                  pl.BlockSpec((nb, 8, DN_CONV_DIM), per_b),
                  pl.BlockSpec((nb, DN_V_HEADS, DN_HEAD_DIM, DN_HEAD_DIM), per_b4),
                  pl.BlockSpec((CONV_W, DN_CONV_DIM), fixed),
                  pl.BlockSpec((1, GATE_LANES), fixed),
                  pl.BlockSpec((1, GATE_LANES), fixed),
                  pl.BlockSpec((1, DN_HEAD_DIM), fixed)],
        out_specs=[pl.BlockSpec((nb, CHUNK, DN_V_DIM), tok),
                   pl.BlockSpec((nb, DN_V_HEADS, DN_HEAD_DIM, DN_HEAD_DIM), per_b4)],
        out_shape=[jax.ShapeDtypeStruct((bsz, t_len, DN_V_DIM), F32),
                   jax.ShapeDtypeStruct((bsz, DN_V_HEADS, DN_HEAD_DIM, DN_HEAD_DIM), F32)],
        scratch_shapes=[pltpu.VMEM((nb, CHUNK + 8, DN_CONV_DIM), F32),
                        pltpu.VMEM((nb * DN_V_HEADS, DN_HEAD_DIM, DN_HEAD_DIM), F32)],
        compiler_params=_params("parallel", "arbitrary"),
        name="deltanet",
    )(u, z, ba, cprev8, s0, conv_w, alog_l, dtb_l, dn_norm_w)


def _bias_kernel(bucket_ref, table_ref, o_ref):
    bucket = bucket_ref[...]
    for h in range(SWA_HEADS):
        acc = jnp.zeros(bucket.shape, F32)
        for b in range(NUM_BUCKETS):
            acc = jnp.where(bucket == b, table_ref[b, h], acc)
        o_ref[h] = acc


def _rel_bias(bucket, table):
    return pl.pallas_call(
        _bias_kernel,
        in_specs=[pl.BlockSpec(memory_space=pltpu.VMEM),
                  pl.BlockSpec(memory_space=pltpu.SMEM)],
        out_specs=pl.BlockSpec(memory_space=pltpu.VMEM),
        out_shape=jax.ShapeDtypeStruct((SWA_HEADS, CHUNK, BAND), F32),
        name="rel_bias",
    )(bucket, table)


def _swa_kernel(q_ref, k0_ref, k1_ref, k2_ref, v0_ref, v1_ref, v2_ref, bias_ref, sink_ref, o_ref, *, hist):
    n = pl.program_id(1)
    kb = jnp.concatenate([k0_ref[0], k1_ref[0], k2_ref[0]], axis=0)
    vb = jnp.concatenate([v0_ref[0], v1_ref[0], v2_ref[0]], axis=0)
    kchunk = lax.broadcasted_iota(jnp.int32, (1, 1, BAND), 2) // CHUNK
    valid = (n + hist + kchunk - WIN_CHUNKS) >= 0
    hd = SWA_HEAD_DIM
    q = jnp.stack([jnp.concatenate([q_ref[0, :, (kv * SWA_GROUP + g) * hd:(kv * SWA_GROUP + g + 1) * hd]
                                    for g in range(SWA_GROUP)], axis=0) for kv in range(SWA_KV_HEADS)])
    k = jnp.stack([kb[:, kv * hd:(kv + 1) * hd] for kv in range(SWA_KV_HEADS)])
    v = jnp.stack([vb[:, kv * hd:(kv + 1) * hd] for kv in range(SWA_KV_HEADS)])
    s = _bmm_nt(q, k) * (hd ** -0.5) + bias_ref[...]
    s = jnp.where(valid, s, NEG)
    sink = sink_ref[...]
    mx = jnp.maximum(jnp.max(s, axis=-1, keepdims=True), sink)
    p = jnp.exp(s - mx)
    p = p / (jnp.sum(p, axis=-1, keepdims=True) + jnp.exp(sink - mx))
    o = _bmm(p, v)
    for kv in range(SWA_KV_HEADS):
        for g in range(SWA_GROUP):
            h = kv * SWA_GROUP + g
            o_ref[0, :, h * hd:(h + 1) * hd] = o[kv, g * CHUNK:(g + 1) * CHUNK]


def _swa(q, k_all, v_all, bias, sinks_l, hist):
    bsz, t_len, _ = q.shape
    tok = lambda b, n: (b, n, 0)

    def band(j):
        return lambda b, n: (b, jnp.maximum(n + hist + j - WIN_CHUNKS, 0), 0)

    kv_specs = [pl.BlockSpec((1, CHUNK, SWA_KV_DIM), band(j)) for j in range(WIN_CHUNKS + 1)]
    return pl.pallas_call(
        functools.partial(_swa_kernel, hist=hist),
        grid=(bsz, t_len // CHUNK),
        in_specs=[pl.BlockSpec((1, CHUNK, SWA_Q_DIM), tok)] + kv_specs + kv_specs
                 + [pl.BlockSpec((SWA_KV_HEADS, SWA_GROUP * CHUNK, BAND), lambda b, n: (0, 0, 0)),
                    pl.BlockSpec((SWA_KV_HEADS, SWA_GROUP * CHUNK, 1), lambda b, n: (0, 0, 0))],
        out_specs=pl.BlockSpec((1, CHUNK, SWA_Q_DIM), tok),
        out_shape=jax.ShapeDtypeStruct((bsz, t_len, SWA_Q_DIM), F32),
        compiler_params=_params("parallel", "arbitrary"),
        name="swa",
    )(q, k_all, k_all, k_all, v_all, v_all, v_all, bias, sinks_l)


def _outproj_kernel(odn_ref, oswa_ref, x_ref, g1_ref, sc_ref, sh_ref, wo1_ref, wo2_ref, bo_ref,
                    nw_ref, wr_ref, br_ref, cnt0_ref, x1_ref, h2_ref, ti_ref, tg_ref, rk_ref, cnt_ref, cnt_scr):
    first = (pl.program_id(0) == 0) & (pl.program_id(1) == 0)

    @pl.when(first)
    def _():
        cnt_scr[...] = cnt0_ref[...]

    mix = _dot(odn_ref[0].astype(BF16), wo1_ref[...]) + _dot(oswa_ref[0].astype(BF16), wo2_ref[...])
    x1 = x_ref[0] + g1_ref[0] * (mix + bo_ref[...])
    x1_ref[0] = x1
    h2 = _rms(x1) * nw_ref[...]
    h2 = h2 * (1.0 + sc_ref[0]) + sh_ref[0]
    h2_ref[0] = h2.astype(BF16)

    logits = _dot_x3(h2, wr_ref[...]) + br_ref[...]
    lane = lax.broadcasted_iota(jnp.int32, logits.shape, 1)
    vals, idxs = [], []
    for _ in range(TOP_K):
        m = jnp.max(logits, axis=-1, keepdims=True)
        i = jnp.min(jnp.where(logits == m, lane, LANES), axis=-1, keepdims=True)
        vals.append(m)
        idxs.append(i)
        logits = jnp.where(lane == i, -jnp.inf, logits)
    es = [jnp.exp(v - vals[0]) for v in vals]
    den = es[0] + es[1] + es[2] + es[3]
    tm = lane.shape[0]
    onehot = jnp.zeros(lane.shape, F32)
    for kk in range(TOP_K):
        onehot = jnp.where(lane == idxs[kk], 1.0, onehot)
    r_i = lax.broadcasted_iota(jnp.int32, (tm, tm), 0)
    c_i = lax.broadcasted_iota(jnp.int32, (tm, tm), 1)
    before = _dot(jnp.where(r_i > c_i, 1.0, 0.0).astype(BF16), onehot.astype(BF16)) + cnt_scr[...]
    cnt_scr[...] = cnt_scr[...] + jnp.sum(onehot, axis=0, keepdims=True)

    ti = jnp.zeros(lane.shape, jnp.int32)
    tg = jnp.zeros(lane.shape, F32)
    rk = jnp.zeros(lane.shape, jnp.int32)
    for kk in range(TOP_K):
        rank = jnp.sum(jnp.where(lane == idxs[kk], before, 0.0), axis=-1, keepdims=True)
        ti = jnp.where(lane == kk, idxs[kk], ti)
        tg = jnp.where(lane == kk, es[kk] / den, tg)
        rk = jnp.where(lane == kk, rank.astype(jnp.int32), rk)
    ti_ref[0] = ti
    tg_ref[0] = tg
    rk_ref[0] = rk
    cnt_ref[...] = cnt_scr[...]


def _outproj(odn, oswa, x, g1, sc, sh, wo1, wo2, bo, norm_w, wr, br, cnt0, tm):
    bsz, t_len, _ = x.shape
    tok = lambda b, i: (b, i, 0)
    per_b = lambda b, i: (b, 0, 0)
    fixed = lambda b, i: (0, 0)
    return pl.pallas_call(
        _outproj_kernel,
        grid=(bsz, t_len // tm),
        in_specs=[pl.BlockSpec((1, tm, DN_V_DIM), tok),
                  pl.BlockSpec((1, tm, SWA_Q_DIM), tok),
                  pl.BlockSpec((1, tm, D_MODEL), tok),
                  pl.BlockSpec((1, 1, D_MODEL), per_b),
                  pl.BlockSpec((1, 1, D_MODEL), per_b),
                  pl.BlockSpec((1, 1, D_MODEL), per_b),
                  pl.BlockSpec((DN_V_DIM, D_MODEL), fixed),
                  pl.BlockSpec((SWA_Q_DIM, D_MODEL), fixed),
                  pl.BlockSpec((1, D_MODEL), fixed),
                  pl.BlockSpec((1, D_MODEL), fixed),
                  pl.BlockSpec((D_MODEL, LANES), fixed),
                  pl.BlockSpec((1, LANES), fixed),
                  pl.BlockSpec((1, LANES), fixed)],
        out_specs=[pl.BlockSpec((1, tm, D_MODEL), tok),
                   pl.BlockSpec((1, tm, D_MODEL), tok),
                   pl.BlockSpec((1, tm, LANES), tok),
                   pl.BlockSpec((1, tm, LANES), tok),
                   pl.BlockSpec((1, tm, LANES), tok),
                   pl.BlockSpec((1, LANES), fixed)],
        out_shape=[jax.ShapeDtypeStruct((bsz, t_len, D_MODEL), F32),
                   jax.ShapeDtypeStruct((bsz, t_len, D_MODEL), BF16),
                   jax.ShapeDtypeStruct((bsz, t_len, LANES), jnp.int32),
                   jax.ShapeDtypeStruct((bsz, t_len, LANES), F32),
                   jax.ShapeDtypeStruct((bsz, t_len, LANES), jnp.int32),
                   jax.ShapeDtypeStruct((1, LANES), F32)],
        scratch_shapes=[pltpu.VMEM((1, LANES), F32)],
        compiler_params=_params("arbitrary", "arbitrary"),
        name="outproj_router",
    )(odn, oswa, x, g1, sc, sh, wo1, wo2, bo, norm_w, wr, br, cnt0)


def _moe_kernel(be_ref, xs_ref, w1_ref, b1_ref, w2_ref, b2_ref, y_ref):
    del be_ref
    up = _dot(xs_ref[...], w1_ref[0]) + b1_ref[0]
    glu = jnp.minimum(up[:, :D_FF], SWIGLU_LIMIT)
    lin = jnp.clip(up[:, D_FF:], -SWIGLU_LIMIT, SWIGLU_LIMIT)
    act = glu * jax.nn.sigmoid(SWIGLU_ALPHA * glu) * (lin + 1.0)
    y_ref[...] = (_dot(act.astype(BF16), w2_ref[0]) + b2_ref[0]).astype(y_ref.dtype)


def _moe_experts(blk_expert, xs, w1, b1, w2, b2):
    n_rows = xs.shape[0]
    n_blocks = n_rows // MOE_ROWS
    grid_spec = pltpu.PrefetchScalarGridSpec(
        num_scalar_prefetch=1,
        grid=(n_blocks,),
        in_specs=[pl.BlockSpec((MOE_ROWS, D_MODEL), lambda i, be: (i, 0)),
                  pl.BlockSpec((1, D_MODEL, 2 * D_FF), lambda i, be: (be[i], 0, 0)),
                  pl.BlockSpec((1, 1, 2 * D_FF), lambda i, be: (be[i], 0, 0)),
                  pl.BlockSpec((1, D_FF, D_MODEL), lambda i, be: (be[i], 0, 0)),
                  pl.BlockSpec((1, 1, D_MODEL), lambda i, be: (be[i], 0, 0))],
        out_specs=pl.BlockSpec((MOE_ROWS, D_MODEL), lambda i, be: (i, 0)),
    )
    return pl.pallas_call(
        _moe_kernel,
        grid_spec=grid_spec,
        out_shape=jax.ShapeDtypeStruct((n_rows, D_MODEL), BF16),
        compiler_params=_params("arbitrary"),
        name="moe_experts",
    )(blk_expert, xs, w1, b1, w2, b2)


def _final_kernel(x1_ref, yg_ref, tg_ref, g2_ref, nw_ref, y_ref):
    tg = tg_ref[0]
    moe = yg_ref[0].astype(F32) * tg[:, 0:1]
    for kk in range(1, TOP_K):
        moe = moe + yg_ref[kk].astype(F32) * tg[:, kk:kk + 1]
    x2 = x1_ref[0] + g2_ref[0] * moe
    y_ref[0] = _rms(x2) * nw_ref[...]


def _final(x1, yg, tg, g2, norm_w, tm, tok_offset):
    bsz, t_len, _ = x1.shape
    tok = lambda b, i: (b, i, 0)
    steps = t_len // tm
    blk0 = tok_offset // tm
    return pl.pallas_call(
        _final_kernel,
        grid=(bsz, steps),
        in_specs=[pl.BlockSpec((1, tm, D_MODEL), tok),
                  pl.BlockSpec((TOP_K, tm, D_MODEL), lambda b, i: (0, blk0 + b * steps + i, 0)),
                  pl.BlockSpec((1, tm, LANES), tok),
                  pl.BlockSpec((1, 1, D_MODEL), lambda b, i: (b, 0, 0)),
                  pl.BlockSpec((1, D_MODEL), lambda b, i: (0, 0))],
        out_specs=pl.BlockSpec((1, tm, D_MODEL), tok),
        out_shape=jax.ShapeDtypeStruct((bsz, t_len, D_MODEL), F32),
        compiler_params=_params("parallel", "arbitrary"),
        name="combine_final",
    )(x1, yg, tg, g2, norm_w)


def _t5_bucket(rel):
    half = NUM_BUCKETS // 2
    max_exact = half // 2
    ret = jnp.where(rel > 0, half, 0)
    n = jnp.abs(rel)
    nf = jnp.maximum(n, 1).astype(jnp.float32)
    large = max_exact + (jnp.log(nf / max_exact) / math.log(MAX_DISTANCE / max_exact)
                         * (half - max_exact)).astype(jnp.int32)
    large = jnp.minimum(large, half - 1)
    return ret + jnp.where(n < max_exact, n, large)


def _route(top_i, rank, sizes):
    n_tok = top_i.shape[0]
    n_asg = n_tok * TOP_K
    start = jnp.cumsum(sizes) - sizes
    padded = (sizes + MOE_ROWS - 1) // MOE_ROWS * MOE_ROWS
    pend = jnp.cumsum(padded)
    pstart = pend - padded
    n_blocks = -(-n_asg // MOE_ROWS) + N_EXPERTS
    n_rows = n_blocks * MOE_ROWS
    blk_expert = jnp.minimum(
        jnp.searchsorted(pend, jnp.arange(n_blocks, dtype=jnp.int32) * MOE_ROWS, side='right'),
        N_EXPERTS - 1).astype(jnp.int32)
    pos = pstart[top_i] + rank
    order = jnp.argsort(top_i.reshape(-1), stable=True)
    row_e = jnp.repeat(blk_expert, MOE_ROWS)
    src = jnp.arange(n_rows, dtype=jnp.int32) - pstart[row_e] + start[row_e]
    tok_pad = (order[jnp.clip(src, 0, n_asg - 1)] // TOP_K).astype(jnp.int32)
    return tok_pad, pos, blk_expert


def kernel(x_prompt, x_sample, c_prompt, c_sample, state_conv, state_delta, cache_swa_k, cache_swa_v,
           w_ada, b_ada, norm_mix, w_in, b_in, conv_w, a_log, dt_bias, dn_norm_w, sinks, rel_bias,
           w_out, b_out, norm_ffn, w_router, b_router, w1, b1, w2, b2, norm_final):
    depth = w_ada.shape[0]
    assert depth == 1, "the final norm is fused into the layer's combine step"
    bp, tp, _ = x_prompt.shape
    bs, ts, _ = x_sample.shape
    groups = [dict(x=x_prompt, c=c_prompt, hist=0), dict(x=x_sample, c=c_sample, hist=WIN_CHUNKS)]

    q_rel = jnp.arange(CHUNK)
    k_rel = jnp.arange(BAND) - WIN_CHUNKS * CHUNK
    bucket = _t5_bucket(k_rel[None, :] - q_rel[:, None]).astype(jnp.int32)
    bias = _rel_bias(bucket, rel_bias)

    nf_w = norm_final.reshape(1, D_MODEL)
    outs = {g: dict(conv=[], delta=[], k=[], v=[]) for g in range(2)}
    xs_cur = [x_prompt, x_sample]

    for l in range(depth):
        o1 = DN_CONV_DIM + DN_V_DIM
        o2 = o1 + 2 * DN_V_HEADS
        wl, bl = w_in[l], b_in[l]
        w_cat = jnp.concatenate(
            [wl[:, :o1], wl[:, o2:], wl[:, o1:o2], jnp.zeros((D_MODEL, GATE_LANES - 2 * DN_V_HEADS), F32)],
            axis=1).astype(BF16)
        b_cat = jnp.concatenate(
            [bl[:o1], bl[o2:], bl[o1:o2], jnp.zeros((GATE_LANES - 2 * DN_V_HEADS,), F32)]).reshape(1, PROJ_DIM)
        pad8 = jnp.zeros((DN_V_HEADS,), F32)
        padr = jnp.zeros((GATE_LANES - 2 * DN_V_HEADS,), F32)
        alog_l = jnp.concatenate([pad8, a_log[l], padr]).reshape(1, GATE_LANES)
        dtb_l = jnp.concatenate([pad8, dt_bias[l], padr]).reshape(1, GATE_LANES)
        wo1 = w_out[l][:DN_V_DIM].astype(BF16)
        wo2 = w_out[l][DN_V_DIM:].astype(BF16)
        wr = jnp.concatenate([w_router[l], jnp.zeros((D_MODEL, LANES - N_EXPERTS), F32)], axis=1)
        br = jnp.concatenate([b_router[l], jnp.full((LANES - N_EXPERTS,), NEG, F32)]).reshape(1, LANES)
        w1b = w1[l].astype(BF16)
        w2b = w2[l].astype(BF16)
        b1l = b1[l].reshape(N_EXPERTS, 1, 2 * D_FF)
        b2l = b2[l].reshape(N_EXPERTS, 1, D_MODEL)

        mod = _modulation(jnp.concatenate([c_prompt, c_sample], axis=0), w_ada[l], b_ada[l])
        stage = []
        cnt = jnp.zeros((1, LANES), F32)
        for gi, grp in enumerate(groups):
            x = xs_cur[gi]
            bsz, t_len, _ = x.shape
            tm = min(TOKEN_TILE, t_len)
            m = mod[:bp] if gi == 0 else mod[bp:]
            sh1, sc1, g1, sh2, sc2, g2 = [a.reshape(bsz, 1, D_MODEL) for a in jnp.split(m, 6, axis=-1)]
            u, z, sq, sk, sv, ba = _inproj(x, sc1, sh1, norm_mix[l].reshape(1, D_MODEL), w_cat, b_cat, tm)
            if gi == 0:
                cprev = jnp.zeros((bsz, CONV_W - 1, DN_CONV_DIM), F32)
                s0 = jnp.zeros((bsz, DN_V_HEADS, DN_HEAD_DIM, DN_HEAD_DIM), F32)
                k_all, v_all = sk, sv
            else:
                cprev = state_conv[l]
                s0 = state_delta[l]
                k_all = jnp.concatenate([cache_swa_k[l].reshape(bsz, -1, SWA_KV_DIM), sk], axis=1)
                v_all = jnp.concatenate([cache_swa_v[l].reshape(bsz, -1, SWA_KV_DIM), sv], axis=1)
            swa_len = cache_swa_k.shape[2]
            cprev8 = jnp.concatenate([jnp.zeros((bsz, 8 - (CONV_W - 1), DN_CONV_DIM), F32), cprev], axis=1)
            o_dn, s_new = _deltanet(u, z, ba, cprev8, s0, conv_w[l], alog_l, dtb_l,
                                    dn_norm_w[l].reshape(1, DN_HEAD_DIM))
            sink_rows = jnp.repeat(sinks[l], CHUNK).reshape(SWA_KV_HEADS, SWA_GROUP * CHUNK, 1)
            o_swa = _swa(sq, k_all, v_all, bias.reshape(SWA_KV_HEADS, SWA_GROUP * CHUNK, BAND), sink_rows,
                         grp["hist"])
            x1, h2, ti, tg, rk, cnt = _outproj(o_dn, o_swa, x, g1, sc2, sh2, wo1, wo2,
                                               b_out[l].reshape(1, D_MODEL), norm_ffn[l].reshape(1, D_MODEL),
                                               wr, br, cnt, tm)
            assert t_len >= CONV_W - 1
            conv_new = u[:, t_len - (CONV_W - 1):]
            k_state = k_all[:, k_all.shape[1] - swa_len:].reshape(bsz, swa_len, SWA_KV_HEADS, SWA_HEAD_DIM)
            v_state = v_all[:, v_all.shape[1] - swa_len:].reshape(bsz, swa_len, SWA_KV_HEADS, SWA_HEAD_DIM)
            outs[gi]["conv"].append(conv_new)
            outs[gi]["delta"].append(s_new)
            outs[gi]["k"].append(k_state)
            outs[gi]["v"].append(v_state)
            stage.append(dict(x1=x1, h2=h2, ti=ti, tg=tg, rk=rk, g2=g2, tm=tm))

        h2_all = jnp.concatenate([s["h2"].reshape(-1, D_MODEL) for s in stage], axis=0)
        ti_all = jnp.concatenate([s["ti"].reshape(-1, LANES)[:, :TOP_K] for s in stage], axis=0)
        rk_all = jnp.concatenate([s["rk"].reshape(-1, LANES)[:, :TOP_K] for s in stage], axis=0)
        sizes = cnt[0, :N_EXPERTS].astype(jnp.int32)
        tok_pad, pos, blk_expert = _route(ti_all, rk_all, sizes)
        xs = jnp.take(h2_all, tok_pad, axis=0)
        y = _moe_experts(blk_expert, xs, w1b, b1l, w2b, b2l)
        yg = jnp.take(y, pos.T, axis=0)
        off = 0
        for gi, s in enumerate(stage):
            bsz, t_len, _ = s["x1"].shape
            xs_cur[gi] = _final(s["x1"], yg, s["tg"], s["g2"], nf_w, s["tm"], off)
            off += bsz * t_len

    res = [xs_cur[0], xs_cur[1]]
    for gi in range(2):
        for name in ("conv", "delta", "k", "v"):
            res.append(jnp.stack(outs[gi][name]))
    return tuple(res)
```

```python
import functools
import math

import jax
import jax.numpy as jnp
from jax import lax
from jax.experimental import pallas as pl
from jax.experimental.pallas import tpu as pltpu
from jax.experimental.pallas import tpu_sc as plsc

F32 = jnp.float32
BF16 = jnp.bfloat16

D_MODEL = 1024
CHUNK = 64
EPS = 1e-6
DN_QK_HEADS = 4
DN_V_HEADS = 8
DN_HEAD_DIM = 64
DN_QK_DIM = DN_QK_HEADS * DN_HEAD_DIM
DN_V_DIM = DN_V_HEADS * DN_HEAD_DIM
DN_CONV_DIM = 2 * DN_QK_DIM + DN_V_DIM
CONV_W = 4
SWA_HEADS = 8
SWA_KV_HEADS = 2
SWA_HEAD_DIM = 64
SWA_GROUP = SWA_HEADS // SWA_KV_HEADS
SWA_Q_DIM = SWA_HEADS * SWA_HEAD_DIM
SWA_KV_DIM = SWA_KV_HEADS * SWA_HEAD_DIM
WINDOW = 128
WIN_CHUNKS = WINDOW // CHUNK
BAND = (WIN_CHUNKS + 1) * CHUNK
NUM_BUCKETS = 32
MAX_DISTANCE = 128
N_EXPERTS = 32
TOP_K = 4
D_FF = 1024
SWIGLU_ALPHA = 1.702
SWIGLU_LIMIT = 7.0

LANES = 128
GATE_LANES = LANES
PROJ_DIM = DN_CONV_DIM + DN_V_DIM + SWA_Q_DIM + 2 * SWA_KV_DIM + GATE_LANES
MOE_ROWS = 512
TOKEN_TILE = 512
DN_SEQS_PER_STEP = 2
SWA_CHUNKS_PER_STEP = 4
SC_WINDOW = 64
VMEM_LIMIT = 56 * 1024 * 1024
NEG = -1e30


def _params(*sem):
    return pltpu.CompilerParams(dimension_semantics=sem, vmem_limit_bytes=VMEM_LIMIT)


def _split2(a):
    hi = a.astype(BF16)
    lo = (a - hi.astype(F32)).astype(BF16)
    return hi, lo


def _dot(a, b):
    return jnp.dot(a, b, preferred_element_type=F32)


def _dot_nt(a, b):
    return lax.dot_general(a, b, (((1,), (1,)), ((), ())), preferred_element_type=F32)


def _dot_x3(a, b):
    a1, a2 = _split2(a)
    b1, b2 = _split2(b)
    return _dot(a1, b1) + (_dot(a1, b2) + _dot(a2, b1))


def _dot_exact_lhs(l01, g):
    g1 = g.astype(BF16)
    r = g - g1.astype(F32)
    g2 = r.astype(BF16)
    g3 = (r - g2.astype(F32)).astype(BF16)
    return _dot(l01, g1) + (_dot(l01, g2) + _dot(l01, g3))


def _pack_bf16_pairs(x):
    w = x.shape[1] // 2
    bits = lax.bitcast_convert_type(x.astype(BF16).astype(F32), jnp.uint32)
    return (bits[:, w:] & jnp.uint32(0xFFFF0000)) | (bits[:, :w] >> 16)


def _unpack_bf16_pairs(p):
    lo = lax.bitcast_convert_type(p << 16, F32)
    hi = lax.bitcast_convert_type(p & jnp.uint32(0xFFFF0000), F32)
    return jnp.concatenate([lo, hi], axis=1)


def _silu(x):
    return x * jax.nn.sigmoid(x)


def _rms(x):
    return x * lax.rsqrt(jnp.mean(x * x, axis=-1, keepdims=True) + EPS)


def _mod_kernel(c_ref, w_ref, b_ref, o_ref):
    o_ref[...] = _dot_x3(_silu(c_ref[...]), w_ref[...]) + b_ref[...]


def _modulation(c, w_ada, b_ada):
    nb = c.shape[0]
    n_out = w_ada.shape[1]
    tn = 1024
    return pl.pallas_call(
        _mod_kernel,
        grid=(n_out // tn,),
        in_specs=[pl.BlockSpec((nb, D_MODEL), lambda j: (0, 0)),
                  pl.BlockSpec((D_MODEL, tn), lambda j: (0, j)),
                  pl.BlockSpec((1, tn), lambda j: (0, j))],
        out_specs=pl.BlockSpec((nb, tn), lambda j: (0, j)),
        out_shape=jax.ShapeDtypeStruct((nb, n_out), F32),
        compiler_params=_params("arbitrary"),
        name="modulation",
    )(c, w_ada, b_ada.reshape(1, n_out))


def _inproj_kernel(x_ref, sc_ref, sh_ref, nw_ref, w_ref, b_ref,
                   u_ref, z_ref, sq_ref, sk_ref, sv_ref, ba_ref):
    h = _rms(x_ref[0]) * nw_ref[...]
    h = h * (1.0 + sc_ref[0]) + sh_ref[0]
    p = _dot(h.astype(BF16), w_ref[...]) + b_ref[...]
    o = 0
    for ref in (u_ref, z_ref, sq_ref, sk_ref, sv_ref, ba_ref):
        w = ref.shape[-1]
        ref[0] = p[:, o:o + w].astype(ref.dtype)
        o += w


def _inproj(x, sc, sh, norm_w, w_cat, b_cat, tm):
    bsz, t_len, _ = x.shape
    widths = (DN_CONV_DIM, DN_V_DIM, SWA_Q_DIM, SWA_KV_DIM, SWA_KV_DIM, GATE_LANES)
    tok = lambda b, i: (b, i, 0)
    per_b = lambda b, i: (b, 0, 0)
    fixed = lambda b, i: (0, 0)
    return pl.pallas_call(
        _inproj_kernel,
        grid=(bsz, t_len // tm),
        in_specs=[pl.BlockSpec((1, tm, D_MODEL), tok),
                  pl.BlockSpec((1, 1, D_MODEL), per_b),
                  pl.BlockSpec((1, 1, D_MODEL), per_b),
                  pl.BlockSpec((1, D_MODEL), fixed),
                  pl.BlockSpec((D_MODEL, PROJ_DIM), fixed),
                  pl.BlockSpec((1, PROJ_DIM), fixed)],
        out_specs=[pl.BlockSpec((1, tm, w), tok) for w in widths],
        out_shape=[jax.ShapeDtypeStruct((bsz, t_len, w), BF16 if i == 2 else F32) for i, w in enumerate(widths)],
        compiler_params=_params("parallel", "arbitrary"),
        name="inproj",
    )(x, sc, sh, norm_w, w_cat, b_cat)


def _bmm(a, b):
    return lax.dot_general(a.astype(BF16), b.astype(BF16), (((2,), (1,)), ((0,), (0,))),
                           preferred_element_type=F32)


def _bmm_nt(a, b):
    return lax.dot_general(a.astype(BF16), b.astype(BF16), (((2,), (2,)), ((0,), (0,))),
                           preferred_element_type=F32)


def _deltanet_kernel(u_ref, z_ref, ba_ref, cprev_ref, s0_ref, cw_ref, alog_ref, dtb_ref, nw_ref,
                     o_ref, sfin_ref, xc_scr, s_scr, *, nb):
    n = pl.program_id(1)
    hd = DN_HEAD_DIM
    rep = DN_V_HEADS // DN_QK_HEADS

    @pl.when(n == 0)
    def _():
        for i in range(nb):
            xc_scr[i, 0:8, :] = cprev_ref[i]
            s_scr[i * DN_V_HEADS:(i + 1) * DN_V_HEADS] = s0_ref[i]

    row = lax.broadcasted_iota(jnp.int32, (CHUNK, CHUNK), 0)
    col = lax.broadcasted_iota(jnp.int32, (CHUNK, CHUNK), 1)
    incl = row >= col
    strict = row > col
    lower01 = jnp.where(incl, 1.0, 0.0).astype(BF16)
    eye = jnp.where(row == col, 1.0, 0.0).astype(F32)
    merge_masks = [(row // 2 == col // 2) & strict]
    blk = 2
    while blk < CHUNK:
        merge_masks.append((row // (2 * blk) == col // (2 * blk)) & (row // blk != col // blk) & strict)
        blk *= 2

    qk_rows, k_items, q_items, v_items, z_items = [], [], [], [], []
    beta_items, gcol_items, grow_items = [], [], []
    for i in range(nb):
        xc_scr[i, 8:8 + CHUNK, :] = u_ref[i]
        conv = xc_scr[i, 5:5 + CHUNK, :] * cw_ref[0:1, :]
        for j in range(1, CONV_W):
            conv = conv + xc_scr[i, 5 + j:5 + j + CHUNK, :] * cw_ref[j:j + 1, :]
        tail = xc_scr[i, CHUNK:CHUNK + 8, :]
        xc_scr[i, 0:8, :] = tail
        cu = _silu(conv)

        ba = ba_ref[i]
        beta_all = jax.nn.sigmoid(ba)
        sp = ba + dtb_ref[...]
        sp = jnp.maximum(sp, 0.0) + jnp.log1p(jnp.exp(-jnp.abs(sp)))
        g_all = -jnp.exp(alog_ref[...]) * sp
        gc_all = _dot_exact_lhs(lower01, g_all)
        gc_t = gc_all.T

        for hq in range(DN_QK_HEADS):
            q = cu[:, hq * hd:(hq + 1) * hd]
            k = cu[:, DN_QK_DIM + hq * hd:DN_QK_DIM + (hq + 1) * hd]
            q = q * lax.rsqrt(jnp.sum(q * q, axis=-1, keepdims=True) + EPS) * (hd ** -0.5)
            k = k * lax.rsqrt(jnp.sum(k * k, axis=-1, keepdims=True) + EPS)
            qk_rows.append(jnp.concatenate([q, k], axis=0))
            for r in range(rep):
                hv = hq * rep + r
                q_items.append(q)
                k_items.append(k)
                v_items.append(cu[:, 2 * DN_QK_DIM + hv * hd:2 * DN_QK_DIM + (hv + 1) * hd])
                z_items.append(z_ref[i, :, hv * hd:(hv + 1) * hd])
                beta_items.append(beta_all[:, hv:hv + 1])
                gcol_items.append(gc_all[:, 8 + hv:9 + hv])
                grow_items.append(gc_t[8 + hv:9 + hv, :])

    q = jnp.stack(q_items)
    k = jnp.stack(k_items)
    v = jnp.stack(v_items)
    beta = jnp.stack(beta_items)
    gcol = jnp.stack(gcol_items)
    grow = jnp.stack(grow_items)
    glast = gcol[:, CHUNK - 1:CHUNK, :]

    qkk = _bmm_nt(jnp.stack(qk_rows), jnp.stack(k_items[::rep]))
    qkk = jnp.stack([qkk[j // rep] for j in range(nb * DN_V_HEADS)])
    qk, kk = qkk[:, :CHUNK], qkk[:, CHUNK:]

    decay = jnp.exp(jnp.where(incl, gcol - grow, NEG))
    eg = jnp.exp(gcol)

    m = jnp.where(strict, (kk * beta) * decay, 0.0)
    t = eye - jnp.where(merge_masks[0], m, 0.0)
    for mask in merge_masks[1:]:
        t = t - _bmm(t, _bmm(jnp.where(mask, m, 0.0), t))
    t0 = t.astype(BF16)
    m_hi, m_lo = _split2(m)
    mt0 = _bmm(jnp.concatenate([m_hi, m_lo], axis=2), jnp.concatenate([t0, t0], axis=1))
    resid = eye - t0.astype(F32) - mt0
    t1 = _bmm(t0, resid).astype(BF16)

    s = s_scr[...]
    ks = _bmm(jnp.concatenate([k * (beta * eg), q * eg], axis=1), s)
    rhs = (v * beta - ks[:, :CHUNK]).astype(BF16)
    vnew = _bmm(jnp.concatenate([t0, t1], axis=2), jnp.concatenate([rhs, rhs], axis=1))
    o = ks[:, CHUNK:] + _bmm(qk * decay, vnew)
    kd = k * jnp.exp(glast - gcol)
    s_scr[...] = s * jnp.exp(glast) + _bmm(jnp.swapaxes(kd, 1, 2), vnew)

    o = _rms(o) * nw_ref[...] * _silu(jnp.stack(z_items))
    for i in range(nb):
        for hv in range(DN_V_HEADS):
            o_ref[i, :, hv * hd:(hv + 1) * hd] = o[i * DN_V_HEADS + hv].astype(o_ref.dtype)

    @pl.when(n == pl.num_programs(1) - 1)
    def _():
        for i in range(nb):
            sfin_ref[i] = s_scr[i * DN_V_HEADS:(i + 1) * DN_V_HEADS]


def _deltanet(u, z, ba, cprev8, s0, conv_w, alog_l, dtb_l, dn_norm_w):
    bsz, t_len, _ = u.shape
    nb = DN_SEQS_PER_STEP
    tok = lambda b, n: (b, n, 0)
    per_b = lambda b, n: (b, 0, 0)
    per_b4 = lambda b, n: (b, 0, 0, 0)
    fixed = lambda b, n: (0, 0)
    return pl.pallas_call(
        functools.partial(_deltanet_kernel, nb=nb),
        grid=(bsz // nb, t_len // CHUNK),
        in_specs=[pl.BlockSpec((nb, CHUNK, DN_CONV_DIM), tok),
                  pl.BlockSpec((nb, CHUNK, DN_V_DIM), tok),
                  pl.BlockSpec((nb, CHUNK, GATE_LANES), tok),
                  pl.BlockSpec((nb, 8, DN_CONV_DIM), per_b),
                  pl.BlockSpec((nb, DN_V_HEADS, DN_HEAD_DIM, DN_HEAD_DIM), per_b4),
                  pl.BlockSpec((CONV_W, DN_CONV_DIM), fixed),
                  pl.BlockSpec((1, GATE_LANES), fixed),
                  pl.BlockSpec((1, GATE_LANES), fixed),
                  pl.BlockSpec((1, DN_HEAD_DIM), fixed)],
        out_specs=[pl.BlockSpec((nb, CHUNK, DN_V_DIM), tok),
                   pl.BlockSpec((nb, DN_V_HEADS, DN_HEAD_DIM, DN_HEAD_DIM), per_b4)],
        out_shape=[jax.ShapeDtypeStruct((bsz, t_len, DN_V_DIM), BF16),
                   jax.ShapeDtypeStruct((bsz, DN_V_HEADS, DN_HEAD_DIM, DN_HEAD_DIM), F32)],
        scratch_shapes=[pltpu.VMEM((nb, CHUNK + 8, DN_CONV_DIM), F32),
                        pltpu.VMEM((nb * DN_V_HEADS, DN_HEAD_DIM, DN_HEAD_DIM), F32)],
        compiler_params=_params("parallel", "arbitrary"),
        name="deltanet",
    )(u, z, ba, cprev8, s0, conv_w, alog_l, dtb_l, dn_norm_w)


def _bias_kernel(bucket_ref, table_ref, o_ref):
    bucket = bucket_ref[...]
    for h in range(SWA_HEADS):
        acc = jnp.zeros(bucket.shape, F32)
        for b in range(NUM_BUCKETS):
            acc = jnp.where(bucket == b, table_ref[b, h], acc)
        o_ref[h] = acc


def _rel_bias(bucket, table):
    return pl.pallas_call(
        _bias_kernel,
        in_specs=[pl.BlockSpec(memory_space=pltpu.VMEM),
                  pl.BlockSpec(memory_space=pltpu.SMEM)],
        out_specs=pl.BlockSpec(memory_space=pltpu.VMEM),
        out_shape=jax.ShapeDtypeStruct((SWA_HEADS, CHUNK, BAND), F32),
        name="rel_bias",
    )(bucket, table)


def _swa_kernel(q_ref, *refs, hist, cps, n_units):
    k_refs, v_refs = refs[:n_units], refs[n_units:2 * n_units]
    bias_ref, sink_ref, o_ref = refs[2 * n_units:]
    n = pl.program_id(1)
    hd = SWA_HEAD_DIM
    kb = jnp.concatenate([r[0] for r in k_refs], axis=0)
    vb = jnp.concatenate([r[0] for r in v_refs], axis=0)
    row0 = (n * cps + hist) * CHUNK - WINDOW
    key = lax.broadcasted_iota(jnp.int32, (1, 1, BAND), 2)
    q_items, k_items, v_items, valid = [], [], [], []
    for c in range(cps):
        for kv in range(SWA_KV_HEADS):
            q_items.append(jnp.concatenate(
                [q_ref[0, c * CHUNK:(c + 1) * CHUNK, (kv * SWA_GROUP + g) * hd:(kv * SWA_GROUP + g + 1) * hd]
                 for g in range(SWA_GROUP)], axis=0))
            k_items.append(kb[c * CHUNK:c * CHUNK + BAND, kv * hd:(kv + 1) * hd])
            v_items.append(vb[c * CHUNK:c * CHUNK + BAND, kv * hd:(kv + 1) * hd])
            valid.append(row0 + c * CHUNK + key >= 0)
    s = _bmm_nt(jnp.stack(q_items), jnp.stack(k_items)) * (hd ** -0.5) + bias_ref[...]
    s = jnp.where(jnp.concatenate(valid, axis=0), s, NEG)
    sink = sink_ref[...]
    mx = jnp.maximum(jnp.max(s, axis=-1, keepdims=True), sink)
    p = jnp.exp(s - mx)
    p = p / (jnp.sum(p, axis=-1, keepdims=True) + jnp.exp(sink - mx))
    o = _bmm(p, jnp.stack(v_items))
    for c in range(cps):
        for kv in range(SWA_KV_HEADS):
            for g in range(SWA_GROUP):
                h = kv * SWA_GROUP + g
                o_ref[0, c * CHUNK:(c + 1) * CHUNK, h * hd:(h + 1) * hd] = (
                    o[c * SWA_KV_HEADS + kv, g * CHUNK:(g + 1) * CHUNK].astype(o_ref.dtype))


def _swa(q, k_all, v_all, bias, sink_rows, hist):
    bsz, t_len, _ = q.shape
    cps = min(SWA_CHUNKS_PER_STEP, t_len // CHUNK)
    unit = WINDOW if cps * CHUNK % WINDOW == 0 else CHUNK
    assert (hist * CHUNK - WINDOW) % unit == 0 and (cps * CHUNK) % unit == 0
    n_units = (WINDOW + cps * CHUNK) // unit
    q_units = cps * CHUNK // unit
    unit0 = (hist * CHUNK - WINDOW) // unit
    tok = lambda b, n: (b, n, 0)

    def band(j):
        return lambda b, n: (b, jnp.maximum(n * q_units + unit0 + j, 0), 0)

    kv_specs = [pl.BlockSpec((1, unit, SWA_KV_DIM), band(j)) for j in range(n_units)]
    fixed3 = lambda b, n: (0, 0, 0)
    return pl.pallas_call(
        functools.partial(_swa_kernel, hist=hist, cps=cps, n_units=n_units),
        grid=(bsz, t_len // (cps * CHUNK)),
        in_specs=[pl.BlockSpec((1, cps * CHUNK, SWA_Q_DIM), tok)] + kv_specs + kv_specs
                 + [pl.BlockSpec((cps * SWA_KV_HEADS, SWA_GROUP * CHUNK, BAND), fixed3),
                    pl.BlockSpec((cps * SWA_KV_HEADS, SWA_GROUP * CHUNK, 1), fixed3)],
        out_specs=pl.BlockSpec((1, cps * CHUNK, SWA_Q_DIM), tok),
        out_shape=jax.ShapeDtypeStruct((bsz, t_len, SWA_Q_DIM), BF16),
        compiler_params=_params("parallel", "arbitrary"),
        name="swa",
    )(q, *([k_all] * n_units), *([v_all] * n_units),
      jnp.tile(bias, (cps, 1, 1)), jnp.tile(sink_rows, (cps, 1, 1)))


def _outproj_kernel(odn_ref, oswa_ref, x_ref, g1_ref, sc_ref, sh_ref, wo1_ref, wo2_ref, bo_ref,
                    nw_ref, wr_ref, br_ref, cnt0_ref, x1_ref, h2_ref, ti_ref, tg_ref, rk_ref, cnt_ref, cnt_scr):
    first = (pl.program_id(0) == 0) & (pl.program_id(1) == 0)

    @pl.when(first)
    def _():
        cnt_scr[...] = cnt0_ref[...]

    mix = _dot(odn_ref[0].astype(BF16), wo1_ref[...]) + _dot(oswa_ref[0].astype(BF16), wo2_ref[...])
    x1 = x_ref[0] + g1_ref[0] * (mix + bo_ref[...])
    x1_ref[0] = x1
    h2 = _rms(x1) * nw_ref[...]
    h2 = h2 * (1.0 + sc_ref[0]) + sh_ref[0]
    h2_ref[0] = _pack_bf16_pairs(h2)

    logits = _dot_x3(h2, wr_ref[...]) + br_ref[...]
    lane = lax.broadcasted_iota(jnp.int32, logits.shape, 1)
    vals, idxs = [], []
    for _ in range(TOP_K):
        m = jnp.max(logits, axis=-1, keepdims=True)
        i = jnp.min(jnp.where(logits == m, lane, LANES), axis=-1, keepdims=True)
        vals.append(m)
        idxs.append(i)
        logits = jnp.where(lane == i, -jnp.inf, logits)
    es = [jnp.exp(v - vals[0]) for v in vals]
    den = es[0] + es[1] + es[2] + es[3]
    tm = lane.shape[0]
    onehot = jnp.zeros(lane.shape, F32)
    for kk in range(TOP_K):
        onehot = jnp.where(lane == idxs[kk], 1.0, onehot)
    r_i = lax.broadcasted_iota(jnp.int32, (tm, tm), 0)
    c_i = lax.broadcasted_iota(jnp.int32, (tm, tm), 1)
    before = _dot(jnp.where(r_i > c_i, 1.0, 0.0).astype(BF16), onehot.astype(BF16)) + cnt_scr[...]
    cnt_scr[...] = cnt_scr[...] + jnp.sum(onehot, axis=0, keepdims=True)

    ti = jnp.zeros(lane.shape, jnp.int32)
    tg = jnp.zeros(lane.shape, F32)
    rk = jnp.zeros(lane.shape, jnp.int32)
    for kk in range(TOP_K):
        rank = jnp.sum(jnp.where(lane == idxs[kk], before, 0.0), axis=-1, keepdims=True)
        ti = jnp.where(lane == kk, idxs[kk], ti)
        tg = jnp.where(lane == kk, es[kk] / den, tg)
        rk = jnp.where(lane == kk, rank.astype(jnp.int32), rk)
    ti_ref[0] = ti
    tg_ref[0] = tg
    rk_ref[0] = rk
    cnt_ref[...] = cnt_scr[...]


def _outproj(odn, oswa, x, g1, sc, sh, wo1, wo2, bo, norm_w, wr, br, cnt0, tm):
    bsz, t_len, _ = x.shape
    tok = lambda b, i: (b, i, 0)
    per_b = lambda b, i: (b, 0, 0)
    fixed = lambda b, i: (0, 0)
    return pl.pallas_call(
        _outproj_kernel,
        grid=(bsz, t_len // tm),
        in_specs=[pl.BlockSpec((1, tm, DN_V_DIM), tok),
                  pl.BlockSpec((1, tm, SWA_Q_DIM), tok),
                  pl.BlockSpec((1, tm, D_MODEL), tok),
                  pl.BlockSpec((1, 1, D_MODEL), per_b),
                  pl.BlockSpec((1, 1, D_MODEL), per_b),
                  pl.BlockSpec((1, 1, D_MODEL), per_b),
                  pl.BlockSpec((DN_V_DIM, D_MODEL), fixed),
                  pl.BlockSpec((SWA_Q_DIM, D_MODEL), fixed),
                  pl.BlockSpec((1, D_MODEL), fixed),
                  pl.BlockSpec((1, D_MODEL), fixed),
                  pl.BlockSpec((D_MODEL, LANES), fixed),
                  pl.BlockSpec((1, LANES), fixed),
                  pl.BlockSpec((1, LANES), fixed)],
        out_specs=[pl.BlockSpec((1, tm, D_MODEL), tok),
                   pl.BlockSpec((1, tm, D_MODEL // 2), tok),
                   pl.BlockSpec((1, tm, LANES), tok),
                   pl.BlockSpec((1, tm, LANES), tok),
                   pl.BlockSpec((1, tm, LANES), tok),
                   pl.BlockSpec((1, LANES), fixed)],
        out_shape=[jax.ShapeDtypeStruct((bsz, t_len, D_MODEL), F32),
                   jax.ShapeDtypeStruct((bsz, t_len, D_MODEL // 2), jnp.uint32),
                   jax.ShapeDtypeStruct((bsz, t_len, LANES), jnp.int32),
                   jax.ShapeDtypeStruct((bsz, t_len, LANES), F32),
                   jax.ShapeDtypeStruct((bsz, t_len, LANES), jnp.int32),
                   jax.ShapeDtypeStruct((1, LANES), F32)],
        scratch_shapes=[pltpu.VMEM((1, LANES), F32)],
        compiler_params=_params("arbitrary", "arbitrary"),
        name="outproj_router",
    )(odn, oswa, x, g1, sc, sh, wo1, wo2, bo, norm_w, wr, br, cnt0)


def _moe_kernel(be_ref, nv_ref, xs_ref, w1_ref, b1_ref, w2_ref, b2_ref, y_ref):
    del be_ref
    nv = nv_ref[pl.program_id(0)]

    @pl.when(nv == 0)
    def _():
        y_ref[...] = jnp.zeros(y_ref.shape, y_ref.dtype)

    @pl.when(nv > 0)
    def _():
        rows = lax.broadcasted_iota(jnp.int32, (MOE_ROWS, 1), 0)
        xb = jnp.where(rows < nv, _unpack_bf16_pairs(xs_ref[...]), 0.0).astype(BF16)
        up = _dot(xb, w1_ref[0]) + b1_ref[0]
        glu = jnp.minimum(up[:, :D_FF], SWIGLU_LIMIT)
        lin = jnp.clip(up[:, D_FF:], -SWIGLU_LIMIT, SWIGLU_LIMIT)
        act = glu * jax.nn.sigmoid(SWIGLU_ALPHA * glu) * (lin + 1.0)
        y_ref[...] = _pack_bf16_pairs(_dot(act.astype(BF16), w2_ref[0]) + b2_ref[0])


def _moe_experts(blk_expert, blk_valid, xs, w1, b1, w2, b2):
    n_rows = xs.shape[0]
    n_blocks = n_rows // MOE_ROWS
    half = D_MODEL // 2
    grid_spec = pltpu.PrefetchScalarGridSpec(
        num_scalar_prefetch=2,
        grid=(n_blocks,),
        in_specs=[pl.BlockSpec((MOE_ROWS, half), lambda i, be, nv: (i, 0)),
                  pl.BlockSpec((1, D_MODEL, 2 * D_FF), lambda i, be, nv: (be[i], 0, 0)),
                  pl.BlockSpec((1, 1, 2 * D_FF), lambda i, be, nv: (be[i], 0, 0)),
                  pl.BlockSpec((1, D_FF, D_MODEL), lambda i, be, nv: (be[i], 0, 0)),
                  pl.BlockSpec((1, 1, D_MODEL), lambda i, be, nv: (be[i], 0, 0))],
        out_specs=pl.BlockSpec((MOE_ROWS, half), lambda i, be, nv: (i, 0)),
    )
    return pl.pallas_call(
        _moe_kernel,
        grid_spec=grid_spec,
        out_shape=jax.ShapeDtypeStruct((n_rows, half), jnp.uint32),
        compiler_params=_params("arbitrary"),
        name="moe_experts",
    )(blk_expert, blk_valid, xs, w1, b1, w2, b2)


def _sc_mesh():
    return plsc.VectorSubcoreMesh(core_axis_name="core", subcore_axis_name="subcore")


def _gather_rows(x, idx):
    m = idx.shape[0]
    w = x.shape[1]

    @pl.kernel(out_type=jax.ShapeDtypeStruct((m, w), x.dtype), mesh=_sc_mesh())
    def gather_kernel(x_hbm, i_hbm, o_hbm):
        def body(i_vmem, o_vmem):
            pltpu.sync_copy(x_hbm.at[i_vmem.at[0]], o_vmem)

        pltpu.emit_pipeline(
            body,
            grid=(m // SC_WINDOW,),
            in_specs=[pl.BlockSpec((1, SC_WINDOW), lambda i: (i, 0))],
            out_specs=[pl.BlockSpec((SC_WINDOW, w), lambda i: (i, 0))],
            core_axis_name=("core", "subcore"),
            dimension_semantics=(pltpu.PARALLEL,),
        )(i_hbm, o_hbm)

    return gather_kernel(x, idx.reshape(m // SC_WINDOW, SC_WINDOW))


def _scatter_rows(x, idx, n_out):
    n, w = x.shape
    kk = idx.shape[1]
    idx3 = jnp.transpose(idx.reshape(n // SC_WINDOW, SC_WINDOW, kk), (0, 2, 1))

    @pl.kernel(out_type=jax.ShapeDtypeStruct((n_out, w), x.dtype), mesh=_sc_mesh())
    def scatter_kernel(x_hbm, i_hbm, o_hbm):
        def body(x_vmem, i_vmem):
            for k in range(kk):
                pltpu.sync_copy(x_vmem, o_hbm.at[i_vmem.at[0, k]])

        pltpu.emit_pipeline(
            body,
            grid=(n // SC_WINDOW,),
            in_specs=[pl.BlockSpec((SC_WINDOW, w), lambda i: (i, 0)),
                      pl.BlockSpec((1, kk, SC_WINDOW), lambda i: (i, 0, 0))],
            out_specs=[],
            core_axis_name=("core", "subcore"),
            dimension_semantics=(pltpu.PARALLEL,),
        )(x_hbm, i_hbm)

    return scatter_kernel(x, idx3)


def _final_kernel(x1_ref, yg_ref, tg_ref, g2_ref, nw_ref, y_ref):
    tg = tg_ref[0]
    moe = _unpack_bf16_pairs(yg_ref[0]) * tg[:, 0:1]
    for kk in range(1, TOP_K):
        moe = moe + _unpack_bf16_pairs(yg_ref[kk]) * tg[:, kk:kk + 1]
    x2 = x1_ref[0] + g2_ref[0] * moe
    y_ref[0] = _rms(x2) * nw_ref[...]


def _final(x1, yg, tg, g2, norm_w, tm, tok_offset):
    bsz, t_len, _ = x1.shape
    tok = lambda b, i: (b, i, 0)
    steps = t_len // tm
    blk0 = tok_offset // tm
    return pl.pallas_call(
        _final_kernel,
        grid=(bsz, steps),
        in_specs=[pl.BlockSpec((1, tm, D_MODEL), tok),
                  pl.BlockSpec((TOP_K, tm, D_MODEL // 2), lambda b, i: (0, blk0 + b * steps + i, 0)),
                  pl.BlockSpec((1, tm, LANES), tok),
                  pl.BlockSpec((1, 1, D_MODEL), lambda b, i: (b, 0, 0)),
                  pl.BlockSpec((1, D_MODEL), lambda b, i: (0, 0))],
        out_specs=pl.BlockSpec((1, tm, D_MODEL), tok),
        out_shape=jax.ShapeDtypeStruct((bsz, t_len, D_MODEL), F32),
        compiler_params=_params("parallel", "arbitrary"),
        name="combine_final",
    )(x1, yg, tg, g2, norm_w)


def _t5_bucket(rel):
    half = NUM_BUCKETS // 2
    max_exact = half // 2
    ret = jnp.where(rel > 0, half, 0)
    n = jnp.abs(rel)
    nf = jnp.maximum(n, 1).astype(jnp.float32)
    large = max_exact + (jnp.log(nf / max_exact) / math.log(MAX_DISTANCE / max_exact)
                         * (half - max_exact)).astype(jnp.int32)
    large = jnp.minimum(large, half - 1)
    return ret + jnp.where(n < max_exact, n, large)


def _route(top_i, rank, sizes):
    n_tok = top_i.shape[0]
    n_asg = n_tok * TOP_K
    padded = (sizes + MOE_ROWS - 1) // MOE_ROWS * MOE_ROWS
    pend = jnp.cumsum(padded)
    pstart = pend - padded
    n_blocks = -(-n_asg // MOE_ROWS) + N_EXPERTS
    blk_row0 = jnp.arange(n_blocks, dtype=jnp.int32) * MOE_ROWS
    blk_expert = jnp.minimum(jnp.sum(pend[None, :] <= blk_row0[:, None], axis=1), N_EXPERTS - 1).astype(jnp.int32)
    blk_valid = jnp.clip(pstart[blk_expert] + sizes[blk_expert] - blk_row0, 0, MOE_ROWS).astype(jnp.int32)
    onehot = top_i[:, :, None] == jnp.arange(N_EXPERTS, dtype=jnp.int32)
    pos = jnp.sum(jnp.where(onehot, pstart, 0), axis=-1) + rank
    return pos, blk_expert, blk_valid, n_blocks * MOE_ROWS


def kernel(x_prompt, x_sample, c_prompt, c_sample, state_conv, state_delta, cache_swa_k, cache_swa_v,
           w_ada, b_ada, norm_mix, w_in, b_in, conv_w, a_log, dt_bias, dn_norm_w, sinks, rel_bias,
           w_out, b_out, norm_ffn, w_router, b_router, w1, b1, w2, b2, norm_final):
    depth = w_ada.shape[0]
    assert depth == 1, "the final norm is fused into the layer's combine step"
    bp, tp, _ = x_prompt.shape
    bs, ts, _ = x_sample.shape
    groups = [dict(x=x_prompt, c=c_prompt, hist=0), dict(x=x_sample, c=c_sample, hist=WIN_CHUNKS)]

    q_rel = jnp.arange(CHUNK)
    k_rel = jnp.arange(BAND) - WIN_CHUNKS * CHUNK
    bucket = _t5_bucket(k_rel[None, :] - q_rel[:, None]).astype(jnp.int32)
    bias = _rel_bias(bucket, rel_bias)

    nf_w = norm_final.reshape(1, D_MODEL)
    outs = {g: dict(conv=[], delta=[], k=[], v=[]) for g in range(2)}
    xs_cur = [x_prompt, x_sample]

    for l in range(depth):
        o1 = DN_CONV_DIM + DN_V_DIM
        o2 = o1 + 2 * DN_V_HEADS
        wl, bl = w_in[l], b_in[l]
        w_cat = jnp.concatenate(
            [wl[:, :o1], wl[:, o2:], wl[:, o1:o2], jnp.zeros((D_MODEL, GATE_LANES - 2 * DN_V_HEADS), F32)],
            axis=1).astype(BF16)
        b_cat = jnp.concatenate(
            [bl[:o1], bl[o2:], bl[o1:o2], jnp.zeros((GATE_LANES - 2 * DN_V_HEADS,), F32)]).reshape(1, PROJ_DIM)
        pad8 = jnp.zeros((DN_V_HEADS,), F32)
        padr = jnp.zeros((GATE_LANES - 2 * DN_V_HEADS,), F32)
        alog_l = jnp.concatenate([pad8, a_log[l], padr]).reshape(1, GATE_LANES)
        dtb_l = jnp.concatenate([pad8, dt_bias[l], padr]).reshape(1, GATE_LANES)
        wo1 = w_out[l][:DN_V_DIM].astype(BF16)
        wo2 = w_out[l][DN_V_DIM:].astype(BF16)
        wr = jnp.concatenate([w_router[l], jnp.zeros((D_MODEL, LANES - N_EXPERTS), F32)], axis=1)
        br = jnp.concatenate([b_router[l], jnp.full((LANES - N_EXPERTS,), NEG, F32)]).reshape(1, LANES)
        w1b = w1[l].astype(BF16)
        w2b = w2[l].astype(BF16)
        b1l = b1[l].reshape(N_EXPERTS, 1, 2 * D_FF)
        b2l = b2[l].reshape(N_EXPERTS, 1, D_MODEL)

        mod = _modulation(jnp.concatenate([c_prompt, c_sample], axis=0), w_ada[l], b_ada[l])
        stage = []
        cnt = jnp.zeros((1, LANES), F32)
        for gi, grp in enumerate(groups):
            x = xs_cur[gi]
            bsz, t_len, _ = x.shape
            tm = min(TOKEN_TILE, t_len)
            m = mod[:bp] if gi == 0 else mod[bp:]
            sh1, sc1, g1, sh2, sc2, g2 = [a.reshape(bsz, 1, D_MODEL) for a in jnp.split(m, 6, axis=-1)]
            u, z, sq, sk, sv, ba = _inproj(x, sc1, sh1, norm_mix[l].reshape(1, D_MODEL), w_cat, b_cat, tm)
            if gi == 0:
                cprev = jnp.zeros((bsz, CONV_W - 1, DN_CONV_DIM), F32)
                s0 = jnp.zeros((bsz, DN_V_HEADS, DN_HEAD_DIM, DN_HEAD_DIM), F32)
                k_all, v_all = sk, sv
            else:
                cprev = state_conv[l]
                s0 = state_delta[l]
                k_all = jnp.concatenate([cache_swa_k[l].reshape(bsz, -1, SWA_KV_DIM), sk], axis=1)
                v_all = jnp.concatenate([cache_swa_v[l].reshape(bsz, -1, SWA_KV_DIM), sv], axis=1)
            swa_len = cache_swa_k.shape[2]
            cprev8 = jnp.concatenate([jnp.zeros((bsz, 8 - (CONV_W - 1), DN_CONV_DIM), F32), cprev], axis=1)
            o_dn, s_new = _deltanet(u, z, ba, cprev8, s0, conv_w[l], alog_l, dtb_l,
                                    dn_norm_w[l].reshape(1, DN_HEAD_DIM))
            sink_rows = jnp.repeat(sinks[l], CHUNK).reshape(SWA_KV_HEADS, SWA_GROUP * CHUNK, 1)
            o_swa = _swa(sq, k_all, v_all, bias.reshape(SWA_KV_HEADS, SWA_GROUP * CHUNK, BAND), sink_rows,
                         grp["hist"])
            x1, h2, ti, tg, rk, cnt = _outproj(o_dn, o_swa, x, g1, sc2, sh2, wo1, wo2,
                                               b_out[l].reshape(1, D_MODEL), norm_ffn[l].reshape(1, D_MODEL),
                                               wr, br, cnt, tm)
            assert t_len >= CONV_W - 1
            conv_new = u[:, t_len - (CONV_W - 1):]
            k_state = k_all[:, k_all.shape[1] - swa_len:].reshape(bsz, swa_len, SWA_KV_HEADS, SWA_HEAD_DIM)
            v_state = v_all[:, v_all.shape[1] - swa_len:].reshape(bsz, swa_len, SWA_KV_HEADS, SWA_HEAD_DIM)
            outs[gi]["conv"].append(conv_new)
            outs[gi]["delta"].append(s_new)
            outs[gi]["k"].append(k_state)
            outs[gi]["v"].append(v_state)
            stage.append(dict(x1=x1, h2=h2, ti=ti, tg=tg, rk=rk, g2=g2, tm=tm))

        h2_all = jnp.concatenate([s["h2"].reshape(-1, D_MODEL // 2) for s in stage], axis=0)
        ti_all = jnp.concatenate([s["ti"].reshape(-1, LANES)[:, :TOP_K] for s in stage], axis=0)
        rk_all = jnp.concatenate([s["rk"].reshape(-1, LANES)[:, :TOP_K] for s in stage], axis=0)
        sizes = cnt[0, :N_EXPERTS].astype(jnp.int32)
        pos, blk_expert, blk_valid, n_rows = _route(ti_all, rk_all, sizes)
        xs = _scatter_rows(h2_all, pos, n_rows)
        y = _moe_experts(blk_expert, blk_valid, xs, w1b, b1l, w2b, b2l)
        n_tok = h2_all.shape[0]
        yg = _gather_rows(y, pos.T.reshape(-1)).reshape(TOP_K, n_tok, D_MODEL // 2)
        off = 0
        for gi, s in enumerate(stage):
            bsz, t_len, _ = s["x1"].shape
            xs_cur[gi] = _final(s["x1"], yg, s["tg"], s["g2"], nf_w, s["tm"], off)
            off += bsz * t_len

    res = [xs_cur[0], xs_cur[1]]
    for gi in range(2):
        for name in ("conv", "delta", "k", "v"):
            res.append(jnp.stack(outs[gi][name]))
    return tuple(res)
```

```python
import functools
import math

import jax
import jax.numpy as jnp
from jax import lax
from jax.experimental import pallas as pl
from jax.experimental.pallas import tpu as pltpu
from jax.experimental.pallas import tpu_sc as plsc

F32 = jnp.float32
BF16 = jnp.bfloat16

D_MODEL = 1024
CHUNK = 64
EPS = 1e-6
DN_QK_HEADS = 4
DN_V_HEADS = 8
DN_HEAD_DIM = 64
DN_QK_DIM = DN_QK_HEADS * DN_HEAD_DIM
DN_V_DIM = DN_V_HEADS * DN_HEAD_DIM
DN_CONV_DIM = 2 * DN_QK_DIM + DN_V_DIM
CONV_W = 4
SWA_HEADS = 8
SWA_KV_HEADS = 2
SWA_HEAD_DIM = 64
SWA_GROUP = SWA_HEADS // SWA_KV_HEADS
SWA_Q_DIM = SWA_HEADS * SWA_HEAD_DIM
SWA_KV_DIM = SWA_KV_HEADS * SWA_HEAD_DIM
WINDOW = 128
WIN_CHUNKS = WINDOW // CHUNK
BAND = (WIN_CHUNKS + 1) * CHUNK
NUM_BUCKETS = 32
MAX_DISTANCE = 128
N_EXPERTS = 32
TOP_K = 4
D_FF = 1024
SWIGLU_ALPHA = 1.702
SWIGLU_LIMIT = 7.0

LANES = 128
GATE_LANES = LANES
PROJ_DIM = DN_CONV_DIM + DN_V_DIM + SWA_Q_DIM + 2 * SWA_KV_DIM + GATE_LANES
MOE_ROWS = 512
TOKEN_TILE = 512
DN_SEQS_PER_STEP = 4
SWA_CHUNKS_PER_STEP = 4
SC_WINDOW = 64
VMEM_LIMIT = 56 * 1024 * 1024
NEG = -1e30


def _params(*sem):
    return pltpu.CompilerParams(dimension_semantics=sem, vmem_limit_bytes=VMEM_LIMIT)


def _split2(a):
    hi = a.astype(BF16)
    lo = (a - hi.astype(F32)).astype(BF16)
    return hi, lo


def _dot(a, b):
    return jnp.dot(a, b, preferred_element_type=F32)


def _dot_nt(a, b):
    return lax.dot_general(a, b, (((1,), (1,)), ((), ())), preferred_element_type=F32)


def _dot_x3(a, b):
    a1, a2 = _split2(a)
    b1, b2 = _split2(b)
    return _dot(a1, b1) + (_dot(a1, b2) + _dot(a2, b1))


def _dot_exact_lhs(l01, g):
    g1 = g.astype(BF16)
    r = g - g1.astype(F32)
    g2 = r.astype(BF16)
    g3 = (r - g2.astype(F32)).astype(BF16)
    return _dot(l01, g1) + (_dot(l01, g2) + _dot(l01, g3))


def _pack_bf16_pairs(x):
    w = x.shape[1] // 2
    bits = lax.bitcast_convert_type(x.astype(BF16).astype(F32), jnp.uint32)
    return (bits[:, w:] & jnp.uint32(0xFFFF0000)) | (bits[:, :w] >> 16)


def _unpack_bf16_pairs(p):
    lo = lax.bitcast_convert_type(p << 16, F32)
    hi = lax.bitcast_convert_type(p & jnp.uint32(0xFFFF0000), F32)
    return jnp.concatenate([lo, hi], axis=1)


def _silu(x):
    return x * jax.nn.sigmoid(x)


def _rms(x):
    return x * lax.rsqrt(jnp.mean(x * x, axis=-1, keepdims=True) + EPS)


def _mod_kernel(c_ref, w_ref, b_ref, o_ref):
    o_ref[...] = _dot_x3(_silu(c_ref[...]), w_ref[...]) + b_ref[...]


def _modulation(c, w_ada, b_ada):
    nb = c.shape[0]
    n_out = w_ada.shape[1]
    tn = 1024
    return pl.pallas_call(
        _mod_kernel,
        grid=(n_out // tn,),
        in_specs=[pl.BlockSpec((nb, D_MODEL), lambda j: (0, 0)),
                  pl.BlockSpec((D_MODEL, tn), lambda j: (0, j)),
                  pl.BlockSpec((1, tn), lambda j: (0, j))],
        out_specs=pl.BlockSpec((nb, tn), lambda j: (0, j)),
        out_shape=jax.ShapeDtypeStruct((nb, n_out), F32),
        compiler_params=_params("arbitrary"),
        name="modulation",
    )(c, w_ada, b_ada.reshape(1, n_out))


def _inproj_kernel(x_ref, sc_ref, sh_ref, nw_ref, w_ref, b_ref,
                   u_ref, z_ref, sq_ref, sk_ref, sv_ref, ba_ref):
    h = _rms(x_ref[0]) * nw_ref[...]
    h = h * (1.0 + sc_ref[0]) + sh_ref[0]
    p = _dot(h.astype(BF16), w_ref[...]) + b_ref[...]
    o = 0
    for ref in (u_ref, z_ref, sq_ref, sk_ref, sv_ref, ba_ref):
        w = ref.shape[-1]
        ref[0] = p[:, o:o + w].astype(ref.dtype)
        o += w


def _inproj(x, sc, sh, norm_w, w_cat, b_cat, tm):
    bsz, t_len, _ = x.shape
    widths = (DN_CONV_DIM, DN_V_DIM, SWA_Q_DIM, SWA_KV_DIM, SWA_KV_DIM, GATE_LANES)
    tok = lambda b, i: (b, i, 0)
    per_b = lambda b, i: (b, 0, 0)
    fixed = lambda b, i: (0, 0)
    return pl.pallas_call(
        _inproj_kernel,
        grid=(bsz, t_len // tm),
        in_specs=[pl.BlockSpec((1, tm, D_MODEL), tok),
                  pl.BlockSpec((1, 1, D_MODEL), per_b),
                  pl.BlockSpec((1, 1, D_MODEL), per_b),
                  pl.BlockSpec((1, D_MODEL), fixed),
                  pl.BlockSpec((D_MODEL, PROJ_DIM), fixed),
                  pl.BlockSpec((1, PROJ_DIM), fixed)],
        out_specs=[pl.BlockSpec((1, tm, w), tok) for w in widths],
        out_shape=[jax.ShapeDtypeStruct((bsz, t_len, w), BF16 if i == 2 else F32) for i, w in enumerate(widths)],
        compiler_params=_params("parallel", "arbitrary"),
        name="inproj",
    )(x, sc, sh, norm_w, w_cat, b_cat)


def _bmm(a, b):
    return lax.dot_general(a.astype(BF16), b.astype(BF16), (((2,), (1,)), ((0,), (0,))),
                           preferred_element_type=F32)


def _bmm_nt(a, b):
    return lax.dot_general(a.astype(BF16), b.astype(BF16), (((2,), (2,)), ((0,), (0,))),
                           preferred_element_type=F32)


def _bd(x):
    x = x.astype(BF16)
    lo = lax.broadcasted_iota(jnp.int32, x.shape, 2) < x.shape[2] // 2
    zero = jnp.zeros_like(x)
    return jnp.concatenate([jnp.where(lo, x, zero), jnp.where(lo, zero, x)], axis=1)


def _pmm(a, b):
    return _bmm(a, _bd(b))


def _half_sums(x, lo):
    s_lo = jnp.sum(jnp.where(lo, x, 0.0), axis=-1, keepdims=True)
    s_hi = jnp.sum(jnp.where(lo, 0.0, x), axis=-1, keepdims=True)
    return jnp.where(lo, s_lo, s_hi)


def _deltanet_pair_kernel(u_ref, z_ref, ba_ref, cprev_ref, s0_ref, cw_ref, alog_ref, dtb_ref, nw_ref,
                          o_ref, sfin_ref, xc_scr, s_scr, *, nb):
    n = pl.program_id(1)
    hd = DN_HEAD_DIM
    pw = 2 * hd
    npair = DN_QK_HEADS

    @pl.when(n == 0)
    def _():
        for i in range(nb):
            xc_scr[i, 0:8, :] = cprev_ref[i]
            for j in range(npair):
                s_scr[i * npair + j] = jnp.concatenate([s0_ref[i, 2 * j], s0_ref[i, 2 * j + 1]], axis=1)

    row = lax.broadcasted_iota(jnp.int32, (CHUNK, pw), 0)
    col = lax.broadcasted_iota(jnp.int32, (CHUNK, pw), 1) % hd
    lo = lax.broadcasted_iota(jnp.int32, (CHUNK, pw), 1) < hd
    incl = row >= col
    strict = row > col
    r1 = lax.broadcasted_iota(jnp.int32, (CHUNK, CHUNK), 0)
    c1 = lax.broadcasted_iota(jnp.int32, (CHUNK, CHUNK), 1)
    lower01 = jnp.where(r1 >= c1, 1.0, 0.0).astype(BF16)
    eye = jnp.where(row == col, 1.0, 0.0).astype(F32)
    merge_masks = [(row // 2 == col // 2) & strict]
    blk = 2
    while blk < CHUNK:
        merge_masks.append((row // (2 * blk) == col // (2 * blk)) & (row // blk != col // blk) & strict)
        blk *= 2

    qk_lhs, qk_rhs, q_items, k_items, v_items, z_items = [], [], [], [], [], []
    zt_items, grow_items = [], []
    for i in range(nb):
        xc_scr[i, 8:8 + CHUNK, :] = u_ref[i]
        conv = xc_scr[i, 5:5 + CHUNK, :] * cw_ref[0:1, :]
        for j in range(1, CONV_W):
            conv = conv + xc_scr[i, 5 + j:5 + j + CHUNK, :] * cw_ref[j:j + 1, :]
        tail = xc_scr[i, CHUNK:CHUNK + 8, :]
        xc_scr[i, 0:8, :] = tail
        cu = _silu(conv)

        ba = ba_ref[i]
        beta_all = jax.nn.sigmoid(ba)
        sp = ba + dtb_ref[...]
        sp = jnp.maximum(sp, 0.0) + jnp.log1p(jnp.exp(-jnp.abs(sp)))
        g_all = -jnp.exp(alog_ref[...]) * sp
        gc_all = _dot_exact_lhs(lower01, g_all)
        gc_t = gc_all.T
        beta_t = beta_all.T

        for c in range(DN_QK_DIM // pw):
            qc = cu[:, c * pw:(c + 1) * pw]
            kc = cu[:, DN_QK_DIM + c * pw:DN_QK_DIM + (c + 1) * pw]
            qc = qc * lax.rsqrt(_half_sums(qc * qc, lo) + EPS) * (hd ** -0.5)
            kc = kc * lax.rsqrt(_half_sums(kc * kc, lo) + EPS)
            qr = pltpu.roll(qc, hd, axis=1)
            kr = pltpu.roll(kc, hd, axis=1)
            for half in range(2):
                sel = lo if half == 0 else jnp.logical_not(lo)
                q_items.append(jnp.where(sel, qc, qr))
                k_items.append(jnp.where(sel, kc, kr))
                km = jnp.where(sel, kc, 0.0)
                qk_lhs.append(jnp.concatenate([qc, kc], axis=0))
                qk_rhs.append(jnp.concatenate([km, km], axis=0))
        for j in range(npair):
            a, b = 2 * j, 2 * j + 1
            v_items.append(cu[:, 2 * DN_QK_DIM + j * pw:2 * DN_QK_DIM + (j + 1) * pw])
            z_items.append(z_ref[i, :, j * pw:(j + 1) * pw])
            ra = jnp.concatenate([beta_t[a:a + 1, :], gc_t[8 + a:9 + a, :]], axis=1)
            rb = jnp.concatenate([beta_t[b:b + 1, :], gc_t[8 + b:9 + b, :]], axis=1)
            zt_items.append(jnp.concatenate([jnp.broadcast_to(ra, (CHUNK, pw)),
                                             jnp.broadcast_to(rb, (CHUNK, pw))], axis=0).T)
            grow_items.append(jnp.broadcast_to(
                jnp.concatenate([gc_t[8 + a:9 + a, :], gc_t[8 + b:9 + b, :]], axis=1), (CHUNK, pw)))

    q = jnp.stack(q_items)
    k = jnp.stack(k_items)
    v = jnp.stack(v_items)
    cols = jnp.stack(zt_items)
    beta, gcol = cols[:, :CHUNK], cols[:, CHUNK:]
    grow = jnp.stack(grow_items)
    glast = gcol[:, CHUNK - 1:CHUNK, :]

    qkk = _bmm_nt(jnp.stack(qk_lhs), jnp.stack(qk_rhs))
    qk, kk = qkk[:, :CHUNK], qkk[:, CHUNK:]

    decay = jnp.exp(jnp.where(incl, gcol - grow, NEG))
    eg = jnp.exp(gcol)

    m = jnp.where(strict, (kk * beta) * decay, 0.0)
    t = eye - jnp.where(merge_masks[0], m, 0.0)
    for mask in merge_masks[1:]:
        t = t - _pmm(t, _pmm(jnp.where(mask, m, 0.0), t))
    t0 = t.astype(BF16)
    t0_bd = _bd(t0)
    m_hi, m_lo = _split2(m)
    mt0 = _bmm(jnp.concatenate([m_hi, m_lo], axis=2), jnp.concatenate([t0_bd, t0_bd], axis=1))
    resid = eye - t0.astype(F32) - mt0
    t1 = _pmm(t0, resid).astype(BF16)

    s = s_scr[...]
    ks = _pmm(jnp.concatenate([k * (beta * eg), q * eg], axis=1), s)
    rhs_bd = _bd(v * beta - ks[:, :CHUNK])
    vnew = _bmm(jnp.concatenate([t0, t1], axis=2), jnp.concatenate([rhs_bd, rhs_bd], axis=1))
    o = ks[:, CHUNK:] + _pmm(qk * decay, vnew)
    kd = k * jnp.exp(glast - gcol)
    kv = _bmm(jnp.swapaxes(kd, 1, 2), vnew)
    s_scr[...] = s * jnp.exp(glast) + jnp.where(lo, kv[:, :hd], kv[:, hd:])

    o = o * lax.rsqrt(_half_sums(o * o, lo) * (1.0 / hd) + EPS) * nw_ref[...] * _silu(jnp.stack(z_items))
    for i in range(nb):
        for j in range(npair):
            o_ref[i, :, j * pw:(j + 1) * pw] = o[i * npair + j].astype(o_ref.dtype)

    @pl.when(n == pl.num_programs(1) - 1)
    def _():
        for i in range(nb):
            for j in range(npair):
                sp2 = s_scr[i * npair + j]
                sfin_ref[i, 2 * j] = sp2[:, :hd]
                sfin_ref[i, 2 * j + 1] = sp2[:, hd:]


def _deltanet_kernel(u_ref, z_ref, ba_ref, cprev_ref, s0_ref, cw_ref, alog_ref, dtb_ref, nw_ref,
                     o_ref, sfin_ref, xc_scr, s_scr, *, nb):
    n = pl.program_id(1)
    hd = DN_HEAD_DIM
    rep = DN_V_HEADS // DN_QK_HEADS

    @pl.when(n == 0)
    def _():
        for i in range(nb):
            xc_scr[i, 0:8, :] = cprev_ref[i]
            s_scr[i * DN_V_HEADS:(i + 1) * DN_V_HEADS] = s0_ref[i]

    row = lax.broadcasted_iota(jnp.int32, (CHUNK, CHUNK), 0)
    col = lax.broadcasted_iota(jnp.int32, (CHUNK, CHUNK), 1)
    incl = row >= col
    strict = row > col
    lower01 = jnp.where(incl, 1.0, 0.0).astype(BF16)
    eye = jnp.where(row == col, 1.0, 0.0).astype(F32)
    merge_masks = [(row // 2 == col // 2) & strict]
    blk = 2
    while blk < CHUNK:
        merge_masks.append((row // (2 * blk) == col // (2 * blk)) & (row // blk != col // blk) & strict)
        blk *= 2

    qk_rows, k_items, q_items, v_items, z_items = [], [], [], [], []
    col_items, grow_items = [], []
    for i in range(nb):
        xc_scr[i, 8:8 + CHUNK, :] = u_ref[i]
        conv = xc_scr[i, 5:5 + CHUNK, :] * cw_ref[0:1, :]
        for j in range(1, CONV_W):
            conv = conv + xc_scr[i, 5 + j:5 + j + CHUNK, :] * cw_ref[j:j + 1, :]
        tail = xc_scr[i, CHUNK:CHUNK + 8, :]
        xc_scr[i, 0:8, :] = tail
        cu = _silu(conv)

        ba = ba_ref[i]
        beta_all = jax.nn.sigmoid(ba)
        sp = ba + dtb_ref[...]
        sp = jnp.maximum(sp, 0.0) + jnp.log1p(jnp.exp(-jnp.abs(sp)))
        g_all = -jnp.exp(alog_ref[...]) * sp
        gc_all = _dot_exact_lhs(lower01, g_all)
        gc_t = gc_all.T
        beta_t = beta_all.T

        for hq in range(DN_QK_HEADS):
            q = cu[:, hq * hd:(hq + 1) * hd]
            k = cu[:, DN_QK_DIM + hq * hd:DN_QK_DIM + (hq + 1) * hd]
            q = q * lax.rsqrt(jnp.sum(q * q, axis=-1, keepdims=True) + EPS) * (hd ** -0.5)
            k = k * lax.rsqrt(jnp.sum(k * k, axis=-1, keepdims=True) + EPS)
            qk_rows.append(jnp.concatenate([q, k], axis=0))
            for r in range(rep):
                hv = hq * rep + r
                q_items.append(q)
                k_items.append(k)
                v_items.append(cu[:, 2 * DN_QK_DIM + hv * hd:2 * DN_QK_DIM + (hv + 1) * hd])
                z_items.append(z_ref[i, :, hv * hd:(hv + 1) * hd])
                brow = jnp.broadcast_to(beta_t[hv:hv + 1, :], (CHUNK, CHUNK))
                grow = jnp.broadcast_to(gc_t[8 + hv:9 + hv, :], (CHUNK, CHUNK))
                col_items.append(jnp.concatenate([brow, grow], axis=1).T)
                grow_items.append(grow)

    q = jnp.stack(q_items)
    k = jnp.stack(k_items)
    v = jnp.stack(v_items)
    cols = jnp.stack(col_items)
    beta, gcol = cols[:, :CHUNK], cols[:, CHUNK:]
    grow = jnp.stack(grow_items)
    glast = gcol[:, CHUNK - 1:CHUNK, :]

    qkk = _bmm_nt(jnp.stack(qk_rows), jnp.stack(k_items[::rep]))
    qkk = jnp.stack([qkk[j // rep] for j in range(nb * DN_V_HEADS)])
    qk, kk = qkk[:, :CHUNK], qkk[:, CHUNK:]

    decay = jnp.exp(jnp.where(incl, gcol - grow, NEG))
    eg = jnp.exp(gcol)

    m = jnp.where(strict, (kk * beta) * decay, 0.0)
    t = eye - jnp.where(merge_masks[0], m, 0.0)
    for mask in merge_masks[1:]:
        t = t - _bmm(t, _bmm(jnp.where(mask, m, 0.0), t))
    t0 = t.astype(BF16)
    m_hi, m_lo = _split2(m)
    mt0 = _bmm(jnp.concatenate([m_hi, m_lo], axis=2), jnp.concatenate([t0, t0], axis=1))
    resid = eye - t0.astype(F32) - mt0
    t1 = _bmm(t0, resid).astype(BF16)

    s = s_scr[...]
    ks = _bmm(jnp.concatenate([k * (beta * eg), q * eg], axis=1), s)
    rhs = (v * beta - ks[:, :CHUNK]).astype(BF16)
    vnew = _bmm(jnp.concatenate([t0, t1], axis=2), jnp.concatenate([rhs, rhs], axis=1))
    o = ks[:, CHUNK:] + _bmm(qk * decay, vnew)
    kd = k * jnp.exp(glast - gcol)
    s_scr[...] = s * jnp.exp(glast) + _bmm(jnp.swapaxes(kd, 1, 2), vnew)

    o = _rms(o) * nw_ref[...] * _silu(jnp.stack(z_items))
    for i in range(nb):
        for hv in range(DN_V_HEADS):
            o_ref[i, :, hv * hd:(hv + 1) * hd] = o[i * DN_V_HEADS + hv].astype(o_ref.dtype)

    @pl.when(n == pl.num_programs(1) - 1)
    def _():
        for i in range(nb):
            sfin_ref[i] = s_scr[i * DN_V_HEADS:(i + 1) * DN_V_HEADS]


def _deltanet(u, z, ba, cprev8, s0, conv_w, alog_l, dtb_l, dn_norm_w):
    bsz, t_len, _ = u.shape
    nb = DN_SEQS_PER_STEP
    tok = lambda b, n: (b, n, 0)
    per_b = lambda b, n: (b, 0, 0)
    per_b4 = lambda b, n: (b, 0, 0, 0)
    fixed = lambda b, n: (0, 0)
    return pl.pallas_call(
        functools.partial(_deltanet_pair_kernel, nb=nb),
        grid=(bsz // nb, t_len // CHUNK),
        in_specs=[pl.BlockSpec((nb, CHUNK, DN_CONV_DIM), tok),
                  pl.BlockSpec((nb, CHUNK, DN_V_DIM), tok),
                  pl.BlockSpec((nb, CHUNK, GATE_LANES), tok),
                  pl.BlockSpec((nb, 8, DN_CONV_DIM), per_b),
                  pl.BlockSpec((nb, DN_V_HEADS, DN_HEAD_DIM, DN_HEAD_DIM), per_b4),
                  pl.BlockSpec((CONV_W, DN_CONV_DIM), fixed),
                  pl.BlockSpec((1, GATE_LANES), fixed),
                  pl.BlockSpec((1, GATE_LANES), fixed),
                  pl.BlockSpec((1, 2 * DN_HEAD_DIM), fixed)],
        out_specs=[pl.BlockSpec((nb, CHUNK, DN_V_DIM), tok),
                   pl.BlockSpec((nb, DN_V_HEADS, DN_HEAD_DIM, DN_HEAD_DIM), per_b4)],
        out_shape=[jax.ShapeDtypeStruct((bsz, t_len, DN_V_DIM), BF16),
                   jax.ShapeDtypeStruct((bsz, DN_V_HEADS, DN_HEAD_DIM, DN_HEAD_DIM), F32)],
        scratch_shapes=[pltpu.VMEM((nb, CHUNK + 8, DN_CONV_DIM), F32),
                        pltpu.VMEM((nb * DN_QK_HEADS, DN_HEAD_DIM, 2 * DN_HEAD_DIM), F32)],
        compiler_params=_params("parallel", "arbitrary"),
        name="deltanet",
    )(u, z, ba, cprev8, s0, conv_w, alog_l, dtb_l, dn_norm_w)


def _bias_kernel(bucket_ref, table_ref, o_ref):
    bucket = bucket_ref[...]
    for h in range(SWA_HEADS):
        acc = jnp.zeros(bucket.shape, F32)
        for b in range(NUM_BUCKETS):
            acc = jnp.where(bucket == b, table_ref[b, h], acc)
        o_ref[h] = acc


def _rel_bias(bucket, table):
    return pl.pallas_call(
        _bias_kernel,
        in_specs=[pl.BlockSpec(memory_space=pltpu.VMEM),
                  pl.BlockSpec(memory_space=pltpu.SMEM)],
        out_specs=pl.BlockSpec(memory_space=pltpu.VMEM),
        out_shape=jax.ShapeDtypeStruct((SWA_HEADS, CHUNK, BAND), F32),
        name="rel_bias",
    )(bucket, table)


def _swa_kernel(q_ref, *refs, hist, cps, n_units):
    k_refs, v_refs = refs[:n_units], refs[n_units:2 * n_units]
    bias_ref, sink_ref, o_ref = refs[2 * n_units:]
    n = pl.program_id(1)
    hd = SWA_HEAD_DIM
    kb = jnp.concatenate([r[0] for r in k_refs], axis=0)
    vb = jnp.concatenate([r[0] for r in v_refs], axis=0)
    row0 = (n * cps + hist) * CHUNK - WINDOW
    key = lax.broadcasted_iota(jnp.int32, (1, 1, BAND), 2)
    q_items, k_items, v_items, valid = [], [], [], []
    for c in range(cps):
        for kv in range(SWA_KV_HEADS):
            q_items.append(jnp.concatenate(
                [q_ref[0, c * CHUNK:(c + 1) * CHUNK, (kv * SWA_GROUP + g) * hd:(kv * SWA_GROUP + g + 1) * hd]
                 for g in range(SWA_GROUP)], axis=0))
            k_items.append(kb[c * CHUNK:c * CHUNK + BAND, kv * hd:(kv + 1) * hd])
            v_items.append(vb[c * CHUNK:c * CHUNK + BAND, kv * hd:(kv + 1) * hd])
            valid.append(row0 + c * CHUNK + key >= 0)
    s = _bmm_nt(jnp.stack(q_items), jnp.stack(k_items)) * (hd ** -0.5) + bias_ref[...]
    s = jnp.where(jnp.concatenate(valid, axis=0), s, NEG)
    sink = sink_ref[...]
    mx = jnp.maximum(jnp.max(s, axis=-1, keepdims=True), sink)
    p = jnp.exp(s - mx)
    p = p / (jnp.sum(p, axis=-1, keepdims=True) + jnp.exp(sink - mx))
    o = _bmm(p, jnp.stack(v_items))
    for c in range(cps):
        for kv in range(SWA_KV_HEADS):
            for g in range(SWA_GROUP):
                h = kv * SWA_GROUP + g
                o_ref[0, c * CHUNK:(c + 1) * CHUNK, h * hd:(h + 1) * hd] = (
                    o[c * SWA_KV_HEADS + kv, g * CHUNK:(g + 1) * CHUNK].astype(o_ref.dtype))


def _swa(q, k_all, v_all, bias, sink_rows, hist):
    bsz, t_len, _ = q.shape
    cps = min(SWA_CHUNKS_PER_STEP, t_len // CHUNK)
    unit = WINDOW if cps * CHUNK % WINDOW == 0 else CHUNK
    assert (hist * CHUNK - WINDOW) % unit == 0 and (cps * CHUNK) % unit == 0
    n_units = (WINDOW + cps * CHUNK) // unit
    q_units = cps * CHUNK // unit
    unit0 = (hist * CHUNK - WINDOW) // unit
    tok = lambda b, n: (b, n, 0)

    def band(j):
        return lambda b, n: (b, jnp.maximum(n * q_units + unit0 + j, 0), 0)

    kv_specs = [pl.BlockSpec((1, unit, SWA_KV_DIM), band(j)) for j in range(n_units)]
    fixed3 = lambda b, n: (0, 0, 0)
    return pl.pallas_call(
        functools.partial(_swa_kernel, hist=hist, cps=cps, n_units=n_units),
        grid=(bsz, t_len // (cps * CHUNK)),
        in_specs=[pl.BlockSpec((1, cps * CHUNK, SWA_Q_DIM), tok)] + kv_specs + kv_specs
                 + [pl.BlockSpec((cps * SWA_KV_HEADS, SWA_GROUP * CHUNK, BAND), fixed3),
                    pl.BlockSpec((cps * SWA_KV_HEADS, SWA_GROUP * CHUNK, 1), fixed3)],
        out_specs=pl.BlockSpec((1, cps * CHUNK, SWA_Q_DIM), tok),
        out_shape=jax.ShapeDtypeStruct((bsz, t_len, SWA_Q_DIM), BF16),
        compiler_params=_params("parallel", "arbitrary"),
        name="swa",
    )(q, *([k_all] * n_units), *([v_all] * n_units),
      jnp.tile(bias, (cps, 1, 1)), jnp.tile(sink_rows, (cps, 1, 1)))


def _outproj_kernel(odn_ref, oswa_ref, x_ref, g1_ref, sc_ref, sh_ref, wo1_ref, wo2_ref, bo_ref,
                    nw_ref, wr_ref, br_ref, cnt0_ref, x1_ref, h2_ref, ti_ref, tg_ref, rk_ref, cnt_ref, cnt_scr):
    first = (pl.program_id(0) == 0) & (pl.program_id(1) == 0)

    @pl.when(first)
    def _():
        cnt_scr[...] = cnt0_ref[...]

    mix = _dot(odn_ref[0].astype(BF16), wo1_ref[...]) + _dot(oswa_ref[0].astype(BF16), wo2_ref[...])
    x1 = x_ref[0] + g1_ref[0] * (mix + bo_ref[...])
    x1_ref[0] = x1
    h2 = _rms(x1) * nw_ref[...]
    h2 = h2 * (1.0 + sc_ref[0]) + sh_ref[0]
    h2_ref[0] = _pack_bf16_pairs(h2)

    logits = _dot_x3(h2, wr_ref[...]) + br_ref[...]
    lane = lax.broadcasted_iota(jnp.int32, logits.shape, 1)
    vals, idxs = [], []
    for _ in range(TOP_K):
        m = jnp.max(logits, axis=-1, keepdims=True)
        i = jnp.min(jnp.where(logits == m, lane, LANES), axis=-1, keepdims=True)
        vals.append(m)
        idxs.append(i)
        logits = jnp.where(lane == i, -jnp.inf, logits)
    es = [jnp.exp(v - vals[0]) for v in vals]
    den = es[0] + es[1] + es[2] + es[3]
    tm = lane.shape[0]
    onehot = jnp.zeros(lane.shape, F32)
    for kk in range(TOP_K):
        onehot = jnp.where(lane == idxs[kk], 1.0, onehot)
    r_i = lax.broadcasted_iota(jnp.int32, (tm, tm), 0)
    c_i = lax.broadcasted_iota(jnp.int32, (tm, tm), 1)
    before = _dot(jnp.where(r_i > c_i, 1.0, 0.0).astype(BF16), onehot.astype(BF16)) + cnt_scr[...]
    cnt_scr[...] = cnt_scr[...] + jnp.sum(onehot, axis=0, keepdims=True)

    ti = jnp.zeros(lane.shape, jnp.int32)
    tg = jnp.zeros(lane.shape, F32)
    rk = jnp.zeros(lane.shape, jnp.int32)
    for kk in range(TOP_K):
        rank = jnp.sum(jnp.where(lane == idxs[kk], before, 0.0), axis=-1, keepdims=True)
        ti = jnp.where(lane == kk, idxs[kk], ti)
        tg = jnp.where(lane == kk, es[kk] / den, tg)
        rk = jnp.where(lane == kk, rank.astype(jnp.int32), rk)
    ti_ref[0] = ti
    tg_ref[0] = tg
    rk_ref[0] = rk
    cnt_ref[...] = cnt_scr[...]


def _outproj(odn, oswa, x, g1, sc, sh, wo1, wo2, bo, norm_w, wr, br, cnt0, tm):
    bsz, t_len, _ = x.shape
    tok = lambda b, i: (b, i, 0)
    per_b = lambda b, i: (b, 0, 0)
    fixed = lambda b, i: (0, 0)
    return pl.pallas_call(
        _outproj_kernel,
        grid=(bsz, t_len // tm),
        in_specs=[pl.BlockSpec((1, tm, DN_V_DIM), tok),
                  pl.BlockSpec((1, tm, SWA_Q_DIM), tok),
                  pl.BlockSpec((1, tm, D_MODEL), tok),
                  pl.BlockSpec((1, 1, D_MODEL), per_b),
                  pl.BlockSpec((1, 1, D_MODEL), per_b),
                  pl.BlockSpec((1, 1, D_MODEL), per_b),
                  pl.BlockSpec((DN_V_DIM, D_MODEL), fixed),
                  pl.BlockSpec((SWA_Q_DIM, D_MODEL), fixed),
                  pl.BlockSpec((1, D_MODEL), fixed),
                  pl.BlockSpec((1, D_MODEL), fixed),
                  pl.BlockSpec((D_MODEL, LANES), fixed),
                  pl.BlockSpec((1, LANES), fixed),
                  pl.BlockSpec((1, LANES), fixed)],
        out_specs=[pl.BlockSpec((1, tm, D_MODEL), tok),
                   pl.BlockSpec((1, tm, D_MODEL // 2), tok),
                   pl.BlockSpec((1, tm, LANES), tok),
                   pl.BlockSpec((1, tm, LANES), tok),
                   pl.BlockSpec((1, tm, LANES), tok),
                   pl.BlockSpec((1, LANES), fixed)],
        out_shape=[jax.ShapeDtypeStruct((bsz, t_len, D_MODEL), F32),
                   jax.ShapeDtypeStruct((bsz, t_len, D_MODEL // 2), jnp.uint32),
                   jax.ShapeDtypeStruct((bsz, t_len, LANES), jnp.int32),
                   jax.ShapeDtypeStruct((bsz, t_len, LANES), F32),
                   jax.ShapeDtypeStruct((bsz, t_len, LANES), jnp.int32),
                   jax.ShapeDtypeStruct((1, LANES), F32)],
        scratch_shapes=[pltpu.VMEM((1, LANES), F32)],
        compiler_params=_params("arbitrary", "arbitrary"),
        name="outproj_router",
    )(odn, oswa, x, g1, sc, sh, wo1, wo2, bo, norm_w, wr, br, cnt0)


def _moe_kernel(be_ref, nv_ref, xs_ref, w1_ref, b1_ref, w2_ref, b2_ref, y_ref, w1b_scr, w2b_scr):
    i = pl.program_id(0)
    nv = nv_ref[i]

    @pl.when((i == 0) | (be_ref[i] != be_ref[jnp.maximum(i - 1, 0)]))
    def _():
        w1b_scr[...] = w1_ref[0].astype(BF16)
        w2b_scr[...] = w2_ref[0].astype(BF16)

    @pl.when(nv == 0)
    def _():
        y_ref[...] = jnp.zeros(y_ref.shape, y_ref.dtype)

    @pl.when(nv > 0)
    def _():
        rows = lax.broadcasted_iota(jnp.int32, (MOE_ROWS, 1), 0)
        xb = jnp.where(rows < nv, _unpack_bf16_pairs(xs_ref[...]), 0.0).astype(BF16)
        up = _dot(xb, w1b_scr[...]) + b1_ref[0]
        glu = jnp.minimum(up[:, :D_FF], SWIGLU_LIMIT)
        lin = jnp.clip(up[:, D_FF:], -SWIGLU_LIMIT, SWIGLU_LIMIT)
        act = glu * jax.nn.sigmoid(SWIGLU_ALPHA * glu) * (lin + 1.0)
        y_ref[...] = _pack_bf16_pairs(_dot(act.astype(BF16), w2b_scr[...]) + b2_ref[0])


def _moe_experts(blk_expert, blk_valid, xs, w1, b1, w2, b2):
    n_rows = xs.shape[0]
    n_blocks = n_rows // MOE_ROWS
    half = D_MODEL // 2
    grid_spec = pltpu.PrefetchScalarGridSpec(
        num_scalar_prefetch=2,
        grid=(n_blocks,),
        in_specs=[pl.BlockSpec((MOE_ROWS, half), lambda i, be, nv: (i, 0)),
                  pl.BlockSpec((1, D_MODEL, 2 * D_FF), lambda i, be, nv: (be[i], 0, 0)),
                  pl.BlockSpec((1, 1, 2 * D_FF), lambda i, be, nv: (be[i], 0, 0)),
                  pl.BlockSpec((1, D_FF, D_MODEL), lambda i, be, nv: (be[i], 0, 0)),
                  pl.BlockSpec((1, 1, D_MODEL), lambda i, be, nv: (be[i], 0, 0))],
        out_specs=pl.BlockSpec((MOE_ROWS, half), lambda i, be, nv: (i, 0)),
        scratch_shapes=[pltpu.VMEM((D_MODEL, 2 * D_FF), BF16), pltpu.VMEM((D_FF, D_MODEL), BF16)],
    )
    return pl.pallas_call(
        _moe_kernel,
        grid_spec=grid_spec,
        out_shape=jax.ShapeDtypeStruct((n_rows, half), jnp.uint32),
        compiler_params=_params("arbitrary"),
        name="moe_experts",
    )(blk_expert, blk_valid, xs, w1, b1, w2, b2)


def _sc_mesh():
    return plsc.VectorSubcoreMesh(core_axis_name="core", subcore_axis_name="subcore")


def _gather_rows(x, idx):
    m = idx.shape[0]
    w = x.shape[1]

    @pl.kernel(out_type=jax.ShapeDtypeStruct((m, w), x.dtype), mesh=_sc_mesh())
    def gather_kernel(x_hbm, i_hbm, o_hbm):
        def body(i_vmem, o_vmem):
            pltpu.sync_copy(x_hbm.at[i_vmem.at[0]], o_vmem)

        pltpu.emit_pipeline(
            body,
            grid=(m // SC_WINDOW,),
            in_specs=[pl.BlockSpec((1, SC_WINDOW), lambda i: (i, 0))],
            out_specs=[pl.BlockSpec((SC_WINDOW, w), lambda i: (i, 0))],
            core_axis_name=("core", "subcore"),
            dimension_semantics=(pltpu.PARALLEL,),
        )(i_hbm, o_hbm)

    return gather_kernel(x, idx.reshape(m // SC_WINDOW, SC_WINDOW))


def _scatter_rows(x, idx, n_out):
    n, w = x.shape
    kk = idx.shape[1]
    idx3 = jnp.transpose(idx.reshape(n // SC_WINDOW, SC_WINDOW, kk), (0, 2, 1))

    @pl.kernel(out_type=jax.ShapeDtypeStruct((n_out, w), x.dtype), mesh=_sc_mesh())
    def scatter_kernel(x_hbm, i_hbm, o_hbm):
        def body(x_vmem, i_vmem):
            for k in range(kk):
                pltpu.sync_copy(x_vmem, o_hbm.at[i_vmem.at[0, k]])

        pltpu.emit_pipeline(
            body,
            grid=(n // SC_WINDOW,),
            in_specs=[pl.BlockSpec((SC_WINDOW, w), lambda i: (i, 0)),
                      pl.BlockSpec((1, kk, SC_WINDOW), lambda i: (i, 0, 0))],
            out_specs=[],
            core_axis_name=("core", "subcore"),
            dimension_semantics=(pltpu.PARALLEL,),
        )(x_hbm, i_hbm)

    return scatter_kernel(x, idx3)


def _final_kernel(x1_ref, yg_ref, tg_ref, g2_ref, nw_ref, y_ref):
    tg = tg_ref[0]
    moe = _unpack_bf16_pairs(yg_ref[0]) * tg[:, 0:1]
    for kk in range(1, TOP_K):
        moe = moe + _unpack_bf16_pairs(yg_ref[kk]) * tg[:, kk:kk + 1]
    x2 = x1_ref[0] + g2_ref[0] * moe
    y_ref[0] = _rms(x2) * nw_ref[...]


def _final(x1, yg, tg, g2, norm_w, tm, tok_offset):
    bsz, t_len, _ = x1.shape
    tok = lambda b, i: (b, i, 0)
    steps = t_len // tm
    blk0 = tok_offset // tm
    return pl.pallas_call(
        _final_kernel,
        grid=(bsz, steps),
        in_specs=[pl.BlockSpec((1, tm, D_MODEL), tok),
                  pl.BlockSpec((TOP_K, tm, D_MODEL // 2), lambda b, i: (0, blk0 + b * steps + i, 0)),
                  pl.BlockSpec((1, tm, LANES), tok),
                  pl.BlockSpec((1, 1, D_MODEL), lambda b, i: (b, 0, 0)),
                  pl.BlockSpec((1, D_MODEL), lambda b, i: (0, 0))],
        out_specs=pl.BlockSpec((1, tm, D_MODEL), tok),
        out_shape=jax.ShapeDtypeStruct((bsz, t_len, D_MODEL), F32),
        compiler_params=_params("parallel", "arbitrary"),
        name="combine_final",
    )(x1, yg, tg, g2, norm_w)


def _t5_bucket(rel):
    half = NUM_BUCKETS // 2
    max_exact = half // 2
    ret = jnp.where(rel > 0, half, 0)
    n = jnp.abs(rel)
    nf = jnp.maximum(n, 1).astype(jnp.float32)
    large = max_exact + (jnp.log(nf / max_exact) / math.log(MAX_DISTANCE / max_exact)
                         * (half - max_exact)).astype(jnp.int32)
    large = jnp.minimum(large, half - 1)
    return ret + jnp.where(n < max_exact, n, large)


def _route(top_i, rank, sizes):
    n_tok = top_i.shape[0]
    n_asg = n_tok * TOP_K
    padded = (sizes + MOE_ROWS - 1) // MOE_ROWS * MOE_ROWS
    pend = jnp.cumsum(padded)
    pstart = pend - padded
    n_blocks = -(-n_asg // MOE_ROWS) + N_EXPERTS
    blk_row0 = jnp.arange(n_blocks, dtype=jnp.int32) * MOE_ROWS
    blk_expert = jnp.minimum(jnp.sum(pend[None, :] <= blk_row0[:, None], axis=1), N_EXPERTS - 1).astype(jnp.int32)
    blk_valid = jnp.clip(pstart[blk_expert] + sizes[blk_expert] - blk_row0, 0, MOE_ROWS).astype(jnp.int32)
    onehot = top_i[:, :, None] == jnp.arange(N_EXPERTS, dtype=jnp.int32)
    pos = jnp.sum(jnp.where(onehot, pstart, 0), axis=-1) + rank
    return pos, blk_expert, blk_valid, n_blocks * MOE_ROWS


def kernel(x_prompt, x_sample, c_prompt, c_sample, state_conv, state_delta, cache_swa_k, cache_swa_v,
           w_ada, b_ada, norm_mix, w_in, b_in, conv_w, a_log, dt_bias, dn_norm_w, sinks, rel_bias,
           w_out, b_out, norm_ffn, w_router, b_router, w1, b1, w2, b2, norm_final):
    depth = w_ada.shape[0]
    assert depth == 1, "the final norm is fused into the layer's combine step"
    bp, tp, _ = x_prompt.shape
    bs, ts, _ = x_sample.shape
    groups = [dict(x=x_prompt, c=c_prompt, hist=0), dict(x=x_sample, c=c_sample, hist=WIN_CHUNKS)]

    q_rel = jnp.arange(CHUNK)
    k_rel = jnp.arange(BAND) - WIN_CHUNKS * CHUNK
    bucket = _t5_bucket(k_rel[None, :] - q_rel[:, None]).astype(jnp.int32)
    bias = _rel_bias(bucket, rel_bias)

    nf_w = norm_final.reshape(1, D_MODEL)
    outs = {g: dict(conv=[], delta=[], k=[], v=[]) for g in range(2)}
    xs_cur = [x_prompt, x_sample]

    for l in range(depth):
        o1 = DN_CONV_DIM + DN_V_DIM
        o2 = o1 + 2 * DN_V_HEADS
        wl, bl = w_in[l], b_in[l]
        w_cat = jnp.concatenate(
            [wl[:, :o1], wl[:, o2:], wl[:, o1:o2], jnp.zeros((D_MODEL, GATE_LANES - 2 * DN_V_HEADS), F32)],
            axis=1).astype(BF16)
        b_cat = jnp.concatenate(
            [bl[:o1], bl[o2:], bl[o1:o2], jnp.zeros((GATE_LANES - 2 * DN_V_HEADS,), F32)]).reshape(1, PROJ_DIM)
        pad8 = jnp.zeros((DN_V_HEADS,), F32)
        padr = jnp.zeros((GATE_LANES - 2 * DN_V_HEADS,), F32)
        alog_l = jnp.concatenate([pad8, a_log[l], padr]).reshape(1, GATE_LANES)
        dtb_l = jnp.concatenate([pad8, dt_bias[l], padr]).reshape(1, GATE_LANES)
        wo1 = w_out[l][:DN_V_DIM].astype(BF16)
        wo2 = w_out[l][DN_V_DIM:].astype(BF16)
        wr = jnp.concatenate([w_router[l], jnp.zeros((D_MODEL, LANES - N_EXPERTS), F32)], axis=1)
        br = jnp.concatenate([b_router[l], jnp.full((LANES - N_EXPERTS,), NEG, F32)]).reshape(1, LANES)
        b1l = b1[l].reshape(N_EXPERTS, 1, 2 * D_FF)
        b2l = b2[l].reshape(N_EXPERTS, 1, D_MODEL)

        mod = _modulation(jnp.concatenate([c_prompt, c_sample], axis=0), w_ada[l], b_ada[l])
        stage = []
        cnt = jnp.zeros((1, LANES), F32)
        for gi, grp in enumerate(groups):
            x = xs_cur[gi]
            bsz, t_len, _ = x.shape
            tm = min(TOKEN_TILE, t_len)
            m = mod[:bp] if gi == 0 else mod[bp:]
            sh1, sc1, g1, sh2, sc2, g2 = [a.reshape(bsz, 1, D_MODEL) for a in jnp.split(m, 6, axis=-1)]
            u, z, sq, sk, sv, ba = _inproj(x, sc1, sh1, norm_mix[l].reshape(1, D_MODEL), w_cat, b_cat, tm)
            if gi == 0:
                cprev = jnp.zeros((bsz, CONV_W - 1, DN_CONV_DIM), F32)
                s0 = jnp.zeros((bsz, DN_V_HEADS, DN_HEAD_DIM, DN_HEAD_DIM), F32)
                k_all, v_all = sk, sv
            else:
                cprev = state_conv[l]
                s0 = state_delta[l]
                k_all = jnp.concatenate([cache_swa_k[l].reshape(bsz, -1, SWA_KV_DIM), sk], axis=1)
                v_all = jnp.concatenate([cache_swa_v[l].reshape(bsz, -1, SWA_KV_DIM), sv], axis=1)
            swa_len = cache_swa_k.shape[2]
            cprev8 = jnp.concatenate([jnp.zeros((bsz, 8 - (CONV_W - 1), DN_CONV_DIM), F32), cprev], axis=1)
            o_dn, s_new = _deltanet(u, z, ba, cprev8, s0, conv_w[l], alog_l, dtb_l,
                                    jnp.tile(dn_norm_w[l], 2).reshape(1, 2 * DN_HEAD_DIM))
            sink_rows = jnp.repeat(sinks[l], CHUNK).reshape(SWA_KV_HEADS, SWA_GROUP * CHUNK, 1)
            o_swa = _swa(sq, k_all, v_all, bias.reshape(SWA_KV_HEADS, SWA_GROUP * CHUNK, BAND), sink_rows,
                         grp["hist"])
            x1, h2, ti, tg, rk, cnt = _outproj(o_dn, o_swa, x, g1, sc2, sh2, wo1, wo2,
                                               b_out[l].reshape(1, D_MODEL), norm_ffn[l].reshape(1, D_MODEL),
                                               wr, br, cnt, tm)
            assert t_len >= CONV_W - 1
            conv_new = u[:, t_len - (CONV_W - 1):]
            k_state = k_all[:, k_all.shape[1] - swa_len:].reshape(bsz, swa_len, SWA_KV_HEADS, SWA_HEAD_DIM)
            v_state = v_all[:, v_all.shape[1] - swa_len:].reshape(bsz, swa_len, SWA_KV_HEADS, SWA_HEAD_DIM)
            outs[gi]["conv"].append(conv_new)
            outs[gi]["delta"].append(s_new)
            outs[gi]["k"].append(k_state)
            outs[gi]["v"].append(v_state)
            stage.append(dict(x1=x1, h2=h2, ti=ti, tg=tg, rk=rk, g2=g2, tm=tm))

        h2_all = jnp.concatenate([s["h2"].reshape(-1, D_MODEL // 2) for s in stage], axis=0)
        ti_all = jnp.concatenate([s["ti"].reshape(-1, LANES)[:, :TOP_K] for s in stage], axis=0)
        rk_all = jnp.concatenate([s["rk"].reshape(-1, LANES)[:, :TOP_K] for s in stage], axis=0)
        sizes = cnt[0, :N_EXPERTS].astype(jnp.int32)
        pos, blk_expert, blk_valid, n_rows = _route(ti_all, rk_all, sizes)
        xs = _scatter_rows(h2_all, pos, n_rows)
        y = _moe_experts(blk_expert, blk_valid, xs, w1[l], b1l, w2[l], b2l)
        n_tok = h2_all.shape[0]
        yg = _gather_rows(y, pos.T.reshape(-1)).reshape(TOP_K, n_tok, D_MODEL // 2)
        off = 0
        for gi, s in enumerate(stage):
            bsz, t_len, _ = s["x1"].shape
            xs_cur[gi] = _final(s["x1"], yg, s["tg"], s["g2"], nf_w, s["tm"], off)
            off += bsz * t_len

    res = [xs_cur[0], xs_cur[1]]
    for gi in range(2):
        for name in ("conv", "delta", "k", "v"):
            res.append(jnp.stack(outs[gi][name]))
    return tuple(res)
```

```python
import functools
import math

import jax
import jax.numpy as jnp
from jax import lax
from jax.experimental import pallas as pl
from jax.experimental.pallas import tpu as pltpu
from jax.experimental.pallas import tpu_sc as plsc

F32 = jnp.float32
BF16 = jnp.bfloat16

D_MODEL = 1024
CHUNK = 64
EPS = 1e-6
DN_QK_HEADS = 4
DN_V_HEADS = 8
DN_HEAD_DIM = 64
DN_QK_DIM = DN_QK_HEADS * DN_HEAD_DIM
DN_V_DIM = DN_V_HEADS * DN_HEAD_DIM
DN_CONV_DIM = 2 * DN_QK_DIM + DN_V_DIM
CONV_W = 4
SWA_HEADS = 8
SWA_KV_HEADS = 2
SWA_HEAD_DIM = 64
SWA_GROUP = SWA_HEADS // SWA_KV_HEADS
SWA_Q_DIM = SWA_HEADS * SWA_HEAD_DIM
SWA_KV_DIM = SWA_KV_HEADS * SWA_HEAD_DIM
WINDOW = 128
WIN_CHUNKS = WINDOW // CHUNK
BAND = (WIN_CHUNKS + 1) * CHUNK
NUM_BUCKETS = 32
MAX_DISTANCE = 128
N_EXPERTS = 32
TOP_K = 4
D_FF = 1024
SWIGLU_ALPHA = 1.702
SWIGLU_LIMIT = 7.0

LANES = 128
GATE_LANES = LANES
PROJ_DIM = DN_CONV_DIM + DN_V_DIM + SWA_Q_DIM + 2 * SWA_KV_DIM + GATE_LANES
MOE_ROWS = 1024
TOKEN_TILE = 512
DN_SEQS_PER_STEP = 4
SWA_CHUNKS_PER_STEP = 4
SC_WINDOW = 64
VMEM_LIMIT = 56 * 1024 * 1024
NEG = -1e30


def _params(*sem):
    return pltpu.CompilerParams(dimension_semantics=sem, vmem_limit_bytes=VMEM_LIMIT)


def _split2(a):
    hi = a.astype(BF16)
    lo = (a - hi.astype(F32)).astype(BF16)
    return hi, lo


def _dot(a, b):
    return jnp.dot(a, b, preferred_element_type=F32)


def _dot_nt(a, b):
    return lax.dot_general(a, b, (((1,), (1,)), ((), ())), preferred_element_type=F32)


def _dot_x3(a, b):
    a1, a2 = _split2(a)
    b1, b2 = _split2(b)
    return _dot(a1, b1) + (_dot(a1, b2) + _dot(a2, b1))


def _dot_exact_lhs(l01, g):
    g1 = g.astype(BF16)
    r = g - g1.astype(F32)
    g2 = r.astype(BF16)
    g3 = (r - g2.astype(F32)).astype(BF16)
    return _dot(l01, g1) + (_dot(l01, g2) + _dot(l01, g3))


def _pack_bf16_pairs(x):
    w = x.shape[1] // 2
    bits = lax.bitcast_convert_type(x.astype(BF16).astype(F32), jnp.uint32)
    return (bits[:, w:] & jnp.uint32(0xFFFF0000)) | (bits[:, :w] >> 16)


def _unpack_bf16_pairs(p):
    lo = lax.bitcast_convert_type(p << 16, F32)
    hi = lax.bitcast_convert_type(p & jnp.uint32(0xFFFF0000), F32)
    return jnp.concatenate([lo, hi], axis=1)


def _silu(x):
    return x * jax.nn.sigmoid(x)


def _rms(x):
    return x * lax.rsqrt(jnp.mean(x * x, axis=-1, keepdims=True) + EPS)


def _mod_kernel(c_ref, w_ref, b_ref, o_ref):
    o_ref[...] = _dot_x3(_silu(c_ref[...]), w_ref[...]) + b_ref[...]


def _modulation(c, w_ada, b_ada):
    nb = c.shape[0]
    n_out = w_ada.shape[1]
    tn = 1024
    return pl.pallas_call(
        _mod_kernel,
        grid=(n_out // tn,),
        in_specs=[pl.BlockSpec((nb, D_MODEL), lambda j: (0, 0)),
                  pl.BlockSpec((D_MODEL, tn), lambda j: (0, j)),
                  pl.BlockSpec((1, tn), lambda j: (0, j))],
        out_specs=pl.BlockSpec((nb, tn), lambda j: (0, j)),
        out_shape=jax.ShapeDtypeStruct((nb, n_out), F32),
        compiler_params=_params("arbitrary"),
        name="modulation",
    )(c, w_ada, b_ada.reshape(1, n_out))


def _inproj_kernel(x_ref, sc_ref, sh_ref, nw_ref, w_ref, b_ref,
                   u_ref, z_ref, sq_ref, sk_ref, sv_ref, ba_ref):
    h = _rms(x_ref[0]) * nw_ref[...]
    h = h * (1.0 + sc_ref[0]) + sh_ref[0]
    p = _dot(h.astype(BF16), w_ref[...]) + b_ref[...]
    o = 0
    for ref in (u_ref, z_ref, sq_ref, sk_ref, sv_ref, ba_ref):
        w = ref.shape[-1]
        ref[0] = p[:, o:o + w].astype(ref.dtype)
        o += w


def _inproj(x, sc, sh, norm_w, w_cat, b_cat, tm):
    bsz, t_len, _ = x.shape
    widths = (DN_CONV_DIM, DN_V_DIM, SWA_Q_DIM, SWA_KV_DIM, SWA_KV_DIM, GATE_LANES)
    tok = lambda b, i: (b, i, 0)
    per_b = lambda b, i: (b, 0, 0)
    fixed = lambda b, i: (0, 0)
    return pl.pallas_call(
        _inproj_kernel,
        grid=(bsz, t_len // tm),
        in_specs=[pl.BlockSpec((1, tm, D_MODEL), tok),
                  pl.BlockSpec((1, 1, D_MODEL), per_b),
                  pl.BlockSpec((1, 1, D_MODEL), per_b),
                  pl.BlockSpec((1, D_MODEL), fixed),
                  pl.BlockSpec((D_MODEL, PROJ_DIM), fixed),
                  pl.BlockSpec((1, PROJ_DIM), fixed)],
        out_specs=[pl.BlockSpec((1, tm, w), tok) for w in widths],
        out_shape=[jax.ShapeDtypeStruct((bsz, t_len, w), BF16 if i == 2 else F32) for i, w in enumerate(widths)],
        compiler_params=_params("parallel", "arbitrary"),
        name="inproj",
    )(x, sc, sh, norm_w, w_cat, b_cat)


def _bmm(a, b):
    return lax.dot_general(a.astype(BF16), b.astype(BF16), (((2,), (1,)), ((0,), (0,))),
                           preferred_element_type=F32)


def _bmm_nt(a, b):
    return lax.dot_general(a.astype(BF16), b.astype(BF16), (((2,), (2,)), ((0,), (0,))),
                           preferred_element_type=F32)


def _bd(x):
    x = x.astype(BF16)
    lo = lax.broadcasted_iota(jnp.int32, x.shape, 2) < x.shape[2] // 2
    zero = jnp.zeros_like(x)
    return jnp.concatenate([jnp.where(lo, x, zero), jnp.where(lo, zero, x)], axis=1)


def _pmm(a, b):
    return _bmm(a, _bd(b))


def _half_sums(x, lo):
    s_lo = jnp.sum(jnp.where(lo, x, 0.0), axis=-1, keepdims=True)
    s_hi = jnp.sum(jnp.where(lo, 0.0, x), axis=-1, keepdims=True)
    return jnp.where(lo, s_lo, s_hi)


def _deltanet_pair_kernel(u_ref, z_ref, ba_ref, cprev_ref, s0_ref, cw_ref, alog_ref, dtb_ref, nw_ref,
                          o_ref, sfin_ref, xc_scr, s_scr, *, nb):
    n = pl.program_id(1)
    hd = DN_HEAD_DIM
    pw = 2 * hd
    npair = DN_QK_HEADS

    @pl.when(n == 0)
    def _():
        for i in range(nb):
            xc_scr[i, 0:8, :] = cprev_ref[i]
            for j in range(npair):
                s_scr[i * npair + j] = jnp.concatenate([s0_ref[i, 2 * j], s0_ref[i, 2 * j + 1]], axis=1)

    row = lax.broadcasted_iota(jnp.int32, (CHUNK, pw), 0)
    col = lax.broadcasted_iota(jnp.int32, (CHUNK, pw), 1) % hd
    lo = lax.broadcasted_iota(jnp.int32, (CHUNK, pw), 1) < hd
    incl = row >= col
    strict = row > col
    r1 = lax.broadcasted_iota(jnp.int32, (CHUNK, CHUNK), 0)
    c1 = lax.broadcasted_iota(jnp.int32, (CHUNK, CHUNK), 1)
    lower01 = jnp.where(r1 >= c1, 1.0, 0.0).astype(BF16)
    eye = jnp.where(row == col, 1.0, 0.0).astype(F32)
    merge_masks = [(row // 2 == col // 2) & strict]
    blk = 2
    while blk < CHUNK:
        merge_masks.append((row // (2 * blk) == col // (2 * blk)) & (row // blk != col // blk) & strict)
        blk *= 2

    qk_lhs, qk_rhs, q_items, k_items, v_items, z_items = [], [], [], [], [], []
    zt_items, grow_items = [], []
    for i in range(nb):
        xc_scr[i, 8:8 + CHUNK, :] = u_ref[i]
        conv = xc_scr[i, 5:5 + CHUNK, :] * cw_ref[0:1, :]
        for j in range(1, CONV_W):
            conv = conv + xc_scr[i, 5 + j:5 + j + CHUNK, :] * cw_ref[j:j + 1, :]
        tail = xc_scr[i, CHUNK:CHUNK + 8, :]
        xc_scr[i, 0:8, :] = tail
        cu = _silu(conv)

        ba = ba_ref[i]
        beta_all = jax.nn.sigmoid(ba)
        sp = ba + dtb_ref[...]
        sp = jnp.maximum(sp, 0.0) + jnp.log1p(jnp.exp(-jnp.abs(sp)))
        g_all = -jnp.exp(alog_ref[...]) * sp
        gc_all = _dot_exact_lhs(lower01, g_all)
        gc_t = gc_all.T
        beta_t = beta_all.T

        for c in range(DN_QK_DIM // pw):
            qc = cu[:, c * pw:(c + 1) * pw]
            kc = cu[:, DN_QK_DIM + c * pw:DN_QK_DIM + (c + 1) * pw]
            qc = qc * lax.rsqrt(_half_sums(qc * qc, lo) + EPS) * (hd ** -0.5)
            kc = kc * lax.rsqrt(_half_sums(kc * kc, lo) + EPS)
            qr = pltpu.roll(qc, hd, axis=1)
            kr = pltpu.roll(kc, hd, axis=1)
            for half in range(2):
                sel = lo if half == 0 else jnp.logical_not(lo)
                q_items.append(jnp.where(sel, qc, qr))
                k_items.append(jnp.where(sel, kc, kr))
                km = jnp.where(sel, kc, 0.0)
                qk_lhs.append(jnp.concatenate([qc, kc], axis=0))
                qk_rhs.append(jnp.concatenate([km, km], axis=0))
        for j in range(npair):
            a, b = 2 * j, 2 * j + 1
            v_items.append(cu[:, 2 * DN_QK_DIM + j * pw:2 * DN_QK_DIM + (j + 1) * pw])
            z_items.append(z_ref[i, :, j * pw:(j + 1) * pw])
            ra = jnp.concatenate([beta_t[a:a + 1, :], gc_t[8 + a:9 + a, :]], axis=1)
            rb = jnp.concatenate([beta_t[b:b + 1, :], gc_t[8 + b:9 + b, :]], axis=1)
            zt_items.append(jnp.concatenate([jnp.broadcast_to(ra, (CHUNK, pw)),
                                             jnp.broadcast_to(rb, (CHUNK, pw))], axis=0).T)
            grow_items.append(jnp.broadcast_to(
                jnp.concatenate([gc_t[8 + a:9 + a, :], gc_t[8 + b:9 + b, :]], axis=1), (CHUNK, pw)))

    q = jnp.stack(q_items)
    k = jnp.stack(k_items)
    v = jnp.stack(v_items)
    cols = jnp.stack(zt_items)
    beta, gcol = cols[:, :CHUNK], cols[:, CHUNK:]
    grow = jnp.stack(grow_items)
    glast = gcol[:, CHUNK - 1:CHUNK, :]

    qkk = _bmm_nt(jnp.stack(qk_lhs), jnp.stack(qk_rhs))
    qk, kk = qkk[:, :CHUNK], qkk[:, CHUNK:]

    decay = jnp.exp(jnp.where(incl, gcol - grow, NEG))
    eg = jnp.exp(gcol)

    m = jnp.where(strict, (kk * beta) * decay, 0.0)
    t = eye - jnp.where(merge_masks[0], m, 0.0)
    for mask in merge_masks[1:]:
        t = t - _pmm(t, _pmm(jnp.where(mask, m, 0.0), t))
    t0 = t.astype(BF16)
    t0_bd = _bd(t0)
    m_hi, m_lo = _split2(m)
    mt0 = _bmm(jnp.concatenate([m_hi, m_lo], axis=2), jnp.concatenate([t0_bd, t0_bd], axis=1))
    resid = eye - t0.astype(F32) - mt0
    t1 = _pmm(t0, resid).astype(BF16)

    s = s_scr[...]
    ks = _pmm(jnp.concatenate([k * (beta * eg), q * eg], axis=1), s)
    rhs_bd = _bd(v * beta - ks[:, :CHUNK])
    vnew = _bmm(jnp.concatenate([t0, t1], axis=2), jnp.concatenate([rhs_bd, rhs_bd], axis=1))
    o = ks[:, CHUNK:] + _pmm(qk * decay, vnew)
    kd = k * jnp.exp(glast - gcol)
    kv = _bmm(jnp.swapaxes(kd, 1, 2), vnew)
    s_scr[...] = s * jnp.exp(glast) + jnp.where(lo, kv[:, :hd], kv[:, hd:])

    o = o * lax.rsqrt(_half_sums(o * o, lo) * (1.0 / hd) + EPS) * nw_ref[...] * _silu(jnp.stack(z_items))
    for i in range(nb):
        for j in range(npair):
            o_ref[i, :, j * pw:(j + 1) * pw] = o[i * npair + j].astype(o_ref.dtype)

    @pl.when(n == pl.num_programs(1) - 1)
    def _():
        for i in range(nb):
            for j in range(npair):
                sp2 = s_scr[i * npair + j]
                sfin_ref[i, 2 * j] = sp2[:, :hd]
                sfin_ref[i, 2 * j + 1] = sp2[:, hd:]


def _deltanet_kernel(u_ref, z_ref, ba_ref, cprev_ref, s0_ref, cw_ref, alog_ref, dtb_ref, nw_ref,
                     o_ref, sfin_ref, xc_scr, s_scr, *, nb):
    n = pl.program_id(1)
    hd = DN_HEAD_DIM
    rep = DN_V_HEADS // DN_QK_HEADS

    @pl.when(n == 0)
    def _():
        for i in range(nb):
            xc_scr[i, 0:8, :] = cprev_ref[i]
            s_scr[i * DN_V_HEADS:(i + 1) * DN_V_HEADS] = s0_ref[i]

    row = lax.broadcasted_iota(jnp.int32, (CHUNK, CHUNK), 0)
    col = lax.broadcasted_iota(jnp.int32, (CHUNK, CHUNK), 1)
    incl = row >= col
    strict = row > col
    lower01 = jnp.where(incl, 1.0, 0.0).astype(BF16)
    eye = jnp.where(row == col, 1.0, 0.0).astype(F32)
    merge_masks = [(row // 2 == col // 2) & strict]
    blk = 2
    while blk < CHUNK:
        merge_masks.append((row // (2 * blk) == col // (2 * blk)) & (row // blk != col // blk) & strict)
        blk *= 2

    qk_rows, k_items, q_items, v_items, z_items = [], [], [], [], []
    col_items, grow_items = [], []
    for i in range(nb):
        xc_scr[i, 8:8 + CHUNK, :] = u_ref[i]
        conv = xc_scr[i, 5:5 + CHUNK, :] * cw_ref[0:1, :]
        for j in range(1, CONV_W):
            conv = conv + xc_scr[i, 5 + j:5 + j + CHUNK, :] * cw_ref[j:j + 1, :]
        tail = xc_scr[i, CHUNK:CHUNK + 8, :]
        xc_scr[i, 0:8, :] = tail
        cu = _silu(conv)

        ba = ba_ref[i]
        beta_all = jax.nn.sigmoid(ba)
        sp = ba + dtb_ref[...]
        sp = jnp.maximum(sp, 0.0) + jnp.log1p(jnp.exp(-jnp.abs(sp)))
        g_all = -jnp.exp(alog_ref[...]) * sp
        gc_all = _dot_exact_lhs(lower01, g_all)
        gc_t = gc_all.T
        beta_t = beta_all.T

        for hq in range(DN_QK_HEADS):
            q = cu[:, hq * hd:(hq + 1) * hd]
            k = cu[:, DN_QK_DIM + hq * hd:DN_QK_DIM + (hq + 1) * hd]
            q = q * lax.rsqrt(jnp.sum(q * q, axis=-1, keepdims=True) + EPS) * (hd ** -0.5)
            k = k * lax.rsqrt(jnp.sum(k * k, axis=-1, keepdims=True) + EPS)
            qk_rows.append(jnp.concatenate([q, k], axis=0))
            for r in range(rep):
                hv = hq * rep + r
                q_items.append(q)
                k_items.append(k)
                v_items.append(cu[:, 2 * DN_QK_DIM + hv * hd:2 * DN_QK_DIM + (hv + 1) * hd])
                z_items.append(z_ref[i, :, hv * hd:(hv + 1) * hd])
                brow = jnp.broadcast_to(beta_t[hv:hv + 1, :], (CHUNK, CHUNK))
                grow = jnp.broadcast_to(gc_t[8 + hv:9 + hv, :], (CHUNK, CHUNK))
                col_items.append(jnp.concatenate([brow, grow], axis=1).T)
                grow_items.append(grow)

    q = jnp.stack(q_items)
    k = jnp.stack(k_items)
    v = jnp.stack(v_items)
    cols = jnp.stack(col_items)
    beta, gcol = cols[:, :CHUNK], cols[:, CHUNK:]
    grow = jnp.stack(grow_items)
    glast = gcol[:, CHUNK - 1:CHUNK, :]

    qkk = _bmm_nt(jnp.stack(qk_rows), jnp.stack(k_items[::rep]))
    qkk = jnp.stack([qkk[j // rep] for j in range(nb * DN_V_HEADS)])
    qk, kk = qkk[:, :CHUNK], qkk[:, CHUNK:]

    decay = jnp.exp(jnp.where(incl, gcol - grow, NEG))
    eg = jnp.exp(gcol)

    m = jnp.where(strict, (kk * beta) * decay, 0.0)
    t = eye - jnp.where(merge_masks[0], m, 0.0)
    for mask in merge_masks[1:]:
        t = t - _bmm(t, _bmm(jnp.where(mask, m, 0.0), t))
    t0 = t.astype(BF16)
    m_hi, m_lo = _split2(m)
    mt0 = _bmm(jnp.concatenate([m_hi, m_lo], axis=2), jnp.concatenate([t0, t0], axis=1))
    resid = eye - t0.astype(F32) - mt0
    t1 = _bmm(t0, resid).astype(BF16)

    s = s_scr[...]
    ks = _bmm(jnp.concatenate([k * (beta * eg), q * eg], axis=1), s)
    rhs = (v * beta - ks[:, :CHUNK]).astype(BF16)
    vnew = _bmm(jnp.concatenate([t0, t1], axis=2), jnp.concatenate([rhs, rhs], axis=1))
    o = ks[:, CHUNK:] + _bmm(qk * decay, vnew)
    kd = k * jnp.exp(glast - gcol)
    s_scr[...] = s * jnp.exp(glast) + _bmm(jnp.swapaxes(kd, 1, 2), vnew)

    o = _rms(o) * nw_ref[...] * _silu(jnp.stack(z_items))
    for i in range(nb):
        for hv in range(DN_V_HEADS):
            o_ref[i, :, hv * hd:(hv + 1) * hd] = o[i * DN_V_HEADS + hv].astype(o_ref.dtype)

    @pl.when(n == pl.num_programs(1) - 1)
    def _():
        for i in range(nb):
            sfin_ref[i] = s_scr[i * DN_V_HEADS:(i + 1) * DN_V_HEADS]


def _deltanet(u, z, ba, cprev8, s0, conv_w, alog_l, dtb_l, dn_norm_w):
    bsz, t_len, _ = u.shape
    nb = DN_SEQS_PER_STEP
    tok = lambda b, n: (b, n, 0)
    per_b = lambda b, n: (b, 0, 0)
    per_b4 = lambda b, n: (b, 0, 0, 0)
    fixed = lambda b, n: (0, 0)
    return pl.pallas_call(
        functools.partial(_deltanet_pair_kernel, nb=nb),
        grid=(bsz // nb, t_len // CHUNK),
        in_specs=[pl.BlockSpec((nb, CHUNK, DN_CONV_DIM), tok),
                  pl.BlockSpec((nb, CHUNK, DN_V_DIM), tok),
                  pl.BlockSpec((nb, CHUNK, GATE_LANES), tok),
                  pl.BlockSpec((nb, 8, DN_CONV_DIM), per_b),
                  pl.BlockSpec((nb, DN_V_HEADS, DN_HEAD_DIM, DN_HEAD_DIM), per_b4),
                  pl.BlockSpec((CONV_W, DN_CONV_DIM), fixed),
                  pl.BlockSpec((1, GATE_LANES), fixed),
                  pl.BlockSpec((1, GATE_LANES), fixed),
                  pl.BlockSpec((1, 2 * DN_HEAD_DIM), fixed)],
        out_specs=[pl.BlockSpec((nb, CHUNK, DN_V_DIM), tok),
                   pl.BlockSpec((nb, DN_V_HEADS, DN_HEAD_DIM, DN_HEAD_DIM), per_b4)],
        out_shape=[jax.ShapeDtypeStruct((bsz, t_len, DN_V_DIM), BF16),
                   jax.ShapeDtypeStruct((bsz, DN_V_HEADS, DN_HEAD_DIM, DN_HEAD_DIM), F32)],
        scratch_shapes=[pltpu.VMEM((nb, CHUNK + 8, DN_CONV_DIM), F32),
                        pltpu.VMEM((nb * DN_QK_HEADS, DN_HEAD_DIM, 2 * DN_HEAD_DIM), F32)],
        compiler_params=_params("parallel", "arbitrary"),
        name="deltanet",
    )(u, z, ba, cprev8, s0, conv_w, alog_l, dtb_l, dn_norm_w)


def _bias_kernel(bucket_ref, table_ref, o_ref):
    bucket = bucket_ref[...]
    for h in range(SWA_HEADS):
        acc = jnp.zeros(bucket.shape, F32)
        for b in range(NUM_BUCKETS):
            acc = jnp.where(bucket == b, table_ref[b, h], acc)
        o_ref[h] = acc


def _rel_bias(bucket, table):
    return pl.pallas_call(
        _bias_kernel,
        in_specs=[pl.BlockSpec(memory_space=pltpu.VMEM),
                  pl.BlockSpec(memory_space=pltpu.SMEM)],
        out_specs=pl.BlockSpec(memory_space=pltpu.VMEM),
        out_shape=jax.ShapeDtypeStruct((SWA_HEADS, CHUNK, BAND), F32),
        name="rel_bias",
    )(bucket, table)


def _swa_kernel(q_ref, *refs, hist, cps, n_units):
    k_refs, v_refs = refs[:n_units], refs[n_units:2 * n_units]
    bias_ref, sink_ref, o_ref = refs[2 * n_units:]
    n = pl.program_id(1)
    hd = SWA_HEAD_DIM
    kb = jnp.concatenate([r[0] for r in k_refs], axis=0)
    vb = jnp.concatenate([r[0] for r in v_refs], axis=0)
    row0 = (n * cps + hist) * CHUNK - WINDOW
    key = lax.broadcasted_iota(jnp.int32, (1, 1, BAND), 2)
    q_items, k_items, v_items, valid = [], [], [], []
    for c in range(cps):
        for kv in range(SWA_KV_HEADS):
            q_items.append(jnp.concatenate(
                [q_ref[0, c * CHUNK:(c + 1) * CHUNK, (kv * SWA_GROUP + g) * hd:(kv * SWA_GROUP + g + 1) * hd]
                 for g in range(SWA_GROUP)], axis=0))
            k_items.append(kb[c * CHUNK:c * CHUNK + BAND, kv * hd:(kv + 1) * hd])
            v_items.append(vb[c * CHUNK:c * CHUNK + BAND, kv * hd:(kv + 1) * hd])
            valid.append(row0 + c * CHUNK + key >= 0)
    s = _bmm_nt(jnp.stack(q_items), jnp.stack(k_items)) * (hd ** -0.5) + bias_ref[...]
    s = jnp.where(jnp.concatenate(valid, axis=0), s, NEG)
    sink = sink_ref[...]
    mx = jnp.maximum(jnp.max(s, axis=-1, keepdims=True), sink)
    p = jnp.exp(s - mx).astype(BF16)
    den = _bmm(p, jnp.ones((len(v_items), BAND, hd), BF16)) + jnp.exp(sink - mx)
    o = _bmm(p, jnp.stack(v_items)) / den
    for c in range(cps):
        for kv in range(SWA_KV_HEADS):
            for g in range(SWA_GROUP):
                h = kv * SWA_GROUP + g
                o_ref[0, c * CHUNK:(c + 1) * CHUNK, h * hd:(h + 1) * hd] = (
                    o[c * SWA_KV_HEADS + kv, g * CHUNK:(g + 1) * CHUNK].astype(o_ref.dtype))


def _swa(q, k_all, v_all, bias, sink_rows, hist):
    bsz, t_len, _ = q.shape
    cps = min(SWA_CHUNKS_PER_STEP, t_len // CHUNK)
    unit = WINDOW if cps * CHUNK % WINDOW == 0 else CHUNK
    assert (hist * CHUNK - WINDOW) % unit == 0 and (cps * CHUNK) % unit == 0
    n_units = (WINDOW + cps * CHUNK) // unit
    q_units = cps * CHUNK // unit
    unit0 = (hist * CHUNK - WINDOW) // unit
    tok = lambda b, n: (b, n, 0)

    def band(j):
        return lambda b, n: (b, jnp.maximum(n * q_units + unit0 + j, 0), 0)

    kv_specs = [pl.BlockSpec((1, unit, SWA_KV_DIM), band(j)) for j in range(n_units)]
    fixed3 = lambda b, n: (0, 0, 0)
    return pl.pallas_call(
        functools.partial(_swa_kernel, hist=hist, cps=cps, n_units=n_units),
        grid=(bsz, t_len // (cps * CHUNK)),
        in_specs=[pl.BlockSpec((1, cps * CHUNK, SWA_Q_DIM), tok)] + kv_specs + kv_specs
                 + [pl.BlockSpec((cps * SWA_KV_HEADS, SWA_GROUP * CHUNK, BAND), fixed3),
                    pl.BlockSpec((cps * SWA_KV_HEADS, SWA_GROUP * CHUNK, 1), fixed3)],
        out_specs=pl.BlockSpec((1, cps * CHUNK, SWA_Q_DIM), tok),
        out_shape=jax.ShapeDtypeStruct((bsz, t_len, SWA_Q_DIM), BF16),
        compiler_params=_params("parallel", "arbitrary"),
        name="swa",
    )(q, *([k_all] * n_units), *([v_all] * n_units),
      jnp.tile(bias, (cps, 1, 1)), jnp.tile(sink_rows, (cps, 1, 1)))


def _outproj_kernel(odn_ref, oswa_ref, x_ref, g1_ref, sc_ref, sh_ref, wo1_ref, wo2_ref, bo_ref,
                    nw_ref, wr_ref, br_ref, cnt0_ref, x1_ref, h2_ref, ti_ref, tg_ref, rk_ref, cnt_ref, cnt_scr):
    first = (pl.program_id(0) == 0) & (pl.program_id(1) == 0)

    @pl.when(first)
    def _():
        cnt_scr[...] = cnt0_ref[...]

    mix = _dot(odn_ref[0].astype(BF16), wo1_ref[...]) + _dot(oswa_ref[0].astype(BF16), wo2_ref[...])
    x1 = x_ref[0] + g1_ref[0] * (mix + bo_ref[...])
    x1_ref[0] = x1
    h2 = _rms(x1) * nw_ref[...]
    h2 = h2 * (1.0 + sc_ref[0]) + sh_ref[0]
    h2_ref[0] = _pack_bf16_pairs(h2)

    logits = _dot_x3(h2, wr_ref[...]) + br_ref[...]
    lane = lax.broadcasted_iota(jnp.int32, logits.shape, 1)
    lane_f = lane.astype(F32)
    vals, idxs = [], []
    for _ in range(TOP_K):
        m = jnp.max(logits, axis=-1, keepdims=True)
        i = jnp.min(jnp.where(logits == m, lane_f, float(LANES)), axis=-1, keepdims=True)
        vals.append(m)
        idxs.append(i)
        logits = jnp.where(lane_f == i, -jnp.inf, logits)
    es = [jnp.exp(v - vals[0]) for v in vals]
    den = es[0] + es[1] + es[2] + es[3]
    tm = lane.shape[0]
    onehot = jnp.zeros(lane.shape, F32)
    for kk in range(TOP_K):
        onehot = jnp.where(lane_f == idxs[kk], 1.0, onehot)
    r_i = lax.broadcasted_iota(jnp.int32, (tm, tm), 0)
    c_i = lax.broadcasted_iota(jnp.int32, (tm, tm), 1)
    before = _dot(jnp.where(r_i > c_i, 1.0, 0.0).astype(BF16), onehot.astype(BF16)) + cnt_scr[...]
    cnt_scr[...] = cnt_scr[...] + jnp.sum(onehot, axis=0, keepdims=True)

    ti = jnp.zeros(lane.shape, jnp.int32)
    tg = jnp.zeros(lane.shape, F32)
    rk = jnp.zeros(lane.shape, jnp.int32)
    for kk in range(TOP_K):
        rank = jnp.sum(jnp.where(lane_f == idxs[kk], before, 0.0), axis=-1, keepdims=True)
        ti = jnp.where(lane == kk, idxs[kk].astype(jnp.int32), ti)
        tg = jnp.where(lane == kk, es[kk] / den, tg)
        rk = jnp.where(lane == kk, rank.astype(jnp.int32), rk)
    ti_ref[0] = ti
    tg_ref[0] = tg
    rk_ref[0] = rk
    cnt_ref[...] = cnt_scr[...]


def _outproj(odn, oswa, x, g1, sc, sh, wo1, wo2, bo, norm_w, wr, br, cnt0, tm):
    bsz, t_len, _ = x.shape
    tok = lambda b, i: (b, i, 0)
    per_b = lambda b, i: (b, 0, 0)
    fixed = lambda b, i: (0, 0)
    return pl.pallas_call(
        _outproj_kernel,
        grid=(bsz, t_len // tm),
        in_specs=[pl.BlockSpec((1, tm, DN_V_DIM), tok),
                  pl.BlockSpec((1, tm, SWA_Q_DIM), tok),
                  pl.BlockSpec((1, tm, D_MODEL), tok),
                  pl.BlockSpec((1, 1, D_MODEL), per_b),
                  pl.BlockSpec((1, 1, D_MODEL), per_b),
                  pl.BlockSpec((1, 1, D_MODEL), per_b),
                  pl.BlockSpec((DN_V_DIM, D_MODEL), fixed),
                  pl.BlockSpec((SWA_Q_DIM, D_MODEL), fixed),
                  pl.BlockSpec((1, D_MODEL), fixed),
                  pl.BlockSpec((1, D_MODEL), fixed),
                  pl.BlockSpec((D_MODEL, LANES), fixed),
                  pl.BlockSpec((1, LANES), fixed),
                  pl.BlockSpec((1, LANES), fixed)],
        out_specs=[pl.BlockSpec((1, tm, D_MODEL), tok),
                   pl.BlockSpec((1, tm, D_MODEL // 2), tok),
                   pl.BlockSpec((1, tm, LANES), tok),
                   pl.BlockSpec((1, tm, LANES), tok),
                   pl.BlockSpec((1, tm, LANES), tok),
                   pl.BlockSpec((1, LANES), fixed)],
        out_shape=[jax.ShapeDtypeStruct((bsz, t_len, D_MODEL), F32),
                   jax.ShapeDtypeStruct((bsz, t_len, D_MODEL // 2), jnp.uint32),
                   jax.ShapeDtypeStruct((bsz, t_len, LANES), jnp.int32),
                   jax.ShapeDtypeStruct((bsz, t_len, LANES), F32),
                   jax.ShapeDtypeStruct((bsz, t_len, LANES), jnp.int32),
                   jax.ShapeDtypeStruct((1, LANES), F32)],
        scratch_shapes=[pltpu.VMEM((1, LANES), F32)],
        compiler_params=_params("arbitrary", "arbitrary"),
        name="outproj_router",
    )(odn, oswa, x, g1, sc, sh, wo1, wo2, bo, norm_w, wr, br, cnt0)


def _moe_kernel(be_ref, nv_ref, xs_ref, w1_ref, b1_ref, w2_ref, b2_ref, y_ref, w1b_scr, w2b_scr):
    i = pl.program_id(0)
    nv = nv_ref[i]

    @pl.when((i == 0) | (be_ref[i] != be_ref[jnp.maximum(i - 1, 0)]))
    def _():
        w1b_scr[...] = w1_ref[0].astype(BF16)
        w2b_scr[...] = w2_ref[0].astype(BF16)

    @pl.when(nv == 0)
    def _():
        y_ref[...] = jnp.zeros(y_ref.shape, y_ref.dtype)

    @pl.when(nv > 0)
    def _():
        rows = lax.broadcasted_iota(jnp.int32, (MOE_ROWS, 1), 0)
        xb = jnp.where(rows < nv, _unpack_bf16_pairs(xs_ref[...]), 0.0).astype(BF16)
        up = _dot(xb, w1b_scr[...]) + b1_ref[0]
        glu = jnp.minimum(up[:, :D_FF], SWIGLU_LIMIT)
        lin = jnp.clip(up[:, D_FF:], -SWIGLU_LIMIT, SWIGLU_LIMIT)
        act = glu * jax.nn.sigmoid(SWIGLU_ALPHA * glu) * (lin + 1.0)
        y_ref[...] = _pack_bf16_pairs(_dot(act.astype(BF16), w2b_scr[...]) + b2_ref[0])


def _moe_experts(blk_expert, blk_valid, xs, w1, b1, w2, b2):
    n_rows = xs.shape[0]
    n_blocks = n_rows // MOE_ROWS
    half = D_MODEL // 2
    grid_spec = pltpu.PrefetchScalarGridSpec(
        num_scalar_prefetch=2,
        grid=(n_blocks,),
        in_specs=[pl.BlockSpec((MOE_ROWS, half), lambda i, be, nv: (i, 0)),
                  pl.BlockSpec((1, D_MODEL, 2 * D_FF), lambda i, be, nv: (be[i], 0, 0)),
                  pl.BlockSpec((1, 1, 2 * D_FF), lambda i, be, nv: (be[i], 0, 0)),
                  pl.BlockSpec((1, D_FF, D_MODEL), lambda i, be, nv: (be[i], 0, 0)),
                  pl.BlockSpec((1, 1, D_MODEL), lambda i, be, nv: (be[i], 0, 0))],
        out_specs=pl.BlockSpec((MOE_ROWS, half), lambda i, be, nv: (i, 0)),
        scratch_shapes=[pltpu.VMEM((D_MODEL, 2 * D_FF), BF16), pltpu.VMEM((D_FF, D_MODEL), BF16)],
    )
    return pl.pallas_call(
        _moe_kernel,
        grid_spec=grid_spec,
        out_shape=jax.ShapeDtypeStruct((n_rows, half), jnp.uint32),
        compiler_params=_params("arbitrary"),
        name="moe_experts",
    )(blk_expert, blk_valid, xs, w1, b1, w2, b2)


def _sc_mesh():
    return plsc.VectorSubcoreMesh(core_axis_name="core", subcore_axis_name="subcore")


def _gather_rows(x, idx):
    m = idx.shape[0]
    w = x.shape[1]

    @pl.kernel(out_type=jax.ShapeDtypeStruct((m, w), x.dtype), mesh=_sc_mesh())
    def gather_kernel(x_hbm, i_hbm, o_hbm):
        def body(i_vmem, o_vmem):
            pltpu.sync_copy(x_hbm.at[i_vmem.at[0]], o_vmem)

        pltpu.emit_pipeline(
            body,
            grid=(m // SC_WINDOW,),
            in_specs=[pl.BlockSpec((1, SC_WINDOW), lambda i: (i, 0))],
            out_specs=[pl.BlockSpec((SC_WINDOW, w), lambda i: (i, 0))],
            core_axis_name=("core", "subcore"),
            dimension_semantics=(pltpu.PARALLEL,),
        )(i_hbm, o_hbm)

    return gather_kernel(x, idx.reshape(m // SC_WINDOW, SC_WINDOW))


def _scatter_rows(x, idx, n_out):
    n, w = x.shape
    kk = idx.shape[1]
    idx3 = jnp.transpose(idx.reshape(n // SC_WINDOW, SC_WINDOW, kk), (0, 2, 1))

    @pl.kernel(out_type=jax.ShapeDtypeStruct((n_out, w), x.dtype), mesh=_sc_mesh())
    def scatter_kernel(x_hbm, i_hbm, o_hbm):
        def body(x_vmem, i_vmem):
            for k in range(kk):
                pltpu.sync_copy(x_vmem, o_hbm.at[i_vmem.at[0, k]])

        pltpu.emit_pipeline(
            body,
            grid=(n // SC_WINDOW,),
            in_specs=[pl.BlockSpec((SC_WINDOW, w), lambda i: (i, 0)),
                      pl.BlockSpec((1, kk, SC_WINDOW), lambda i: (i, 0, 0))],
            out_specs=[],
            core_axis_name=("core", "subcore"),
            dimension_semantics=(pltpu.PARALLEL,),
        )(x_hbm, i_hbm)

    return scatter_kernel(x, idx3)


def _final_kernel(x1_ref, yg_ref, tg_ref, g2_ref, nw_ref, y_ref):
    tg = tg_ref[0]
    moe = _unpack_bf16_pairs(yg_ref[0]) * tg[:, 0:1]
    for kk in range(1, TOP_K):
        moe = moe + _unpack_bf16_pairs(yg_ref[kk]) * tg[:, kk:kk + 1]
    x2 = x1_ref[0] + g2_ref[0] * moe
    y_ref[0] = _rms(x2) * nw_ref[...]


def _final(x1, yg, tg, g2, norm_w, tm, tok_offset):
    bsz, t_len, _ = x1.shape
    tok = lambda b, i: (b, i, 0)
    steps = t_len // tm
    blk0 = tok_offset // tm
    return pl.pallas_call(
        _final_kernel,
        grid=(bsz, steps),
        in_specs=[pl.BlockSpec((1, tm, D_MODEL), tok),
                  pl.BlockSpec((TOP_K, tm, D_MODEL // 2), lambda b, i: (0, blk0 + b * steps + i, 0)),
                  pl.BlockSpec((1, tm, LANES), tok),
                  pl.BlockSpec((1, 1, D_MODEL), lambda b, i: (b, 0, 0)),
                  pl.BlockSpec((1, D_MODEL), lambda b, i: (0, 0))],
        out_specs=pl.BlockSpec((1, tm, D_MODEL), tok),
        out_shape=jax.ShapeDtypeStruct((bsz, t_len, D_MODEL), F32),
        compiler_params=_params("parallel", "arbitrary"),
        name="combine_final",
    )(x1, yg, tg, g2, norm_w)


def _t5_bucket(rel):
    half = NUM_BUCKETS // 2
    max_exact = half // 2
    ret = jnp.where(rel > 0, half, 0)
    n = jnp.abs(rel)
    nf = jnp.maximum(n, 1).astype(jnp.float32)
    large = max_exact + (jnp.log(nf / max_exact) / math.log(MAX_DISTANCE / max_exact)
                         * (half - max_exact)).astype(jnp.int32)
    large = jnp.minimum(large, half - 1)
    return ret + jnp.where(n < max_exact, n, large)


def _route(top_i, rank, sizes):
    n_tok = top_i.shape[0]
    n_asg = n_tok * TOP_K
    padded = (sizes + MOE_ROWS - 1) // MOE_ROWS * MOE_ROWS
    pend = jnp.cumsum(padded)
    pstart = pend - padded
    n_blocks = -(-n_asg // MOE_ROWS) + N_EXPERTS
    blk_row0 = jnp.arange(n_blocks, dtype=jnp.int32) * MOE_ROWS
    blk_expert = jnp.minimum(jnp.sum(pend[None, :] <= blk_row0[:, None], axis=1), N_EXPERTS - 1).astype(jnp.int32)
    blk_valid = jnp.clip(pstart[blk_expert] + sizes[blk_expert] - blk_row0, 0, MOE_ROWS).astype(jnp.int32)
    onehot = top_i[:, :, None] == jnp.arange(N_EXPERTS, dtype=jnp.int32)
    pos = jnp.sum(jnp.where(onehot, pstart, 0), axis=-1) + rank
    return pos, blk_expert, blk_valid, n_blocks * MOE_ROWS


def kernel(x_prompt, x_sample, c_prompt, c_sample, state_conv, state_delta, cache_swa_k, cache_swa_v,
           w_ada, b_ada, norm_mix, w_in, b_in, conv_w, a_log, dt_bias, dn_norm_w, sinks, rel_bias,
           w_out, b_out, norm_ffn, w_router, b_router, w1, b1, w2, b2, norm_final):
    depth = w_ada.shape[0]
    assert depth == 1, "the final norm is fused into the layer's combine step"
    bp, tp, _ = x_prompt.shape
    bs, ts, _ = x_sample.shape
    groups = [dict(x=x_prompt, c=c_prompt, hist=0), dict(x=x_sample, c=c_sample, hist=WIN_CHUNKS)]

    q_rel = jnp.arange(CHUNK)
    k_rel = jnp.arange(BAND) - WIN_CHUNKS * CHUNK
    bucket = _t5_bucket(k_rel[None, :] - q_rel[:, None]).astype(jnp.int32)
    bias = _rel_bias(bucket, rel_bias)

    nf_w = norm_final.reshape(1, D_MODEL)
    outs = {g: dict(conv=[], delta=[], k=[], v=[]) for g in range(2)}
    xs_cur = [x_prompt, x_sample]

    for l in range(depth):
        o1 = DN_CONV_DIM + DN_V_DIM
        o2 = o1 + 2 * DN_V_HEADS
        wl, bl = w_in[l], b_in[l]
        w_cat = jnp.concatenate(
            [wl[:, :o1], wl[:, o2:], wl[:, o1:o2], jnp.zeros((D_MODEL, GATE_LANES - 2 * DN_V_HEADS), F32)],
            axis=1).astype(BF16)
        b_cat = jnp.concatenate(
            [bl[:o1], bl[o2:], bl[o1:o2], jnp.zeros((GATE_LANES - 2 * DN_V_HEADS,), F32)]).reshape(1, PROJ_DIM)
        pad8 = jnp.zeros((DN_V_HEADS,), F32)
        padr = jnp.zeros((GATE_LANES - 2 * DN_V_HEADS,), F32)
        alog_l = jnp.concatenate([pad8, a_log[l], padr]).reshape(1, GATE_LANES)
        dtb_l = jnp.concatenate([pad8, dt_bias[l], padr]).reshape(1, GATE_LANES)
        wo1 = w_out[l][:DN_V_DIM].astype(BF16)
        wo2 = w_out[l][DN_V_DIM:].astype(BF16)
        wr = jnp.concatenate([w_router[l], jnp.zeros((D_MODEL, LANES - N_EXPERTS), F32)], axis=1)
        br = jnp.concatenate([b_router[l], jnp.full((LANES - N_EXPERTS,), NEG, F32)]).reshape(1, LANES)
        b1l = b1[l].reshape(N_EXPERTS, 1, 2 * D_FF)
        b2l = b2[l].reshape(N_EXPERTS, 1, D_MODEL)

        mod = _modulation(jnp.concatenate([c_prompt, c_sample], axis=0), w_ada[l], b_ada[l])
        stage = []
        cnt = jnp.zeros((1, LANES), F32)
        for gi, grp in enumerate(groups):
            x = xs_cur[gi]
            bsz, t_len, _ = x.shape
            tm = min(TOKEN_TILE, t_len)
            m = mod[:bp] if gi == 0 else mod[bp:]
            sh1, sc1, g1, sh2, sc2, g2 = [a.reshape(bsz, 1, D_MODEL) for a in jnp.split(m, 6, axis=-1)]
            u, z, sq, sk, sv, ba = _inproj(x, sc1, sh1, norm_mix[l].reshape(1, D_MODEL), w_cat, b_cat, tm)
            if gi == 0:
                cprev = jnp.zeros((bsz, CONV_W - 1, DN_CONV_DIM), F32)
                s0 = jnp.zeros((bsz, DN_V_HEADS, DN_HEAD_DIM, DN_HEAD_DIM), F32)
                k_all, v_all = sk, sv
            else:
                cprev = state_conv[l]
                s0 = state_delta[l]
                k_all = jnp.concatenate([cache_swa_k[l].reshape(bsz, -1, SWA_KV_DIM), sk], axis=1)
                v_all = jnp.concatenate([cache_swa_v[l].reshape(bsz, -1, SWA_KV_DIM), sv], axis=1)
            swa_len = cache_swa_k.shape[2]
            cprev8 = jnp.concatenate([jnp.zeros((bsz, 8 - (CONV_W - 1), DN_CONV_DIM), F32), cprev], axis=1)
            o_dn, s_new = _deltanet(u, z, ba, cprev8, s0, conv_w[l], alog_l, dtb_l,
                                    jnp.tile(dn_norm_w[l], 2).reshape(1, 2 * DN_HEAD_DIM))
            sink_rows = jnp.repeat(sinks[l], CHUNK).reshape(SWA_KV_HEADS, SWA_GROUP * CHUNK, 1)
            o_swa = _swa(sq, k_all, v_all, bias.reshape(SWA_KV_HEADS, SWA_GROUP * CHUNK, BAND), sink_rows,
                         grp["hist"])
            x1, h2, ti, tg, rk, cnt = _outproj(o_dn, o_swa, x, g1, sc2, sh2, wo1, wo2,
                                               b_out[l].reshape(1, D_MODEL), norm_ffn[l].reshape(1, D_MODEL),
                                               wr, br, cnt, tm)
            assert t_len >= CONV_W - 1
            conv_new = u[:, t_len - (CONV_W - 1):]
            k_state = k_all[:, k_all.shape[1] - swa_len:].reshape(bsz, swa_len, SWA_KV_HEADS, SWA_HEAD_DIM)
            v_state = v_all[:, v_all.shape[1] - swa_len:].reshape(bsz, swa_len, SWA_KV_HEADS, SWA_HEAD_DIM)
            outs[gi]["conv"].append(conv_new)
            outs[gi]["delta"].append(s_new)
            outs[gi]["k"].append(k_state)
            outs[gi]["v"].append(v_state)
            stage.append(dict(x1=x1, h2=h2, ti=ti, tg=tg, rk=rk, g2=g2, tm=tm))

        h2_all = jnp.concatenate([s["h2"].reshape(-1, D_MODEL // 2) for s in stage], axis=0)
        ti_all = jnp.concatenate([s["ti"].reshape(-1, LANES)[:, :TOP_K] for s in stage], axis=0)
        rk_all = jnp.concatenate([s["rk"].reshape(-1, LANES)[:, :TOP_K] for s in stage], axis=0)
        sizes = cnt[0, :N_EXPERTS].astype(jnp.int32)
        pos, blk_expert, blk_valid, n_rows = _route(ti_all, rk_all, sizes)
        xs = _scatter_rows(h2_all, pos, n_rows)
        y = _moe_experts(blk_expert, blk_valid, xs, w1[l], b1l, w2[l], b2l)
        n_tok = h2_all.shape[0]
        yg = _gather_rows(y, pos.T.reshape(-1)).reshape(TOP_K, n_tok, D_MODEL // 2)
        off = 0
        for gi, s in enumerate(stage):
            bsz, t_len, _ = s["x1"].shape
            xs_cur[gi] = _final(s["x1"], yg, s["tg"], s["g2"], nf_w, s["tm"], off)
            off += bsz * t_len

    res = [xs_cur[0], xs_cur[1]]
    for gi in range(2):
        for name in ("conv", "delta", "k", "v"):
            res.append(jnp.stack(outs[gi][name]))
    return tuple(res)
```

```python
import functools
import math

import jax
import jax.numpy as jnp
from jax import lax
from jax.experimental import pallas as pl
from jax.experimental.pallas import tpu as pltpu
from jax.experimental.pallas import tpu_sc as plsc

F32 = jnp.float32
BF16 = jnp.bfloat16

D_MODEL = 1024
CHUNK = 64
EPS = 1e-6
DN_QK_HEADS = 4
DN_V_HEADS = 8
DN_HEAD_DIM = 64
DN_QK_DIM = DN_QK_HEADS * DN_HEAD_DIM
DN_V_DIM = DN_V_HEADS * DN_HEAD_DIM
DN_CONV_DIM = 2 * DN_QK_DIM + DN_V_DIM
CONV_W = 4
SWA_HEADS = 8
SWA_KV_HEADS = 2
SWA_HEAD_DIM = 64
SWA_GROUP = SWA_HEADS // SWA_KV_HEADS
SWA_Q_DIM = SWA_HEADS * SWA_HEAD_DIM
SWA_KV_DIM = SWA_KV_HEADS * SWA_HEAD_DIM
WINDOW = 128
WIN_CHUNKS = WINDOW // CHUNK
BAND = (WIN_CHUNKS + 1) * CHUNK
NUM_BUCKETS = 32
MAX_DISTANCE = 128
N_EXPERTS = 32
TOP_K = 4
D_FF = 1024
SWIGLU_ALPHA = 1.702
SWIGLU_LIMIT = 7.0

LANES = 128
GATE_LANES = LANES
PROJ_DIM = DN_CONV_DIM + DN_V_DIM + SWA_Q_DIM + 2 * SWA_KV_DIM + GATE_LANES
MOE_ROWS = 1024
TOKEN_TILE = 512
DN_SEQS_PER_STEP = 4
SWA_CHUNKS_PER_STEP = 4
SC_WINDOW = 64
VMEM_LIMIT = 56 * 1024 * 1024
NEG = -1e30


def _params(*sem):
    return pltpu.CompilerParams(dimension_semantics=sem, vmem_limit_bytes=VMEM_LIMIT)


def _split2(a):
    hi = a.astype(BF16)
    lo = (a - hi.astype(F32)).astype(BF16)
    return hi, lo


def _dot(a, b):
    return jnp.dot(a, b, preferred_element_type=F32)


def _dot_nt(a, b):
    return lax.dot_general(a, b, (((1,), (1,)), ((), ())), preferred_element_type=F32)


def _dot_x3(a, b):
    a1, a2 = _split2(a)
    b1, b2 = _split2(b)
    return _dot(a1, b1) + (_dot(a1, b2) + _dot(a2, b1))


def _dot_exact_lhs(l01, g):
    g1 = g.astype(BF16)
    r = g - g1.astype(F32)
    g2 = r.astype(BF16)
    g3 = (r - g2.astype(F32)).astype(BF16)
    return _dot(l01, g1) + (_dot(l01, g2) + _dot(l01, g3))


def _pack_bf16_pairs(x):
    w = x.shape[1] // 2
    bits = lax.bitcast_convert_type(x.astype(BF16).astype(F32), jnp.uint32)
    return (bits[:, w:] & jnp.uint32(0xFFFF0000)) | (bits[:, :w] >> 16)


def _unpack_bf16_pairs(p):
    lo = lax.bitcast_convert_type(p << 16, F32)
    hi = lax.bitcast_convert_type(p & jnp.uint32(0xFFFF0000), F32)
    return jnp.concatenate([lo, hi], axis=1)


def _silu(x):
    return x * jax.nn.sigmoid(x)


def _rms(x):
    return x * lax.rsqrt(jnp.mean(x * x, axis=-1, keepdims=True) + EPS)


def _mod_kernel(c_ref, w_ref, b_ref, o_ref):
    o_ref[...] = _dot_x3(_silu(c_ref[...]), w_ref[...]) + b_ref[...]


def _modulation(c, w_ada, b_ada):
    nb = c.shape[0]
    n_out = w_ada.shape[1]
    tn = 1024
    return pl.pallas_call(
        _mod_kernel,
        grid=(n_out // tn,),
        in_specs=[pl.BlockSpec((nb, D_MODEL), lambda j: (0, 0)),
                  pl.BlockSpec((D_MODEL, tn), lambda j: (0, j)),
                  pl.BlockSpec((1, tn), lambda j: (0, j))],
        out_specs=pl.BlockSpec((nb, tn), lambda j: (0, j)),
        out_shape=jax.ShapeDtypeStruct((nb, n_out), F32),
        compiler_params=_params("arbitrary"),
        name="modulation",
    )(c, w_ada, b_ada.reshape(1, n_out))


def _inproj_kernel(x_ref, sc_ref, sh_ref, nw_ref, w_ref, b_ref,
                   u_ref, z_ref, sq_ref, sk_ref, sv_ref, ba_ref):
    h = _rms(x_ref[0]) * nw_ref[...]
    h = h * (1.0 + sc_ref[0]) + sh_ref[0]
    p = _dot(h.astype(BF16), w_ref[...]) + b_ref[...]
    o = 0
    for ref in (u_ref, z_ref, sq_ref, sk_ref, sv_ref, ba_ref):
        w = ref.shape[-1]
        ref[0] = p[:, o:o + w].astype(ref.dtype)
        o += w


def _inproj(x, sc, sh, norm_w, w_cat, b_cat, tm):
    bsz, t_len, _ = x.shape
    widths = (DN_CONV_DIM, DN_V_DIM, SWA_Q_DIM, SWA_KV_DIM, SWA_KV_DIM, GATE_LANES)
    tok = lambda b, i: (b, i, 0)
    per_b = lambda b, i: (b, 0, 0)
    fixed = lambda b, i: (0, 0)
    return pl.pallas_call(
        _inproj_kernel,
        grid=(bsz, t_len // tm),
        in_specs=[pl.BlockSpec((1, tm, D_MODEL), tok),
                  pl.BlockSpec((1, 1, D_MODEL), per_b),
                  pl.BlockSpec((1, 1, D_MODEL), per_b),
                  pl.BlockSpec((1, D_MODEL), fixed),
                  pl.BlockSpec((D_MODEL, PROJ_DIM), fixed),
                  pl.BlockSpec((1, PROJ_DIM), fixed)],
        out_specs=[pl.BlockSpec((1, tm, w), tok) for w in widths],
        out_shape=[jax.ShapeDtypeStruct((bsz, t_len, w), BF16 if i == 2 else F32) for i, w in enumerate(widths)],
        compiler_params=_params("parallel", "arbitrary"),
        name="inproj",
    )(x, sc, sh, norm_w, w_cat, b_cat)


def _bmm(a, b):
    return lax.dot_general(a.astype(BF16), b.astype(BF16), (((2,), (1,)), ((0,), (0,))),
                           preferred_element_type=F32)


def _bmm_nt(a, b):
    return lax.dot_general(a.astype(BF16), b.astype(BF16), (((2,), (2,)), ((0,), (0,))),
                           preferred_element_type=F32)


def _bd(x):
    x = x.astype(BF16)
    lo = lax.broadcasted_iota(jnp.int32, x.shape, 2) < x.shape[2] // 2
    zero = jnp.zeros_like(x)
    return jnp.concatenate([jnp.where(lo, x, zero), jnp.where(lo, zero, x)], axis=1)


def _pmm(a, b):
    return _bmm(a, _bd(b))


def _half_sums(x, lo):
    s_lo = jnp.sum(jnp.where(lo, x, 0.0), axis=-1, keepdims=True)
    s_hi = jnp.sum(jnp.where(lo, 0.0, x), axis=-1, keepdims=True)
    return jnp.where(lo, s_lo, s_hi)


def _deltanet_pair_kernel(u_ref, z_ref, ba_ref, cprev_ref, s0_ref, cw_ref, alog_ref, dtb_ref, nw_ref,
                          o_ref, sfin_ref, xc_scr, s_scr, *, nb):
    n = pl.program_id(1)
    hd = DN_HEAD_DIM
    pw = 2 * hd
    npair = DN_QK_HEADS

    @pl.when(n == 0)
    def _():
        for i in range(nb):
            xc_scr[i, 0:8, :] = cprev_ref[i]
            for j in range(npair):
                s_scr[i * npair + j] = jnp.concatenate([s0_ref[i, 2 * j], s0_ref[i, 2 * j + 1]], axis=1)

    row = lax.broadcasted_iota(jnp.int32, (CHUNK, pw), 0)
    col = lax.broadcasted_iota(jnp.int32, (CHUNK, pw), 1) % hd
    lo = lax.broadcasted_iota(jnp.int32, (CHUNK, pw), 1) < hd
    incl = row >= col
    strict = row > col
    r1 = lax.broadcasted_iota(jnp.int32, (CHUNK, CHUNK), 0)
    c1 = lax.broadcasted_iota(jnp.int32, (CHUNK, CHUNK), 1)
    lower01 = jnp.where(r1 >= c1, 1.0, 0.0).astype(BF16)
    eye = jnp.where(row == col, 1.0, 0.0).astype(F32)
    merge_masks = [(row // 2 == col // 2) & strict]
    blk = 2
    while blk < CHUNK:
        merge_masks.append((row // (2 * blk) == col // (2 * blk)) & (row // blk != col // blk) & strict)
        blk *= 2

    qk_lhs, qk_rhs, q_items, k_items, v_items, z_items = [], [], [], [], [], []
    zt_items, grow_items = [], []
    for i in range(nb):
        xc_scr[i, 8:8 + CHUNK, :] = u_ref[i]
        conv = xc_scr[i, 5:5 + CHUNK, :] * cw_ref[0:1, :]
        for j in range(1, CONV_W):
            conv = conv + xc_scr[i, 5 + j:5 + j + CHUNK, :] * cw_ref[j:j + 1, :]
        tail = xc_scr[i, CHUNK:CHUNK + 8, :]
        xc_scr[i, 0:8, :] = tail
        cu = _silu(conv)

        ba = ba_ref[i]
        beta_all = jax.nn.sigmoid(ba)
        sp = ba + dtb_ref[...]
        sp = jnp.maximum(sp, 0.0) + jnp.log1p(jnp.exp(-jnp.abs(sp)))
        g_all = -jnp.exp(alog_ref[...]) * sp
        gc_all = _dot_exact_lhs(lower01, g_all)
        gc_t = gc_all.T
        beta_t = beta_all.T

        for c in range(DN_QK_DIM // pw):
            qc = cu[:, c * pw:(c + 1) * pw]
            kc = cu[:, DN_QK_DIM + c * pw:DN_QK_DIM + (c + 1) * pw]
            qc = qc * lax.rsqrt(_half_sums(qc * qc, lo) + EPS) * (hd ** -0.5)
            kc = kc * lax.rsqrt(_half_sums(kc * kc, lo) + EPS)
            qr = pltpu.roll(qc, hd, axis=1)
            kr = pltpu.roll(kc, hd, axis=1)
            for half in range(2):
                sel = lo if half == 0 else jnp.logical_not(lo)
                q_items.append(jnp.where(sel, qc, qr))
                k_items.append(jnp.where(sel, kc, kr))
                km = jnp.where(sel, kc, 0.0)
                qk_lhs.append(jnp.concatenate([qc, kc], axis=0))
                qk_rhs.append(jnp.concatenate([km, km], axis=0))
        for j in range(npair):
            a, b = 2 * j, 2 * j + 1
            v_items.append(cu[:, 2 * DN_QK_DIM + j * pw:2 * DN_QK_DIM + (j + 1) * pw])
            z_items.append(z_ref[i, :, j * pw:(j + 1) * pw])
            ra = jnp.concatenate([beta_t[a:a + 1, :], gc_t[8 + a:9 + a, :]], axis=1)
            rb = jnp.concatenate([beta_t[b:b + 1, :], gc_t[8 + b:9 + b, :]], axis=1)
            zt_items.append(jnp.concatenate([jnp.broadcast_to(ra, (CHUNK, pw)),
                                             jnp.broadcast_to(rb, (CHUNK, pw))], axis=0).T)
            grow_items.append(jnp.broadcast_to(
                jnp.concatenate([gc_t[8 + a:9 + a, :], gc_t[8 + b:9 + b, :]], axis=1), (CHUNK, pw)))

    q = jnp.stack(q_items)
    k = jnp.stack(k_items)
    v = jnp.stack(v_items)
    cols = jnp.stack(zt_items)
    beta, gcol = cols[:, :CHUNK], cols[:, CHUNK:]
    grow = jnp.stack(grow_items)
    glast = gcol[:, CHUNK - 1:CHUNK, :]

    qkk = _bmm_nt(jnp.stack(qk_lhs), jnp.stack(qk_rhs))
    qk, kk = qkk[:, :CHUNK], qkk[:, CHUNK:]

    decay = jnp.exp(jnp.where(incl, gcol - grow, NEG))
    eg = jnp.exp(gcol)

    m = jnp.where(strict, (kk * beta) * decay, 0.0)
    t = eye - jnp.where(merge_masks[0], m, 0.0)
    for mask in merge_masks[1:]:
        t = t - _pmm(t, _pmm(jnp.where(mask, m, 0.0), t))
    t0 = t.astype(BF16)
    t0_bd = _bd(t0)
    m_hi, m_lo = _split2(m)
    mt0 = _bmm(jnp.concatenate([m_hi, m_lo], axis=2), jnp.concatenate([t0_bd, t0_bd], axis=1))
    resid = eye - t0.astype(F32) - mt0
    t1 = _pmm(t0, resid).astype(BF16)

    s = s_scr[...]
    ks = _pmm(jnp.concatenate([k * (beta * eg), q * eg], axis=1), s)
    rhs_bd = _bd(v * beta - ks[:, :CHUNK])
    vnew = _bmm(jnp.concatenate([t0, t1], axis=2), jnp.concatenate([rhs_bd, rhs_bd], axis=1))
    o = ks[:, CHUNK:] + _pmm(qk * decay, vnew)
    kd = k * jnp.exp(glast - gcol)
    kv = _bmm(jnp.swapaxes(kd, 1, 2), vnew)
    s_scr[...] = s * jnp.exp(glast) + jnp.where(lo, kv[:, :hd], kv[:, hd:])

    o = o * lax.rsqrt(_half_sums(o * o, lo) * (1.0 / hd) + EPS) * nw_ref[...] * _silu(jnp.stack(z_items))
    for i in range(nb):
        for j in range(npair):
            o_ref[i, :, j * pw:(j + 1) * pw] = o[i * npair + j].astype(o_ref.dtype)

    @pl.when(n == pl.num_programs(1) - 1)
    def _():
        for i in range(nb):
            for j in range(npair):
                sp2 = s_scr[i * npair + j]
                sfin_ref[i, 2 * j] = sp2[:, :hd]
                sfin_ref[i, 2 * j + 1] = sp2[:, hd:]


def _deltanet_kernel(u_ref, z_ref, ba_ref, cprev_ref, s0_ref, cw_ref, alog_ref, dtb_ref, nw_ref,
                     o_ref, sfin_ref, xc_scr, s_scr, *, nb):
    n = pl.program_id(1)
    hd = DN_HEAD_DIM
    rep = DN_V_HEADS // DN_QK_HEADS

    @pl.when(n == 0)
    def _():
        for i in range(nb):
            xc_scr[i, 0:8, :] = cprev_ref[i]
            s_scr[i * DN_V_HEADS:(i + 1) * DN_V_HEADS] = s0_ref[i]

    row = lax.broadcasted_iota(jnp.int32, (CHUNK, CHUNK), 0)
    col = lax.broadcasted_iota(jnp.int32, (CHUNK, CHUNK), 1)
    incl = row >= col
    strict = row > col
    lower01 = jnp.where(incl, 1.0, 0.0).astype(BF16)
    eye = jnp.where(row == col, 1.0, 0.0).astype(F32)
    merge_masks = [(row // 2 == col // 2) & strict]
    blk = 2
    while blk < CHUNK:
        merge_masks.append((row // (2 * blk) == col // (2 * blk)) & (row // blk != col // blk) & strict)
        blk *= 2

    qk_rows, k_items, q_items, v_items, z_items = [], [], [], [], []
    col_items, grow_items = [], []
    for i in range(nb):
        xc_scr[i, 8:8 + CHUNK, :] = u_ref[i]
        conv = xc_scr[i, 5:5 + CHUNK, :] * cw_ref[0:1, :]
        for j in range(1, CONV_W):
            conv = conv + xc_scr[i, 5 + j:5 + j + CHUNK, :] * cw_ref[j:j + 1, :]
        tail = xc_scr[i, CHUNK:CHUNK + 8, :]
        xc_scr[i, 0:8, :] = tail
        cu = _silu(conv)

        ba = ba_ref[i]
        beta_all = jax.nn.sigmoid(ba)
        sp = ba + dtb_ref[...]
        sp = jnp.maximum(sp, 0.0) + jnp.log1p(jnp.exp(-jnp.abs(sp)))
        g_all = -jnp.exp(alog_ref[...]) * sp
        gc_all = _dot_exact_lhs(lower01, g_all)
        gc_t = gc_all.T
        beta_t = beta_all.T

        for hq in range(DN_QK_HEADS):
            q = cu[:, hq * hd:(hq + 1) * hd]
            k = cu[:, DN_QK_DIM + hq * hd:DN_QK_DIM + (hq + 1) * hd]
            q = q * lax.rsqrt(jnp.sum(q * q, axis=-1, keepdims=True) + EPS) * (hd ** -0.5)
            k = k * lax.rsqrt(jnp.sum(k * k, axis=-1, keepdims=True) + EPS)
            qk_rows.append(jnp.concatenate([q, k], axis=0))
            for r in range(rep):
                hv = hq * rep + r
                q_items.append(q)
                k_items.append(k)
                v_items.append(cu[:, 2 * DN_QK_DIM + hv * hd:2 * DN_QK_DIM + (hv + 1) * hd])
                z_items.append(z_ref[i, :, hv * hd:(hv + 1) * hd])
                brow = jnp.broadcast_to(beta_t[hv:hv + 1, :], (CHUNK, CHUNK))
                grow = jnp.broadcast_to(gc_t[8 + hv:9 + hv, :], (CHUNK, CHUNK))
                col_items.append(jnp.concatenate([brow, grow], axis=1).T)
                grow_items.append(grow)

    q = jnp.stack(q_items)
    k = jnp.stack(k_items)
    v = jnp.stack(v_items)
    cols = jnp.stack(col_items)
    beta, gcol = cols[:, :CHUNK], cols[:, CHUNK:]
    grow = jnp.stack(grow_items)
    glast = gcol[:, CHUNK - 1:CHUNK, :]

    qkk = _bmm_nt(jnp.stack(qk_rows), jnp.stack(k_items[::rep]))
    qkk = jnp.stack([qkk[j // rep] for j in range(nb * DN_V_HEADS)])
    qk, kk = qkk[:, :CHUNK], qkk[:, CHUNK:]

    decay = jnp.exp(jnp.where(incl, gcol - grow, NEG))
    eg = jnp.exp(gcol)

    m = jnp.where(strict, (kk * beta) * decay, 0.0)
    t = eye - jnp.where(merge_masks[0], m, 0.0)
    for mask in merge_masks[1:]:
        t = t - _bmm(t, _bmm(jnp.where(mask, m, 0.0), t))
    t0 = t.astype(BF16)
    m_hi, m_lo = _split2(m)
    mt0 = _bmm(jnp.concatenate([m_hi, m_lo], axis=2), jnp.concatenate([t0, t0], axis=1))
    resid = eye - t0.astype(F32) - mt0
    t1 = _bmm(t0, resid).astype(BF16)

    s = s_scr[...]
    ks = _bmm(jnp.concatenate([k * (beta * eg), q * eg], axis=1), s)
    rhs = (v * beta - ks[:, :CHUNK]).astype(BF16)
    vnew = _bmm(jnp.concatenate([t0, t1], axis=2), jnp.concatenate([rhs, rhs], axis=1))
    o = ks[:, CHUNK:] + _bmm(qk * decay, vnew)
    kd = k * jnp.exp(glast - gcol)
    s_scr[...] = s * jnp.exp(glast) + _bmm(jnp.swapaxes(kd, 1, 2), vnew)

    o = _rms(o) * nw_ref[...] * _silu(jnp.stack(z_items))
    for i in range(nb):
        for hv in range(DN_V_HEADS):
            o_ref[i, :, hv * hd:(hv + 1) * hd] = o[i * DN_V_HEADS + hv].astype(o_ref.dtype)

    @pl.when(n == pl.num_programs(1) - 1)
    def _():
        for i in range(nb):
            sfin_ref[i] = s_scr[i * DN_V_HEADS:(i + 1) * DN_V_HEADS]


def _deltanet(u, z, ba, cprev8, s0, conv_w, alog_l, dtb_l, dn_norm_w):
    bsz, t_len, _ = u.shape
    nb = DN_SEQS_PER_STEP
    tok = lambda b, n: (b, n, 0)
    per_b = lambda b, n: (b, 0, 0)
    per_b4 = lambda b, n: (b, 0, 0, 0)
    fixed = lambda b, n: (0, 0)
    return pl.pallas_call(
        functools.partial(_deltanet_pair_kernel, nb=nb),
        grid=(bsz // nb, t_len // CHUNK),
        in_specs=[pl.BlockSpec((nb, CHUNK, DN_CONV_DIM), tok),
                  pl.BlockSpec((nb, CHUNK, DN_V_DIM), tok),
                  pl.BlockSpec((nb, CHUNK, GATE_LANES), tok),
                  pl.BlockSpec((nb, 8, DN_CONV_DIM), per_b),
                  pl.BlockSpec((nb, DN_V_HEADS, DN_HEAD_DIM, DN_HEAD_DIM), per_b4),
                  pl.BlockSpec((CONV_W, DN_CONV_DIM), fixed),
                  pl.BlockSpec((1, GATE_LANES), fixed),
                  pl.BlockSpec((1, GATE_LANES), fixed),
                  pl.BlockSpec((1, 2 * DN_HEAD_DIM), fixed)],
        out_specs=[pl.BlockSpec((nb, CHUNK, DN_V_DIM), tok),
                   pl.BlockSpec((nb, DN_V_HEADS, DN_HEAD_DIM, DN_HEAD_DIM), per_b4)],
        out_shape=[jax.ShapeDtypeStruct((bsz, t_len, DN_V_DIM), BF16),
                   jax.ShapeDtypeStruct((bsz, DN_V_HEADS, DN_HEAD_DIM, DN_HEAD_DIM), F32)],
        scratch_shapes=[pltpu.VMEM((nb, CHUNK + 8, DN_CONV_DIM), F32),
                        pltpu.VMEM((nb * DN_QK_HEADS, DN_HEAD_DIM, 2 * DN_HEAD_DIM), F32)],
        compiler_params=_params("parallel", "arbitrary"),
        name="deltanet",
    )(u, z, ba, cprev8, s0, conv_w, alog_l, dtb_l, dn_norm_w)


def _bias_kernel(bucket_ref, table_ref, o_ref):
    bucket = bucket_ref[...]
    for h in range(SWA_HEADS):
        acc = jnp.zeros(bucket.shape, F32)
        for b in range(NUM_BUCKETS):
            acc = jnp.where(bucket == b, table_ref[b, h], acc)
        o_ref[h] = acc


def _rel_bias(bucket, table):
    return pl.pallas_call(
        _bias_kernel,
        in_specs=[pl.BlockSpec(memory_space=pltpu.VMEM),
                  pl.BlockSpec(memory_space=pltpu.SMEM)],
        out_specs=pl.BlockSpec(memory_space=pltpu.VMEM),
        out_shape=jax.ShapeDtypeStruct((SWA_HEADS, CHUNK, BAND), F32),
        name="rel_bias",
    )(bucket, table)


def _swa_kernel(q_ref, *refs, hist, cps, n_units):
    k_refs, v_refs = refs[:n_units], refs[n_units:2 * n_units]
    bias_ref, sink_ref, o_ref = refs[2 * n_units:]
    n = pl.program_id(1)
    hd = SWA_HEAD_DIM
    kb = jnp.concatenate([r[0] for r in k_refs], axis=0)
    vb = jnp.concatenate([r[0] for r in v_refs], axis=0)
    row0 = (n * cps + hist) * CHUNK - WINDOW
    key = lax.broadcasted_iota(jnp.int32, (1, 1, BAND), 2)
    q_items, k_items, v_items, valid = [], [], [], []
    for c in range(cps):
        for kv in range(SWA_KV_HEADS):
            q_items.append(jnp.concatenate(
                [q_ref[0, c * CHUNK:(c + 1) * CHUNK, (kv * SWA_GROUP + g) * hd:(kv * SWA_GROUP + g + 1) * hd]
                 for g in range(SWA_GROUP)], axis=0))
            k_items.append(kb[c * CHUNK:c * CHUNK + BAND, kv * hd:(kv + 1) * hd])
            v_items.append(vb[c * CHUNK:c * CHUNK + BAND, kv * hd:(kv + 1) * hd])
            valid.append(row0 + c * CHUNK + key >= 0)
    s = _bmm_nt(jnp.stack(q_items), jnp.stack(k_items)) * (hd ** -0.5) + bias_ref[...]
    s = jnp.where(jnp.concatenate(valid, axis=0), s, NEG)
    sink = sink_ref[...]
    mx = jnp.maximum(jnp.max(s, axis=-1, keepdims=True), sink)
    p = jnp.exp(s - mx).astype(BF16)
    den = _bmm(p, jnp.ones((len(v_items), BAND, hd), BF16)) + jnp.exp(sink - mx)
    o = _bmm(p, jnp.stack(v_items)) / den
    for c in range(cps):
        for kv in range(SWA_KV_HEADS):
            for g in range(SWA_GROUP):
                h = kv * SWA_GROUP + g
                o_ref[0, c * CHUNK:(c + 1) * CHUNK, h * hd:(h + 1) * hd] = (
                    o[c * SWA_KV_HEADS + kv, g * CHUNK:(g + 1) * CHUNK].astype(o_ref.dtype))


def _swa(q, k_all, v_all, bias, sink_rows, hist):
    bsz, t_len, _ = q.shape
    cps = min(SWA_CHUNKS_PER_STEP, t_len // CHUNK)
    unit = WINDOW if cps * CHUNK % WINDOW == 0 else CHUNK
    assert (hist * CHUNK - WINDOW) % unit == 0 and (cps * CHUNK) % unit == 0
    n_units = (WINDOW + cps * CHUNK) // unit
    q_units = cps * CHUNK // unit
    unit0 = (hist * CHUNK - WINDOW) // unit
    tok = lambda b, n: (b, n, 0)

    def band(j):
        return lambda b, n: (b, jnp.maximum(n * q_units + unit0 + j, 0), 0)

    kv_specs = [pl.BlockSpec((1, unit, SWA_KV_DIM), band(j)) for j in range(n_units)]
    fixed3 = lambda b, n: (0, 0, 0)
    return pl.pallas_call(
        functools.partial(_swa_kernel, hist=hist, cps=cps, n_units=n_units),
        grid=(bsz, t_len // (cps * CHUNK)),
        in_specs=[pl.BlockSpec((1, cps * CHUNK, SWA_Q_DIM), tok)] + kv_specs + kv_specs
                 + [pl.BlockSpec((cps * SWA_KV_HEADS, SWA_GROUP * CHUNK, BAND), fixed3),
                    pl.BlockSpec((cps * SWA_KV_HEADS, SWA_GROUP * CHUNK, 1), fixed3)],
        out_specs=pl.BlockSpec((1, cps * CHUNK, SWA_Q_DIM), tok),
        out_shape=jax.ShapeDtypeStruct((bsz, t_len, SWA_Q_DIM), BF16),
        compiler_params=_params("parallel", "arbitrary"),
        name="swa",
    )(q, *([k_all] * n_units), *([v_all] * n_units),
      jnp.tile(bias, (cps, 1, 1)), jnp.tile(sink_rows, (cps, 1, 1)))


def _outproj_kernel(odn_ref, oswa_ref, x_ref, g1_ref, sc_ref, sh_ref, wo1_ref, wo2_ref, bo_ref,
                    nw_ref, wr_ref, br_ref, cnt0_ref, x1_ref, h2_ref, ti_ref, tg_ref, rk_ref, cnt_ref, cnt_scr):
    first = (pl.program_id(0) == 0) & (pl.program_id(1) == 0)

    @pl.when(first)
    def _():
        cnt_scr[...] = cnt0_ref[...]

    mix = _dot(odn_ref[0].astype(BF16), wo1_ref[...]) + _dot(oswa_ref[0].astype(BF16), wo2_ref[...])
    x1 = x_ref[0] + g1_ref[0] * (mix + bo_ref[...])
    x1_ref[0] = x1
    h2 = _rms(x1) * nw_ref[...]
    h2 = h2 * (1.0 + sc_ref[0]) + sh_ref[0]
    h2_ref[0] = _pack_bf16_pairs(h2)

    logits = _dot_x3(h2, wr_ref[...]) + br_ref[...]
    lane = lax.broadcasted_iota(jnp.int32, logits.shape, 1)
    lane_f = lane.astype(F32)
    vals, idxs = [], []
    for _ in range(TOP_K):
        m = jnp.max(logits, axis=-1, keepdims=True)
        i = jnp.min(jnp.where(logits == m, lane_f, float(LANES)), axis=-1, keepdims=True)
        vals.append(m)
        idxs.append(i)
        logits = jnp.where(lane_f == i, -jnp.inf, logits)
    es = [jnp.exp(v - vals[0]) for v in vals]
    den = es[0] + es[1] + es[2] + es[3]
    tm = lane.shape[0]
    onehot = jnp.zeros(lane.shape, F32)
    for kk in range(TOP_K):
        onehot = jnp.where(lane_f == idxs[kk], 1.0, onehot)
    r_i = lax.broadcasted_iota(jnp.int32, (tm, tm), 0)
    c_i = lax.broadcasted_iota(jnp.int32, (tm, tm), 1)
    before = _dot(jnp.where(r_i > c_i, 1.0, 0.0).astype(BF16), onehot.astype(BF16)) + cnt_scr[...]
    cnt_scr[...] = cnt_scr[...] + jnp.sum(onehot, axis=0, keepdims=True)

    ti = jnp.zeros(lane.shape, jnp.int32)
    tg = jnp.zeros(lane.shape, F32)
    rk = jnp.zeros(lane.shape, jnp.int32)
    for kk in range(TOP_K):
        rank = jnp.sum(jnp.where(lane_f == idxs[kk], before, 0.0), axis=-1, keepdims=True)
        ti = jnp.where(lane == kk, idxs[kk].astype(jnp.int32), ti)
        tg = jnp.where(lane == kk, es[kk] / den, tg)
        rk = jnp.where(lane == kk, rank.astype(jnp.int32), rk)
    ti_ref[0] = ti
    tg_ref[0] = tg
    rk_ref[0] = rk
    cnt_ref[...] = cnt_scr[...]


def _outproj(odn, oswa, x, g1, sc, sh, wo1, wo2, bo, norm_w, wr, br, cnt0, tm, b0, bsz):
    t_len = x.shape[1]
    tok_out = lambda b, i: (b, i, 0)
    tok = lambda b, i: (b0 + b, i, 0)
    per_b = lambda b, i: (b0 + b, 0, 0)
    fixed = lambda b, i: (0, 0)
    return pl.pallas_call(
        _outproj_kernel,
        grid=(bsz, t_len // tm),
        in_specs=[pl.BlockSpec((1, tm, DN_V_DIM), tok),
                  pl.BlockSpec((1, tm, SWA_Q_DIM), tok),
                  pl.BlockSpec((1, tm, D_MODEL), tok),
                  pl.BlockSpec((1, 1, D_MODEL), per_b),
                  pl.BlockSpec((1, 1, D_MODEL), per_b),
                  pl.BlockSpec((1, 1, D_MODEL), per_b),
                  pl.BlockSpec((DN_V_DIM, D_MODEL), fixed),
                  pl.BlockSpec((SWA_Q_DIM, D_MODEL), fixed),
                  pl.BlockSpec((1, D_MODEL), fixed),
                  pl.BlockSpec((1, D_MODEL), fixed),
                  pl.BlockSpec((D_MODEL, LANES), fixed),
                  pl.BlockSpec((1, LANES), fixed),
                  pl.BlockSpec((1, LANES), fixed)],
        out_specs=[pl.BlockSpec((1, tm, D_MODEL), tok_out),
                   pl.BlockSpec((1, tm, D_MODEL // 2), tok_out),
                   pl.BlockSpec((1, tm, LANES), tok_out),
                   pl.BlockSpec((1, tm, LANES), tok_out),
                   pl.BlockSpec((1, tm, LANES), tok_out),
                   pl.BlockSpec((1, LANES), fixed)],
        out_shape=[jax.ShapeDtypeStruct((bsz, t_len, D_MODEL), F32),
                   jax.ShapeDtypeStruct((bsz, t_len, D_MODEL // 2), jnp.uint32),
                   jax.ShapeDtypeStruct((bsz, t_len, LANES), jnp.int32),
                   jax.ShapeDtypeStruct((bsz, t_len, LANES), F32),
                   jax.ShapeDtypeStruct((bsz, t_len, LANES), jnp.int32),
                   jax.ShapeDtypeStruct((1, LANES), F32)],
        scratch_shapes=[pltpu.VMEM((1, LANES), F32)],
        compiler_params=_params("arbitrary", "arbitrary"),
        name="outproj_router",
    )(odn, oswa, x, g1, sc, sh, wo1, wo2, bo, norm_w, wr, br, cnt0)


def _moe_kernel(be_ref, nv_ref, xs_ref, w1_ref, b1_ref, w2_ref, b2_ref, y_ref, w1b_scr, w2b_scr):
    i = pl.program_id(0)
    nv = nv_ref[i]

    @pl.when((i == 0) | (be_ref[i] != be_ref[jnp.maximum(i - 1, 0)]))
    def _():
        w1b_scr[...] = w1_ref[0].astype(BF16)
        w2b_scr[...] = w2_ref[0].astype(BF16)

    @pl.when(nv == 0)
    def _():
        y_ref[...] = jnp.zeros(y_ref.shape, y_ref.dtype)

    @pl.when(nv > 0)
    def _():
        rows = lax.broadcasted_iota(jnp.int32, (MOE_ROWS, 1), 0)
        xb = jnp.where(rows < nv, _unpack_bf16_pairs(xs_ref[...]), 0.0).astype(BF16)
        up = _dot(xb, w1b_scr[...]) + b1_ref[0]
        glu = jnp.minimum(up[:, :D_FF], SWIGLU_LIMIT)
        lin = jnp.clip(up[:, D_FF:], -SWIGLU_LIMIT, SWIGLU_LIMIT)
        act = glu * jax.nn.sigmoid(SWIGLU_ALPHA * glu) * (lin + 1.0)
        y_ref[...] = _pack_bf16_pairs(_dot(act.astype(BF16), w2b_scr[...]) + b2_ref[0])


def _moe_experts(blk_expert, blk_valid, xs, w1, b1, w2, b2):
    n_rows = xs.shape[0]
    n_blocks = n_rows // MOE_ROWS
    half = D_MODEL // 2
    grid_spec = pltpu.PrefetchScalarGridSpec(
        num_scalar_prefetch=2,
        grid=(n_blocks,),
        in_specs=[pl.BlockSpec((MOE_ROWS, half), lambda i, be, nv: (i, 0)),
                  pl.BlockSpec((1, D_MODEL, 2 * D_FF), lambda i, be, nv: (be[i], 0, 0)),
                  pl.BlockSpec((1, 1, 2 * D_FF), lambda i, be, nv: (be[i], 0, 0)),
                  pl.BlockSpec((1, D_FF, D_MODEL), lambda i, be, nv: (be[i], 0, 0)),
                  pl.BlockSpec((1, 1, D_MODEL), lambda i, be, nv: (be[i], 0, 0))],
        out_specs=pl.BlockSpec((MOE_ROWS, half), lambda i, be, nv: (i, 0)),
        scratch_shapes=[pltpu.VMEM((D_MODEL, 2 * D_FF), BF16), pltpu.VMEM((D_FF, D_MODEL), BF16)],
    )
    return pl.pallas_call(
        _moe_kernel,
        grid_spec=grid_spec,
        out_shape=jax.ShapeDtypeStruct((n_rows, half), jnp.uint32),
        compiler_params=_params("arbitrary"),
        name="moe_experts",
    )(blk_expert, blk_valid, xs, w1, b1, w2, b2)


def _sc_mesh():
    return plsc.VectorSubcoreMesh(core_axis_name="core", subcore_axis_name="subcore")


def _gather_rows(x, idx):
    m = idx.shape[0]
    w = x.shape[1]

    @pl.kernel(out_type=jax.ShapeDtypeStruct((m, w), x.dtype), mesh=_sc_mesh())
    def gather_kernel(x_hbm, i_hbm, o_hbm):
        def body(i_vmem, o_vmem):
            pltpu.sync_copy(x_hbm.at[i_vmem.at[0]], o_vmem)

        pltpu.emit_pipeline(
            body,
            grid=(m // SC_WINDOW,),
            in_specs=[pl.BlockSpec((1, SC_WINDOW), lambda i: (i, 0))],
            out_specs=[pl.BlockSpec((SC_WINDOW, w), lambda i: (i, 0))],
            core_axis_name=("core", "subcore"),
            dimension_semantics=(pltpu.PARALLEL,),
        )(i_hbm, o_hbm)

    return gather_kernel(x, idx.reshape(m // SC_WINDOW, SC_WINDOW))


def _scatter_rows(x, idx, n_out):
    n, w = x.shape
    kk = idx.shape[1]
    idx3 = jnp.transpose(idx.reshape(n // SC_WINDOW, SC_WINDOW, kk), (0, 2, 1))

    @pl.kernel(out_type=jax.ShapeDtypeStruct((n_out, w), x.dtype), mesh=_sc_mesh())
    def scatter_kernel(x_hbm, i_hbm, o_hbm):
        def body(x_vmem, i_vmem):
            for k in range(kk):
                pltpu.sync_copy(x_vmem, o_hbm.at[i_vmem.at[0, k]])

        pltpu.emit_pipeline(
            body,
            grid=(n // SC_WINDOW,),
            in_specs=[pl.BlockSpec((SC_WINDOW, w), lambda i: (i, 0)),
                      pl.BlockSpec((1, kk, SC_WINDOW), lambda i: (i, 0, 0))],
            out_specs=[],
            core_axis_name=("core", "subcore"),
            dimension_semantics=(pltpu.PARALLEL,),
        )(x_hbm, i_hbm)

    return scatter_kernel(x, idx3)


def _final_kernel(x1_ref, yg_ref, tg_ref, g2_ref, nw_ref, *rest):
    y_ref = rest[-1]
    tg = tg_ref[0]
    moe = _unpack_bf16_pairs(yg_ref[0]) * tg[:, 0:1]
    for kk in range(1, TOP_K):
        moe = moe + _unpack_bf16_pairs(yg_ref[kk]) * tg[:, kk:kk + 1]
    x2 = x1_ref[0] + g2_ref[0] * moe
    y_ref[0] = _rms(x2) * nw_ref[...]


def _final(x1, yg, tg, g2, norm_w, tm, tok_offset, b0, bsz_total, prev=None):
    bsz, t_len, _ = x1.shape
    tok = lambda b, i: (b, i, 0)
    steps = t_len // tm
    blk0 = tok_offset // tm
    in_specs = [pl.BlockSpec((1, tm, D_MODEL), tok),
                pl.BlockSpec((TOP_K, tm, D_MODEL // 2), lambda b, i: (0, blk0 + b * steps + i, 0)),
                pl.BlockSpec((1, tm, LANES), tok),
                pl.BlockSpec((1, 1, D_MODEL), lambda b, i: (b0 + b, 0, 0)),
                pl.BlockSpec((1, D_MODEL), lambda b, i: (0, 0))]
    args = [x1, yg, tg, g2, norm_w]
    aliases = {}
    if prev is not None:
        in_specs.append(pl.BlockSpec(memory_space=pl.ANY))
        aliases = {len(args): 0}
        args.append(prev)
    return pl.pallas_call(
        _final_kernel,
        grid=(bsz, steps),
        in_specs=in_specs,
        out_specs=pl.BlockSpec((1, tm, D_MODEL), lambda b, i: (b0 + b, i, 0)),
        out_shape=jax.ShapeDtypeStruct((bsz_total, t_len, D_MODEL), F32),
        input_output_aliases=aliases,
        compiler_params=_params("parallel", "arbitrary"),
        name="combine_final",
    )(*args)


def _t5_bucket(rel):
    half = NUM_BUCKETS // 2
    max_exact = half // 2
    ret = jnp.where(rel > 0, half, 0)
    n = jnp.abs(rel)
    nf = jnp.maximum(n, 1).astype(jnp.float32)
    large = max_exact + (jnp.log(nf / max_exact) / math.log(MAX_DISTANCE / max_exact)
                         * (half - max_exact)).astype(jnp.int32)
    large = jnp.minimum(large, half - 1)
    return ret + jnp.where(n < max_exact, n, large)


def _route(top_i, rank, sizes):
    n_tok = top_i.shape[0]
    n_asg = n_tok * TOP_K
    padded = (sizes + MOE_ROWS - 1) // MOE_ROWS * MOE_ROWS
    pend = jnp.cumsum(padded)
    pstart = pend - padded
    n_blocks = -(-n_asg // MOE_ROWS) + N_EXPERTS
    blk_row0 = jnp.arange(n_blocks, dtype=jnp.int32) * MOE_ROWS
    blk_expert = jnp.minimum(jnp.sum(pend[None, :] <= blk_row0[:, None], axis=1), N_EXPERTS - 1).astype(jnp.int32)
    blk_valid = jnp.clip(pstart[blk_expert] + sizes[blk_expert] - blk_row0, 0, MOE_ROWS).astype(jnp.int32)
    onehot = top_i[:, :, None] == jnp.arange(N_EXPERTS, dtype=jnp.int32)
    pos = jnp.sum(jnp.where(onehot, pstart, 0), axis=-1) + rank
    return pos, blk_expert, blk_valid, n_blocks * MOE_ROWS


def kernel(x_prompt, x_sample, c_prompt, c_sample, state_conv, state_delta, cache_swa_k, cache_swa_v,
           w_ada, b_ada, norm_mix, w_in, b_in, conv_w, a_log, dt_bias, dn_norm_w, sinks, rel_bias,
           w_out, b_out, norm_ffn, w_router, b_router, w1, b1, w2, b2, norm_final):
    depth = w_ada.shape[0]
    assert depth == 1, "the final norm is fused into the layer's combine step"
    bp, tp, _ = x_prompt.shape
    bs, ts, _ = x_sample.shape
    groups = [dict(x=x_prompt, c=c_prompt, hist=0), dict(x=x_sample, c=c_sample, hist=WIN_CHUNKS)]

    q_rel = jnp.arange(CHUNK)
    k_rel = jnp.arange(BAND) - WIN_CHUNKS * CHUNK
    bucket = _t5_bucket(k_rel[None, :] - q_rel[:, None]).astype(jnp.int32)
    bias = _rel_bias(bucket, rel_bias)

    nf_w = norm_final.reshape(1, D_MODEL)
    outs = {g: dict(conv=[], delta=[], k=[], v=[]) for g in range(2)}
    xs_cur = [x_prompt, x_sample]

    for l in range(depth):
        o1 = DN_CONV_DIM + DN_V_DIM
        o2 = o1 + 2 * DN_V_HEADS
        wl, bl = w_in[l], b_in[l]
        w_cat = jnp.concatenate(
            [wl[:, :o1], wl[:, o2:], wl[:, o1:o2], jnp.zeros((D_MODEL, GATE_LANES - 2 * DN_V_HEADS), F32)],
            axis=1).astype(BF16)
        b_cat = jnp.concatenate(
            [bl[:o1], bl[o2:], bl[o1:o2], jnp.zeros((GATE_LANES - 2 * DN_V_HEADS,), F32)]).reshape(1, PROJ_DIM)
        pad8 = jnp.zeros((DN_V_HEADS,), F32)
        padr = jnp.zeros((GATE_LANES - 2 * DN_V_HEADS,), F32)
        alog_l = jnp.concatenate([pad8, a_log[l], padr]).reshape(1, GATE_LANES)
        dtb_l = jnp.concatenate([pad8, dt_bias[l], padr]).reshape(1, GATE_LANES)
        wo1 = w_out[l][:DN_V_DIM].astype(BF16)
        wo2 = w_out[l][DN_V_DIM:].astype(BF16)
        wr = jnp.concatenate([w_router[l], jnp.zeros((D_MODEL, LANES - N_EXPERTS), F32)], axis=1)
        br = jnp.concatenate([b_router[l], jnp.full((LANES - N_EXPERTS,), NEG, F32)]).reshape(1, LANES)
        b1l = b1[l].reshape(N_EXPERTS, 1, 2 * D_FF)
        b2l = b2[l].reshape(N_EXPERTS, 1, D_MODEL)

        mod = _modulation(jnp.concatenate([c_prompt, c_sample], axis=0), w_ada[l], b_ada[l])
        att = []
        for gi, grp in enumerate(groups):
            x = xs_cur[gi]
            bsz, t_len, _ = x.shape
            tm = min(TOKEN_TILE, t_len)
            m = mod[:bp] if gi == 0 else mod[bp:]
            sh1, sc1, g1, sh2, sc2, g2 = [a.reshape(bsz, 1, D_MODEL) for a in jnp.split(m, 6, axis=-1)]
            u, z, sq, sk, sv, ba = _inproj(x, sc1, sh1, norm_mix[l].reshape(1, D_MODEL), w_cat, b_cat, tm)
            if gi == 0:
                cprev = jnp.zeros((bsz, CONV_W - 1, DN_CONV_DIM), F32)
                s0 = jnp.zeros((bsz, DN_V_HEADS, DN_HEAD_DIM, DN_HEAD_DIM), F32)
                k_all, v_all = sk, sv
            else:
                cprev = state_conv[l]
                s0 = state_delta[l]
                k_all = jnp.concatenate([cache_swa_k[l].reshape(bsz, -1, SWA_KV_DIM), sk], axis=1)
                v_all = jnp.concatenate([cache_swa_v[l].reshape(bsz, -1, SWA_KV_DIM), sv], axis=1)
            swa_len = cache_swa_k.shape[2]
            cprev8 = jnp.concatenate([jnp.zeros((bsz, 8 - (CONV_W - 1), DN_CONV_DIM), F32), cprev], axis=1)
            o_dn, s_new = _deltanet(u, z, ba, cprev8, s0, conv_w[l], alog_l, dtb_l,
                                    jnp.tile(dn_norm_w[l], 2).reshape(1, 2 * DN_HEAD_DIM))
            sink_rows = jnp.repeat(sinks[l], CHUNK).reshape(SWA_KV_HEADS, SWA_GROUP * CHUNK, 1)
            o_swa = _swa(sq, k_all, v_all, bias.reshape(SWA_KV_HEADS, SWA_GROUP * CHUNK, BAND), sink_rows,
                         grp["hist"])
            assert t_len >= CONV_W - 1
            conv_new = u[:, t_len - (CONV_W - 1):]
            k_state = k_all[:, k_all.shape[1] - swa_len:].reshape(bsz, swa_len, SWA_KV_HEADS, SWA_HEAD_DIM)
            v_state = v_all[:, v_all.shape[1] - swa_len:].reshape(bsz, swa_len, SWA_KV_HEADS, SWA_HEAD_DIM)
            outs[gi]["conv"].append(conv_new)
            outs[gi]["delta"].append(s_new)
            outs[gi]["k"].append(k_state)
            outs[gi]["v"].append(v_state)
            att.append(dict(x=x, o_dn=o_dn, o_swa=o_swa, g1=g1, sc2=sc2, sh2=sh2, g2=g2, tm=tm,
                            bsz=bsz, t_len=t_len))

        half = bp // 2
        parts = [[(1, 0, bs), (0, 0, half)], [(0, half, bp - half)]] if half else [[(1, 0, bs), (0, 0, bp)]]
        work = []
        for part in parts:
            cnt = jnp.zeros((1, LANES), F32)
            segs = []
            for gi, b0, nseq in part:
                a = att[gi]
                x1, h2, ti, tg, rk, cnt = _outproj(a["o_dn"], a["o_swa"], a["x"], a["g1"], a["sc2"], a["sh2"],
                                                   wo1, wo2, b_out[l].reshape(1, D_MODEL),
                                                   norm_ffn[l].reshape(1, D_MODEL), wr, br, cnt, a["tm"], b0, nseq)
                segs.append(dict(gi=gi, b0=b0, nseq=nseq, x1=x1, h2=h2, ti=ti, tg=tg, rk=rk))
            h2_all = jnp.concatenate([s["h2"].reshape(-1, D_MODEL // 2) for s in segs], axis=0)
            ti_all = jnp.concatenate([s["ti"].reshape(-1, LANES)[:, :TOP_K] for s in segs], axis=0)
            rk_all = jnp.concatenate([s["rk"].reshape(-1, LANES)[:, :TOP_K] for s in segs], axis=0)
            sizes = cnt[0, :N_EXPERTS].astype(jnp.int32)
            pos, blk_expert, blk_valid, n_rows = _route(ti_all, rk_all, sizes)
            xs = _scatter_rows(h2_all, pos, n_rows)
            work.append(dict(segs=segs, pos=pos, blk_expert=blk_expert, blk_valid=blk_valid, xs=xs,
                             n_tok=h2_all.shape[0]))
        for wk in work:
            wk["y"] = _moe_experts(wk["blk_expert"], wk["blk_valid"], wk["xs"], w1[l], b1l, w2[l], b2l)
        for wk in work:
            wk["yg"] = _gather_rows(wk["y"], wk["pos"].T.reshape(-1)).reshape(TOP_K, wk["n_tok"], D_MODEL // 2)
        y_out = [None, None]
        for wk in work:
            off = 0
            for s in wk["segs"]:
                a = att[s["gi"]]
                y_out[s["gi"]] = _final(s["x1"], wk["yg"], s["tg"], a["g2"], nf_w, a["tm"], off, s["b0"],
                                        a["bsz"], prev=y_out[s["gi"]])
                off += s["nseq"] * a["t_len"]
        xs_cur = y_out

    res = [xs_cur[0], xs_cur[1]]
    for gi in range(2):
        for name in ("conv", "delta", "k", "v"):
            res.append(jnp.stack(outs[gi][name]))
    return tuple(res)
```

```python
import functools
import math

import jax
import jax.numpy as jnp
from jax import lax
from jax.experimental import pallas as pl
from jax.experimental.pallas import tpu as pltpu
from jax.experimental.pallas import tpu_sc as plsc

F32 = jnp.float32
BF16 = jnp.bfloat16

D_MODEL = 1024
CHUNK = 64
EPS = 1e-6
DN_QK_HEADS = 4
DN_V_HEADS = 8
DN_HEAD_DIM = 64
DN_QK_DIM = DN_QK_HEADS * DN_HEAD_DIM
DN_V_DIM = DN_V_HEADS * DN_HEAD_DIM
DN_CONV_DIM = 2 * DN_QK_DIM + DN_V_DIM
CONV_W = 4
SWA_HEADS = 8
SWA_KV_HEADS = 2
SWA_HEAD_DIM = 64
SWA_GROUP = SWA_HEADS // SWA_KV_HEADS
SWA_Q_DIM = SWA_HEADS * SWA_HEAD_DIM
SWA_KV_DIM = SWA_KV_HEADS * SWA_HEAD_DIM
WINDOW = 128
WIN_CHUNKS = WINDOW // CHUNK
BAND = (WIN_CHUNKS + 1) * CHUNK
NUM_BUCKETS = 32
MAX_DISTANCE = 128
N_EXPERTS = 32
TOP_K = 4
D_FF = 1024
SWIGLU_ALPHA = 1.702
SWIGLU_LIMIT = 7.0

LANES = 128
GATE_LANES = LANES
PROJ_DIM = DN_CONV_DIM + DN_V_DIM + SWA_Q_DIM + 2 * SWA_KV_DIM + GATE_LANES
MOE_ROWS = 1024
TOKEN_TILE = 512
DN_SEQS_PER_STEP = 4
SWA_CHUNKS_PER_STEP = 4
SC_WINDOW = 64
VMEM_LIMIT = 56 * 1024 * 1024
NEG = -1e30


def _params(*sem):
    return pltpu.CompilerParams(dimension_semantics=sem, vmem_limit_bytes=VMEM_LIMIT)


def _split2(a):
    hi = a.astype(BF16)
    lo = (a - hi.astype(F32)).astype(BF16)
    return hi, lo


def _dot(a, b):
    return jnp.dot(a, b, preferred_element_type=F32)


def _dot_nt(a, b):
    return lax.dot_general(a, b, (((1,), (1,)), ((), ())), preferred_element_type=F32)


def _dot_x3(a, b):
    a1, a2 = _split2(a)
    b1, b2 = _split2(b)
    return _dot(a1, b1) + (_dot(a1, b2) + _dot(a2, b1))


def _dot_exact_lhs(l01, g):
    g1 = g.astype(BF16)
    r = g - g1.astype(F32)
    g2 = r.astype(BF16)
    g3 = (r - g2.astype(F32)).astype(BF16)
    return _dot(l01, g1) + (_dot(l01, g2) + _dot(l01, g3))


def _pack_bf16_pairs(x):
    w = x.shape[1] // 2
    bits = lax.bitcast_convert_type(x.astype(BF16).astype(F32), jnp.uint32)
    return (bits[:, w:] & jnp.uint32(0xFFFF0000)) | (bits[:, :w] >> 16)


def _unpack_bf16_pairs(p):
    lo = lax.bitcast_convert_type(p << 16, F32)
    hi = lax.bitcast_convert_type(p & jnp.uint32(0xFFFF0000), F32)
    return jnp.concatenate([lo, hi], axis=1)


def _silu(x):
    return x * jax.nn.sigmoid(x)


def _rms(x):
    return x * lax.rsqrt(jnp.mean(x * x, axis=-1, keepdims=True) + EPS)


def _mod_kernel(c_ref, w_ref, b_ref, o_ref):
    o_ref[...] = _dot_x3(_silu(c_ref[...]), w_ref[...]) + b_ref[...]


def _modulation(c, w_ada, b_ada):
    nb = c.shape[0]
    n_out = w_ada.shape[1]
    tn = 1024
    return pl.pallas_call(
        _mod_kernel,
        grid=(n_out // tn,),
        in_specs=[pl.BlockSpec((nb, D_MODEL), lambda j: (0, 0)),
                  pl.BlockSpec((D_MODEL, tn), lambda j: (0, j)),
                  pl.BlockSpec((1, tn), lambda j: (0, j))],
        out_specs=pl.BlockSpec((nb, tn), lambda j: (0, j)),
        out_shape=jax.ShapeDtypeStruct((nb, n_out), F32),
        compiler_params=_params("arbitrary"),
        name="modulation",
    )(c, w_ada, b_ada.reshape(1, n_out))


def _inproj_kernel(x_ref, sc_ref, sh_ref, nw_ref, w_ref, b_ref,
                   u_ref, z_ref, sq_ref, sk_ref, sv_ref, ba_ref):
    h = _rms(x_ref[0]) * nw_ref[...]
    h = h * (1.0 + sc_ref[0]) + sh_ref[0]
    p = _dot(h.astype(BF16), w_ref[...]) + b_ref[...]
    o = 0
    for ref in (u_ref, z_ref, sq_ref, sk_ref, sv_ref, ba_ref):
        w = ref.shape[-1]
        ref[0] = p[:, o:o + w].astype(ref.dtype)
        o += w


def _inproj(x, sc, sh, norm_w, w_cat, b_cat, tm):
    bsz, t_len, _ = x.shape
    widths = (DN_CONV_DIM, DN_V_DIM, SWA_Q_DIM, SWA_KV_DIM, SWA_KV_DIM, GATE_LANES)
    tok = lambda b, i: (b, i, 0)
    per_b = lambda b, i: (b, 0, 0)
    fixed = lambda b, i: (0, 0)
    return pl.pallas_call(
        _inproj_kernel,
        grid=(bsz, t_len // tm),
        in_specs=[pl.BlockSpec((1, tm, D_MODEL), tok),
                  pl.BlockSpec((1, 1, D_MODEL), per_b),
                  pl.BlockSpec((1, 1, D_MODEL), per_b),
                  pl.BlockSpec((1, D_MODEL), fixed),
                  pl.BlockSpec((D_MODEL, PROJ_DIM), fixed),
                  pl.BlockSpec((1, PROJ_DIM), fixed)],
        out_specs=[pl.BlockSpec((1, tm, w), tok) for w in widths],
        out_shape=[jax.ShapeDtypeStruct((bsz, t_len, w), BF16 if i == 2 else F32) for i, w in enumerate(widths)],
        compiler_params=_params("parallel", "arbitrary"),
        name="inproj",
    )(x, sc, sh, norm_w, w_cat, b_cat)


def _bmm(a, b):
    return lax.dot_general(a.astype(BF16), b.astype(BF16), (((2,), (1,)), ((0,), (0,))),
                           preferred_element_type=F32)


def _bmm_nt(a, b):
    return lax.dot_general(a.astype(BF16), b.astype(BF16), (((2,), (2,)), ((0,), (0,))),
                           preferred_element_type=F32)


def _bd(x):
    x = x.astype(BF16)
    lo = lax.broadcasted_iota(jnp.int32, x.shape, 2) < x.shape[2] // 2
    zero = jnp.zeros_like(x)
    return jnp.concatenate([jnp.where(lo, x, zero), jnp.where(lo, zero, x)], axis=1)


def _pmm(a, b):
    return _bmm(a, _bd(b))


def _half_sums(x, lo):
    s_lo = jnp.sum(jnp.where(lo, x, 0.0), axis=-1, keepdims=True)
    s_hi = jnp.sum(jnp.where(lo, 0.0, x), axis=-1, keepdims=True)
    return jnp.where(lo, s_lo, s_hi)


def _deltanet_pair_kernel(u_ref, z_ref, ba_ref, cprev_ref, s0_ref, cw_ref, alog_ref, dtb_ref, nw_ref,
                          o_ref, sfin_ref, xc_scr, s_scr, *, nb):
    n = pl.program_id(1)
    hd = DN_HEAD_DIM
    pw = 2 * hd
    npair = DN_QK_HEADS

    @pl.when(n == 0)
    def _():
        for i in range(nb):
            xc_scr[i, 0:8, :] = cprev_ref[i]
            for j in range(npair):
                s_scr[i * npair + j] = jnp.concatenate([s0_ref[i, 2 * j], s0_ref[i, 2 * j + 1]], axis=1)

    row = lax.broadcasted_iota(jnp.int32, (CHUNK, pw), 0)
    col = lax.broadcasted_iota(jnp.int32, (CHUNK, pw), 1) % hd
    lo = lax.broadcasted_iota(jnp.int32, (CHUNK, pw), 1) < hd
    incl = row >= col
    strict = row > col
    r1 = lax.broadcasted_iota(jnp.int32, (CHUNK, CHUNK), 0)
    c1 = lax.broadcasted_iota(jnp.int32, (CHUNK, CHUNK), 1)
    lower01 = jnp.where(r1 >= c1, 1.0, 0.0).astype(BF16)
    eye = jnp.where(row == col, 1.0, 0.0).astype(F32)
    merge_masks = [(row // 2 == col // 2) & strict]
    blk = 2
    while blk < CHUNK:
        merge_masks.append((row // (2 * blk) == col // (2 * blk)) & (row // blk != col // blk) & strict)
        blk *= 2

    qk_lhs, qk_rhs, q_items, k_items, v_items, z_items = [], [], [], [], [], []
    zt_items, grow_items = [], []
    for i in range(nb):
        xc_scr[i, 8:8 + CHUNK, :] = u_ref[i]
        conv = xc_scr[i, 5:5 + CHUNK, :] * cw_ref[0:1, :]
        for j in range(1, CONV_W):
            conv = conv + xc_scr[i, 5 + j:5 + j + CHUNK, :] * cw_ref[j:j + 1, :]
        tail = xc_scr[i, CHUNK:CHUNK + 8, :]
        xc_scr[i, 0:8, :] = tail
        cu = _silu(conv)

        ba = ba_ref[i]
        beta_all = jax.nn.sigmoid(ba)
        sp = ba + dtb_ref[...]
        sp = jnp.maximum(sp, 0.0) + jnp.log1p(jnp.exp(-jnp.abs(sp)))
        g_all = -jnp.exp(alog_ref[...]) * sp
        gc_all = _dot_exact_lhs(lower01, g_all)
        gc_t = gc_all.T
        beta_t = beta_all.T

        for c in range(DN_QK_DIM // pw):
            qc = cu[:, c * pw:(c + 1) * pw]
            kc = cu[:, DN_QK_DIM + c * pw:DN_QK_DIM + (c + 1) * pw]
            qc = qc * lax.rsqrt(_half_sums(qc * qc, lo) + EPS) * (hd ** -0.5)
            kc = kc * lax.rsqrt(_half_sums(kc * kc, lo) + EPS)
            qr = pltpu.roll(qc, hd, axis=1)
            kr = pltpu.roll(kc, hd, axis=1)
            for half in range(2):
                sel = lo if half == 0 else jnp.logical_not(lo)
                q_items.append(jnp.where(sel, qc, qr))
                k_items.append(jnp.where(sel, kc, kr))
                km = jnp.where(sel, kc, 0.0)
                qk_lhs.append(jnp.concatenate([qc, kc], axis=0))
                qk_rhs.append(jnp.concatenate([km, km], axis=0))
        for j in range(npair):
            a, b = 2 * j, 2 * j + 1
            v_items.append(cu[:, 2 * DN_QK_DIM + j * pw:2 * DN_QK_DIM + (j + 1) * pw])
            z_items.append(z_ref[i, :, j * pw:(j + 1) * pw])
            ra = jnp.concatenate([beta_t[a:a + 1, :], gc_t[8 + a:9 + a, :]], axis=1)
            rb = jnp.concatenate([beta_t[b:b + 1, :], gc_t[8 + b:9 + b, :]], axis=1)
            zt_items.append(jnp.concatenate([jnp.broadcast_to(ra, (CHUNK, pw)),
                                             jnp.broadcast_to(rb, (CHUNK, pw))], axis=0).T)
            grow_items.append(jnp.broadcast_to(
                jnp.concatenate([gc_t[8 + a:9 + a, :], gc_t[8 + b:9 + b, :]], axis=1), (CHUNK, pw)))

    q = jnp.stack(q_items)
    k = jnp.stack(k_items)
    v = jnp.stack(v_items)
    cols = jnp.stack(zt_items)
    beta, gcol = cols[:, :CHUNK], cols[:, CHUNK:]
    grow = jnp.stack(grow_items)
    glast = gcol[:, CHUNK - 1:CHUNK, :]

    qkk = _bmm_nt(jnp.stack(qk_lhs), jnp.stack(qk_rhs))
    qk, kk = qkk[:, :CHUNK], qkk[:, CHUNK:]

    decay = jnp.exp(jnp.where(incl, gcol - grow, NEG))
    eg = jnp.exp(gcol)

    m = jnp.where(strict, (kk * beta) * decay, 0.0)
    t = eye - jnp.where(merge_masks[0], m, 0.0)
    for mask in merge_masks[1:]:
        t = t - _pmm(t, _pmm(jnp.where(mask, m, 0.0), t))
    t0 = t.astype(BF16)
    t0_bd = _bd(t0)
    m_hi, m_lo = _split2(m)
    mt0 = _bmm(jnp.concatenate([m_hi, m_lo], axis=2), jnp.concatenate([t0_bd, t0_bd], axis=1))
    resid = eye - t0.astype(F32) - mt0
    t1 = _pmm(t0, resid).astype(BF16)

    s = s_scr[...]
    ks = _pmm(jnp.concatenate([k * (beta * eg), q * eg], axis=1), s)
    rhs_bd = _bd(v * beta - ks[:, :CHUNK])
    vnew = _bmm(jnp.concatenate([t0, t1], axis=2), jnp.concatenate([rhs_bd, rhs_bd], axis=1))
    o = ks[:, CHUNK:] + _pmm(qk * decay, vnew)
    kd = k * jnp.exp(glast - gcol)
    kv = _bmm(jnp.swapaxes(kd, 1, 2), vnew)
    s_scr[...] = s * jnp.exp(glast) + jnp.where(lo, kv[:, :hd], kv[:, hd:])

    o = o * lax.rsqrt(_half_sums(o * o, lo) * (1.0 / hd) + EPS) * nw_ref[...] * _silu(jnp.stack(z_items))
    for i in range(nb):
        for j in range(npair):
            o_ref[i, :, j * pw:(j + 1) * pw] = o[i * npair + j].astype(o_ref.dtype)

    @pl.when(n == pl.num_programs(1) - 1)
    def _():
        for i in range(nb):
            for j in range(npair):
                sp2 = s_scr[i * npair + j]
                sfin_ref[i, 2 * j] = sp2[:, :hd]
                sfin_ref[i, 2 * j + 1] = sp2[:, hd:]


def _deltanet_kernel(u_ref, z_ref, ba_ref, cprev_ref, s0_ref, cw_ref, alog_ref, dtb_ref, nw_ref,
                     o_ref, sfin_ref, xc_scr, s_scr, *, nb):
    n = pl.program_id(1)
    hd = DN_HEAD_DIM
    rep = DN_V_HEADS // DN_QK_HEADS

    @pl.when(n == 0)
    def _():
        for i in range(nb):
            xc_scr[i, 0:8, :] = cprev_ref[i]
            s_scr[i * DN_V_HEADS:(i + 1) * DN_V_HEADS] = s0_ref[i]

    row = lax.broadcasted_iota(jnp.int32, (CHUNK, CHUNK), 0)
    col = lax.broadcasted_iota(jnp.int32, (CHUNK, CHUNK), 1)
    incl = row >= col
    strict = row > col
    lower01 = jnp.where(incl, 1.0, 0.0).astype(BF16)
    eye = jnp.where(row == col, 1.0, 0.0).astype(F32)
    merge_masks = [(row // 2 == col // 2) & strict]
    blk = 2
    while blk < CHUNK:
        merge_masks.append((row // (2 * blk) == col // (2 * blk)) & (row // blk != col // blk) & strict)
        blk *= 2

    qk_rows, k_items, q_items, v_items, z_items = [], [], [], [], []
    col_items, grow_items = [], []
    for i in range(nb):
        xc_scr[i, 8:8 + CHUNK, :] = u_ref[i]
        conv = xc_scr[i, 5:5 + CHUNK, :] * cw_ref[0:1, :]
        for j in range(1, CONV_W):
            conv = conv + xc_scr[i, 5 + j:5 + j + CHUNK, :] * cw_ref[j:j + 1, :]
        tail = xc_scr[i, CHUNK:CHUNK + 8, :]
        xc_scr[i, 0:8, :] = tail
        cu = _silu(conv)

        ba = ba_ref[i]
        beta_all = jax.nn.sigmoid(ba)
        sp = ba + dtb_ref[...]
        sp = jnp.maximum(sp, 0.0) + jnp.log1p(jnp.exp(-jnp.abs(sp)))
        g_all = -jnp.exp(alog_ref[...]) * sp
        gc_all = _dot_exact_lhs(lower01, g_all)
        gc_t = gc_all.T
        beta_t = beta_all.T

        for hq in range(DN_QK_HEADS):
            q = cu[:, hq * hd:(hq + 1) * hd]
            k = cu[:, DN_QK_DIM + hq * hd:DN_QK_DIM + (hq + 1) * hd]
            q = q * lax.rsqrt(jnp.sum(q * q, axis=-1, keepdims=True) + EPS) * (hd ** -0.5)
            k = k * lax.rsqrt(jnp.sum(k * k, axis=-1, keepdims=True) + EPS)
            qk_rows.append(jnp.concatenate([q, k], axis=0))
            for r in range(rep):
                hv = hq * rep + r
                q_items.append(q)
                k_items.append(k)
                v_items.append(cu[:, 2 * DN_QK_DIM + hv * hd:2 * DN_QK_DIM + (hv + 1) * hd])
                z_items.append(z_ref[i, :, hv * hd:(hv + 1) * hd])
                brow = jnp.broadcast_to(beta_t[hv:hv + 1, :], (CHUNK, CHUNK))
                grow = jnp.broadcast_to(gc_t[8 + hv:9 + hv, :], (CHUNK, CHUNK))
                col_items.append(jnp.concatenate([brow, grow], axis=1).T)
                grow_items.append(grow)

    q = jnp.stack(q_items)
    k = jnp.stack(k_items)
    v = jnp.stack(v_items)
    cols = jnp.stack(col_items)
    beta, gcol = cols[:, :CHUNK], cols[:, CHUNK:]
    grow = jnp.stack(grow_items)
    glast = gcol[:, CHUNK - 1:CHUNK, :]

    qkk = _bmm_nt(jnp.stack(qk_rows), jnp.stack(k_items[::rep]))
    qkk = jnp.stack([qkk[j // rep] for j in range(nb * DN_V_HEADS)])
    qk, kk = qkk[:, :CHUNK], qkk[:, CHUNK:]

    decay = jnp.exp(jnp.where(incl, gcol - grow, NEG))
    eg = jnp.exp(gcol)

    m = jnp.where(strict, (kk * beta) * decay, 0.0)
    t = eye - jnp.where(merge_masks[0], m, 0.0)
    for mask in merge_masks[1:]:
        t = t - _bmm(t, _bmm(jnp.where(mask, m, 0.0), t))
    t0 = t.astype(BF16)
    m_hi, m_lo = _split2(m)
    mt0 = _bmm(jnp.concatenate([m_hi, m_lo], axis=2), jnp.concatenate([t0, t0], axis=1))
    resid = eye - t0.astype(F32) - mt0
    t1 = _bmm(t0, resid).astype(BF16)

    s = s_scr[...]
    ks = _bmm(jnp.concatenate([k * (beta * eg), q * eg], axis=1), s)
    rhs = (v * beta - ks[:, :CHUNK]).astype(BF16)
    vnew = _bmm(jnp.concatenate([t0, t1], axis=2), jnp.concatenate([rhs, rhs], axis=1))
    o = ks[:, CHUNK:] + _bmm(qk * decay, vnew)
    kd = k * jnp.exp(glast - gcol)
    s_scr[...] = s * jnp.exp(glast) + _bmm(jnp.swapaxes(kd, 1, 2), vnew)

    o = _rms(o) * nw_ref[...] * _silu(jnp.stack(z_items))
    for i in range(nb):
        for hv in range(DN_V_HEADS):
            o_ref[i, :, hv * hd:(hv + 1) * hd] = o[i * DN_V_HEADS + hv].astype(o_ref.dtype)

    @pl.when(n == pl.num_programs(1) - 1)
    def _():
        for i in range(nb):
            sfin_ref[i] = s_scr[i * DN_V_HEADS:(i + 1) * DN_V_HEADS]


def _deltanet(u, z, ba, cprev8, s0, conv_w, alog_l, dtb_l, dn_norm_w):
    bsz, t_len, _ = u.shape
    nb = DN_SEQS_PER_STEP
    tok = lambda b, n: (b, n, 0)
    per_b = lambda b, n: (b, 0, 0)
    per_b4 = lambda b, n: (b, 0, 0, 0)
    fixed = lambda b, n: (0, 0)
    return pl.pallas_call(
        functools.partial(_deltanet_pair_kernel, nb=nb),
        grid=(bsz // nb, t_len // CHUNK),
        in_specs=[pl.BlockSpec((nb, CHUNK, DN_CONV_DIM), tok),
                  pl.BlockSpec((nb, CHUNK, DN_V_DIM), tok),
                  pl.BlockSpec((nb, CHUNK, GATE_LANES), tok),
                  pl.BlockSpec((nb, 8, DN_CONV_DIM), per_b),
                  pl.BlockSpec((nb, DN_V_HEADS, DN_HEAD_DIM, DN_HEAD_DIM), per_b4),
                  pl.BlockSpec((CONV_W, DN_CONV_DIM), fixed),
                  pl.BlockSpec((1, GATE_LANES), fixed),
                  pl.BlockSpec((1, GATE_LANES), fixed),
                  pl.BlockSpec((1, 2 * DN_HEAD_DIM), fixed)],
        out_specs=[pl.BlockSpec((nb, CHUNK, DN_V_DIM), tok),
                   pl.BlockSpec((nb, DN_V_HEADS, DN_HEAD_DIM, DN_HEAD_DIM), per_b4)],
        out_shape=[jax.ShapeDtypeStruct((bsz, t_len, DN_V_DIM), BF16),
                   jax.ShapeDtypeStruct((bsz, DN_V_HEADS, DN_HEAD_DIM, DN_HEAD_DIM), F32)],
        scratch_shapes=[pltpu.VMEM((nb, CHUNK + 8, DN_CONV_DIM), F32),
                        pltpu.VMEM((nb * DN_QK_HEADS, DN_HEAD_DIM, 2 * DN_HEAD_DIM), F32)],
        compiler_params=_params("parallel", "arbitrary"),
        name="deltanet",
    )(u, z, ba, cprev8, s0, conv_w, alog_l, dtb_l, dn_norm_w)


def _bias_kernel(bucket_ref, table_ref, o_ref):
    bucket = bucket_ref[...]
    for h in range(SWA_HEADS):
        acc = jnp.zeros(bucket.shape, F32)
        for b in range(NUM_BUCKETS):
            acc = jnp.where(bucket == b, table_ref[b, h], acc)
        o_ref[h] = acc


def _rel_bias(bucket, table):
    return pl.pallas_call(
        _bias_kernel,
        in_specs=[pl.BlockSpec(memory_space=pltpu.VMEM),
                  pl.BlockSpec(memory_space=pltpu.SMEM)],
        out_specs=pl.BlockSpec(memory_space=pltpu.VMEM),
        out_shape=jax.ShapeDtypeStruct((SWA_HEADS, CHUNK, BAND), F32),
        name="rel_bias",
    )(bucket, table)


def _swa_kernel(q_ref, *refs, hist, cps, n_units):
    k_refs, v_refs = refs[:n_units], refs[n_units:2 * n_units]
    bias_ref, sink_ref, o_ref = refs[2 * n_units:]
    n = pl.program_id(1)
    hd = SWA_HEAD_DIM
    kb = jnp.concatenate([r[0] for r in k_refs], axis=0)
    vb = jnp.concatenate([r[0] for r in v_refs], axis=0)
    row0 = (n * cps + hist) * CHUNK - WINDOW
    key = lax.broadcasted_iota(jnp.int32, (1, 1, BAND), 2)
    q_items, k_items, v_items, valid = [], [], [], []
    for c in range(cps):
        for kv in range(SWA_KV_HEADS):
            q_items.append(jnp.concatenate(
                [q_ref[0, c * CHUNK:(c + 1) * CHUNK, (kv * SWA_GROUP + g) * hd:(kv * SWA_GROUP + g + 1) * hd]
                 for g in range(SWA_GROUP)], axis=0))
            k_items.append(kb[c * CHUNK:c * CHUNK + BAND, kv * hd:(kv + 1) * hd])
            v_items.append(vb[c * CHUNK:c * CHUNK + BAND, kv * hd:(kv + 1) * hd])
            valid.append(row0 + c * CHUNK + key >= 0)
    s = _bmm_nt(jnp.stack(q_items), jnp.stack(k_items)) * (hd ** -0.5) + bias_ref[...]
    s = jnp.where(jnp.concatenate(valid, axis=0), s, NEG)
    sink = sink_ref[...]
    mx = jnp.maximum(jnp.max(s, axis=-1, keepdims=True), sink)
    p = jnp.exp(s - mx).astype(BF16)
    den = _bmm(p, jnp.ones((len(v_items), BAND, hd), BF16)) + jnp.exp(sink - mx)
    o = _bmm(p, jnp.stack(v_items)) / den
    for c in range(cps):
        for kv in range(SWA_KV_HEADS):
            for g in range(SWA_GROUP):
                h = kv * SWA_GROUP + g
                o_ref[0, c * CHUNK:(c + 1) * CHUNK, h * hd:(h + 1) * hd] = (
                    o[c * SWA_KV_HEADS + kv, g * CHUNK:(g + 1) * CHUNK].astype(o_ref.dtype))


def _swa(q, k_all, v_all, bias, sink_rows, hist):
    bsz, t_len, _ = q.shape
    cps = min(SWA_CHUNKS_PER_STEP, t_len // CHUNK)
    unit = WINDOW if cps * CHUNK % WINDOW == 0 else CHUNK
    assert (hist * CHUNK - WINDOW) % unit == 0 and (cps * CHUNK) % unit == 0
    n_units = (WINDOW + cps * CHUNK) // unit
    q_units = cps * CHUNK // unit
    unit0 = (hist * CHUNK - WINDOW) // unit
    tok = lambda b, n: (b, n, 0)

    def band(j):
        return lambda b, n: (b, jnp.maximum(n * q_units + unit0 + j, 0), 0)

    kv_specs = [pl.BlockSpec((1, unit, SWA_KV_DIM), band(j)) for j in range(n_units)]
    fixed3 = lambda b, n: (0, 0, 0)
    return pl.pallas_call(
        functools.partial(_swa_kernel, hist=hist, cps=cps, n_units=n_units),
        grid=(bsz, t_len // (cps * CHUNK)),
        in_specs=[pl.BlockSpec((1, cps * CHUNK, SWA_Q_DIM), tok)] + kv_specs + kv_specs
                 + [pl.BlockSpec((cps * SWA_KV_HEADS, SWA_GROUP * CHUNK, BAND), fixed3),
                    pl.BlockSpec((cps * SWA_KV_HEADS, SWA_GROUP * CHUNK, 1), fixed3)],
        out_specs=pl.BlockSpec((1, cps * CHUNK, SWA_Q_DIM), tok),
        out_shape=jax.ShapeDtypeStruct((bsz, t_len, SWA_Q_DIM), BF16),
        compiler_params=_params("parallel", "arbitrary"),
        name="swa",
    )(q, *([k_all] * n_units), *([v_all] * n_units),
      jnp.tile(bias, (cps, 1, 1)), jnp.tile(sink_rows, (cps, 1, 1)))


ROUTE_EXPERT, ROUTE_RANK, ROUTE_GATE = 0, TOP_K, 2 * TOP_K


def _outproj_kernel(odn_ref, oswa_ref, x_ref, g1_ref, sc_ref, sh_ref, wo1_ref, wo2_ref, bo_ref,
                    nw_ref, wr_ref, br_ref, cnt0_ref, *rest):
    x1_ref, h2_ref, route_ref, cnt_ref, cnt_scr = rest[-5:]
    first = (pl.program_id(0) == 0) & (pl.program_id(1) == 0)

    @pl.when(first)
    def _():
        cnt_scr[...] = cnt0_ref[...]

    mix = _dot(odn_ref[0].astype(BF16), wo1_ref[...]) + _dot(oswa_ref[0].astype(BF16), wo2_ref[...])
    x1 = x_ref[0] + g1_ref[0] * (mix + bo_ref[...])
    x1_ref[0] = x1
    h2 = _rms(x1) * nw_ref[...]
    h2 = h2 * (1.0 + sc_ref[0]) + sh_ref[0]
    h2_ref[...] = _pack_bf16_pairs(h2)

    logits = _dot_x3(h2, wr_ref[...]) + br_ref[...]
    lane = lax.broadcasted_iota(jnp.int32, logits.shape, 1)
    lane_f = lane.astype(F32)
    vals, idxs = [], []
    for _ in range(TOP_K):
        m = jnp.max(logits, axis=-1, keepdims=True)
        i = jnp.min(jnp.where(logits == m, lane_f, float(LANES)), axis=-1, keepdims=True)
        vals.append(m)
        idxs.append(i)
        logits = jnp.where(lane_f == i, -jnp.inf, logits)
    es = [jnp.exp(v - vals[0]) for v in vals]
    den = es[0] + es[1] + es[2] + es[3]
    tm = lane.shape[0]
    onehot = jnp.zeros(lane.shape, F32)
    for kk in range(TOP_K):
        onehot = jnp.where(lane_f == idxs[kk], 1.0, onehot)
    r_i = lax.broadcasted_iota(jnp.int32, (tm, tm), 0)
    c_i = lax.broadcasted_iota(jnp.int32, (tm, tm), 1)
    before = _dot(jnp.where(r_i > c_i, 1.0, 0.0).astype(BF16), onehot.astype(BF16)) + cnt_scr[...]
    cnt_scr[...] = cnt_scr[...] + jnp.sum(onehot, axis=0, keepdims=True)

    route = jnp.zeros(lane.shape, F32)
    for kk in range(TOP_K):
        rank = jnp.sum(jnp.where(lane_f == idxs[kk], before, 0.0), axis=-1, keepdims=True)
        route = jnp.where(lane == ROUTE_EXPERT + kk, idxs[kk], route)
        route = jnp.where(lane == ROUTE_RANK + kk, rank, route)
        route = jnp.where(lane == ROUTE_GATE + kk, es[kk] / den, route)
    route_ref[0] = route
    cnt_ref[...] = cnt_scr[...]


def _outproj(odn, oswa, x, g1, sc, sh, wo1, wo2, bo, norm_w, wr, br, cnt0, tm, b0, bsz, h2_rows, h2_row0, h2_prev):
    t_len = x.shape[1]
    steps = t_len // tm
    blk0 = h2_row0 // tm
    tok_out = lambda b, i: (b, i, 0)
    tok = lambda b, i: (b0 + b, i, 0)
    per_b = lambda b, i: (b0 + b, 0, 0)
    fixed = lambda b, i: (0, 0)
    in_specs = [pl.BlockSpec((1, tm, DN_V_DIM), tok),
                pl.BlockSpec((1, tm, SWA_Q_DIM), tok),
                pl.BlockSpec((1, tm, D_MODEL), tok),
                pl.BlockSpec((1, 1, D_MODEL), per_b),
                pl.BlockSpec((1, 1, D_MODEL), per_b),
                pl.BlockSpec((1, 1, D_MODEL), per_b),
                pl.BlockSpec((DN_V_DIM, D_MODEL), fixed),
                pl.BlockSpec((SWA_Q_DIM, D_MODEL), fixed),
                pl.BlockSpec((1, D_MODEL), fixed),
                pl.BlockSpec((1, D_MODEL), fixed),
                pl.BlockSpec((D_MODEL, LANES), fixed),
                pl.BlockSpec((1, LANES), fixed),
                pl.BlockSpec((1, LANES), fixed)]
    args = [odn, oswa, x, g1, sc, sh, wo1, wo2, bo, norm_w, wr, br, cnt0]
    aliases = {}
    if h2_prev is not None:
        in_specs.append(pl.BlockSpec(memory_space=pl.ANY))
        aliases = {len(args): 1}
        args.append(h2_prev)
    return pl.pallas_call(
        _outproj_kernel,
        grid=(bsz, steps),
        in_specs=in_specs,
        out_specs=[pl.BlockSpec((1, tm, D_MODEL), tok_out),
                   pl.BlockSpec((tm, D_MODEL // 2), lambda b, i: (blk0 + b * steps + i, 0)),
                   pl.BlockSpec((1, tm, LANES), tok_out),
                   pl.BlockSpec((1, LANES), fixed)],
        out_shape=[jax.ShapeDtypeStruct((bsz, t_len, D_MODEL), F32),
                   jax.ShapeDtypeStruct((h2_rows, D_MODEL // 2), jnp.uint32),
                   jax.ShapeDtypeStruct((bsz, t_len, LANES), F32),
                   jax.ShapeDtypeStruct((1, LANES), F32)],
        scratch_shapes=[pltpu.VMEM((1, LANES), F32)],
        input_output_aliases=aliases,
        compiler_params=_params("arbitrary", "arbitrary"),
        name="outproj_router",
    )(*args)


def _moe_kernel(be_ref, nv_ref, xs_ref, w1_ref, b1_ref, w2_ref, b2_ref, y_ref, w1b_scr, w2b_scr):
    i = pl.program_id(0)
    nv = nv_ref[i]

    @pl.when((i == 0) | (be_ref[i] != be_ref[jnp.maximum(i - 1, 0)]))
    def _():
        w1b_scr[...] = w1_ref[0].astype(BF16)
        w2b_scr[...] = w2_ref[0].astype(BF16)

    def ffn(n_rows):
        rows = lax.broadcasted_iota(jnp.int32, (n_rows, 1), 0)
        xb = jnp.where(rows < nv, _unpack_bf16_pairs(xs_ref[0:n_rows, :]), 0.0).astype(BF16)
        up = _dot(xb, w1b_scr[...]) + b1_ref[0]
        glu = jnp.minimum(up[:, :D_FF], SWIGLU_LIMIT)
        lin = jnp.clip(up[:, D_FF:], -SWIGLU_LIMIT, SWIGLU_LIMIT)
        act = glu * jax.nn.sigmoid(SWIGLU_ALPHA * glu) * (lin + 1.0)
        y_ref[0:n_rows, :] = _pack_bf16_pairs(_dot(act.astype(BF16), w2b_scr[...]) + b2_ref[0])

    half = MOE_ROWS // 2

    @pl.when(nv == 0)
    def _():
        y_ref[...] = jnp.zeros(y_ref.shape, y_ref.dtype)

    @pl.when((nv > 0) & (nv <= half))
    def _():
        ffn(half)
        y_ref[half:, :] = jnp.zeros((MOE_ROWS - half, y_ref.shape[1]), y_ref.dtype)

    @pl.when(nv > half)
    def _():
        ffn(MOE_ROWS)


def _moe_experts(blk_expert, blk_valid, xs, w1, b1, w2, b2):
    n_rows = xs.shape[0]
    n_blocks = n_rows // MOE_ROWS
    half = D_MODEL // 2
    grid_spec = pltpu.PrefetchScalarGridSpec(
        num_scalar_prefetch=2,
        grid=(n_blocks,),
        in_specs=[pl.BlockSpec((MOE_ROWS, half), lambda i, be, nv: (i, 0)),
                  pl.BlockSpec((1, D_MODEL, 2 * D_FF), lambda i, be, nv: (be[i], 0, 0)),
                  pl.BlockSpec((1, 1, 2 * D_FF), lambda i, be, nv: (be[i], 0, 0)),
                  pl.BlockSpec((1, D_FF, D_MODEL), lambda i, be, nv: (be[i], 0, 0)),
                  pl.BlockSpec((1, 1, D_MODEL), lambda i, be, nv: (be[i], 0, 0))],
        out_specs=pl.BlockSpec((MOE_ROWS, half), lambda i, be, nv: (i, 0)),
        scratch_shapes=[pltpu.VMEM((D_MODEL, 2 * D_FF), BF16), pltpu.VMEM((D_FF, D_MODEL), BF16)],
    )
    return pl.pallas_call(
        _moe_kernel,
        grid_spec=grid_spec,
        out_shape=jax.ShapeDtypeStruct((n_rows, half), jnp.uint32),
        compiler_params=_params("arbitrary"),
        name="moe_experts",
    )(blk_expert, blk_valid, xs, w1, b1, w2, b2)


def _sc_mesh():
    return plsc.VectorSubcoreMesh(core_axis_name="core", subcore_axis_name="subcore")


def _gather_rows(x, idx):
    m = idx.shape[0]
    w = x.shape[1]

    @pl.kernel(out_type=jax.ShapeDtypeStruct((m, w), x.dtype), mesh=_sc_mesh())
    def gather_kernel(x_hbm, i_hbm, o_hbm):
        def body(i_vmem, o_vmem):
            pltpu.sync_copy(x_hbm.at[i_vmem.at[0]], o_vmem)

        pltpu.emit_pipeline(
            body,
            grid=(m // SC_WINDOW,),
            in_specs=[pl.BlockSpec((1, SC_WINDOW), lambda i: (i, 0))],
            out_specs=[pl.BlockSpec((SC_WINDOW, w), lambda i: (i, 0))],
            core_axis_name=("core", "subcore"),
            dimension_semantics=(pltpu.PARALLEL,),
        )(i_hbm, o_hbm)

    return gather_kernel(x, idx.reshape(m // SC_WINDOW, SC_WINDOW))


def _scatter_rows(x, idx, n_out):
    n, w = x.shape
    kk = idx.shape[1]
    idx3 = jnp.transpose(idx.reshape(n // SC_WINDOW, SC_WINDOW, kk), (0, 2, 1))

    @pl.kernel(out_type=jax.ShapeDtypeStruct((n_out, w), x.dtype), mesh=_sc_mesh())
    def scatter_kernel(x_hbm, i_hbm, o_hbm):
        def body(x_vmem, i_vmem):
            for k in range(kk):
                pltpu.sync_copy(x_vmem, o_hbm.at[i_vmem.at[0, k]])

        pltpu.emit_pipeline(
            body,
            grid=(n // SC_WINDOW,),
            in_specs=[pl.BlockSpec((SC_WINDOW, w), lambda i: (i, 0)),
                      pl.BlockSpec((1, kk, SC_WINDOW), lambda i: (i, 0, 0))],
            out_specs=[],
            core_axis_name=("core", "subcore"),
            dimension_semantics=(pltpu.PARALLEL,),
        )(x_hbm, i_hbm)

    return scatter_kernel(x, idx3)


def _final_kernel(x1_ref, yg_ref, tg_ref, g2_ref, nw_ref, *rest):
    y_ref = rest[-1]
    tg = tg_ref[0]
    moe = _unpack_bf16_pairs(yg_ref[0]) * tg[:, ROUTE_GATE:ROUTE_GATE + 1]
    for kk in range(1, TOP_K):
        moe = moe + _unpack_bf16_pairs(yg_ref[kk]) * tg[:, ROUTE_GATE + kk:ROUTE_GATE + kk + 1]
    x2 = x1_ref[0] + g2_ref[0] * moe
    y_ref[0] = _rms(x2) * nw_ref[...]


def _final(x1, yg, tg, g2, norm_w, tm, tok_offset, b0, bsz_total, prev=None):
    bsz, t_len, _ = x1.shape
    tok = lambda b, i: (b, i, 0)
    steps = t_len // tm
    blk0 = tok_offset // tm
    in_specs = [pl.BlockSpec((1, tm, D_MODEL), tok),
                pl.BlockSpec((TOP_K, tm, D_MODEL // 2), lambda b, i: (0, blk0 + b * steps + i, 0)),
                pl.BlockSpec((1, tm, LANES), tok),
                pl.BlockSpec((1, 1, D_MODEL), lambda b, i: (b0 + b, 0, 0)),
                pl.BlockSpec((1, D_MODEL), lambda b, i: (0, 0))]
    args = [x1, yg, tg, g2, norm_w]
    aliases = {}
    if prev is not None:
        in_specs.append(pl.BlockSpec(memory_space=pl.ANY))
        aliases = {len(args): 0}
        args.append(prev)
    return pl.pallas_call(
        _final_kernel,
        grid=(bsz, steps),
        in_specs=in_specs,
        out_specs=pl.BlockSpec((1, tm, D_MODEL), lambda b, i: (b0 + b, i, 0)),
        out_shape=jax.ShapeDtypeStruct((bsz_total, t_len, D_MODEL), F32),
        input_output_aliases=aliases,
        compiler_params=_params("parallel", "arbitrary"),
        name="combine_final",
    )(*args)


def _t5_bucket(rel):
    half = NUM_BUCKETS // 2
    max_exact = half // 2
    ret = jnp.where(rel > 0, half, 0)
    n = jnp.abs(rel)
    nf = jnp.maximum(n, 1).astype(jnp.float32)
    large = max_exact + (jnp.log(nf / max_exact) / math.log(MAX_DISTANCE / max_exact)
                         * (half - max_exact)).astype(jnp.int32)
    large = jnp.minimum(large, half - 1)
    return ret + jnp.where(n < max_exact, n, large)


def _route(top_i, rank, sizes):
    n_tok = top_i.shape[0]
    n_asg = n_tok * TOP_K
    padded = (sizes + MOE_ROWS - 1) // MOE_ROWS * MOE_ROWS
    pend = jnp.cumsum(padded)
    pstart = pend - padded
    n_blocks = -(-n_asg // MOE_ROWS) + N_EXPERTS
    blk_row0 = jnp.arange(n_blocks, dtype=jnp.int32) * MOE_ROWS
    blk_expert = jnp.minimum(jnp.sum(pend[None, :] <= blk_row0[:, None], axis=1), N_EXPERTS - 1).astype(jnp.int32)
    blk_valid = jnp.clip(pstart[blk_expert] + sizes[blk_expert] - blk_row0, 0, MOE_ROWS).astype(jnp.int32)
    onehot = top_i[:, :, None] == jnp.arange(N_EXPERTS, dtype=jnp.int32)
    pos = jnp.sum(jnp.where(onehot, pstart, 0), axis=-1) + rank
    return pos, blk_expert, blk_valid, n_blocks * MOE_ROWS


def kernel(x_prompt, x_sample, c_prompt, c_sample, state_conv, state_delta, cache_swa_k, cache_swa_v,
           w_ada, b_ada, norm_mix, w_in, b_in, conv_w, a_log, dt_bias, dn_norm_w, sinks, rel_bias,
           w_out, b_out, norm_ffn, w_router, b_router, w1, b1, w2, b2, norm_final):
    depth = w_ada.shape[0]
    assert depth == 1, "the final norm is fused into the layer's combine step"
    bp, tp, _ = x_prompt.shape
    bs, ts, _ = x_sample.shape
    groups = [dict(x=x_prompt, c=c_prompt, hist=0), dict(x=x_sample, c=c_sample, hist=WIN_CHUNKS)]

    q_rel = jnp.arange(CHUNK)
    k_rel = jnp.arange(BAND) - WIN_CHUNKS * CHUNK
    bucket = _t5_bucket(k_rel[None, :] - q_rel[:, None]).astype(jnp.int32)
    bias = _rel_bias(bucket, rel_bias)

    nf_w = norm_final.reshape(1, D_MODEL)
    outs = {g: dict(conv=[], delta=[], k=[], v=[]) for g in range(2)}
    xs_cur = [x_prompt, x_sample]

    for l in range(depth):
        o1 = DN_CONV_DIM + DN_V_DIM
        o2 = o1 + 2 * DN_V_HEADS
        wl, bl = w_in[l], b_in[l]
        w_cat = jnp.concatenate(
            [wl[:, :o1], wl[:, o2:], wl[:, o1:o2], jnp.zeros((D_MODEL, GATE_LANES - 2 * DN_V_HEADS), F32)],
            axis=1).astype(BF16)
        b_cat = jnp.concatenate(
            [bl[:o1], bl[o2:], bl[o1:o2], jnp.zeros((GATE_LANES - 2 * DN_V_HEADS,), F32)]).reshape(1, PROJ_DIM)
        pad8 = jnp.zeros((DN_V_HEADS,), F32)
        padr = jnp.zeros((GATE_LANES - 2 * DN_V_HEADS,), F32)
        alog_l = jnp.concatenate([pad8, a_log[l], padr]).reshape(1, GATE_LANES)
        dtb_l = jnp.concatenate([pad8, dt_bias[l], padr]).reshape(1, GATE_LANES)
        wo1 = w_out[l][:DN_V_DIM].astype(BF16)
        wo2 = w_out[l][DN_V_DIM:].astype(BF16)
        wr = jnp.concatenate([w_router[l], jnp.zeros((D_MODEL, LANES - N_EXPERTS), F32)], axis=1)
        br = jnp.concatenate([b_router[l], jnp.full((LANES - N_EXPERTS,), NEG, F32)]).reshape(1, LANES)
        b1l = b1[l].reshape(N_EXPERTS, 1, 2 * D_FF)
        b2l = b2[l].reshape(N_EXPERTS, 1, D_MODEL)

        mod = _modulation(jnp.concatenate([c_prompt, c_sample], axis=0), w_ada[l], b_ada[l])
        att = []
        for gi, grp in enumerate(groups):
            x = xs_cur[gi]
            bsz, t_len, _ = x.shape
            tm = min(TOKEN_TILE, t_len)
            m = mod[:bp] if gi == 0 else mod[bp:]
            sh1, sc1, g1, sh2, sc2, g2 = [a.reshape(bsz, 1, D_MODEL) for a in jnp.split(m, 6, axis=-1)]
            u, z, sq, sk, sv, ba = _inproj(x, sc1, sh1, norm_mix[l].reshape(1, D_MODEL), w_cat, b_cat, tm)
            if gi == 0:
                cprev = jnp.zeros((bsz, CONV_W - 1, DN_CONV_DIM), F32)
                s0 = jnp.zeros((bsz, DN_V_HEADS, DN_HEAD_DIM, DN_HEAD_DIM), F32)
                k_all, v_all = sk, sv
            else:
                cprev = state_conv[l]
                s0 = state_delta[l]
                k_all = jnp.concatenate([cache_swa_k[l].reshape(bsz, -1, SWA_KV_DIM), sk], axis=1)
                v_all = jnp.concatenate([cache_swa_v[l].reshape(bsz, -1, SWA_KV_DIM), sv], axis=1)
            swa_len = cache_swa_k.shape[2]
            cprev8 = jnp.concatenate([jnp.zeros((bsz, 8 - (CONV_W - 1), DN_CONV_DIM), F32), cprev], axis=1)
            o_dn, s_new = _deltanet(u, z, ba, cprev8, s0, conv_w[l], alog_l, dtb_l,
                                    jnp.tile(dn_norm_w[l], 2).reshape(1, 2 * DN_HEAD_DIM))
            sink_rows = jnp.repeat(sinks[l], CHUNK).reshape(SWA_KV_HEADS, SWA_GROUP * CHUNK, 1)
            o_swa = _swa(sq, k_all, v_all, bias.reshape(SWA_KV_HEADS, SWA_GROUP * CHUNK, BAND), sink_rows,
                         grp["hist"])
            assert t_len >= CONV_W - 1
            conv_new = u[:, t_len - (CONV_W - 1):]
            k_state = k_all[:, k_all.shape[1] - swa_len:].reshape(bsz, swa_len, SWA_KV_HEADS, SWA_HEAD_DIM)
            v_state = v_all[:, v_all.shape[1] - swa_len:].reshape(bsz, swa_len, SWA_KV_HEADS, SWA_HEAD_DIM)
            outs[gi]["conv"].append(conv_new)
            outs[gi]["delta"].append(s_new)
            outs[gi]["k"].append(k_state)
            outs[gi]["v"].append(v_state)
            att.append(dict(x=x, o_dn=o_dn, o_swa=o_swa, g1=g1, sc2=sc2, sh2=sh2, g2=g2, tm=tm,
                            bsz=bsz, t_len=t_len))

        half = bp // 2
        parts = [[(1, 0, bs), (0, 0, half)], [(0, half, bp - half)]] if half else [[(1, 0, bs), (0, 0, bp)]]
        work = []
        for part in parts:
            cnt = jnp.zeros((1, LANES), F32)
            n_part = sum(nseq * att[gi]["t_len"] for gi, _, nseq in part)
            h2_all = jnp.zeros((n_part, D_MODEL // 2), jnp.uint32) if len(part) > 1 else None
            segs, row0 = [], 0
            for gi, b0, nseq in part:
                a = att[gi]
                x1, h2_all, route, cnt = _outproj(a["o_dn"], a["o_swa"], a["x"], a["g1"], a["sc2"], a["sh2"],
                                                  wo1, wo2, b_out[l].reshape(1, D_MODEL),
                                                  norm_ffn[l].reshape(1, D_MODEL), wr, br, cnt, a["tm"], b0, nseq,
                                                  n_part, row0, h2_all)
                segs.append(dict(gi=gi, b0=b0, nseq=nseq, x1=x1, tg=route))
                row0 += nseq * a["t_len"]
            route_all = jnp.concatenate(
                [s["tg"].reshape(-1, LANES)[:, :ROUTE_GATE] for s in segs], axis=0).astype(jnp.int32)
            ti_all = route_all[:, ROUTE_EXPERT:ROUTE_EXPERT + TOP_K]
            rk_all = route_all[:, ROUTE_RANK:ROUTE_RANK + TOP_K]
            sizes = cnt[0, :N_EXPERTS].astype(jnp.int32)
            pos, blk_expert, blk_valid, n_rows = _route(ti_all, rk_all, sizes)
            xs = _scatter_rows(h2_all, pos, n_rows)
            work.append(dict(segs=segs, pos=pos, blk_expert=blk_expert, blk_valid=blk_valid, xs=xs,
                             n_tok=h2_all.shape[0]))
        for wk in work:
            wk["y"] = _moe_experts(wk["blk_expert"], wk["blk_valid"], wk["xs"], w1[l], b1l, w2[l], b2l)
        for wk in work:
            wk["yg"] = _gather_rows(wk["y"], wk["pos"].T.reshape(-1)).reshape(TOP_K, wk["n_tok"], D_MODEL // 2)
        y_out = [None, None]
        for wk in work:
            off = 0
            for s in wk["segs"]:
                a = att[s["gi"]]
                y_out[s["gi"]] = _final(s["x1"], wk["yg"], s["tg"], a["g2"], nf_w, a["tm"], off, s["b0"],
                                        a["bsz"], prev=y_out[s["gi"]])
                off += s["nseq"] * a["t_len"]
        xs_cur = y_out

    res = [xs_cur[0], xs_cur[1]]
    for gi in range(2):
        for name in ("conv", "delta", "k", "v"):
            res.append(jnp.stack(outs[gi][name]))
    return tuple(res)
```

```python
import functools
import math

import jax
import jax.numpy as jnp
from jax import lax
from jax.experimental import pallas as pl
from jax.experimental.pallas import tpu as pltpu
from jax.experimental.pallas import tpu_sc as plsc

F32 = jnp.float32
BF16 = jnp.bfloat16

D_MODEL = 1024
CHUNK = 64
EPS = 1e-6
DN_QK_HEADS = 4
DN_V_HEADS = 8
DN_HEAD_DIM = 64
DN_QK_DIM = DN_QK_HEADS * DN_HEAD_DIM
DN_V_DIM = DN_V_HEADS * DN_HEAD_DIM
DN_CONV_DIM = 2 * DN_QK_DIM + DN_V_DIM
CONV_W = 4
SWA_HEADS = 8
SWA_KV_HEADS = 2
SWA_HEAD_DIM = 64
SWA_GROUP = SWA_HEADS // SWA_KV_HEADS
SWA_Q_DIM = SWA_HEADS * SWA_HEAD_DIM
SWA_KV_DIM = SWA_KV_HEADS * SWA_HEAD_DIM
WINDOW = 128
WIN_CHUNKS = WINDOW // CHUNK
BAND = (WIN_CHUNKS + 1) * CHUNK
NUM_BUCKETS = 32
MAX_DISTANCE = 128
N_EXPERTS = 32
TOP_K = 4
D_FF = 1024
SWIGLU_ALPHA = 1.702
SWIGLU_LIMIT = 7.0

LANES = 128
GATE_LANES = LANES
PROJ_DIM = DN_CONV_DIM + DN_V_DIM + SWA_Q_DIM + 2 * SWA_KV_DIM + GATE_LANES
MOE_ROWS = 1024
MOE_FF_CHUNK = 1024
TOKEN_TILE = 512
DN_SEQS_PER_STEP = 4
SWA_CHUNKS_PER_STEP = 16
SC_WINDOW = 64
VMEM_LIMIT = 56 * 1024 * 1024
NEG = -1e30


def _params(*sem):
    return pltpu.CompilerParams(dimension_semantics=sem, vmem_limit_bytes=VMEM_LIMIT)


def _split2(a):
    hi = a.astype(BF16)
    lo = (a - hi.astype(F32)).astype(BF16)
    return hi, lo


def _dot(a, b):
    return jnp.dot(a, b, preferred_element_type=F32)


def _dot_nt(a, b):
    return lax.dot_general(a, b, (((1,), (1,)), ((), ())), preferred_element_type=F32)


def _dot_x3(a, b):
    a1, a2 = _split2(a)
    b1, b2 = _split2(b)
    return _dot(a1, b1) + (_dot(a1, b2) + _dot(a2, b1))


def _dot_exact_lhs(l01, g):
    g1 = g.astype(BF16)
    r = g - g1.astype(F32)
    g2 = r.astype(BF16)
    g3 = (r - g2.astype(F32)).astype(BF16)
    return _dot(l01, g1) + (_dot(l01, g2) + _dot(l01, g3))


def _pack_bf16_pairs(x):
    w = x.shape[1] // 2
    bits = lax.bitcast_convert_type(x.astype(BF16).astype(F32), jnp.uint32)
    return (bits[:, w:] & jnp.uint32(0xFFFF0000)) | (bits[:, :w] >> 16)


def _unpack_bf16_pairs(p):
    lo = lax.bitcast_convert_type(p << 16, F32)
    hi = lax.bitcast_convert_type(p & jnp.uint32(0xFFFF0000), F32)
    return jnp.concatenate([lo, hi], axis=1)


def _silu(x):
    return x * jax.nn.sigmoid(x)


def _rms(x):
    return x * lax.rsqrt(jnp.mean(x * x, axis=-1, keepdims=True) + EPS)


def _mod_kernel(c_ref, w_ref, b_ref, o_ref):
    o_ref[...] = _dot_x3(_silu(c_ref[...]), w_ref[...]) + b_ref[...]


def _modulation(c, w_ada, b_ada):
    nb = c.shape[0]
    n_out = w_ada.shape[1]
    tn = 1024
    return pl.pallas_call(
        _mod_kernel,
        grid=(n_out // tn,),
        in_specs=[pl.BlockSpec((nb, D_MODEL), lambda j: (0, 0)),
                  pl.BlockSpec((D_MODEL, tn), lambda j: (0, j)),
                  pl.BlockSpec((1, tn), lambda j: (0, j))],
        out_specs=pl.BlockSpec((nb, tn), lambda j: (0, j)),
        out_shape=jax.ShapeDtypeStruct((nb, n_out), F32),
        compiler_params=_params("arbitrary"),
        name="modulation",
    )(c, w_ada, b_ada.reshape(1, n_out))


def _inproj_kernel(x_ref, sc_ref, sh_ref, nw_ref, w_ref, b_ref,
                   cprev_ref, cw_ref, qkv_ref, z_ref, sq_ref, sk_ref, sv_ref, ba_ref, utail_ref, xc_scr):
    tm = x_ref.shape[1]

    @pl.when(pl.program_id(1) == 0)
    def _():
        xc_scr[0:8, :] = cprev_ref[0]

    h = _rms(x_ref[0]) * nw_ref[...]
    h = h * (1.0 + sc_ref[0]) + sh_ref[0]
    p = _dot(h.astype(BF16), w_ref[...]) + b_ref[...]
    o = DN_CONV_DIM
    for ref in (z_ref, sq_ref, sk_ref, sv_ref, ba_ref):
        w = ref.shape[-1]
        ref[0] = p[:, o:o + w].astype(ref.dtype)
        o += w

    xc_scr[8:8 + tm, :] = p[:, :DN_CONV_DIM]
    conv = xc_scr[5:5 + tm, :] * cw_ref[0:1, :]
    for j in range(1, CONV_W):
        conv = conv + xc_scr[5 + j:5 + j + tm, :] * cw_ref[j:j + 1, :]
    tail = xc_scr[tm:tm + 8, :]
    xc_scr[0:8, :] = tail
    utail_ref[0] = tail
    cu = _silu(conv)

    pw = 2 * DN_HEAD_DIM
    lo = lax.broadcasted_iota(jnp.int32, (tm, pw), 1) < DN_HEAD_DIM
    for c in range(DN_QK_DIM // pw):
        qc = cu[:, c * pw:(c + 1) * pw]
        kc = cu[:, DN_QK_DIM + c * pw:DN_QK_DIM + (c + 1) * pw]
        qkv_ref[0, :, c * pw:(c + 1) * pw] = (
            qc * lax.rsqrt(_half_sums(qc * qc, lo) + EPS) * (DN_HEAD_DIM ** -0.5))
        qkv_ref[0, :, DN_QK_DIM + c * pw:DN_QK_DIM + (c + 1) * pw] = kc * lax.rsqrt(_half_sums(kc * kc, lo) + EPS)
    qkv_ref[0, :, 2 * DN_QK_DIM:] = cu[:, 2 * DN_QK_DIM:]


def _inproj(x, sc, sh, norm_w, w_cat, b_cat, cprev8, conv_w, tm):
    bsz, t_len, _ = x.shape
    widths = (DN_CONV_DIM, DN_V_DIM, SWA_Q_DIM, SWA_KV_DIM, SWA_KV_DIM, GATE_LANES)
    tok = lambda b, i: (b, i, 0)
    per_b = lambda b, i: (b, 0, 0)
    fixed = lambda b, i: (0, 0)
    out_shape = [jax.ShapeDtypeStruct((bsz, t_len, w), BF16 if i == 2 else F32) for i, w in enumerate(widths)]
    return pl.pallas_call(
        _inproj_kernel,
        grid=(bsz, t_len // tm),
        in_specs=[pl.BlockSpec((1, tm, D_MODEL), tok),
                  pl.BlockSpec((1, 1, D_MODEL), per_b),
                  pl.BlockSpec((1, 1, D_MODEL), per_b),
                  pl.BlockSpec((1, D_MODEL), fixed),
                  pl.BlockSpec((D_MODEL, PROJ_DIM), fixed),
                  pl.BlockSpec((1, PROJ_DIM), fixed),
                  pl.BlockSpec((1, 8, DN_CONV_DIM), per_b),
                  pl.BlockSpec((CONV_W, DN_CONV_DIM), fixed)],
        out_specs=[pl.BlockSpec((1, tm, w), tok) for w in widths] + [pl.BlockSpec((1, 8, DN_CONV_DIM), per_b)],
        out_shape=out_shape + [jax.ShapeDtypeStruct((bsz, 8, DN_CONV_DIM), F32)],
        scratch_shapes=[pltpu.VMEM((tm + 8, DN_CONV_DIM), F32)],
        compiler_params=_params("parallel", "arbitrary"),
        name="inproj",
    )(x, sc, sh, norm_w, w_cat, b_cat, cprev8, conv_w)


def _bmm(a, b):
    return lax.dot_general(a.astype(BF16), b.astype(BF16), (((2,), (1,)), ((0,), (0,))),
                           preferred_element_type=F32)


def _bmm_nt(a, b):
    return lax.dot_general(a.astype(BF16), b.astype(BF16), (((2,), (2,)), ((0,), (0,))),
                           preferred_element_type=F32)


def _bd(x):
    x = x.astype(BF16)
    lo = lax.broadcasted_iota(jnp.int32, x.shape, 2) < x.shape[2] // 2
    zero = jnp.zeros_like(x)
    return jnp.concatenate([jnp.where(lo, x, zero), jnp.where(lo, zero, x)], axis=1)


def _pmm(a, b):
    return _bmm(a, _bd(b))


def _half_sums(x, lo):
    s_lo = jnp.sum(jnp.where(lo, x, 0.0), axis=-1, keepdims=True)
    s_hi = jnp.sum(jnp.where(lo, 0.0, x), axis=-1, keepdims=True)
    return jnp.where(lo, s_lo, s_hi)


def _deltanet_pair_kernel(u_ref, z_ref, ba_ref, s0_ref, alog_ref, dtb_ref, nw_ref,
                          o_ref, sfin_ref, s_scr, *, nb):
    n = pl.program_id(1)
    hd = DN_HEAD_DIM
    pw = 2 * hd
    npair = DN_QK_HEADS

    @pl.when(n == 0)
    def _():
        for i in range(nb):
            for j in range(npair):
                s_scr[i * npair + j] = jnp.concatenate([s0_ref[i, 2 * j], s0_ref[i, 2 * j + 1]], axis=1)

    row = lax.broadcasted_iota(jnp.int32, (CHUNK, pw), 0)
    col = lax.broadcasted_iota(jnp.int32, (CHUNK, pw), 1) % hd
    lo = lax.broadcasted_iota(jnp.int32, (CHUNK, pw), 1) < hd
    incl = row >= col
    strict = row > col
    r1 = lax.broadcasted_iota(jnp.int32, (CHUNK, CHUNK), 0)
    c1 = lax.broadcasted_iota(jnp.int32, (CHUNK, CHUNK), 1)
    lower01 = jnp.where(r1 >= c1, 1.0, 0.0).astype(BF16)
    eye = jnp.where(row == col, 1.0, 0.0).astype(F32)
    merge_masks = [(row // 2 == col // 2) & strict]
    blk = 2
    while blk < CHUNK:
        merge_masks.append((row // (2 * blk) == col // (2 * blk)) & (row // blk != col // blk) & strict)
        blk *= 2

    qk_lhs, qk_rhs, q_items, k_items, v_items, z_items = [], [], [], [], [], []
    zt_items, grow_items = [], []
    for i in range(nb):
        ba = ba_ref[i]
        beta_all = jax.nn.sigmoid(ba)
        sp = ba + dtb_ref[...]
        sp = jnp.maximum(sp, 0.0) + jnp.log1p(jnp.exp(-jnp.abs(sp)))
        g_all = -jnp.exp(alog_ref[...]) * sp
        gc_all = _dot_exact_lhs(lower01, g_all)
        gc_t = gc_all.T
        beta_t = beta_all.T

        for c in range(DN_QK_DIM // pw):
            qc = u_ref[i, :, c * pw:(c + 1) * pw]
            kc = u_ref[i, :, DN_QK_DIM + c * pw:DN_QK_DIM + (c + 1) * pw]
            qr = pltpu.roll(qc, hd, axis=1)
            kr = pltpu.roll(kc, hd, axis=1)
            for half in range(2):
                sel = lo if half == 0 else jnp.logical_not(lo)
                q_items.append(jnp.where(sel, qc, qr))
                k_items.append(jnp.where(sel, kc, kr))
                km = jnp.where(sel, kc, 0.0)
                qk_lhs.append(jnp.concatenate([qc, kc], axis=0))
                qk_rhs.append(jnp.concatenate([km, km], axis=0))
        for j in range(npair):
            a, b = 2 * j, 2 * j + 1
            v_items.append(u_ref[i, :, 2 * DN_QK_DIM + j * pw:2 * DN_QK_DIM + (j + 1) * pw])
            z_items.append(z_ref[i, :, j * pw:(j + 1) * pw])
            ra = jnp.concatenate([beta_t[a:a + 1, :], gc_t[8 + a:9 + a, :]], axis=1)
            rb = jnp.concatenate([beta_t[b:b + 1, :], gc_t[8 + b:9 + b, :]], axis=1)
            zt_items.append(jnp.concatenate([jnp.broadcast_to(ra, (CHUNK, pw)),
                                             jnp.broadcast_to(rb, (CHUNK, pw))], axis=0).T)
            grow_items.append(jnp.broadcast_to(
                jnp.concatenate([gc_t[8 + a:9 + a, :], gc_t[8 + b:9 + b, :]], axis=1), (CHUNK, pw)))

    q = jnp.stack(q_items)
    k = jnp.stack(k_items)
    v = jnp.stack(v_items)
    cols = jnp.stack(zt_items)
    beta, gcol = cols[:, :CHUNK], cols[:, CHUNK:]
    grow = jnp.stack(grow_items)
    glast = gcol[:, CHUNK - 1:CHUNK, :]

    qkk = _bmm_nt(jnp.stack(qk_lhs), jnp.stack(qk_rhs))
    qk, kk = qkk[:, :CHUNK], qkk[:, CHUNK:]

    decay = jnp.exp(jnp.where(incl, gcol - grow, NEG))
    eg = jnp.exp(gcol)

    m = jnp.where(strict, (kk * beta) * decay, 0.0)
    t = eye - jnp.where(merge_masks[0], m, 0.0)
    for mask in merge_masks[1:]:
        t = t - _pmm(t, _pmm(jnp.where(mask, m, 0.0), t))
    t0 = t.astype(BF16)
    t0_bd = _bd(t0)
    m_hi, m_lo = _split2(m)
    mt0 = _bmm(jnp.concatenate([m_hi, m_lo], axis=2), jnp.concatenate([t0_bd, t0_bd], axis=1))
    resid = eye - t0.astype(F32) - mt0
    t1 = _pmm(t0, resid).astype(BF16)

    s = s_scr[...]
    ks = _pmm(jnp.concatenate([k * (beta * eg), q * eg], axis=1), s)
    rhs_bd = _bd(v * beta - ks[:, :CHUNK])
    vnew = _bmm(jnp.concatenate([t0, t1], axis=2), jnp.concatenate([rhs_bd, rhs_bd], axis=1))
    o = ks[:, CHUNK:] + _pmm(qk * decay, vnew)
    kd = k * jnp.exp(glast - gcol)
    kv = _bmm(jnp.swapaxes(kd, 1, 2), vnew)
    s_scr[...] = s * jnp.exp(glast) + jnp.where(lo, kv[:, :hd], kv[:, hd:])

    o = o * lax.rsqrt(_half_sums(o * o, lo) * (1.0 / hd) + EPS) * nw_ref[...] * _silu(jnp.stack(z_items))
    for i in range(nb):
        for j in range(npair):
            o_ref[i, :, j * pw:(j + 1) * pw] = o[i * npair + j].astype(o_ref.dtype)

    @pl.when(n == pl.num_programs(1) - 1)
    def _():
        for i in range(nb):
            for j in range(npair):
                sp2 = s_scr[i * npair + j]
                sfin_ref[i, 2 * j] = sp2[:, :hd]
                sfin_ref[i, 2 * j + 1] = sp2[:, hd:]


def _deltanet_kernel(u_ref, z_ref, ba_ref, cprev_ref, s0_ref, cw_ref, alog_ref, dtb_ref, nw_ref,
                     o_ref, sfin_ref, xc_scr, s_scr, *, nb):
    n = pl.program_id(1)
    hd = DN_HEAD_DIM
    rep = DN_V_HEADS // DN_QK_HEADS

    @pl.when(n == 0)
    def _():
        for i in range(nb):
            xc_scr[i, 0:8, :] = cprev_ref[i]
            s_scr[i * DN_V_HEADS:(i + 1) * DN_V_HEADS] = s0_ref[i]

    row = lax.broadcasted_iota(jnp.int32, (CHUNK, CHUNK), 0)
    col = lax.broadcasted_iota(jnp.int32, (CHUNK, CHUNK), 1)
    incl = row >= col
    strict = row > col
    lower01 = jnp.where(incl, 1.0, 0.0).astype(BF16)
    eye = jnp.where(row == col, 1.0, 0.0).astype(F32)
    merge_masks = [(row // 2 == col // 2) & strict]
    blk = 2
    while blk < CHUNK:
        merge_masks.append((row // (2 * blk) == col // (2 * blk)) & (row // blk != col // blk) & strict)
        blk *= 2

    qk_rows, k_items, q_items, v_items, z_items = [], [], [], [], []
    col_items, grow_items = [], []
    for i in range(nb):
        xc_scr[i, 8:8 + CHUNK, :] = u_ref[i]
        conv = xc_scr[i, 5:5 + CHUNK, :] * cw_ref[0:1, :]
        for j in range(1, CONV_W):
            conv = conv + xc_scr[i, 5 + j:5 + j + CHUNK, :] * cw_ref[j:j + 1, :]
        tail = xc_scr[i, CHUNK:CHUNK + 8, :]
        xc_scr[i, 0:8, :] = tail
        cu = _silu(conv)

        ba = ba_ref[i]
        beta_all = jax.nn.sigmoid(ba)
        sp = ba + dtb_ref[...]
        sp = jnp.maximum(sp, 0.0) + jnp.log1p(jnp.exp(-jnp.abs(sp)))
        g_all = -jnp.exp(alog_ref[...]) * sp
        gc_all = _dot_exact_lhs(lower01, g_all)
        gc_t = gc_all.T
        beta_t = beta_all.T

        for hq in range(DN_QK_HEADS):
            q = cu[:, hq * hd:(hq + 1) * hd]
            k = cu[:, DN_QK_DIM + hq * hd:DN_QK_DIM + (hq + 1) * hd]
            q = q * lax.rsqrt(jnp.sum(q * q, axis=-1, keepdims=True) + EPS) * (hd ** -0.5)
            k = k * lax.rsqrt(jnp.sum(k * k, axis=-1, keepdims=True) + EPS)
            qk_rows.append(jnp.concatenate([q, k], axis=0))
            for r in range(rep):
                hv = hq * rep + r
                q_items.append(q)
                k_items.append(k)
                v_items.append(cu[:, 2 * DN_QK_DIM + hv * hd:2 * DN_QK_DIM + (hv + 1) * hd])
                z_items.append(z_ref[i, :, hv * hd:(hv + 1) * hd])
                brow = jnp.broadcast_to(beta_t[hv:hv + 1, :], (CHUNK, CHUNK))
                grow = jnp.broadcast_to(gc_t[8 + hv:9 + hv, :], (CHUNK, CHUNK))
                col_items.append(jnp.concatenate([brow, grow], axis=1).T)
                grow_items.append(grow)

    q = jnp.stack(q_items)
    k = jnp.stack(k_items)
    v = jnp.stack(v_items)
    cols = jnp.stack(col_items)
    beta, gcol = cols[:, :CHUNK], cols[:, CHUNK:]
    grow = jnp.stack(grow_items)
    glast = gcol[:, CHUNK - 1:CHUNK, :]

    qkk = _bmm_nt(jnp.stack(qk_rows), jnp.stack(k_items[::rep]))
    qkk = jnp.stack([qkk[j // rep] for j in range(nb * DN_V_HEADS)])
    qk, kk = qkk[:, :CHUNK], qkk[:, CHUNK:]

    decay = jnp.exp(jnp.where(incl, gcol - grow, NEG))
    eg = jnp.exp(gcol)

    m = jnp.where(strict, (kk * beta) * decay, 0.0)
    t = eye - jnp.where(merge_masks[0], m, 0.0)
    for mask in merge_masks[1:]:
        t = t - _bmm(t, _bmm(jnp.where(mask, m, 0.0), t))
    t0 = t.astype(BF16)
    m_hi, m_lo = _split2(m)
    mt0 = _bmm(jnp.concatenate([m_hi, m_lo], axis=2), jnp.concatenate([t0, t0], axis=1))
    resid = eye - t0.astype(F32) - mt0
    t1 = _bmm(t0, resid).astype(BF16)

    s = s_scr[...]
    ks = _bmm(jnp.concatenate([k * (beta * eg), q * eg], axis=1), s)
    rhs = (v * beta - ks[:, :CHUNK]).astype(BF16)
    vnew = _bmm(jnp.concatenate([t0, t1], axis=2), jnp.concatenate([rhs, rhs], axis=1))
    o = ks[:, CHUNK:] + _bmm(qk * decay, vnew)
    kd = k * jnp.exp(glast - gcol)
    s_scr[...] = s * jnp.exp(glast) + _bmm(jnp.swapaxes(kd, 1, 2), vnew)

    o = _rms(o) * nw_ref[...] * _silu(jnp.stack(z_items))
    for i in range(nb):
        for hv in range(DN_V_HEADS):
            o_ref[i, :, hv * hd:(hv + 1) * hd] = o[i * DN_V_HEADS + hv].astype(o_ref.dtype)

    @pl.when(n == pl.num_programs(1) - 1)
    def _():
        for i in range(nb):
            sfin_ref[i] = s_scr[i * DN_V_HEADS:(i + 1) * DN_V_HEADS]


def _deltanet(qkv, z, ba, s0, alog_l, dtb_l, dn_norm_w):
    bsz, t_len, _ = qkv.shape
    nb = DN_SEQS_PER_STEP
    tok = lambda b, n: (b, n, 0)
    per_b4 = lambda b, n: (b, 0, 0, 0)
    fixed = lambda b, n: (0, 0)
    return pl.pallas_call(
        functools.partial(_deltanet_pair_kernel, nb=nb),
        grid=(bsz // nb, t_len // CHUNK),
        in_specs=[pl.BlockSpec((nb, CHUNK, DN_CONV_DIM), tok),
                  pl.BlockSpec((nb, CHUNK, DN_V_DIM), tok),
                  pl.BlockSpec((nb, CHUNK, GATE_LANES), tok),
                  pl.BlockSpec((nb, DN_V_HEADS, DN_HEAD_DIM, DN_HEAD_DIM), per_b4),
                  pl.BlockSpec((1, GATE_LANES), fixed),
                  pl.BlockSpec((1, GATE_LANES), fixed),
                  pl.BlockSpec((1, 2 * DN_HEAD_DIM), fixed)],
        out_specs=[pl.BlockSpec((nb, CHUNK, DN_V_DIM), tok),
                   pl.BlockSpec((nb, DN_V_HEADS, DN_HEAD_DIM, DN_HEAD_DIM), per_b4)],
        out_shape=[jax.ShapeDtypeStruct((bsz, t_len, DN_V_DIM), BF16),
                   jax.ShapeDtypeStruct((bsz, DN_V_HEADS, DN_HEAD_DIM, DN_HEAD_DIM), F32)],
        scratch_shapes=[pltpu.VMEM((nb * DN_QK_HEADS, DN_HEAD_DIM, 2 * DN_HEAD_DIM), F32)],
        compiler_params=_params("parallel", "arbitrary"),
        name="deltanet",
    )(qkv, z, ba, s0, alog_l, dtb_l, dn_norm_w)


def _bias_kernel(bucket_ref, table_ref, o_ref):
    bucket = bucket_ref[...]
    for h in range(SWA_HEADS):
        acc = jnp.zeros(bucket.shape, F32)
        for b in range(NUM_BUCKETS):
            acc = jnp.where(bucket == b, table_ref[b, h], acc)
        o_ref[h] = acc


def _rel_bias(bucket, table):
    return pl.pallas_call(
        _bias_kernel,
        in_specs=[pl.BlockSpec(memory_space=pltpu.VMEM),
                  pl.BlockSpec(memory_space=pltpu.SMEM)],
        out_specs=pl.BlockSpec(memory_space=pltpu.VMEM),
        out_shape=jax.ShapeDtypeStruct((SWA_HEADS, CHUNK, BAND), F32),
        name="rel_bias",
    )(bucket, table)


def _swa_kernel(q_ref, *refs, hist, cps, n_units):
    k_refs, v_refs = refs[:n_units], refs[n_units:2 * n_units]
    bias_ref, sink_ref, o_ref = refs[2 * n_units:]
    n = pl.program_id(1)
    hd = SWA_HEAD_DIM
    kb = jnp.concatenate([r[0] for r in k_refs], axis=0)
    vb = jnp.concatenate([r[0] for r in v_refs], axis=0)
    row0 = (n * cps + hist) * CHUNK - WINDOW
    key = lax.broadcasted_iota(jnp.int32, (1, 1, BAND), 2)
    q_items, k_items, v_items, valid = [], [], [], []
    for c in range(cps):
        for kv in range(SWA_KV_HEADS):
            q_items.append(jnp.concatenate(
                [q_ref[0, c * CHUNK:(c + 1) * CHUNK, (kv * SWA_GROUP + g) * hd:(kv * SWA_GROUP + g + 1) * hd]
                 for g in range(SWA_GROUP)], axis=0))
            k_items.append(kb[c * CHUNK:c * CHUNK + BAND, kv * hd:(kv + 1) * hd])
            v_items.append(vb[c * CHUNK:c * CHUNK + BAND, kv * hd:(kv + 1) * hd])
            valid.append(row0 + c * CHUNK + key >= 0)
    s = _bmm_nt(jnp.stack(q_items), jnp.stack(k_items)) * (hd ** -0.5) + bias_ref[...]
    s = jnp.where(jnp.concatenate(valid, axis=0), s, NEG)
    sink = sink_ref[...]
    mx = jnp.maximum(jnp.max(s, axis=-1, keepdims=True), sink)
    p = jnp.exp(s - mx).astype(BF16)
    den = _bmm(p, jnp.ones((len(v_items), BAND, hd), BF16)) + jnp.exp(sink - mx)
    o = _bmm(p, jnp.stack(v_items)) / den
    for c in range(cps):
        for kv in range(SWA_KV_HEADS):
            for g in range(SWA_GROUP):
                h = kv * SWA_GROUP + g
                o_ref[0, c * CHUNK:(c + 1) * CHUNK, h * hd:(h + 1) * hd] = (
                    o[c * SWA_KV_HEADS + kv, g * CHUNK:(g + 1) * CHUNK].astype(o_ref.dtype))


def _swa(q, k_all, v_all, bias, sink_rows, hist):
    bsz, t_len, _ = q.shape
    cps = min(SWA_CHUNKS_PER_STEP, t_len // CHUNK)
    unit = WINDOW if cps * CHUNK % WINDOW == 0 else CHUNK
    assert (hist * CHUNK - WINDOW) % unit == 0 and (cps * CHUNK) % unit == 0
    n_units = (WINDOW + cps * CHUNK) // unit
    q_units = cps * CHUNK // unit
    unit0 = (hist * CHUNK - WINDOW) // unit
    tok = lambda b, n: (b, n, 0)

    def band(j):
        return lambda b, n: (b, jnp.maximum(n * q_units + unit0 + j, 0), 0)

    kv_specs = [pl.BlockSpec((1, unit, SWA_KV_DIM), band(j)) for j in range(n_units)]
    fixed3 = lambda b, n: (0, 0, 0)
    return pl.pallas_call(
        functools.partial(_swa_kernel, hist=hist, cps=cps, n_units=n_units),
        grid=(bsz, t_len // (cps * CHUNK)),
        in_specs=[pl.BlockSpec((1, cps * CHUNK, SWA_Q_DIM), tok)] + kv_specs + kv_specs
                 + [pl.BlockSpec((cps * SWA_KV_HEADS, SWA_GROUP * CHUNK, BAND), fixed3),
                    pl.BlockSpec((cps * SWA_KV_HEADS, SWA_GROUP * CHUNK, 1), fixed3)],
        out_specs=pl.BlockSpec((1, cps * CHUNK, SWA_Q_DIM), tok),
        out_shape=jax.ShapeDtypeStruct((bsz, t_len, SWA_Q_DIM), BF16),
        compiler_params=_params("parallel", "arbitrary"),
        name="swa",
    )(q, *([k_all] * n_units), *([v_all] * n_units),
      jnp.tile(bias, (cps, 1, 1)), jnp.tile(sink_rows, (cps, 1, 1)))


ROUTE_EXPERT, ROUTE_RANK, ROUTE_GATE = 0, TOP_K, 2 * TOP_K


def _outproj_kernel(odn_ref, oswa_ref, x_ref, g1_ref, sc_ref, sh_ref, wo1_ref, wo2_ref, bo_ref,
                    nw_ref, wr_ref, br_ref, cnt0_ref, *rest):
    x1_ref, h2_ref, route_ref, cnt_ref, cnt_scr = rest[-5:]
    first = (pl.program_id(0) == 0) & (pl.program_id(1) == 0)

    @pl.when(first)
    def _():
        cnt_scr[...] = cnt0_ref[...]

    mix = _dot(odn_ref[0].astype(BF16), wo1_ref[...]) + _dot(oswa_ref[0].astype(BF16), wo2_ref[...])
    x1 = x_ref[0] + g1_ref[0] * (mix + bo_ref[...])
    x1_ref[0] = x1
    h2 = _rms(x1) * nw_ref[...]
    h2 = h2 * (1.0 + sc_ref[0]) + sh_ref[0]
    h2_ref[...] = _pack_bf16_pairs(h2)

    logits = _dot_x3(h2, wr_ref[...]) + br_ref[...]
    lane = lax.broadcasted_iota(jnp.int32, logits.shape, 1)
    lane_f = lane.astype(F32)
    vals, idxs = [], []
    for _ in range(TOP_K):
        m = jnp.max(logits, axis=-1, keepdims=True)
        i = jnp.min(jnp.where(logits == m, lane_f, float(LANES)), axis=-1, keepdims=True)
        vals.append(m)
        idxs.append(i)
        logits = jnp.where(lane_f == i, -jnp.inf, logits)
    es = [jnp.exp(v - vals[0]) for v in vals]
    den = es[0] + es[1] + es[2] + es[3]
    tm = lane.shape[0]
    onehot = jnp.zeros(lane.shape, F32)
    for kk in range(TOP_K):
        onehot = jnp.where(lane_f == idxs[kk], 1.0, onehot)
    r_i = lax.broadcasted_iota(jnp.int32, (tm, tm), 0)
    c_i = lax.broadcasted_iota(jnp.int32, (tm, tm), 1)
    before = _dot(jnp.where(r_i > c_i, 1.0, 0.0).astype(BF16), onehot.astype(BF16)) + cnt_scr[...]
    cnt_scr[...] = cnt_scr[...] + jnp.sum(onehot, axis=0, keepdims=True)

    route = jnp.zeros(lane.shape, F32)
    for kk in range(TOP_K):
        rank = jnp.sum(jnp.where(lane_f == idxs[kk], before, 0.0), axis=-1, keepdims=True)
        route = jnp.where(lane == ROUTE_EXPERT + kk, idxs[kk], route)
        route = jnp.where(lane == ROUTE_RANK + kk, rank, route)
        route = jnp.where(lane == ROUTE_GATE + kk, es[kk] / den, route)
    route_ref[0] = route
    cnt_ref[...] = cnt_scr[...]


def _outproj(odn, oswa, x, g1, sc, sh, wo1, wo2, bo, norm_w, wr, br, cnt0, tm, b0, bsz, h2_rows, h2_row0, h2_prev):
    t_len = x.shape[1]
    steps = t_len // tm
    assert h2_row0 % tm == 0
    blk0 = h2_row0 // tm
    tok_out = lambda b, i: (b, i, 0)
    tok = lambda b, i: (b0 + b, i, 0)
    per_b = lambda b, i: (b0 + b, 0, 0)
    fixed = lambda b, i: (0, 0)
    in_specs = [pl.BlockSpec((1, tm, DN_V_DIM), tok),
                pl.BlockSpec((1, tm, SWA_Q_DIM), tok),
                pl.BlockSpec((1, tm, D_MODEL), tok),
                pl.BlockSpec((1, 1, D_MODEL), per_b),
                pl.BlockSpec((1, 1, D_MODEL), per_b),
                pl.BlockSpec((1, 1, D_MODEL), per_b),
                pl.BlockSpec((DN_V_DIM, D_MODEL), fixed),
                pl.BlockSpec((SWA_Q_DIM, D_MODEL), fixed),
                pl.BlockSpec((1, D_MODEL), fixed),
                pl.BlockSpec((1, D_MODEL), fixed),
                pl.BlockSpec((D_MODEL, LANES), fixed),
                pl.BlockSpec((1, LANES), fixed),
                pl.BlockSpec((1, LANES), fixed)]
    args = [odn, oswa, x, g1, sc, sh, wo1, wo2, bo, norm_w, wr, br, cnt0]
    aliases = {}
    if h2_prev is not None:
        in_specs.append(pl.BlockSpec(memory_space=pl.ANY))
        aliases = {len(args): 1}
        args.append(h2_prev)
    return pl.pallas_call(
        _outproj_kernel,
        grid=(bsz, steps),
        in_specs=in_specs,
        out_specs=[pl.BlockSpec((1, tm, D_MODEL), tok_out),
                   pl.BlockSpec((tm, D_MODEL // 2), lambda b, i: (blk0 + b * steps + i, 0)),
                   pl.BlockSpec((1, tm, LANES), tok_out),
                   pl.BlockSpec((1, LANES), fixed)],
        out_shape=[jax.ShapeDtypeStruct((bsz, t_len, D_MODEL), F32),
                   jax.ShapeDtypeStruct((h2_rows, D_MODEL // 2), jnp.uint32),
                   jax.ShapeDtypeStruct((bsz, t_len, LANES), F32),
                   jax.ShapeDtypeStruct((1, LANES), F32)],
        scratch_shapes=[pltpu.VMEM((1, LANES), F32)],
        input_output_aliases=aliases,
        compiler_params=_params("arbitrary", "arbitrary"),
        name="outproj_router",
    )(*args)


def _moe_kernel(be_ref, nv_ref, xs_ref, w1_ref, b1_ref, w2_ref, b2_ref, y_ref, w1b_scr, w2b_scr):
    i = pl.program_id(0)
    nv = nv_ref[i]

    @pl.when((i == 0) | (be_ref[i] != be_ref[jnp.maximum(i - 1, 0)]))
    def _():
        w1b_scr[...] = w1_ref[0].astype(BF16)
        w2b_scr[...] = w2_ref[0].astype(BF16)

    def ffn(n_rows):
        rows = lax.broadcasted_iota(jnp.int32, (n_rows, 1), 0)
        xb = jnp.where(rows < nv, _unpack_bf16_pairs(xs_ref[0:n_rows, :]), 0.0).astype(BF16)
        y = b2_ref[0]
        for c in range(D_FF // MOE_FF_CHUNK):
            lo, hi = c * MOE_FF_CHUNK, (c + 1) * MOE_FF_CHUNK
            glu = _dot(xb, w1b_scr[:, lo:hi]) + b1_ref[0, :, lo:hi]
            lin = _dot(xb, w1b_scr[:, D_FF + lo:D_FF + hi]) + b1_ref[0, :, D_FF + lo:D_FF + hi]
            glu = jnp.minimum(glu, SWIGLU_LIMIT)
            lin = jnp.clip(lin, -SWIGLU_LIMIT, SWIGLU_LIMIT)
            act = glu * jax.nn.sigmoid(SWIGLU_ALPHA * glu) * (lin + 1.0)
            y = y + _dot(act.astype(BF16), w2b_scr[lo:hi, :])
        y_ref[0:n_rows, :] = _pack_bf16_pairs(y)

    half = MOE_ROWS // 2

    @pl.when(nv == 0)
    def _():
        y_ref[...] = jnp.zeros(y_ref.shape, y_ref.dtype)

    @pl.when((nv > 0) & (nv <= half))
    def _():
        ffn(half)
        y_ref[half:, :] = jnp.zeros((MOE_ROWS - half, y_ref.shape[1]), y_ref.dtype)

    @pl.when(nv > half)
    def _():
        ffn(MOE_ROWS)


def _moe_experts(blk_expert, blk_valid, xs, w1, b1, w2, b2):
    n_rows = xs.shape[0]
    n_blocks = n_rows // MOE_ROWS
    half = D_MODEL // 2
    grid_spec = pltpu.PrefetchScalarGridSpec(
        num_scalar_prefetch=2,
        grid=(n_blocks,),
        in_specs=[pl.BlockSpec((MOE_ROWS, half), lambda i, be, nv: (i, 0)),
                  pl.BlockSpec((1, D_MODEL, 2 * D_FF), lambda i, be, nv: (be[i], 0, 0)),
                  pl.BlockSpec((1, 1, 2 * D_FF), lambda i, be, nv: (be[i], 0, 0)),
                  pl.BlockSpec((1, D_FF, D_MODEL), lambda i, be, nv: (be[i], 0, 0)),
                  pl.BlockSpec((1, 1, D_MODEL), lambda i, be, nv: (be[i], 0, 0))],
        out_specs=pl.BlockSpec((MOE_ROWS, half), lambda i, be, nv: (i, 0)),
        scratch_shapes=[pltpu.VMEM((D_MODEL, 2 * D_FF), BF16), pltpu.VMEM((D_FF, D_MODEL), BF16)],
    )
    return pl.pallas_call(
        _moe_kernel,
        grid_spec=grid_spec,
        out_shape=jax.ShapeDtypeStruct((n_rows, half), jnp.uint32),
        compiler_params=_params("arbitrary"),
        name="moe_experts",
    )(blk_expert, blk_valid, xs, w1, b1, w2, b2)


def _sc_mesh():
    return plsc.VectorSubcoreMesh(core_axis_name="core", subcore_axis_name="subcore")


def _gather_rows(x, idx):
    m = idx.shape[0]
    w = x.shape[1]

    @pl.kernel(out_type=jax.ShapeDtypeStruct((m, w), x.dtype), mesh=_sc_mesh())
    def gather_kernel(x_hbm, i_hbm, o_hbm):
        def body(i_vmem, o_vmem):
            pltpu.sync_copy(x_hbm.at[i_vmem.at[0]], o_vmem)

        pltpu.emit_pipeline(
            body,
            grid=(m // SC_WINDOW,),
            in_specs=[pl.BlockSpec((1, SC_WINDOW), lambda i: (i, 0))],
            out_specs=[pl.BlockSpec((SC_WINDOW, w), lambda i: (i, 0))],
            core_axis_name=("core", "subcore"),
            dimension_semantics=(pltpu.PARALLEL,),
        )(i_hbm, o_hbm)

    return gather_kernel(x, idx.reshape(m // SC_WINDOW, SC_WINDOW))


def _scatter_rows(x, idx, n_out):
    n, w = x.shape
    kk = idx.shape[1]
    idx3 = jnp.transpose(idx.reshape(n // SC_WINDOW, SC_WINDOW, kk), (0, 2, 1))

    @pl.kernel(out_type=jax.ShapeDtypeStruct((n_out, w), x.dtype), mesh=_sc_mesh())
    def scatter_kernel(x_hbm, i_hbm, o_hbm):
        def body(x_vmem, i_vmem):
            for k in range(kk):
                pltpu.sync_copy(x_vmem, o_hbm.at[i_vmem.at[0, k]])

        pltpu.emit_pipeline(
            body,
            grid=(n // SC_WINDOW,),
            in_specs=[pl.BlockSpec((SC_WINDOW, w), lambda i: (i, 0)),
                      pl.BlockSpec((1, kk, SC_WINDOW), lambda i: (i, 0, 0))],
            out_specs=[],
            core_axis_name=("core", "subcore"),
            dimension_semantics=(pltpu.PARALLEL,),
        )(x_hbm, i_hbm)

    return scatter_kernel(x, idx3)


def _final_kernel(x1_ref, yg_ref, tg_ref, g2_ref, nw_ref, *rest):
    y_ref = rest[-1]
    tg = tg_ref[0]
    moe = _unpack_bf16_pairs(yg_ref[0]) * tg[:, ROUTE_GATE:ROUTE_GATE + 1]
    for kk in range(1, TOP_K):
        moe = moe + _unpack_bf16_pairs(yg_ref[kk]) * tg[:, ROUTE_GATE + kk:ROUTE_GATE + kk + 1]
    x2 = x1_ref[0] + g2_ref[0] * moe
    y_ref[0] = _rms(x2) * nw_ref[...]


def _final(x1, yg, tg, g2, norm_w, tm, tok_offset, b0, bsz_total, prev=None):
    bsz, t_len, _ = x1.shape
    tok = lambda b, i: (b, i, 0)
    steps = t_len // tm
    assert tok_offset % tm == 0
    blk0 = tok_offset // tm
    in_specs = [pl.BlockSpec((1, tm, D_MODEL), tok),
                pl.BlockSpec((TOP_K, tm, D_MODEL // 2), lambda b, i: (0, blk0 + b * steps + i, 0)),
                pl.BlockSpec((1, tm, LANES), tok),
                pl.BlockSpec((1, 1, D_MODEL), lambda b, i: (b0 + b, 0, 0)),
                pl.BlockSpec((1, D_MODEL), lambda b, i: (0, 0))]
    args = [x1, yg, tg, g2, norm_w]
    aliases = {}
    if prev is not None:
        in_specs.append(pl.BlockSpec(memory_space=pl.ANY))
        aliases = {len(args): 0}
        args.append(prev)
    return pl.pallas_call(
        _final_kernel,
        grid=(bsz, steps),
        in_specs=in_specs,
        out_specs=pl.BlockSpec((1, tm, D_MODEL), lambda b, i: (b0 + b, i, 0)),
        out_shape=jax.ShapeDtypeStruct((bsz_total, t_len, D_MODEL), F32),
        input_output_aliases=aliases,
        compiler_params=_params("parallel", "arbitrary"),
        name="combine_final",
    )(*args)


def _t5_bucket(rel):
    half = NUM_BUCKETS // 2
    max_exact = half // 2
    ret = jnp.where(rel > 0, half, 0)
    n = jnp.abs(rel)
    nf = jnp.maximum(n, 1).astype(jnp.float32)
    large = max_exact + (jnp.log(nf / max_exact) / math.log(MAX_DISTANCE / max_exact)
                         * (half - max_exact)).astype(jnp.int32)
    large = jnp.minimum(large, half - 1)
    return ret + jnp.where(n < max_exact, n, large)


def _route(top_i, rank, sizes):
    n_tok = top_i.shape[0]
    n_asg = n_tok * TOP_K
    padded = (sizes + MOE_ROWS - 1) // MOE_ROWS * MOE_ROWS
    pend = jnp.cumsum(padded)
    pstart = pend - padded
    n_blocks = -(-n_asg // MOE_ROWS) + N_EXPERTS
    blk_row0 = jnp.arange(n_blocks, dtype=jnp.int32) * MOE_ROWS
    blk_expert = jnp.minimum(jnp.sum(pend[None, :] <= blk_row0[:, None], axis=1), N_EXPERTS - 1).astype(jnp.int32)
    blk_valid = jnp.clip(pstart[blk_expert] + sizes[blk_expert] - blk_row0, 0, MOE_ROWS).astype(jnp.int32)
    onehot = top_i[:, :, None] == jnp.arange(N_EXPERTS, dtype=jnp.int32)
    pos = jnp.sum(jnp.where(onehot, pstart, 0), axis=-1) + rank
    return pos, blk_expert, blk_valid, n_blocks * MOE_ROWS


def kernel(x_prompt, x_sample, c_prompt, c_sample, state_conv, state_delta, cache_swa_k, cache_swa_v,
           w_ada, b_ada, norm_mix, w_in, b_in, conv_w, a_log, dt_bias, dn_norm_w, sinks, rel_bias,
           w_out, b_out, norm_ffn, w_router, b_router, w1, b1, w2, b2, norm_final):
    depth = w_ada.shape[0]
    assert depth == 1, "the final norm is fused into the layer's combine step"
    bp, tp, _ = x_prompt.shape
    bs, ts, _ = x_sample.shape
    groups = [dict(x=x_prompt, c=c_prompt, hist=0), dict(x=x_sample, c=c_sample, hist=WIN_CHUNKS)]

    q_rel = jnp.arange(CHUNK)
    k_rel = jnp.arange(BAND) - WIN_CHUNKS * CHUNK
    bucket = _t5_bucket(k_rel[None, :] - q_rel[:, None]).astype(jnp.int32)
    bias = _rel_bias(bucket, rel_bias)

    nf_w = norm_final.reshape(1, D_MODEL)
    outs = {g: dict(conv=[], delta=[], k=[], v=[]) for g in range(2)}
    xs_cur = [x_prompt, x_sample]

    for l in range(depth):
        o1 = DN_CONV_DIM + DN_V_DIM
        o2 = o1 + 2 * DN_V_HEADS
        wl, bl = w_in[l], b_in[l]
        w_cat = jnp.concatenate(
            [wl[:, :o1], wl[:, o2:], wl[:, o1:o2], jnp.zeros((D_MODEL, GATE_LANES - 2 * DN_V_HEADS), F32)],
            axis=1).astype(BF16)
        b_cat = jnp.concatenate(
            [bl[:o1], bl[o2:], bl[o1:o2], jnp.zeros((GATE_LANES - 2 * DN_V_HEADS,), F32)]).reshape(1, PROJ_DIM)
        pad8 = jnp.zeros((DN_V_HEADS,), F32)
        padr = jnp.zeros((GATE_LANES - 2 * DN_V_HEADS,), F32)
        alog_l = jnp.concatenate([pad8, a_log[l], padr]).reshape(1, GATE_LANES)
        dtb_l = jnp.concatenate([pad8, dt_bias[l], padr]).reshape(1, GATE_LANES)
        wo1 = w_out[l][:DN_V_DIM].astype(BF16)
        wo2 = w_out[l][DN_V_DIM:].astype(BF16)
        wr = jnp.concatenate([w_router[l], jnp.zeros((D_MODEL, LANES - N_EXPERTS), F32)], axis=1)
        br = jnp.concatenate([b_router[l], jnp.full((LANES - N_EXPERTS,), NEG, F32)]).reshape(1, LANES)
        b1l = b1[l].reshape(N_EXPERTS, 1, 2 * D_FF)
        b2l = b2[l].reshape(N_EXPERTS, 1, D_MODEL)

        mod = _modulation(jnp.concatenate([c_prompt, c_sample], axis=0), w_ada[l], b_ada[l])
        att = []
        for gi, grp in enumerate(groups):
            x = xs_cur[gi]
            bsz, t_len, _ = x.shape
            tm = min(TOKEN_TILE, t_len)
            m = mod[:bp] if gi == 0 else mod[bp:]
            sh1, sc1, g1, sh2, sc2, g2 = [a.reshape(bsz, 1, D_MODEL) for a in jnp.split(m, 6, axis=-1)]
            if gi == 0:
                cprev = jnp.zeros((bsz, CONV_W - 1, DN_CONV_DIM), F32)
                s0 = jnp.zeros((bsz, DN_V_HEADS, DN_HEAD_DIM, DN_HEAD_DIM), F32)
            else:
                cprev = state_conv[l]
                s0 = state_delta[l]
            cprev8 = jnp.concatenate([jnp.zeros((bsz, 8 - (CONV_W - 1), DN_CONV_DIM), F32), cprev], axis=1)
            qkv, z, sq, sk, sv, ba, utail = _inproj(x, sc1, sh1, norm_mix[l].reshape(1, D_MODEL), w_cat, b_cat,
                                                    cprev8, conv_w[l], tm)
            if gi == 0:
                k_all, v_all = sk, sv
            else:
                k_all = jnp.concatenate([cache_swa_k[l].reshape(bsz, -1, SWA_KV_DIM), sk], axis=1)
                v_all = jnp.concatenate([cache_swa_v[l].reshape(bsz, -1, SWA_KV_DIM), sv], axis=1)
            swa_len = cache_swa_k.shape[2]
            o_dn, s_new = _deltanet(qkv, z, ba, s0, alog_l, dtb_l,
                                    jnp.tile(dn_norm_w[l], 2).reshape(1, 2 * DN_HEAD_DIM))
            sink_rows = jnp.repeat(sinks[l], CHUNK).reshape(SWA_KV_HEADS, SWA_GROUP * CHUNK, 1)
            o_swa = _swa(sq, k_all, v_all, bias.reshape(SWA_KV_HEADS, SWA_GROUP * CHUNK, BAND), sink_rows,
                         grp["hist"])
            conv_new = utail[:, 8 - (CONV_W - 1):]
            k_state = k_all[:, k_all.shape[1] - swa_len:].reshape(bsz, swa_len, SWA_KV_HEADS, SWA_HEAD_DIM)
            v_state = v_all[:, v_all.shape[1] - swa_len:].reshape(bsz, swa_len, SWA_KV_HEADS, SWA_HEAD_DIM)
            outs[gi]["conv"].append(conv_new)
            outs[gi]["delta"].append(s_new)
            outs[gi]["k"].append(k_state)
            outs[gi]["v"].append(v_state)
            att.append(dict(x=x, o_dn=o_dn, o_swa=o_swa, g1=g1, sc2=sc2, sh2=sh2, g2=g2, tm=tm,
                            bsz=bsz, t_len=t_len))

        half = bp // 2
        parts = [[(1, 0, bs), (0, 0, half)], [(0, half, bp - half)]] if half else [[(1, 0, bs), (0, 0, bp)]]
        work = []
        for part in parts:
            cnt = jnp.zeros((1, LANES), F32)
            n_part = sum(nseq * att[gi]["t_len"] for gi, _, nseq in part)
            h2_all = jnp.zeros((n_part, D_MODEL // 2), jnp.uint32) if len(part) > 1 else None
            segs, row0 = [], 0
            for gi, b0, nseq in part:
                a = att[gi]
                x1, h2_all, route, cnt = _outproj(a["o_dn"], a["o_swa"], a["x"], a["g1"], a["sc2"], a["sh2"],
                                                  wo1, wo2, b_out[l].reshape(1, D_MODEL),
                                                  norm_ffn[l].reshape(1, D_MODEL), wr, br, cnt, a["tm"], b0, nseq,
                                                  n_part, row0, h2_all)
                segs.append(dict(gi=gi, b0=b0, nseq=nseq, x1=x1, tg=route))
                row0 += nseq * a["t_len"]
            route_all = jnp.concatenate(
                [s["tg"].reshape(-1, LANES)[:, :ROUTE_GATE] for s in segs], axis=0).astype(jnp.int32)
            ti_all = route_all[:, ROUTE_EXPERT:ROUTE_EXPERT + TOP_K]
            rk_all = route_all[:, ROUTE_RANK:ROUTE_RANK + TOP_K]
            sizes = cnt[0, :N_EXPERTS].astype(jnp.int32)
            pos, blk_expert, blk_valid, n_rows = _route(ti_all, rk_all, sizes)
            xs = _scatter_rows(h2_all, pos, n_rows)
            work.append(dict(segs=segs, pos=pos, blk_expert=blk_expert, blk_valid=blk_valid, xs=xs,
                             n_tok=h2_all.shape[0]))
        for wk in work:
            wk["y"] = _moe_experts(wk["blk_expert"], wk["blk_valid"], wk["xs"], w1[l], b1l, w2[l], b2l)
        for wk in work:
            wk["yg"] = _gather_rows(wk["y"], wk["pos"].T.reshape(-1)).reshape(TOP_K, wk["n_tok"], D_MODEL // 2)
        y_out = [None, None]
        for wk in work:
            off = 0
            for s in wk["segs"]:
                a = att[s["gi"]]
                y_out[s["gi"]] = _final(s["x1"], wk["yg"], s["tg"], a["g2"], nf_w, a["tm"], off, s["b0"],
                                        a["bsz"], prev=y_out[s["gi"]])
                off += s["nseq"] * a["t_len"]
        xs_cur = y_out

    res = [xs_cur[0], xs_cur[1]]
    for gi in range(2):
        for name in ("conv", "delta", "k", "v"):
            res.append(jnp.stack(outs[gi][name]))
    return tuple(res)
```

```python
import functools
import math

import jax
import jax.numpy as jnp
from jax import lax
from jax.experimental import pallas as pl
from jax.experimental.pallas import tpu as pltpu
from jax.experimental.pallas import tpu_sc as plsc

F32 = jnp.float32
BF16 = jnp.bfloat16

D_MODEL = 1024
CHUNK = 64
EPS = 1e-6
DN_QK_HEADS = 4
DN_V_HEADS = 8
DN_HEAD_DIM = 64
DN_QK_DIM = DN_QK_HEADS * DN_HEAD_DIM
DN_V_DIM = DN_V_HEADS * DN_HEAD_DIM
DN_CONV_DIM = 2 * DN_QK_DIM + DN_V_DIM
CONV_W = 4
SWA_HEADS = 8
SWA_KV_HEADS = 2
SWA_HEAD_DIM = 64
SWA_GROUP = SWA_HEADS // SWA_KV_HEADS
SWA_Q_DIM = SWA_HEADS * SWA_HEAD_DIM
SWA_KV_DIM = SWA_KV_HEADS * SWA_HEAD_DIM
WINDOW = 128
WIN_CHUNKS = WINDOW // CHUNK
BAND = (WIN_CHUNKS + 1) * CHUNK
NUM_BUCKETS = 32
MAX_DISTANCE = 128
N_EXPERTS = 32
TOP_K = 4
D_FF = 1024
SWIGLU_ALPHA = 1.702
SWIGLU_LIMIT = 7.0

LANES = 128
GATE_LANES = LANES
PROJ_DIM = DN_CONV_DIM + DN_V_DIM + SWA_Q_DIM + 2 * SWA_KV_DIM + GATE_LANES
MOE_ROWS = 1024
MOE_FF_CHUNK = 1024
TOKEN_TILE = 512
DN_SEQS_PER_STEP = 8
SWA_CHUNKS_PER_STEP = 16
SC_WINDOW = 64
VMEM_LIMIT = 56 * 1024 * 1024
NEG = -1e30


def _params(*sem):
    return pltpu.CompilerParams(dimension_semantics=sem, vmem_limit_bytes=VMEM_LIMIT)


def _split2(a):
    hi = a.astype(BF16)
    lo = (a - hi.astype(F32)).astype(BF16)
    return hi, lo


def _dot(a, b):
    return jnp.dot(a, b, preferred_element_type=F32)


def _dot_x3(a, b):
    a1, a2 = _split2(a)
    b1, b2 = _split2(b)
    return _dot(a1, b1) + (_dot(a1, b2) + _dot(a2, b1))


def _dot_exact_lhs(l01, g):
    g1 = g.astype(BF16)
    r = g - g1.astype(F32)
    g2 = r.astype(BF16)
    g3 = (r - g2.astype(F32)).astype(BF16)
    return _dot(l01, g1) + (_dot(l01, g2) + _dot(l01, g3))


def _pack_bf16_pairs(x):
    w = x.shape[1] // 2
    bits = lax.bitcast_convert_type(x.astype(BF16).astype(F32), jnp.uint32)
    return (bits[:, w:] & jnp.uint32(0xFFFF0000)) | (bits[:, :w] >> 16)


def _unpack_bf16_pairs(p):
    lo = lax.bitcast_convert_type(p << 16, F32)
    hi = lax.bitcast_convert_type(p & jnp.uint32(0xFFFF0000), F32)
    return jnp.concatenate([lo, hi], axis=1)


def _silu(x):
    return x * jax.nn.sigmoid(x)


def _rms(x):
    return x * lax.rsqrt(jnp.mean(x * x, axis=-1, keepdims=True) + EPS)


def _mod_kernel(c_ref, w_ref, b_ref, o_ref):
    o_ref[...] = _dot_x3(_silu(c_ref[...]), w_ref[...]) + b_ref[...]


def _modulation(c, w_ada, b_ada):
    nb = c.shape[0]
    n_out = w_ada.shape[1]
    tn = 1024
    return pl.pallas_call(
        _mod_kernel,
        grid=(n_out // tn,),
        in_specs=[pl.BlockSpec((nb, D_MODEL), lambda j: (0, 0)),
                  pl.BlockSpec((D_MODEL, tn), lambda j: (0, j)),
                  pl.BlockSpec((1, tn), lambda j: (0, j))],
        out_specs=pl.BlockSpec((nb, tn), lambda j: (0, j)),
        out_shape=jax.ShapeDtypeStruct((nb, n_out), F32),
        compiler_params=_params("arbitrary"),
        name="modulation",
    )(c, w_ada, b_ada.reshape(1, n_out))


def _inproj_kernel(x_ref, sc_ref, sh_ref, nw_ref, w_ref, b_ref,
                   cprev_ref, cw_ref, qkv_ref, z_ref, sq_ref, sk_ref, sv_ref, ba_ref, utail_ref, xc_scr):
    tm = x_ref.shape[1]

    @pl.when(pl.program_id(1) == 0)
    def _():
        xc_scr[0:8, :] = cprev_ref[0]

    h = _rms(x_ref[0]) * nw_ref[...]
    h = h * (1.0 + sc_ref[0]) + sh_ref[0]
    p = _dot(h.astype(BF16), w_ref[...]) + b_ref[...]
    o = DN_CONV_DIM
    for ref in (z_ref, sq_ref, sk_ref, sv_ref, ba_ref):
        w = ref.shape[-1]
        ref[0] = p[:, o:o + w].astype(ref.dtype)
        o += w

    xc_scr[8:8 + tm, :] = p[:, :DN_CONV_DIM]
    conv = xc_scr[5:5 + tm, :] * cw_ref[0:1, :]
    for j in range(1, CONV_W):
        conv = conv + xc_scr[5 + j:5 + j + tm, :] * cw_ref[j:j + 1, :]
    tail = xc_scr[tm:tm + 8, :]
    xc_scr[0:8, :] = tail
    utail_ref[0] = tail
    cu = _silu(conv)

    pw = 2 * DN_HEAD_DIM
    lo = lax.broadcasted_iota(jnp.int32, (tm, pw), 1) < DN_HEAD_DIM
    for c in range(DN_QK_DIM // pw):
        qc = cu[:, c * pw:(c + 1) * pw]
        kc = cu[:, DN_QK_DIM + c * pw:DN_QK_DIM + (c + 1) * pw]
        qkv_ref[0, :, c * pw:(c + 1) * pw] = (
            qc * lax.rsqrt(_half_sums(qc * qc, lo) + EPS) * (DN_HEAD_DIM ** -0.5))
        qkv_ref[0, :, DN_QK_DIM + c * pw:DN_QK_DIM + (c + 1) * pw] = kc * lax.rsqrt(_half_sums(kc * kc, lo) + EPS)
    qkv_ref[0, :, 2 * DN_QK_DIM:] = cu[:, 2 * DN_QK_DIM:]


def _inproj(x, sc, sh, norm_w, w_cat, b_cat, cprev8, conv_w, tm):
    bsz, t_len, _ = x.shape
    widths = (DN_CONV_DIM, DN_V_DIM, SWA_Q_DIM, SWA_KV_DIM, SWA_KV_DIM, GATE_LANES)
    tok = lambda b, i: (b, i, 0)
    per_b = lambda b, i: (b, 0, 0)
    fixed = lambda b, i: (0, 0)
    out_shape = [jax.ShapeDtypeStruct((bsz, t_len, w), BF16 if i == 2 else F32) for i, w in enumerate(widths)]
    return pl.pallas_call(
        _inproj_kernel,
        grid=(bsz, t_len // tm),
        in_specs=[pl.BlockSpec((1, tm, D_MODEL), tok),
                  pl.BlockSpec((1, 1, D_MODEL), per_b),
                  pl.BlockSpec((1, 1, D_MODEL), per_b),
                  pl.BlockSpec((1, D_MODEL), fixed),
                  pl.BlockSpec((D_MODEL, PROJ_DIM), fixed),
                  pl.BlockSpec((1, PROJ_DIM), fixed),
                  pl.BlockSpec((1, 8, DN_CONV_DIM), per_b),
                  pl.BlockSpec((CONV_W, DN_CONV_DIM), fixed)],
        out_specs=[pl.BlockSpec((1, tm, w), tok) for w in widths] + [pl.BlockSpec((1, 8, DN_CONV_DIM), per_b)],
        out_shape=out_shape + [jax.ShapeDtypeStruct((bsz, 8, DN_CONV_DIM), F32)],
        scratch_shapes=[pltpu.VMEM((tm + 8, DN_CONV_DIM), F32)],
        compiler_params=_params("parallel", "arbitrary"),
        name="inproj",
    )(x, sc, sh, norm_w, w_cat, b_cat, cprev8, conv_w)


def _bmm(a, b):
    return lax.dot_general(a.astype(BF16), b.astype(BF16), (((2,), (1,)), ((0,), (0,))),
                           preferred_element_type=F32)


def _bmm_nt(a, b):
    return lax.dot_general(a.astype(BF16), b.astype(BF16), (((2,), (2,)), ((0,), (0,))),
                           preferred_element_type=F32)


def _bd(x):
    x = x.astype(BF16)
    lo = lax.broadcasted_iota(jnp.int32, x.shape, 2) < x.shape[2] // 2
    zero = jnp.zeros_like(x)
    return jnp.concatenate([jnp.where(lo, x, zero), jnp.where(lo, zero, x)], axis=1)


def _pmm(a, b):
    return _bmm(a, _bd(b))


def _half_sums(x, lo):
    s_lo = jnp.sum(jnp.where(lo, x, 0.0), axis=-1, keepdims=True)
    s_hi = jnp.sum(jnp.where(lo, 0.0, x), axis=-1, keepdims=True)
    return jnp.where(lo, s_lo, s_hi)


def _deltanet_pair_kernel(u_ref, z_ref, ba_ref, s0_ref, alog_ref, dtb_ref, nw_ref,
                          o_ref, sfin_ref, s_scr, *, nb):
    n = pl.program_id(1)
    hd = DN_HEAD_DIM
    pw = 2 * hd
    npair = DN_QK_HEADS

    @pl.when(n == 0)
    def _():
        for i in range(nb):
            for j in range(npair):
                s_scr[i * npair + j] = jnp.concatenate([s0_ref[i, 2 * j], s0_ref[i, 2 * j + 1]], axis=1)

    row = lax.broadcasted_iota(jnp.int32, (CHUNK, pw), 0)
    col = lax.broadcasted_iota(jnp.int32, (CHUNK, pw), 1) % hd
    lo = lax.broadcasted_iota(jnp.int32, (CHUNK, pw), 1) < hd
    incl = row >= col
    strict = row > col
    r1 = lax.broadcasted_iota(jnp.int32, (CHUNK, CHUNK), 0)
    c1 = lax.broadcasted_iota(jnp.int32, (CHUNK, CHUNK), 1)
    lower01 = jnp.where(r1 >= c1, 1.0, 0.0).astype(BF16)
    eye = jnp.where(row == col, 1.0, 0.0).astype(F32)
    merge_masks = [(row // 2 == col // 2) & strict]
    blk = 2
    while blk < CHUNK:
        merge_masks.append((row // (2 * blk) == col // (2 * blk)) & (row // blk != col // blk) & strict)
        blk *= 2

    qk_lhs, qk_rhs, q_items, k_items, v_items, z_items = [], [], [], [], [], []
    zt_items, grow_items = [], []
    for i in range(nb):
        ba = ba_ref[i]
        beta_all = jax.nn.sigmoid(ba)
        sp = ba + dtb_ref[...]
        sp = jnp.maximum(sp, 0.0) + jnp.log1p(jnp.exp(-jnp.abs(sp)))
        g_all = -jnp.exp(alog_ref[...]) * sp
        gc_all = _dot_exact_lhs(lower01, g_all)
        gc_t = gc_all.T
        beta_t = beta_all.T

        for c in range(DN_QK_DIM // pw):
            qc = u_ref[i, :, c * pw:(c + 1) * pw]
            kc = u_ref[i, :, DN_QK_DIM + c * pw:DN_QK_DIM + (c + 1) * pw]
            qr = pltpu.roll(qc, hd, axis=1)
            kr = pltpu.roll(kc, hd, axis=1)
            for half in range(2):
                sel = lo if half == 0 else jnp.logical_not(lo)
                q_items.append(jnp.where(sel, qc, qr))
                k_items.append(jnp.where(sel, kc, kr))
                km = jnp.where(sel, kc, 0.0)
                qk_lhs.append(jnp.concatenate([qc, kc], axis=0))
                qk_rhs.append(jnp.concatenate([km, km], axis=0))
        for j in range(npair):
            a, b = 2 * j, 2 * j + 1
            v_items.append(u_ref[i, :, 2 * DN_QK_DIM + j * pw:2 * DN_QK_DIM + (j + 1) * pw])
            z_items.append(z_ref[i, :, j * pw:(j + 1) * pw])
            ra = jnp.concatenate([beta_t[a:a + 1, :], gc_t[8 + a:9 + a, :]], axis=1)
            rb = jnp.concatenate([beta_t[b:b + 1, :], gc_t[8 + b:9 + b, :]], axis=1)
            zt_items.append(jnp.concatenate([jnp.broadcast_to(ra, (CHUNK, pw)),
                                             jnp.broadcast_to(rb, (CHUNK, pw))], axis=0).T)
            grow_items.append(jnp.broadcast_to(
                jnp.concatenate([gc_t[8 + a:9 + a, :], gc_t[8 + b:9 + b, :]], axis=1), (CHUNK, pw)))

    q = jnp.stack(q_items)
    k = jnp.stack(k_items)
    v = jnp.stack(v_items)
    cols = jnp.stack(zt_items)
    beta, gcol = cols[:, :CHUNK], cols[:, CHUNK:]
    grow = jnp.stack(grow_items)
    glast = gcol[:, CHUNK - 1:CHUNK, :]

    qkk = _bmm_nt(jnp.stack(qk_lhs), jnp.stack(qk_rhs))
    qk, kk = qkk[:, :CHUNK], qkk[:, CHUNK:]

    decay = jnp.exp(jnp.where(incl, gcol - grow, NEG))
    eg = jnp.exp(gcol)

    m = jnp.where(strict, (kk * beta) * decay, 0.0)
    t = eye - jnp.where(merge_masks[0], m, 0.0)
    for mask in merge_masks[1:]:
        t = t - _pmm(t, _pmm(jnp.where(mask, m, 0.0), t))
    t0 = t.astype(BF16)
    t0_bd = _bd(t0)
    m_hi, m_lo = _split2(m)
    mt0 = _bmm(jnp.concatenate([m_hi, m_lo], axis=2), jnp.concatenate([t0_bd, t0_bd], axis=1))
    resid = eye - t0.astype(F32) - mt0
    t1 = _pmm(t0, resid).astype(BF16)

    s = s_scr[...]
    ks = _pmm(jnp.concatenate([k * (beta * eg), q * eg], axis=1), s)
    rhs_bd = _bd(v * beta - ks[:, :CHUNK])
    vnew = _bmm(jnp.concatenate([t0, t1], axis=2), jnp.concatenate([rhs_bd, rhs_bd], axis=1))
    o = ks[:, CHUNK:] + _pmm(qk * decay, vnew)
    kd = k * jnp.exp(glast - gcol)
    kv = _bmm(jnp.swapaxes(kd, 1, 2), vnew)
    s_scr[...] = s * jnp.exp(glast) + jnp.where(lo, kv[:, :hd], kv[:, hd:])

    o = o * lax.rsqrt(_half_sums(o * o, lo) * (1.0 / hd) + EPS) * nw_ref[...] * _silu(jnp.stack(z_items))
    for i in range(nb):
        for j in range(npair):
            o_ref[i, :, j * pw:(j + 1) * pw] = o[i * npair + j].astype(o_ref.dtype)

    @pl.when(n == pl.num_programs(1) - 1)
    def _():
        for i in range(nb):
            for j in range(npair):
                sp2 = s_scr[i * npair + j]
                sfin_ref[i, 2 * j] = sp2[:, :hd]
                sfin_ref[i, 2 * j + 1] = sp2[:, hd:]


def _deltanet(qkv, z, ba, s0, alog_l, dtb_l, dn_norm_w):
    bsz, t_len, _ = qkv.shape
    nb = DN_SEQS_PER_STEP
    tok = lambda b, n: (b, n, 0)
    per_b4 = lambda b, n: (b, 0, 0, 0)
    fixed = lambda b, n: (0, 0)
    return pl.pallas_call(
        functools.partial(_deltanet_pair_kernel, nb=nb),
        grid=(bsz // nb, t_len // CHUNK),
        in_specs=[pl.BlockSpec((nb, CHUNK, DN_CONV_DIM), tok),
                  pl.BlockSpec((nb, CHUNK, DN_V_DIM), tok),
                  pl.BlockSpec((nb, CHUNK, GATE_LANES), tok),
                  pl.BlockSpec((nb, DN_V_HEADS, DN_HEAD_DIM, DN_HEAD_DIM), per_b4),
                  pl.BlockSpec((1, GATE_LANES), fixed),
                  pl.BlockSpec((1, GATE_LANES), fixed),
                  pl.BlockSpec((1, 2 * DN_HEAD_DIM), fixed)],
        out_specs=[pl.BlockSpec((nb, CHUNK, DN_V_DIM), tok),
                   pl.BlockSpec((nb, DN_V_HEADS, DN_HEAD_DIM, DN_HEAD_DIM), per_b4)],
        out_shape=[jax.ShapeDtypeStruct((bsz, t_len, DN_V_DIM), BF16),
                   jax.ShapeDtypeStruct((bsz, DN_V_HEADS, DN_HEAD_DIM, DN_HEAD_DIM), F32)],
        scratch_shapes=[pltpu.VMEM((nb * DN_QK_HEADS, DN_HEAD_DIM, 2 * DN_HEAD_DIM), F32)],
        compiler_params=_params("parallel", "arbitrary"),
        name="deltanet",
    )(qkv, z, ba, s0, alog_l, dtb_l, dn_norm_w)


def _bias_kernel(bucket_ref, table_ref, o_ref):
    bucket = bucket_ref[...]
    for h in range(SWA_HEADS):
        acc = jnp.zeros(bucket.shape, F32)
        for b in range(NUM_BUCKETS):
            acc = jnp.where(bucket == b, table_ref[b, h], acc)
        o_ref[h] = acc


def _rel_bias(bucket, table):
    return pl.pallas_call(
        _bias_kernel,
        in_specs=[pl.BlockSpec(memory_space=pltpu.VMEM),
                  pl.BlockSpec(memory_space=pltpu.SMEM)],
        out_specs=pl.BlockSpec(memory_space=pltpu.VMEM),
        out_shape=jax.ShapeDtypeStruct((SWA_HEADS, CHUNK, BAND), F32),
        name="rel_bias",
    )(bucket, table)


def _swa_kernel(q_ref, *refs, hist, cps, n_units):
    k_refs, v_refs = refs[:n_units], refs[n_units:2 * n_units]
    bias_ref, sink_ref, o_ref = refs[2 * n_units:]
    n = pl.program_id(1)
    hd = SWA_HEAD_DIM
    kb = jnp.concatenate([r[0] for r in k_refs], axis=0)
    vb = jnp.concatenate([r[0] for r in v_refs], axis=0)
    row0 = (n * cps + hist) * CHUNK - WINDOW
    key = lax.broadcasted_iota(jnp.int32, (1, 1, BAND), 2)
    q_items, k_items, v_items, valid = [], [], [], []
    for c in range(cps):
        for kv in range(SWA_KV_HEADS):
            q_items.append(jnp.concatenate(
                [q_ref[0, c * CHUNK:(c + 1) * CHUNK, (kv * SWA_GROUP + g) * hd:(kv * SWA_GROUP + g + 1) * hd]
                 for g in range(SWA_GROUP)], axis=0))
            k_items.append(kb[c * CHUNK:c * CHUNK + BAND, kv * hd:(kv + 1) * hd])
            v_items.append(vb[c * CHUNK:c * CHUNK + BAND, kv * hd:(kv + 1) * hd])
            valid.append(row0 + c * CHUNK + key >= 0)
    s = _bmm_nt(jnp.stack(q_items), jnp.stack(k_items)) * (hd ** -0.5) + bias_ref[...]
    s = jnp.where(jnp.concatenate(valid, axis=0), s, NEG)
    sink = sink_ref[...]
    mx = jnp.maximum(jnp.max(s, axis=-1, keepdims=True), sink)
    p = jnp.exp(s - mx).astype(BF16)
    den = _bmm(p, jnp.ones((len(v_items), BAND, hd), BF16)) + jnp.exp(sink - mx)
    o = _bmm(p, jnp.stack(v_items)) / den
    for c in range(cps):
        for kv in range(SWA_KV_HEADS):
            for g in range(SWA_GROUP):
                h = kv * SWA_GROUP + g
                o_ref[0, c * CHUNK:(c + 1) * CHUNK, h * hd:(h + 1) * hd] = (
                    o[c * SWA_KV_HEADS + kv, g * CHUNK:(g + 1) * CHUNK].astype(o_ref.dtype))


def _swa(q, k_all, v_all, bias, sink_rows, hist):
    bsz, t_len, _ = q.shape
    cps = min(SWA_CHUNKS_PER_STEP, t_len // CHUNK)
    unit = WINDOW if cps * CHUNK % WINDOW == 0 else CHUNK
    assert (hist * CHUNK - WINDOW) % unit == 0 and (cps * CHUNK) % unit == 0
    n_units = (WINDOW + cps * CHUNK) // unit
    q_units = cps * CHUNK // unit
    unit0 = (hist * CHUNK - WINDOW) // unit
    tok = lambda b, n: (b, n, 0)

    def band(j):
        return lambda b, n: (b, jnp.maximum(n * q_units + unit0 + j, 0), 0)

    kv_specs = [pl.BlockSpec((1, unit, SWA_KV_DIM), band(j)) for j in range(n_units)]
    fixed3 = lambda b, n: (0, 0, 0)
    return pl.pallas_call(
        functools.partial(_swa_kernel, hist=hist, cps=cps, n_units=n_units),
        grid=(bsz, t_len // (cps * CHUNK)),
        in_specs=[pl.BlockSpec((1, cps * CHUNK, SWA_Q_DIM), tok)] + kv_specs + kv_specs
                 + [pl.BlockSpec((cps * SWA_KV_HEADS, SWA_GROUP * CHUNK, BAND), fixed3),
                    pl.BlockSpec((cps * SWA_KV_HEADS, SWA_GROUP * CHUNK, 1), fixed3)],
        out_specs=pl.BlockSpec((1, cps * CHUNK, SWA_Q_DIM), tok),
        out_shape=jax.ShapeDtypeStruct((bsz, t_len, SWA_Q_DIM), BF16),
        compiler_params=_params("parallel", "arbitrary"),
        name="swa",
    )(q, *([k_all] * n_units), *([v_all] * n_units),
      jnp.tile(bias, (cps, 1, 1)), jnp.tile(sink_rows, (cps, 1, 1)))


ROUTE_EXPERT, ROUTE_RANK, ROUTE_GATE = 0, TOP_K, 2 * TOP_K


def _outproj_kernel(odn_ref, oswa_ref, x_ref, g1_ref, sc_ref, sh_ref, wo1_ref, wo2_ref, bo_ref,
                    nw_ref, wr_ref, br_ref, cnt0_ref, *rest):
    x1_ref, h2_ref, route_ref, cnt_ref, cnt_scr, tri_scr = rest[-6:]
    first = (pl.program_id(0) == 0) & (pl.program_id(1) == 0)

    @pl.when(first)
    def _():
        cnt_scr[...] = cnt0_ref[...]
        r_i = lax.broadcasted_iota(jnp.int32, tri_scr.shape, 0)
        c_i = lax.broadcasted_iota(jnp.int32, tri_scr.shape, 1)
        tri_scr[...] = jnp.where(r_i > c_i, 1.0, 0.0).astype(BF16)

    mix = _dot(odn_ref[0].astype(BF16), wo1_ref[...]) + _dot(oswa_ref[0].astype(BF16), wo2_ref[...])
    x1 = x_ref[0] + g1_ref[0] * (mix + bo_ref[...])
    x1_ref[0] = x1
    h2 = _rms(x1) * nw_ref[...]
    h2 = h2 * (1.0 + sc_ref[0]) + sh_ref[0]
    h2_ref[...] = _pack_bf16_pairs(h2)

    logits = _dot_x3(h2, wr_ref[...]) + br_ref[...]
    lane = lax.broadcasted_iota(jnp.int32, logits.shape, 1)
    lane_f = lane.astype(F32)
    vals, idxs = [], []
    for _ in range(TOP_K):
        m = jnp.max(logits, axis=-1, keepdims=True)
        i = jnp.min(jnp.where(logits == m, lane_f, float(LANES)), axis=-1, keepdims=True)
        vals.append(m)
        idxs.append(i)
        logits = jnp.where(lane_f == i, -jnp.inf, logits)
    es = [jnp.exp(v - vals[0]) for v in vals]
    den = es[0] + es[1] + es[2] + es[3]
    onehot = jnp.zeros(lane.shape, F32)
    for kk in range(TOP_K):
        onehot = jnp.where(lane_f == idxs[kk], 1.0, onehot)
    before = _dot(tri_scr[...], onehot.astype(BF16)) + cnt_scr[...]
    cnt_scr[...] = cnt_scr[...] + jnp.sum(onehot, axis=0, keepdims=True)

    route = jnp.zeros(lane.shape, F32)
    for kk in range(TOP_K):
        rank = jnp.sum(jnp.where(lane_f == idxs[kk], before, 0.0), axis=-1, keepdims=True)
        route = jnp.where(lane == ROUTE_EXPERT + kk, idxs[kk], route)
        route = jnp.where(lane == ROUTE_RANK + kk, rank, route)
        route = jnp.where(lane == ROUTE_GATE + kk, es[kk] / den, route)
    route_ref[0] = route
    cnt_ref[...] = cnt_scr[...]


def _outproj(odn, oswa, x, g1, sc, sh, wo1, wo2, bo, norm_w, wr, br, cnt0, tm, b0, bsz, h2_rows, h2_row0, h2_prev):
    t_len = x.shape[1]
    steps = t_len // tm
    assert h2_row0 % tm == 0
    blk0 = h2_row0 // tm
    tok_out = lambda b, i: (b, i, 0)
    tok = lambda b, i: (b0 + b, i, 0)
    per_b = lambda b, i: (b0 + b, 0, 0)
    fixed = lambda b, i: (0, 0)
    in_specs = [pl.BlockSpec((1, tm, DN_V_DIM), tok),
                pl.BlockSpec((1, tm, SWA_Q_DIM), tok),
                pl.BlockSpec((1, tm, D_MODEL), tok),
                pl.BlockSpec((1, 1, D_MODEL), per_b),
                pl.BlockSpec((1, 1, D_MODEL), per_b),
                pl.BlockSpec((1, 1, D_MODEL), per_b),
                pl.BlockSpec((DN_V_DIM, D_MODEL), fixed),
                pl.BlockSpec((SWA_Q_DIM, D_MODEL), fixed),
                pl.BlockSpec((1, D_MODEL), fixed),
                pl.BlockSpec((1, D_MODEL), fixed),
                pl.BlockSpec((D_MODEL, LANES), fixed),
                pl.BlockSpec((1, LANES), fixed),
                pl.BlockSpec((1, LANES), fixed)]
    args = [odn, oswa, x, g1, sc, sh, wo1, wo2, bo, norm_w, wr, br, cnt0]
    aliases = {}
    if h2_prev is not None:
        in_specs.append(pl.BlockSpec(memory_space=pl.ANY))
        aliases = {len(args): 1}
        args.append(h2_prev)
    return pl.pallas_call(
        _outproj_kernel,
        grid=(bsz, steps),
        in_specs=in_specs,
        out_specs=[pl.BlockSpec((1, tm, D_MODEL), tok_out),
                   pl.BlockSpec((tm, D_MODEL // 2), lambda b, i: (blk0 + b * steps + i, 0)),
                   pl.BlockSpec((1, tm, LANES), tok_out),
                   pl.BlockSpec((1, LANES), fixed)],
        out_shape=[jax.ShapeDtypeStruct((bsz, t_len, D_MODEL), F32),
                   jax.ShapeDtypeStruct((h2_rows, D_MODEL // 2), jnp.uint32),
                   jax.ShapeDtypeStruct((bsz, t_len, LANES), F32),
                   jax.ShapeDtypeStruct((1, LANES), F32)],
        scratch_shapes=[pltpu.VMEM((1, LANES), F32), pltpu.VMEM((tm, tm), BF16)],
        input_output_aliases=aliases,
        compiler_params=_params("arbitrary", "arbitrary"),
        name="outproj_router",
    )(*args)


def _moe_kernel(be_ref, nv_ref, xs_ref, w1_ref, b1_ref, w2_ref, b2_ref, y_ref, w1b_scr, w2b_scr):
    i = pl.program_id(0)
    nv = nv_ref[i]

    @pl.when((i == 0) | (be_ref[i] != be_ref[jnp.maximum(i - 1, 0)]))
    def _():
        w1b_scr[...] = w1_ref[0].astype(BF16)
        w2b_scr[...] = w2_ref[0].astype(BF16)

    def ffn(n_rows):
        rows = lax.broadcasted_iota(jnp.int32, (n_rows, 1), 0)
        xb = jnp.where(rows < nv, _unpack_bf16_pairs(xs_ref[0:n_rows, :]), 0.0).astype(BF16)
        y = b2_ref[0]
        for c in range(D_FF // MOE_FF_CHUNK):
            lo, hi = c * MOE_FF_CHUNK, (c + 1) * MOE_FF_CHUNK
            glu = _dot(xb, w1b_scr[:, lo:hi]) + b1_ref[0, :, lo:hi]
            lin = _dot(xb, w1b_scr[:, D_FF + lo:D_FF + hi]) + b1_ref[0, :, D_FF + lo:D_FF + hi]
            glu = jnp.minimum(glu, SWIGLU_LIMIT)
            lin = jnp.clip(lin, -SWIGLU_LIMIT, SWIGLU_LIMIT)
            act = glu * jax.nn.sigmoid(SWIGLU_ALPHA * glu) * (lin + 1.0)
            y = y + _dot(act.astype(BF16), w2b_scr[lo:hi, :])
        y_ref[0:n_rows, :] = _pack_bf16_pairs(y)

    half = MOE_ROWS // 2

    @pl.when(nv == 0)
    def _():
        y_ref[...] = jnp.zeros(y_ref.shape, y_ref.dtype)

    @pl.when((nv > 0) & (nv <= half))
    def _():
        ffn(half)
        y_ref[half:, :] = jnp.zeros((MOE_ROWS - half, y_ref.shape[1]), y_ref.dtype)

    @pl.when(nv > half)
    def _():
        ffn(MOE_ROWS)


def _moe_experts(blk_expert, blk_valid, xs, w1, b1, w2, b2):
    n_rows = xs.shape[0]
    n_blocks = n_rows // MOE_ROWS
    half = D_MODEL // 2
    grid_spec = pltpu.PrefetchScalarGridSpec(
        num_scalar_prefetch=2,
        grid=(n_blocks,),
        in_specs=[pl.BlockSpec((MOE_ROWS, half), lambda i, be, nv: (i, 0)),
                  pl.BlockSpec((1, D_MODEL, 2 * D_FF), lambda i, be, nv: (be[i], 0, 0)),
                  pl.BlockSpec((1, 1, 2 * D_FF), lambda i, be, nv: (be[i], 0, 0)),
                  pl.BlockSpec((1, D_FF, D_MODEL), lambda i, be, nv: (be[i], 0, 0)),
                  pl.BlockSpec((1, 1, D_MODEL), lambda i, be, nv: (be[i], 0, 0))],
        out_specs=pl.BlockSpec((MOE_ROWS, half), lambda i, be, nv: (i, 0)),
        scratch_shapes=[pltpu.VMEM((D_MODEL, 2 * D_FF), BF16), pltpu.VMEM((D_FF, D_MODEL), BF16)],
    )
    return pl.pallas_call(
        _moe_kernel,
        grid_spec=grid_spec,
        out_shape=jax.ShapeDtypeStruct((n_rows, half), jnp.uint32),
        compiler_params=_params("arbitrary"),
        name="moe_experts",
    )(blk_expert, blk_valid, xs, w1, b1, w2, b2)


def _sc_mesh():
    return plsc.VectorSubcoreMesh(core_axis_name="core", subcore_axis_name="subcore")


def _gather_rows(x, idx):
    m = idx.shape[0]
    w = x.shape[1]

    @pl.kernel(out_type=jax.ShapeDtypeStruct((m, w), x.dtype), mesh=_sc_mesh())
    def gather_kernel(x_hbm, i_hbm, o_hbm):
        def body(i_vmem, o_vmem):
            pltpu.sync_copy(x_hbm.at[i_vmem.at[0]], o_vmem)

        pltpu.emit_pipeline(
            body,
            grid=(m // SC_WINDOW,),
            in_specs=[pl.BlockSpec((1, SC_WINDOW), lambda i: (i, 0))],
            out_specs=[pl.BlockSpec((SC_WINDOW, w), lambda i: (i, 0))],
            core_axis_name=("core", "subcore"),
            dimension_semantics=(pltpu.PARALLEL,),
        )(i_hbm, o_hbm)

    return gather_kernel(x, idx.reshape(m // SC_WINDOW, SC_WINDOW))


def _scatter_rows(x, idx, n_out):
    n, w = x.shape
    kk = idx.shape[1]
    idx3 = jnp.transpose(idx.reshape(n // SC_WINDOW, SC_WINDOW, kk), (0, 2, 1))

    @pl.kernel(out_type=jax.ShapeDtypeStruct((n_out, w), x.dtype), mesh=_sc_mesh())
    def scatter_kernel(x_hbm, i_hbm, o_hbm):
        def body(x_vmem, i_vmem):
            for k in range(kk):
                pltpu.sync_copy(x_vmem, o_hbm.at[i_vmem.at[0, k]])

        pltpu.emit_pipeline(
            body,
            grid=(n // SC_WINDOW,),
            in_specs=[pl.BlockSpec((SC_WINDOW, w), lambda i: (i, 0)),
                      pl.BlockSpec((1, kk, SC_WINDOW), lambda i: (i, 0, 0))],
            out_specs=[],
            core_axis_name=("core", "subcore"),
            dimension_semantics=(pltpu.PARALLEL,),
        )(x_hbm, i_hbm)

    return scatter_kernel(x, idx3)


def _final_kernel(x1_ref, yg_ref, tg_ref, g2_ref, nw_ref, *rest):
    y_ref = rest[-1]
    tg = tg_ref[0]
    moe = _unpack_bf16_pairs(yg_ref[0]) * tg[:, ROUTE_GATE:ROUTE_GATE + 1]
    for kk in range(1, TOP_K):
        moe = moe + _unpack_bf16_pairs(yg_ref[kk]) * tg[:, ROUTE_GATE + kk:ROUTE_GATE + kk + 1]
    x2 = x1_ref[0] + g2_ref[0] * moe
    y_ref[0] = _rms(x2) * nw_ref[...]


def _final(x1, yg, tg, g2, norm_w, tm, tok_offset, b0, bsz_total, prev=None):
    bsz, t_len, _ = x1.shape
    tok = lambda b, i: (b, i, 0)
    steps = t_len // tm
    assert tok_offset % tm == 0
    blk0 = tok_offset // tm
    in_specs = [pl.BlockSpec((1, tm, D_MODEL), tok),
                pl.BlockSpec((TOP_K, tm, D_MODEL // 2), lambda b, i: (0, blk0 + b * steps + i, 0)),
                pl.BlockSpec((1, tm, LANES), tok),
                pl.BlockSpec((1, 1, D_MODEL), lambda b, i: (b0 + b, 0, 0)),
                pl.BlockSpec((1, D_MODEL), lambda b, i: (0, 0))]
    args = [x1, yg, tg, g2, norm_w]
    aliases = {}
    if prev is not None:
        in_specs.append(pl.BlockSpec(memory_space=pl.ANY))
        aliases = {len(args): 0}
        args.append(prev)
    return pl.pallas_call(
        _final_kernel,
        grid=(bsz, steps),
        in_specs=in_specs,
        out_specs=pl.BlockSpec((1, tm, D_MODEL), lambda b, i: (b0 + b, i, 0)),
        out_shape=jax.ShapeDtypeStruct((bsz_total, t_len, D_MODEL), F32),
        input_output_aliases=aliases,
        compiler_params=_params("parallel", "arbitrary"),
        name="combine_final",
    )(*args)


def _t5_bucket(rel):
    half = NUM_BUCKETS // 2
    max_exact = half // 2
    ret = jnp.where(rel > 0, half, 0)
    n = jnp.abs(rel)
    nf = jnp.maximum(n, 1).astype(jnp.float32)
    large = max_exact + (jnp.log(nf / max_exact) / math.log(MAX_DISTANCE / max_exact)
                         * (half - max_exact)).astype(jnp.int32)
    large = jnp.minimum(large, half - 1)
    return ret + jnp.where(n < max_exact, n, large)


def _route(top_i, rank, sizes):
    n_tok = top_i.shape[0]
    n_asg = n_tok * TOP_K
    padded = (sizes + MOE_ROWS - 1) // MOE_ROWS * MOE_ROWS
    pend = jnp.cumsum(padded)
    pstart = pend - padded
    n_blocks = -(-n_asg // MOE_ROWS) + N_EXPERTS
    blk_row0 = jnp.arange(n_blocks, dtype=jnp.int32) * MOE_ROWS
    blk_expert = jnp.minimum(jnp.sum(pend[None, :] <= blk_row0[:, None], axis=1), N_EXPERTS - 1).astype(jnp.int32)
    blk_valid = jnp.clip(pstart[blk_expert] + sizes[blk_expert] - blk_row0, 0, MOE_ROWS).astype(jnp.int32)
    onehot = top_i[:, :, None] == jnp.arange(N_EXPERTS, dtype=jnp.int32)
    pos = jnp.sum(jnp.where(onehot, pstart, 0), axis=-1) + rank
    return pos, blk_expert, blk_valid, n_blocks * MOE_ROWS


def kernel(x_prompt, x_sample, c_prompt, c_sample, state_conv, state_delta, cache_swa_k, cache_swa_v,
           w_ada, b_ada, norm_mix, w_in, b_in, conv_w, a_log, dt_bias, dn_norm_w, sinks, rel_bias,
           w_out, b_out, norm_ffn, w_router, b_router, w1, b1, w2, b2, norm_final):
    depth = w_ada.shape[0]
    assert depth == 1, "the final norm is fused into the layer's combine step"
    bp, tp, _ = x_prompt.shape
    bs, ts, _ = x_sample.shape
    groups = [dict(x=x_prompt, c=c_prompt, hist=0), dict(x=x_sample, c=c_sample, hist=WIN_CHUNKS)]

    q_rel = jnp.arange(CHUNK)
    k_rel = jnp.arange(BAND) - WIN_CHUNKS * CHUNK
    bucket = _t5_bucket(k_rel[None, :] - q_rel[:, None]).astype(jnp.int32)
    bias = _rel_bias(bucket, rel_bias)

    nf_w = norm_final.reshape(1, D_MODEL)
    outs = {g: dict(conv=[], delta=[], k=[], v=[]) for g in range(2)}
    xs_cur = [x_prompt, x_sample]

    for l in range(depth):
        o1 = DN_CONV_DIM + DN_V_DIM
        o2 = o1 + 2 * DN_V_HEADS
        wl, bl = w_in[l], b_in[l]
        w_cat = jnp.concatenate(
            [wl[:, :o1], wl[:, o2:], wl[:, o1:o2], jnp.zeros((D_MODEL, GATE_LANES - 2 * DN_V_HEADS), F32)],
            axis=1).astype(BF16)
        b_cat = jnp.concatenate(
            [bl[:o1], bl[o2:], bl[o1:o2], jnp.zeros((GATE_LANES - 2 * DN_V_HEADS,), F32)]).reshape(1, PROJ_DIM)
        pad8 = jnp.zeros((DN_V_HEADS,), F32)
        padr = jnp.zeros((GATE_LANES - 2 * DN_V_HEADS,), F32)
        alog_l = jnp.concatenate([pad8, a_log[l], padr]).reshape(1, GATE_LANES)
        dtb_l = jnp.concatenate([pad8, dt_bias[l], padr]).reshape(1, GATE_LANES)
        wo1 = w_out[l][:DN_V_DIM].astype(BF16)
        wo2 = w_out[l][DN_V_DIM:].astype(BF16)
        wr = jnp.concatenate([w_router[l], jnp.zeros((D_MODEL, LANES - N_EXPERTS), F32)], axis=1)
        br = jnp.concatenate([b_router[l], jnp.full((LANES - N_EXPERTS,), NEG, F32)]).reshape(1, LANES)
        b1l = b1[l].reshape(N_EXPERTS, 1, 2 * D_FF)
        b2l = b2[l].reshape(N_EXPERTS, 1, D_MODEL)

        mod = _modulation(jnp.concatenate([c_prompt, c_sample], axis=0), w_ada[l], b_ada[l])
        att = []
        for gi, grp in enumerate(groups):
            x = xs_cur[gi]
            bsz, t_len, _ = x.shape
            tm = min(TOKEN_TILE, t_len)
            m = mod[:bp] if gi == 0 else mod[bp:]
            sh1, sc1, g1, sh2, sc2, g2 = [a.reshape(bsz, 1, D_MODEL) for a in jnp.split(m, 6, axis=-1)]
            if gi == 0:
                cprev = jnp.zeros((bsz, CONV_W - 1, DN_CONV_DIM), F32)
                s0 = jnp.zeros((bsz, DN_V_HEADS, DN_HEAD_DIM, DN_HEAD_DIM), F32)
            else:
                cprev = state_conv[l]
                s0 = state_delta[l]
            cprev8 = jnp.concatenate([jnp.zeros((bsz, 8 - (CONV_W - 1), DN_CONV_DIM), F32), cprev], axis=1)
            qkv, z, sq, sk, sv, ba, utail = _inproj(x, sc1, sh1, norm_mix[l].reshape(1, D_MODEL), w_cat, b_cat,
                                                    cprev8, conv_w[l], tm)
            if gi == 0:
                k_all, v_all = sk, sv
            else:
                k_all = jnp.concatenate([cache_swa_k[l].reshape(bsz, -1, SWA_KV_DIM), sk], axis=1)
                v_all = jnp.concatenate([cache_swa_v[l].reshape(bsz, -1, SWA_KV_DIM), sv], axis=1)
            swa_len = cache_swa_k.shape[2]
            o_dn, s_new = _deltanet(qkv, z, ba, s0, alog_l, dtb_l,
                                    jnp.tile(dn_norm_w[l], 2).reshape(1, 2 * DN_HEAD_DIM))
            sink_rows = jnp.repeat(sinks[l], CHUNK).reshape(SWA_KV_HEADS, SWA_GROUP * CHUNK, 1)
            o_swa = _swa(sq, k_all, v_all, bias.reshape(SWA_KV_HEADS, SWA_GROUP * CHUNK, BAND), sink_rows,
                         grp["hist"])
            conv_new = utail[:, 8 - (CONV_W - 1):]
            k_state = k_all[:, k_all.shape[1] - swa_len:].reshape(bsz, swa_len, SWA_KV_HEADS, SWA_HEAD_DIM)
            v_state = v_all[:, v_all.shape[1] - swa_len:].reshape(bsz, swa_len, SWA_KV_HEADS, SWA_HEAD_DIM)
            outs[gi]["conv"].append(conv_new)
            outs[gi]["delta"].append(s_new)
            outs[gi]["k"].append(k_state)
            outs[gi]["v"].append(v_state)
            att.append(dict(x=x, o_dn=o_dn, o_swa=o_swa, g1=g1, sc2=sc2, sh2=sh2, g2=g2, tm=tm,
                            bsz=bsz, t_len=t_len))

        half = bp // 2
        parts = [[(1, 0, bs), (0, 0, half)], [(0, half, bp - half)]] if half else [[(1, 0, bs), (0, 0, bp)]]
        work = []
        for part in parts:
            cnt = jnp.zeros((1, LANES), F32)
            n_part = sum(nseq * att[gi]["t_len"] for gi, _, nseq in part)
            h2_all = jnp.zeros((n_part, D_MODEL // 2), jnp.uint32) if len(part) > 1 else None
            segs, row0 = [], 0
            for gi, b0, nseq in part:
                a = att[gi]
                x1, h2_all, route, cnt = _outproj(a["o_dn"], a["o_swa"], a["x"], a["g1"], a["sc2"], a["sh2"],
                                                  wo1, wo2, b_out[l].reshape(1, D_MODEL),
                                                  norm_ffn[l].reshape(1, D_MODEL), wr, br, cnt, a["tm"], b0, nseq,
                                                  n_part, row0, h2_all)
                segs.append(dict(gi=gi, b0=b0, nseq=nseq, x1=x1, tg=route))
                row0 += nseq * a["t_len"]
            route_all = jnp.concatenate(
                [s["tg"].reshape(-1, LANES)[:, :ROUTE_GATE] for s in segs], axis=0).astype(jnp.int32)
            ti_all = route_all[:, ROUTE_EXPERT:ROUTE_EXPERT + TOP_K]
            rk_all = route_all[:, ROUTE_RANK:ROUTE_RANK + TOP_K]
            sizes = cnt[0, :N_EXPERTS].astype(jnp.int32)
            pos, blk_expert, blk_valid, n_rows = _route(ti_all, rk_all, sizes)
            xs = _scatter_rows(h2_all, pos, n_rows)
            work.append(dict(segs=segs, pos=pos, blk_expert=blk_expert, blk_valid=blk_valid, xs=xs,
                             n_tok=h2_all.shape[0]))
        for wk in work:
            wk["y"] = _moe_experts(wk["blk_expert"], wk["blk_valid"], wk["xs"], w1[l], b1l, w2[l], b2l)
        for wk in work:
            wk["yg"] = _gather_rows(wk["y"], wk["pos"].T.reshape(-1)).reshape(TOP_K, wk["n_tok"], D_MODEL // 2)
        y_out = [None, None]
        for wk in work:
            off = 0
            for s in wk["segs"]:
                a = att[s["gi"]]
                y_out[s["gi"]] = _final(s["x1"], wk["yg"], s["tg"], a["g2"], nf_w, a["tm"], off, s["b0"],
                                        a["bsz"], prev=y_out[s["gi"]])
                off += s["nseq"] * a["t_len"]
        xs_cur = y_out

    res = [xs_cur[0], xs_cur[1]]
    for gi in range(2):
        for name in ("conv", "delta", "k", "v"):
            res.append(jnp.stack(outs[gi][name]))
    return tuple(res)
```

```python
import functools
import math

import jax
import jax.numpy as jnp
from jax import lax
from jax.experimental import pallas as pl
from jax.experimental.pallas import tpu as pltpu
from jax.experimental.pallas import tpu_sc as plsc

F32 = jnp.float32
BF16 = jnp.bfloat16

D_MODEL = 1024
CHUNK = 64
EPS = 1e-6
DN_QK_HEADS = 4
DN_V_HEADS = 8
DN_HEAD_DIM = 64
DN_QK_DIM = DN_QK_HEADS * DN_HEAD_DIM
DN_V_DIM = DN_V_HEADS * DN_HEAD_DIM
DN_CONV_DIM = 2 * DN_QK_DIM + DN_V_DIM
CONV_W = 4
SWA_HEADS = 8
SWA_KV_HEADS = 2
SWA_HEAD_DIM = 64
SWA_GROUP = SWA_HEADS // SWA_KV_HEADS
SWA_Q_DIM = SWA_HEADS * SWA_HEAD_DIM
SWA_KV_DIM = SWA_KV_HEADS * SWA_HEAD_DIM
WINDOW = 128
WIN_CHUNKS = WINDOW // CHUNK
BAND = (WIN_CHUNKS + 1) * CHUNK
NUM_BUCKETS = 32
MAX_DISTANCE = 128
N_EXPERTS = 32
TOP_K = 4
D_FF = 1024
SWIGLU_ALPHA = 1.702
SWIGLU_LIMIT = 7.0

LANES = 128
GATE_LANES = LANES
PROJ_DIM = DN_CONV_DIM + DN_V_DIM + SWA_Q_DIM + 2 * SWA_KV_DIM + GATE_LANES
MOE_ROWS = 1024
MOE_PART_SPLIT = (3, 1)
MOE_FF_CHUNK = 1024
TOKEN_TILE = 512
DN_SEQS_PER_STEP = 8
SWA_CHUNKS_PER_STEP = 16
SC_WINDOW = 64
VMEM_LIMIT = 56 * 1024 * 1024
NEG = -1e30


def _params(*sem):
    return pltpu.CompilerParams(dimension_semantics=sem, vmem_limit_bytes=VMEM_LIMIT)


def _split2(a):
    hi = a.astype(BF16)
    lo = (a - hi.astype(F32)).astype(BF16)
    return hi, lo


def _dot(a, b):
    return jnp.dot(a, b, preferred_element_type=F32)


def _dot_x3(a, b):
    a1, a2 = _split2(a)
    b1, b2 = _split2(b)
    return _dot(a1, b1) + (_dot(a1, b2) + _dot(a2, b1))


def _dot_exact_lhs(l01, g):
    g1 = g.astype(BF16)
    r = g - g1.astype(F32)
    g2 = r.astype(BF16)
    g3 = (r - g2.astype(F32)).astype(BF16)
    return _dot(l01, g1) + (_dot(l01, g2) + _dot(l01, g3))


def _pack_bf16_pairs(x):
    w = x.shape[1] // 2
    bits = lax.bitcast_convert_type(x.astype(BF16).astype(F32), jnp.uint32)
    return (bits[:, w:] & jnp.uint32(0xFFFF0000)) | (bits[:, :w] >> 16)


def _unpack_bf16_pairs(p):
    lo = lax.bitcast_convert_type(p << 16, F32)
    hi = lax.bitcast_convert_type(p & jnp.uint32(0xFFFF0000), F32)
    return jnp.concatenate([lo, hi], axis=1)


def _silu(x):
    return x * jax.nn.sigmoid(x)


def _rms(x):
    return x * lax.rsqrt(jnp.mean(x * x, axis=-1, keepdims=True) + EPS)


def _mod_kernel(c_ref, w_ref, b_ref, o_ref):
    o_ref[...] = _dot_x3(_silu(c_ref[...]), w_ref[...]) + b_ref[...]


def _modulation(c, w_ada, b_ada):
    nb = c.shape[0]
    n_out = w_ada.shape[1]
    tn = 1024
    return pl.pallas_call(
        _mod_kernel,
        grid=(n_out // tn,),
        in_specs=[pl.BlockSpec((nb, D_MODEL), lambda j: (0, 0)),
                  pl.BlockSpec((D_MODEL, tn), lambda j: (0, j)),
                  pl.BlockSpec((1, tn), lambda j: (0, j))],
        out_specs=pl.BlockSpec((nb, tn), lambda j: (0, j)),
        out_shape=jax.ShapeDtypeStruct((nb, n_out), F32),
        compiler_params=_params("arbitrary"),
        name="modulation",
    )(c, w_ada, b_ada.reshape(1, n_out))


def _inproj_kernel(x_ref, sc_ref, sh_ref, nw_ref, w_ref, b_ref,
                   cprev_ref, cw_ref, qkv_ref, z_ref, sq_ref, sk_ref, sv_ref, ba_ref, utail_ref, xc_scr):
    tm = x_ref.shape[1]

    @pl.when(pl.program_id(1) == 0)
    def _():
        xc_scr[0:8, :] = cprev_ref[0]

    h = _rms(x_ref[0]) * nw_ref[...]
    h = h * (1.0 + sc_ref[0]) + sh_ref[0]
    p = _dot(h.astype(BF16), w_ref[...]) + b_ref[...]
    o = DN_CONV_DIM
    for ref in (z_ref, sq_ref, sk_ref, sv_ref, ba_ref):
        w = ref.shape[-1]
        ref[0] = p[:, o:o + w].astype(ref.dtype)
        o += w

    xc_scr[8:8 + tm, :] = p[:, :DN_CONV_DIM]
    conv = xc_scr[5:5 + tm, :] * cw_ref[0:1, :]
    for j in range(1, CONV_W):
        conv = conv + xc_scr[5 + j:5 + j + tm, :] * cw_ref[j:j + 1, :]
    tail = xc_scr[tm:tm + 8, :]
    xc_scr[0:8, :] = tail
    utail_ref[0] = tail
    cu = _silu(conv)

    pw = 2 * DN_HEAD_DIM
    lo = lax.broadcasted_iota(jnp.int32, (tm, pw), 1) < DN_HEAD_DIM
    for c in range(DN_QK_DIM // pw):
        qc = cu[:, c * pw:(c + 1) * pw]
        kc = cu[:, DN_QK_DIM + c * pw:DN_QK_DIM + (c + 1) * pw]
        qkv_ref[0, :, c * pw:(c + 1) * pw] = (
            qc * lax.rsqrt(_half_sums(qc * qc, lo) + EPS) * (DN_HEAD_DIM ** -0.5))
        qkv_ref[0, :, DN_QK_DIM + c * pw:DN_QK_DIM + (c + 1) * pw] = kc * lax.rsqrt(_half_sums(kc * kc, lo) + EPS)
    qkv_ref[0, :, 2 * DN_QK_DIM:] = cu[:, 2 * DN_QK_DIM:]


def _inproj(x, sc, sh, norm_w, w_cat, b_cat, cprev8, conv_w, tm):
    bsz, t_len, _ = x.shape
    widths = (DN_CONV_DIM, DN_V_DIM, SWA_Q_DIM, SWA_KV_DIM, SWA_KV_DIM, GATE_LANES)
    tok = lambda b, i: (b, i, 0)
    per_b = lambda b, i: (b, 0, 0)
    fixed = lambda b, i: (0, 0)
    out_shape = [jax.ShapeDtypeStruct((bsz, t_len, w), BF16 if i == 2 else F32) for i, w in enumerate(widths)]
    return pl.pallas_call(
        _inproj_kernel,
        grid=(bsz, t_len // tm),
        in_specs=[pl.BlockSpec((1, tm, D_MODEL), tok),
                  pl.BlockSpec((1, 1, D_MODEL), per_b),
                  pl.BlockSpec((1, 1, D_MODEL), per_b),
                  pl.BlockSpec((1, D_MODEL), fixed),
                  pl.BlockSpec((D_MODEL, PROJ_DIM), fixed),
                  pl.BlockSpec((1, PROJ_DIM), fixed),
                  pl.BlockSpec((1, 8, DN_CONV_DIM), per_b),
                  pl.BlockSpec((CONV_W, DN_CONV_DIM), fixed)],
        out_specs=[pl.BlockSpec((1, tm, w), tok) for w in widths] + [pl.BlockSpec((1, 8, DN_CONV_DIM), per_b)],
        out_shape=out_shape + [jax.ShapeDtypeStruct((bsz, 8, DN_CONV_DIM), F32)],
        scratch_shapes=[pltpu.VMEM((tm + 8, DN_CONV_DIM), F32)],
        compiler_params=_params("parallel", "arbitrary"),
        name="inproj",
    )(x, sc, sh, norm_w, w_cat, b_cat, cprev8, conv_w)


def _bmm(a, b):
    return lax.dot_general(a.astype(BF16), b.astype(BF16), (((2,), (1,)), ((0,), (0,))),
                           preferred_element_type=F32)


def _bmm_nt(a, b):
    return lax.dot_general(a.astype(BF16), b.astype(BF16), (((2,), (2,)), ((0,), (0,))),
                           preferred_element_type=F32)


def _bd(x):
    x = x.astype(BF16)
    lo = lax.broadcasted_iota(jnp.int32, x.shape, 2) < x.shape[2] // 2
    zero = jnp.zeros_like(x)
    return jnp.concatenate([jnp.where(lo, x, zero), jnp.where(lo, zero, x)], axis=1)


def _pmm(a, b):
    return _bmm(a, _bd(b))


def _half_sums(x, lo):
    s_lo = jnp.sum(jnp.where(lo, x, 0.0), axis=-1, keepdims=True)
    s_hi = jnp.sum(jnp.where(lo, 0.0, x), axis=-1, keepdims=True)
    return jnp.where(lo, s_lo, s_hi)


def _deltanet_pair_kernel(u_ref, z_ref, ba_ref, s0_ref, alog_ref, dtb_ref, nw_ref,
                          o_ref, sfin_ref, s_scr, *, nb):
    n = pl.program_id(1)
    hd = DN_HEAD_DIM
    pw = 2 * hd
    npair = DN_QK_HEADS

    @pl.when(n == 0)
    def _():
        for i in range(nb):
            for j in range(npair):
                s_scr[i * npair + j] = jnp.concatenate([s0_ref[i, 2 * j], s0_ref[i, 2 * j + 1]], axis=1)

    row = lax.broadcasted_iota(jnp.int32, (CHUNK, pw), 0)
    col = lax.broadcasted_iota(jnp.int32, (CHUNK, pw), 1) % hd
    lo = lax.broadcasted_iota(jnp.int32, (CHUNK, pw), 1) < hd
    incl = row >= col
    strict = row > col
    r1 = lax.broadcasted_iota(jnp.int32, (CHUNK, CHUNK), 0)
    c1 = lax.broadcasted_iota(jnp.int32, (CHUNK, CHUNK), 1)
    lower01 = jnp.where(r1 >= c1, 1.0, 0.0).astype(BF16)
    eye = jnp.where(row == col, 1.0, 0.0).astype(F32)
    merge_masks = [(row // 2 == col // 2) & strict]
    blk = 2
    while blk < CHUNK:
        merge_masks.append((row // (2 * blk) == col // (2 * blk)) & (row // blk != col // blk) & strict)
        blk *= 2

    qk_lhs, qk_rhs, q_items, k_items, v_items, z_items = [], [], [], [], [], []
    zt_items, grow_items = [], []
    for i in range(nb):
        ba = ba_ref[i]
        beta_all = jax.nn.sigmoid(ba)
        sp = ba + dtb_ref[...]
        sp = jnp.maximum(sp, 0.0) + jnp.log1p(jnp.exp(-jnp.abs(sp)))
        g_all = -jnp.exp(alog_ref[...]) * sp
        gc_all = _dot_exact_lhs(lower01, g_all)
        gc_t = gc_all.T
        beta_t = beta_all.T

        for c in range(DN_QK_DIM // pw):
            qc = u_ref[i, :, c * pw:(c + 1) * pw]
            kc = u_ref[i, :, DN_QK_DIM + c * pw:DN_QK_DIM + (c + 1) * pw]
            qr = pltpu.roll(qc, hd, axis=1)
            kr = pltpu.roll(kc, hd, axis=1)
            for half in range(2):
                sel = lo if half == 0 else jnp.logical_not(lo)
                q_items.append(jnp.where(sel, qc, qr))
                k_items.append(jnp.where(sel, kc, kr))
                km = jnp.where(sel, kc, 0.0)
                qk_lhs.append(jnp.concatenate([qc, kc], axis=0))
                qk_rhs.append(jnp.concatenate([km, km], axis=0))
        for j in range(npair):
            a, b = 2 * j, 2 * j + 1
            v_items.append(u_ref[i, :, 2 * DN_QK_DIM + j * pw:2 * DN_QK_DIM + (j + 1) * pw])
            z_items.append(z_ref[i, :, j * pw:(j + 1) * pw])
            ra = jnp.concatenate([beta_t[a:a + 1, :], gc_t[8 + a:9 + a, :]], axis=1)
            rb = jnp.concatenate([beta_t[b:b + 1, :], gc_t[8 + b:9 + b, :]], axis=1)
            zt_items.append(jnp.concatenate([jnp.broadcast_to(ra, (CHUNK, pw)),
                                             jnp.broadcast_to(rb, (CHUNK, pw))], axis=0).T)
            grow_items.append(jnp.broadcast_to(
                jnp.concatenate([gc_t[8 + a:9 + a, :], gc_t[8 + b:9 + b, :]], axis=1), (CHUNK, pw)))

    q = jnp.stack(q_items)
    k = jnp.stack(k_items)
    v = jnp.stack(v_items)
    cols = jnp.stack(zt_items)
    beta, gcol = cols[:, :CHUNK], cols[:, CHUNK:]
    grow = jnp.stack(grow_items)
    glast = gcol[:, CHUNK - 1:CHUNK, :]

    qkk = _bmm_nt(jnp.stack(qk_lhs), jnp.stack(qk_rhs))
    qk, kk = qkk[:, :CHUNK], qkk[:, CHUNK:]

    decay = jnp.exp(jnp.where(incl, gcol - grow, NEG))
    eg = jnp.exp(gcol)

    m = jnp.where(strict, (kk * beta) * decay, 0.0)
    t = eye - jnp.where(merge_masks[0], m, 0.0)
    for mask in merge_masks[1:]:
        t = t - _pmm(t, _pmm(jnp.where(mask, m, 0.0), t))
    t0 = t.astype(BF16)
    t0_bd = _bd(t0)
    m_hi, m_lo = _split2(m)
    mt0 = _bmm(jnp.concatenate([m_hi, m_lo], axis=2), jnp.concatenate([t0_bd, t0_bd], axis=1))
    resid = eye - t0.astype(F32) - mt0
    t1 = _pmm(t0, resid).astype(BF16)

    s = s_scr[...]
    ks = _pmm(jnp.concatenate([k * (beta * eg), q * eg], axis=1), s)
    rhs_bd = _bd(v * beta - ks[:, :CHUNK])
    vnew = _bmm(jnp.concatenate([t0, t1], axis=2), jnp.concatenate([rhs_bd, rhs_bd], axis=1))
    o = ks[:, CHUNK:] + _pmm(qk * decay, vnew)
    kd = k * jnp.exp(glast - gcol)
    kv = _bmm(jnp.swapaxes(kd, 1, 2), vnew)
    s_scr[...] = s * jnp.exp(glast) + jnp.where(lo, kv[:, :hd], kv[:, hd:])

    o = o * lax.rsqrt(_half_sums(o * o, lo) * (1.0 / hd) + EPS) * nw_ref[...] * _silu(jnp.stack(z_items))
    for i in range(nb):
        for j in range(npair):
            o_ref[i, :, j * pw:(j + 1) * pw] = o[i * npair + j].astype(o_ref.dtype)

    @pl.when(n == pl.num_programs(1) - 1)
    def _():
        for i in range(nb):
            for j in range(npair):
                sp2 = s_scr[i * npair + j]
                sfin_ref[i, 2 * j] = sp2[:, :hd]
                sfin_ref[i, 2 * j + 1] = sp2[:, hd:]


def _deltanet(qkv, z, ba, s0, alog_l, dtb_l, dn_norm_w):
    bsz, t_len, _ = qkv.shape
    nb = DN_SEQS_PER_STEP
    tok = lambda b, n: (b, n, 0)
    per_b4 = lambda b, n: (b, 0, 0, 0)
    fixed = lambda b, n: (0, 0)
    return pl.pallas_call(
        functools.partial(_deltanet_pair_kernel, nb=nb),
        grid=(bsz // nb, t_len // CHUNK),
        in_specs=[pl.BlockSpec((nb, CHUNK, DN_CONV_DIM), tok),
                  pl.BlockSpec((nb, CHUNK, DN_V_DIM), tok),
                  pl.BlockSpec((nb, CHUNK, GATE_LANES), tok),
                  pl.BlockSpec((nb, DN_V_HEADS, DN_HEAD_DIM, DN_HEAD_DIM), per_b4),
                  pl.BlockSpec((1, GATE_LANES), fixed),
                  pl.BlockSpec((1, GATE_LANES), fixed),
                  pl.BlockSpec((1, 2 * DN_HEAD_DIM), fixed)],
        out_specs=[pl.BlockSpec((nb, CHUNK, DN_V_DIM), tok),
                   pl.BlockSpec((nb, DN_V_HEADS, DN_HEAD_DIM, DN_HEAD_DIM), per_b4)],
        out_shape=[jax.ShapeDtypeStruct((bsz, t_len, DN_V_DIM), BF16),
                   jax.ShapeDtypeStruct((bsz, DN_V_HEADS, DN_HEAD_DIM, DN_HEAD_DIM), F32)],
        scratch_shapes=[pltpu.VMEM((nb * DN_QK_HEADS, DN_HEAD_DIM, 2 * DN_HEAD_DIM), F32)],
        compiler_params=_params("parallel", "arbitrary"),
        name="deltanet",
    )(qkv, z, ba, s0, alog_l, dtb_l, dn_norm_w)


def _bias_kernel(bucket_ref, table_ref, o_ref):
    bucket = bucket_ref[...]
    for h in range(SWA_HEADS):
        acc = jnp.zeros(bucket.shape, F32)
        for b in range(NUM_BUCKETS):
            acc = jnp.where(bucket == b, table_ref[b, h], acc)
        o_ref[h] = acc


def _rel_bias(bucket, table):
    return pl.pallas_call(
        _bias_kernel,
        in_specs=[pl.BlockSpec(memory_space=pltpu.VMEM),
                  pl.BlockSpec(memory_space=pltpu.SMEM)],
        out_specs=pl.BlockSpec(memory_space=pltpu.VMEM),
        out_shape=jax.ShapeDtypeStruct((SWA_HEADS, CHUNK, BAND), F32),
        name="rel_bias",
    )(bucket, table)


def _swa_kernel(q_ref, *refs, hist, cps, n_units):
    k_refs, v_refs = refs[:n_units], refs[n_units:2 * n_units]
    bias_ref, sink_ref, o_ref = refs[2 * n_units:]
    n = pl.program_id(1)
    hd = SWA_HEAD_DIM
    kb = jnp.concatenate([r[0] for r in k_refs], axis=0)
    vb = jnp.concatenate([r[0] for r in v_refs], axis=0)
    row0 = (n * cps + hist) * CHUNK - WINDOW
    key = lax.broadcasted_iota(jnp.int32, (1, 1, BAND), 2)
    q_items, k_items, v_items, valid = [], [], [], []
    for c in range(cps):
        for kv in range(SWA_KV_HEADS):
            q_items.append(jnp.concatenate(
                [q_ref[0, c * CHUNK:(c + 1) * CHUNK, (kv * SWA_GROUP + g) * hd:(kv * SWA_GROUP + g + 1) * hd]
                 for g in range(SWA_GROUP)], axis=0))
            k_items.append(kb[c * CHUNK:c * CHUNK + BAND, kv * hd:(kv + 1) * hd])
            v_items.append(vb[c * CHUNK:c * CHUNK + BAND, kv * hd:(kv + 1) * hd])
            valid.append(row0 + c * CHUNK + key >= 0)
    s = _bmm_nt(jnp.stack(q_items), jnp.stack(k_items)) * (hd ** -0.5) + bias_ref[...]
    s = jnp.where(jnp.concatenate(valid, axis=0), s, NEG)
    sink = sink_ref[...]
    mx = jnp.maximum(jnp.max(s, axis=-1, keepdims=True), sink)
    p = jnp.exp(s - mx).astype(BF16)
    den = _bmm(p, jnp.ones((len(v_items), BAND, hd), BF16)) + jnp.exp(sink - mx)
    o = _bmm(p, jnp.stack(v_items)) / den
    for c in range(cps):
        for kv in range(SWA_KV_HEADS):
            for g in range(SWA_GROUP):
                h = kv * SWA_GROUP + g
                o_ref[0, c * CHUNK:(c + 1) * CHUNK, h * hd:(h + 1) * hd] = (
                    o[c * SWA_KV_HEADS + kv, g * CHUNK:(g + 1) * CHUNK].astype(o_ref.dtype))


def _swa(q, k_all, v_all, bias, sink_rows, hist):
    bsz, t_len, _ = q.shape
    cps = min(SWA_CHUNKS_PER_STEP, t_len // CHUNK)
    unit = WINDOW if cps * CHUNK % WINDOW == 0 else CHUNK
    assert (hist * CHUNK - WINDOW) % unit == 0 and (cps * CHUNK) % unit == 0
    n_units = (WINDOW + cps * CHUNK) // unit
    q_units = cps * CHUNK // unit
    unit0 = (hist * CHUNK - WINDOW) // unit
    tok = lambda b, n: (b, n, 0)

    def band(j):
        return lambda b, n: (b, jnp.maximum(n * q_units + unit0 + j, 0), 0)

    kv_specs = [pl.BlockSpec((1, unit, SWA_KV_DIM), band(j)) for j in range(n_units)]
    fixed3 = lambda b, n: (0, 0, 0)
    return pl.pallas_call(
        functools.partial(_swa_kernel, hist=hist, cps=cps, n_units=n_units),
        grid=(bsz, t_len // (cps * CHUNK)),
        in_specs=[pl.BlockSpec((1, cps * CHUNK, SWA_Q_DIM), tok)] + kv_specs + kv_specs
                 + [pl.BlockSpec((cps * SWA_KV_HEADS, SWA_GROUP * CHUNK, BAND), fixed3),
                    pl.BlockSpec((cps * SWA_KV_HEADS, SWA_GROUP * CHUNK, 1), fixed3)],
        out_specs=pl.BlockSpec((1, cps * CHUNK, SWA_Q_DIM), tok),
        out_shape=jax.ShapeDtypeStruct((bsz, t_len, SWA_Q_DIM), BF16),
        compiler_params=_params("parallel", "arbitrary"),
        name="swa",
    )(q, *([k_all] * n_units), *([v_all] * n_units),
      jnp.tile(bias, (cps, 1, 1)), jnp.tile(sink_rows, (cps, 1, 1)))


ROUTE_EXPERT, ROUTE_RANK, ROUTE_GATE = 0, TOP_K, 2 * TOP_K


def _outproj_kernel(odn_ref, oswa_ref, x_ref, g1_ref, sc_ref, sh_ref, wo1_ref, wo2_ref, bo_ref,
                    nw_ref, wr_ref, br_ref, cnt0_ref, *rest):
    x1_ref, h2_ref, route_ref, cnt_ref, cnt_scr, tri_scr = rest[-6:]
    first = (pl.program_id(0) == 0) & (pl.program_id(1) == 0)

    @pl.when(first)
    def _():
        cnt_scr[...] = cnt0_ref[...]
        r_i = lax.broadcasted_iota(jnp.int32, tri_scr.shape, 0)
        c_i = lax.broadcasted_iota(jnp.int32, tri_scr.shape, 1)
        tri_scr[...] = jnp.where(r_i > c_i, 1.0, 0.0).astype(BF16)

    nsq, tm, _ = x_ref.shape
    rows = nsq * tm
    mix = (_dot(odn_ref[...].reshape(rows, DN_V_DIM).astype(BF16), wo1_ref[...])
           + _dot(oswa_ref[...].reshape(rows, SWA_Q_DIM).astype(BF16), wo2_ref[...]))
    x1 = x_ref[...] + g1_ref[...] * (mix + bo_ref[...]).reshape(nsq, tm, D_MODEL)
    x1_ref[...] = x1
    h2 = _rms(x1) * nw_ref[...]
    h2 = (h2 * (1.0 + sc_ref[...]) + sh_ref[...]).reshape(rows, D_MODEL)
    h2_ref[...] = _pack_bf16_pairs(h2)

    logits = _dot_x3(h2, wr_ref[...]) + br_ref[...]
    lane = lax.broadcasted_iota(jnp.int32, logits.shape, 1)
    lane_f = lane.astype(F32)
    vals, idxs = [], []
    for _ in range(TOP_K):
        m = jnp.max(logits, axis=-1, keepdims=True)
        i = jnp.min(jnp.where(logits == m, lane_f, float(LANES)), axis=-1, keepdims=True)
        vals.append(m)
        idxs.append(i)
        logits = jnp.where(lane_f == i, -jnp.inf, logits)
    es = [jnp.exp(v - vals[0]) for v in vals]
    den = es[0] + es[1] + es[2] + es[3]
    onehot = jnp.zeros(lane.shape, F32)
    for kk in range(TOP_K):
        onehot = jnp.where(lane_f == idxs[kk], 1.0, onehot)
    before = _dot(tri_scr[...], onehot.astype(BF16)) + cnt_scr[...]
    cnt_scr[...] = cnt_scr[...] + jnp.sum(onehot, axis=0, keepdims=True)

    route = jnp.zeros(lane.shape, F32)
    for kk in range(TOP_K):
        rank = jnp.sum(jnp.where(lane_f == idxs[kk], before, 0.0), axis=-1, keepdims=True)
        route = jnp.where(lane == ROUTE_EXPERT + kk, idxs[kk], route)
        route = jnp.where(lane == ROUTE_RANK + kk, rank, route)
        route = jnp.where(lane == ROUTE_GATE + kk, es[kk] / den, route)
    route_ref[...] = route.reshape(nsq, tm, LANES)
    cnt_ref[...] = cnt_scr[...]


def _seqs_per_tile(t_len, tm, bsz):
    nsq = max(1, TOKEN_TILE // t_len) if tm == t_len else 1
    while bsz % nsq:
        nsq -= 1
    return nsq


def _outproj(odn, oswa, x, g1, sc, sh, wo1, wo2, bo, norm_w, wr, br, cnt0, tm, b0, bsz, h2_rows, h2_row0, h2_prev):
    t_len = x.shape[1]
    steps = t_len // tm
    nsq = _seqs_per_tile(t_len, tm, bsz)
    rows = nsq * tm
    assert h2_row0 % rows == 0 and b0 % nsq == 0
    blk0 = h2_row0 // rows
    sb0 = b0 // nsq
    tok_out = lambda b, i: (b, i, 0)
    tok = lambda b, i: (sb0 + b, i, 0)
    per_b = lambda b, i: (sb0 + b, 0, 0)
    fixed = lambda b, i: (0, 0)
    in_specs = [pl.BlockSpec((nsq, tm, DN_V_DIM), tok),
                pl.BlockSpec((nsq, tm, SWA_Q_DIM), tok),
                pl.BlockSpec((nsq, tm, D_MODEL), tok),
                pl.BlockSpec((nsq, 1, D_MODEL), per_b),
                pl.BlockSpec((nsq, 1, D_MODEL), per_b),
                pl.BlockSpec((nsq, 1, D_MODEL), per_b),
                pl.BlockSpec((DN_V_DIM, D_MODEL), fixed),
                pl.BlockSpec((SWA_Q_DIM, D_MODEL), fixed),
                pl.BlockSpec((1, D_MODEL), fixed),
                pl.BlockSpec((1, D_MODEL), fixed),
                pl.BlockSpec((D_MODEL, LANES), fixed),
                pl.BlockSpec((1, LANES), fixed),
                pl.BlockSpec((1, LANES), fixed)]
    args = [odn, oswa, x, g1, sc, sh, wo1, wo2, bo, norm_w, wr, br, cnt0]
    aliases = {}
    if h2_prev is not None:
        in_specs.append(pl.BlockSpec(memory_space=pl.ANY))
        aliases = {len(args): 1}
        args.append(h2_prev)
    return pl.pallas_call(
        _outproj_kernel,
        grid=(bsz // nsq, steps),
        in_specs=in_specs,
        out_specs=[pl.BlockSpec((nsq, tm, D_MODEL), tok_out),
                   pl.BlockSpec((rows, D_MODEL // 2), lambda b, i: (blk0 + b * steps + i, 0)),
                   pl.BlockSpec((nsq, tm, LANES), tok_out),
                   pl.BlockSpec((1, LANES), fixed)],
        out_shape=[jax.ShapeDtypeStruct((bsz, t_len, D_MODEL), F32),
                   jax.ShapeDtypeStruct((h2_rows, D_MODEL // 2), jnp.uint32),
                   jax.ShapeDtypeStruct((bsz, t_len, LANES), F32),
                   jax.ShapeDtypeStruct((1, LANES), F32)],
        scratch_shapes=[pltpu.VMEM((1, LANES), F32), pltpu.VMEM((rows, rows), BF16)],
        input_output_aliases=aliases,
        compiler_params=_params("arbitrary", "arbitrary"),
        name="outproj_router",
    )(*args)


def _moe_kernel(be_ref, nv_ref, xs_ref, w1_ref, b1_ref, w2_ref, b2_ref, y_ref):
    del be_ref
    nv = nv_ref[pl.program_id(0)]

    def ffn(n_rows):
        rows = lax.broadcasted_iota(jnp.int32, (n_rows, 1), 0)
        xb = jnp.where(rows < nv, _unpack_bf16_pairs(xs_ref[0:n_rows, :]), 0.0).astype(BF16)
        y = b2_ref[0]
        for c in range(D_FF // MOE_FF_CHUNK):
            lo, hi = c * MOE_FF_CHUNK, (c + 1) * MOE_FF_CHUNK
            glu = _dot(xb.astype(F32), w1_ref[0, :, lo:hi]) + b1_ref[0, :, lo:hi]
            lin = _dot(xb.astype(F32), w1_ref[0, :, D_FF + lo:D_FF + hi]) + b1_ref[0, :, D_FF + lo:D_FF + hi]
            glu = jnp.minimum(glu, SWIGLU_LIMIT)
            lin = jnp.clip(lin, -SWIGLU_LIMIT, SWIGLU_LIMIT)
            act = glu * jax.nn.sigmoid(SWIGLU_ALPHA * glu) * (lin + 1.0)
            y = y + _dot(act.astype(BF16).astype(F32), w2_ref[0, lo:hi, :])
        y_ref[0:n_rows, :] = _pack_bf16_pairs(y)

    half = MOE_ROWS // 2

    @pl.when(nv == 0)
    def _():
        y_ref[...] = jnp.zeros(y_ref.shape, y_ref.dtype)

    @pl.when((nv > 0) & (nv <= half))
    def _():
        ffn(half)
        y_ref[half:, :] = jnp.zeros((MOE_ROWS - half, y_ref.shape[1]), y_ref.dtype)

    @pl.when(nv > half)
    def _():
        ffn(MOE_ROWS)


def _moe_experts(blk_expert, blk_valid, xs, w1, b1, w2, b2):
    n_rows = xs.shape[0]
    n_blocks = n_rows // MOE_ROWS
    half = D_MODEL // 2
    grid_spec = pltpu.PrefetchScalarGridSpec(
        num_scalar_prefetch=2,
        grid=(n_blocks,),
        in_specs=[pl.BlockSpec((MOE_ROWS, half), lambda i, be, nv: (i, 0)),
                  pl.BlockSpec((1, D_MODEL, 2 * D_FF), lambda i, be, nv: (be[i], 0, 0)),
                  pl.BlockSpec((1, 1, 2 * D_FF), lambda i, be, nv: (be[i], 0, 0)),
                  pl.BlockSpec((1, D_FF, D_MODEL), lambda i, be, nv: (be[i], 0, 0)),
                  pl.BlockSpec((1, 1, D_MODEL), lambda i, be, nv: (be[i], 0, 0))],
        out_specs=pl.BlockSpec((MOE_ROWS, half), lambda i, be, nv: (i, 0)),
    )
    return pl.pallas_call(
        _moe_kernel,
        grid_spec=grid_spec,
        out_shape=jax.ShapeDtypeStruct((n_rows, half), jnp.uint32),
        compiler_params=_params("arbitrary"),
        name="moe_experts",
    )(blk_expert, blk_valid, xs, w1, b1, w2, b2)


def _sc_mesh():
    return plsc.VectorSubcoreMesh(core_axis_name="core", subcore_axis_name="subcore")


def _gather_rows(x, idx):
    m = idx.shape[0]
    w = x.shape[1]

    @pl.kernel(out_type=jax.ShapeDtypeStruct((m, w), x.dtype), mesh=_sc_mesh())
    def gather_kernel(x_hbm, i_hbm, o_hbm):
        def body(i_vmem, o_vmem):
            pltpu.sync_copy(x_hbm.at[i_vmem.at[0]], o_vmem)

        pltpu.emit_pipeline(
            body,
            grid=(m // SC_WINDOW,),
            in_specs=[pl.BlockSpec((1, SC_WINDOW), lambda i: (i, 0))],
            out_specs=[pl.BlockSpec((SC_WINDOW, w), lambda i: (i, 0))],
            core_axis_name=("core", "subcore"),
            dimension_semantics=(pltpu.PARALLEL,),
        )(i_hbm, o_hbm)

    return gather_kernel(x, idx.reshape(m // SC_WINDOW, SC_WINDOW))


def _scatter_rows(x, idx, n_out):
    n, w = x.shape
    kk = idx.shape[1]
    idx3 = jnp.transpose(idx.reshape(n // SC_WINDOW, SC_WINDOW, kk), (0, 2, 1))

    @pl.kernel(out_type=jax.ShapeDtypeStruct((n_out, w), x.dtype), mesh=_sc_mesh())
    def scatter_kernel(x_hbm, i_hbm, o_hbm):
        def body(x_vmem, i_vmem):
            for k in range(kk):
                pltpu.sync_copy(x_vmem, o_hbm.at[i_vmem.at[0, k]])

        pltpu.emit_pipeline(
            body,
            grid=(n // SC_WINDOW,),
            in_specs=[pl.BlockSpec((SC_WINDOW, w), lambda i: (i, 0)),
                      pl.BlockSpec((1, kk, SC_WINDOW), lambda i: (i, 0, 0))],
            out_specs=[],
            core_axis_name=("core", "subcore"),
            dimension_semantics=(pltpu.PARALLEL,),
        )(x_hbm, i_hbm)

    return scatter_kernel(x, idx3)


def _final_kernel(x1_ref, yg_ref, tg_ref, g2_ref, nw_ref, *rest):
    y_ref = rest[-1]
    nsq, tm, _ = x1_ref.shape
    tg = tg_ref[...].reshape(nsq * tm, LANES)
    moe = _unpack_bf16_pairs(yg_ref[0]) * tg[:, ROUTE_GATE:ROUTE_GATE + 1]
    for kk in range(1, TOP_K):
        moe = moe + _unpack_bf16_pairs(yg_ref[kk]) * tg[:, ROUTE_GATE + kk:ROUTE_GATE + kk + 1]
    x2 = x1_ref[...] + g2_ref[...] * moe.reshape(nsq, tm, D_MODEL)
    y_ref[...] = _rms(x2) * nw_ref[...]


def _final(x1, yg, tg, g2, norm_w, tm, tok_offset, b0, bsz_total, prev=None):
    bsz, t_len, _ = x1.shape
    tok = lambda b, i: (b, i, 0)
    steps = t_len // tm
    nsq = _seqs_per_tile(t_len, tm, bsz)
    rows = nsq * tm
    assert tok_offset % rows == 0 and b0 % nsq == 0
    blk0 = tok_offset // rows
    sb0 = b0 // nsq
    in_specs = [pl.BlockSpec((nsq, tm, D_MODEL), tok),
                pl.BlockSpec((TOP_K, rows, D_MODEL // 2), lambda b, i: (0, blk0 + b * steps + i, 0)),
                pl.BlockSpec((nsq, tm, LANES), tok),
                pl.BlockSpec((nsq, 1, D_MODEL), lambda b, i: (sb0 + b, 0, 0)),
                pl.BlockSpec((1, D_MODEL), lambda b, i: (0, 0))]
    args = [x1, yg, tg, g2, norm_w]
    aliases = {}
    if prev is not None:
        in_specs.append(pl.BlockSpec(memory_space=pl.ANY))
        aliases = {len(args): 0}
        args.append(prev)
    return pl.pallas_call(
        _final_kernel,
        grid=(bsz // nsq, steps),
        in_specs=in_specs,
        out_specs=pl.BlockSpec((nsq, tm, D_MODEL), lambda b, i: (sb0 + b, i, 0)),
        out_shape=jax.ShapeDtypeStruct((bsz_total, t_len, D_MODEL), F32),
        input_output_aliases=aliases,
        compiler_params=_params("parallel", "arbitrary"),
        name="combine_final",
    )(*args)


def _t5_bucket(rel):
    half = NUM_BUCKETS // 2
    max_exact = half // 2
    ret = jnp.where(rel > 0, half, 0)
    n = jnp.abs(rel)
    nf = jnp.maximum(n, 1).astype(jnp.float32)
    large = max_exact + (jnp.log(nf / max_exact) / math.log(MAX_DISTANCE / max_exact)
                         * (half - max_exact)).astype(jnp.int32)
    large = jnp.minimum(large, half - 1)
    return ret + jnp.where(n < max_exact, n, large)


def _route(top_i, rank, sizes):
    n_tok = top_i.shape[0]
    n_asg = n_tok * TOP_K
    padded = (sizes + MOE_ROWS - 1) // MOE_ROWS * MOE_ROWS
    pend = jnp.cumsum(padded)
    pstart = pend - padded
    n_blocks = -(-n_asg // MOE_ROWS) + N_EXPERTS
    blk_row0 = jnp.arange(n_blocks, dtype=jnp.int32) * MOE_ROWS
    blk_expert = jnp.minimum(jnp.sum(pend[None, :] <= blk_row0[:, None], axis=1), N_EXPERTS - 1).astype(jnp.int32)
    blk_valid = jnp.clip(pstart[blk_expert] + sizes[blk_expert] - blk_row0, 0, MOE_ROWS).astype(jnp.int32)
    onehot = top_i[:, :, None] == jnp.arange(N_EXPERTS, dtype=jnp.int32)
    pos = jnp.sum(jnp.where(onehot, pstart, 0), axis=-1) + rank
    return pos, blk_expert, blk_valid, n_blocks * MOE_ROWS


def kernel(x_prompt, x_sample, c_prompt, c_sample, state_conv, state_delta, cache_swa_k, cache_swa_v,
           w_ada, b_ada, norm_mix, w_in, b_in, conv_w, a_log, dt_bias, dn_norm_w, sinks, rel_bias,
           w_out, b_out, norm_ffn, w_router, b_router, w1, b1, w2, b2, norm_final):
    depth = w_ada.shape[0]
    assert depth == 1, "the final norm is fused into the layer's combine step"
    bp, tp, _ = x_prompt.shape
    bs, ts, _ = x_sample.shape
    groups = [dict(x=x_prompt, c=c_prompt, hist=0), dict(x=x_sample, c=c_sample, hist=WIN_CHUNKS)]

    q_rel = jnp.arange(CHUNK)
    k_rel = jnp.arange(BAND) - WIN_CHUNKS * CHUNK
    bucket = _t5_bucket(k_rel[None, :] - q_rel[:, None]).astype(jnp.int32)
    bias = _rel_bias(bucket, rel_bias)

    nf_w = norm_final.reshape(1, D_MODEL)
    outs = {g: dict(conv=[], delta=[], k=[], v=[]) for g in range(2)}
    xs_cur = [x_prompt, x_sample]

    for l in range(depth):
        o1 = DN_CONV_DIM + DN_V_DIM
        o2 = o1 + 2 * DN_V_HEADS
        wl, bl = w_in[l], b_in[l]
        w_cat = jnp.concatenate(
            [wl[:, :o1], wl[:, o2:], wl[:, o1:o2], jnp.zeros((D_MODEL, GATE_LANES - 2 * DN_V_HEADS), F32)],
            axis=1).astype(BF16)
        b_cat = jnp.concatenate(
            [bl[:o1], bl[o2:], bl[o1:o2], jnp.zeros((GATE_LANES - 2 * DN_V_HEADS,), F32)]).reshape(1, PROJ_DIM)
        pad8 = jnp.zeros((DN_V_HEADS,), F32)
        padr = jnp.zeros((GATE_LANES - 2 * DN_V_HEADS,), F32)
        alog_l = jnp.concatenate([pad8, a_log[l], padr]).reshape(1, GATE_LANES)
        dtb_l = jnp.concatenate([pad8, dt_bias[l], padr]).reshape(1, GATE_LANES)
        wo1 = w_out[l][:DN_V_DIM].astype(BF16)
        wo2 = w_out[l][DN_V_DIM:].astype(BF16)
        wr = jnp.concatenate([w_router[l], jnp.zeros((D_MODEL, LANES - N_EXPERTS), F32)], axis=1)
        br = jnp.concatenate([b_router[l], jnp.full((LANES - N_EXPERTS,), NEG, F32)]).reshape(1, LANES)
        b1l = b1[l].reshape(N_EXPERTS, 1, 2 * D_FF)
        b2l = b2[l].reshape(N_EXPERTS, 1, D_MODEL)

        mod = _modulation(jnp.concatenate([c_prompt, c_sample], axis=0), w_ada[l], b_ada[l])
        att = []
        for gi, grp in enumerate(groups):
            x = xs_cur[gi]
            bsz, t_len, _ = x.shape
            tm = min(TOKEN_TILE, t_len)
            m = mod[:bp] if gi == 0 else mod[bp:]
            sh1, sc1, g1, sh2, sc2, g2 = [a.reshape(bsz, 1, D_MODEL) for a in jnp.split(m, 6, axis=-1)]
            if gi == 0:
                cprev = jnp.zeros((bsz, CONV_W - 1, DN_CONV_DIM), F32)
                s0 = jnp.zeros((bsz, DN_V_HEADS, DN_HEAD_DIM, DN_HEAD_DIM), F32)
            else:
                cprev = state_conv[l]
                s0 = state_delta[l]
            cprev8 = jnp.concatenate([jnp.zeros((bsz, 8 - (CONV_W - 1), DN_CONV_DIM), F32), cprev], axis=1)
            qkv, z, sq, sk, sv, ba, utail = _inproj(x, sc1, sh1, norm_mix[l].reshape(1, D_MODEL), w_cat, b_cat,
                                                    cprev8, conv_w[l], tm)
            if gi == 0:
                k_all, v_all = sk, sv
            else:
                k_all = jnp.concatenate([cache_swa_k[l].reshape(bsz, -1, SWA_KV_DIM), sk], axis=1)
                v_all = jnp.concatenate([cache_swa_v[l].reshape(bsz, -1, SWA_KV_DIM), sv], axis=1)
            swa_len = cache_swa_k.shape[2]
            o_dn, s_new = _deltanet(qkv, z, ba, s0, alog_l, dtb_l,
                                    jnp.tile(dn_norm_w[l], 2).reshape(1, 2 * DN_HEAD_DIM))
            sink_rows = jnp.repeat(sinks[l], CHUNK).reshape(SWA_KV_HEADS, SWA_GROUP * CHUNK, 1)
            o_swa = _swa(sq, k_all, v_all, bias.reshape(SWA_KV_HEADS, SWA_GROUP * CHUNK, BAND), sink_rows,
                         grp["hist"])
            conv_new = utail[:, 8 - (CONV_W - 1):]
            k_state = k_all[:, k_all.shape[1] - swa_len:].reshape(bsz, swa_len, SWA_KV_HEADS, SWA_HEAD_DIM)
            v_state = v_all[:, v_all.shape[1] - swa_len:].reshape(bsz, swa_len, SWA_KV_HEADS, SWA_HEAD_DIM)
            outs[gi]["conv"].append(conv_new)
            outs[gi]["delta"].append(s_new)
            outs[gi]["k"].append(k_state)
            outs[gi]["v"].append(v_state)
            att.append(dict(x=x, o_dn=o_dn, o_swa=o_swa, g1=g1, sc2=sc2, sh2=sh2, g2=g2, tm=tm,
                            bsz=bsz, t_len=t_len))

        n_first = bp * (MOE_PART_SPLIT[0]) // sum(MOE_PART_SPLIT)
        parts = ([[(1, 0, bs), (0, 0, n_first)], [(0, n_first, bp - n_first)]] if 0 < n_first < bp
                 else [[(1, 0, bs), (0, 0, bp)]])
        work = []
        for part in parts:
            cnt = jnp.zeros((1, LANES), F32)
            n_part = sum(nseq * att[gi]["t_len"] for gi, _, nseq in part)
            h2_all = jnp.zeros((n_part, D_MODEL // 2), jnp.uint32) if len(part) > 1 else None
            segs, row0 = [], 0
            for gi, b0, nseq in part:
                a = att[gi]
                x1, h2_all, route, cnt = _outproj(a["o_dn"], a["o_swa"], a["x"], a["g1"], a["sc2"], a["sh2"],
                                                  wo1, wo2, b_out[l].reshape(1, D_MODEL),
                                                  norm_ffn[l].reshape(1, D_MODEL), wr, br, cnt, a["tm"], b0, nseq,
                                                  n_part, row0, h2_all)
                segs.append(dict(gi=gi, b0=b0, nseq=nseq, x1=x1, tg=route))
                row0 += nseq * a["t_len"]
            route_all = jnp.concatenate(
                [s["tg"].reshape(-1, LANES)[:, :ROUTE_GATE] for s in segs], axis=0).astype(jnp.int32)
            ti_all = route_all[:, ROUTE_EXPERT:ROUTE_EXPERT + TOP_K]
            rk_all = route_all[:, ROUTE_RANK:ROUTE_RANK + TOP_K]
            sizes = cnt[0, :N_EXPERTS].astype(jnp.int32)
            pos, blk_expert, blk_valid, n_rows = _route(ti_all, rk_all, sizes)
            xs = _scatter_rows(h2_all, pos, n_rows)
            work.append(dict(segs=segs, pos=pos, blk_expert=blk_expert, blk_valid=blk_valid, xs=xs,
                             n_tok=h2_all.shape[0]))
        for wk in work:
            wk["y"] = _moe_experts(wk["blk_expert"], wk["blk_valid"], wk["xs"], w1[l], b1l, w2[l], b2l)
        for wk in work:
            wk["yg"] = _gather_rows(wk["y"], wk["pos"].T.reshape(-1)).reshape(TOP_K, wk["n_tok"], D_MODEL // 2)
        y_out = [None, None]
        for wk in work:
            off = 0
            for s in wk["segs"]:
                a = att[s["gi"]]
                y_out[s["gi"]] = _final(s["x1"], wk["yg"], s["tg"], a["g2"], nf_w, a["tm"], off, s["b0"],
                                        a["bsz"], prev=y_out[s["gi"]])
                off += s["nseq"] * a["t_len"]
        xs_cur = y_out

    res = [xs_cur[0], xs_cur[1]]
    for gi in range(2):
        for name in ("conv", "delta", "k", "v"):
            res.append(jnp.stack(outs[gi][name]))
    return tuple(res)
```

```python
import functools
import math

import jax
import jax.numpy as jnp
from jax import lax
from jax.experimental import pallas as pl
from jax.experimental.pallas import tpu as pltpu
from jax.experimental.pallas import tpu_sc as plsc

F32 = jnp.float32
BF16 = jnp.bfloat16

D_MODEL = 1024
CHUNK = 64
EPS = 1e-6
DN_QK_HEADS = 4
DN_V_HEADS = 8
DN_HEAD_DIM = 64
DN_QK_DIM = DN_QK_HEADS * DN_HEAD_DIM
DN_V_DIM = DN_V_HEADS * DN_HEAD_DIM
DN_CONV_DIM = 2 * DN_QK_DIM + DN_V_DIM
CONV_W = 4
SWA_HEADS = 8
SWA_KV_HEADS = 2
SWA_HEAD_DIM = 64
SWA_GROUP = SWA_HEADS // SWA_KV_HEADS
SWA_Q_DIM = SWA_HEADS * SWA_HEAD_DIM
SWA_KV_DIM = SWA_KV_HEADS * SWA_HEAD_DIM
WINDOW = 128
WIN_CHUNKS = WINDOW // CHUNK
BAND = (WIN_CHUNKS + 1) * CHUNK
NUM_BUCKETS = 32
MAX_DISTANCE = 128
N_EXPERTS = 32
TOP_K = 4
D_FF = 1024
SWIGLU_ALPHA = 1.702
SWIGLU_LIMIT = 7.0

LANES = 128
GATE_LANES = LANES
PROJ_DIM = DN_CONV_DIM + DN_V_DIM + SWA_Q_DIM + 2 * SWA_KV_DIM + GATE_LANES
MOE_ROWS = 1024
FINAL_TILE = 1024
MOE_PART_SPLIT = (1, 1)
TOKEN_TILE = 512
DN_SEQS_PER_STEP = 8
SWA_CHUNKS_PER_STEP = 16
SC_WINDOW = 64
V7X_VMEM_BYTES = 64 * 1024 * 1024
VMEM_LIMIT = V7X_VMEM_BYTES - 8 * 1024 * 1024
NEG = -1e30


def _params(*sem):
    return pltpu.CompilerParams(dimension_semantics=sem, vmem_limit_bytes=VMEM_LIMIT)


def _split2(a):
    hi = a.astype(BF16)
    lo = (a - hi.astype(F32)).astype(BF16)
    return hi, lo


def _dot(a, b):
    return jnp.dot(a, b, preferred_element_type=F32)


def _dot_x3(a, b):
    a1, a2 = _split2(a)
    b1, b2 = _split2(b)
    return _dot(a1, b1) + (_dot(a1, b2) + _dot(a2, b1))


def _dot_exact_lhs(l01, g):
    g1 = g.astype(BF16)
    r = g - g1.astype(F32)
    g2 = r.astype(BF16)
    g3 = (r - g2.astype(F32)).astype(BF16)
    return _dot(l01, g1) + (_dot(l01, g2) + _dot(l01, g3))


def _pack_bf16_pairs(x):
    w = x.shape[1] // 2
    bits = lax.bitcast_convert_type(x.astype(BF16).astype(F32), jnp.uint32)
    return (bits[:, w:] & jnp.uint32(0xFFFF0000)) | (bits[:, :w] >> 16)


def _unpack_bf16_pairs(p):
    lo = lax.bitcast_convert_type(p << 16, F32)
    hi = lax.bitcast_convert_type(p & jnp.uint32(0xFFFF0000), F32)
    return jnp.concatenate([lo, hi], axis=1)


def _silu(x):
    return x * jax.nn.sigmoid(x)


def _rms(x):
    return x * lax.rsqrt(jnp.mean(x * x, axis=-1, keepdims=True) + EPS)


def _mod_kernel(c_ref, w_ref, b_ref, o_ref):
    o_ref[...] = _dot_x3(_silu(c_ref[...]), w_ref[...]) + b_ref[...]


def _modulation(c, w_ada, b_ada):
    nb = c.shape[0]
    n_out = w_ada.shape[1]
    tn = 1024
    return pl.pallas_call(
        _mod_kernel,
        grid=(n_out // tn,),
        in_specs=[pl.BlockSpec((nb, D_MODEL), lambda j: (0, 0)),
                  pl.BlockSpec((D_MODEL, tn), lambda j: (0, j)),
                  pl.BlockSpec((1, tn), lambda j: (0, j))],
        out_specs=pl.BlockSpec((nb, tn), lambda j: (0, j)),
        out_shape=jax.ShapeDtypeStruct((nb, n_out), F32),
        compiler_params=_params("arbitrary"),
        name="modulation",
    )(c, w_ada, b_ada.reshape(1, n_out))


def _inproj_kernel(x_ref, sc_ref, sh_ref, nw_ref, w_ref, b_ref,
                   cprev_ref, cw_ref, qkv_ref, z_ref, sq_ref, sk_ref, sv_ref, ba_ref, utail_ref, xc_scr):
    nsq, tm, _ = x_ref.shape
    rows = nsq * tm

    @pl.when(pl.program_id(1) == 0)
    def _():
        xc_scr[:, 0:8, :] = cprev_ref[...]

    h = _rms(x_ref[...]) * nw_ref[...]
    h = (h * (1.0 + sc_ref[...]) + sh_ref[...]).reshape(rows, D_MODEL)
    p = _dot(h.astype(BF16), w_ref[...]) + b_ref[...]
    o = DN_CONV_DIM
    for ref in (z_ref, sq_ref, sk_ref, sv_ref, ba_ref):
        w = ref.shape[-1]
        ref[...] = p[:, o:o + w].astype(ref.dtype).reshape(nsq, tm, w)
        o += w

    xc_scr[:, 8:8 + tm, :] = p[:, :DN_CONV_DIM].reshape(nsq, tm, DN_CONV_DIM)
    conv = xc_scr[:, 5:5 + tm, :] * cw_ref[0:1, :]
    for j in range(1, CONV_W):
        conv = conv + xc_scr[:, 5 + j:5 + j + tm, :] * cw_ref[j:j + 1, :]
    tail = xc_scr[:, tm:tm + 8, :]
    xc_scr[:, 0:8, :] = tail
    utail_ref[...] = tail
    cu = _silu(conv)

    pw = 2 * DN_HEAD_DIM
    lo = lax.broadcasted_iota(jnp.int32, (nsq, tm, pw), 2) < DN_HEAD_DIM
    for c in range(DN_QK_DIM // pw):
        qc = cu[:, :, c * pw:(c + 1) * pw]
        kc = cu[:, :, DN_QK_DIM + c * pw:DN_QK_DIM + (c + 1) * pw]
        qkv_ref[:, :, c * pw:(c + 1) * pw] = (
            qc * lax.rsqrt(_half_sums(qc * qc, lo) + EPS) * (DN_HEAD_DIM ** -0.5))
        qkv_ref[:, :, DN_QK_DIM + c * pw:DN_QK_DIM + (c + 1) * pw] = kc * lax.rsqrt(_half_sums(kc * kc, lo) + EPS)
    qkv_ref[:, :, 2 * DN_QK_DIM:] = cu[:, :, 2 * DN_QK_DIM:]


def _inproj(x, sc, sh, norm_w, w_cat, b_cat, cprev8, conv_w, tm):
    bsz, t_len, _ = x.shape
    widths = (DN_CONV_DIM, DN_V_DIM, SWA_Q_DIM, SWA_KV_DIM, SWA_KV_DIM, GATE_LANES)
    nsq = _seqs_per_tile(t_len, tm, bsz)
    tok = lambda b, i: (b, i, 0)
    per_b = lambda b, i: (b, 0, 0)
    fixed = lambda b, i: (0, 0)
    out_shape = [jax.ShapeDtypeStruct((bsz, t_len, w), BF16 if i == 2 else F32) for i, w in enumerate(widths)]
    return pl.pallas_call(
        _inproj_kernel,
        grid=(bsz // nsq, t_len // tm),
        in_specs=[pl.BlockSpec((nsq, tm, D_MODEL), tok),
                  pl.BlockSpec((nsq, 1, D_MODEL), per_b),
                  pl.BlockSpec((nsq, 1, D_MODEL), per_b),
                  pl.BlockSpec((1, D_MODEL), fixed),
                  pl.BlockSpec((D_MODEL, PROJ_DIM), fixed),
                  pl.BlockSpec((1, PROJ_DIM), fixed),
                  pl.BlockSpec((nsq, 8, DN_CONV_DIM), per_b),
                  pl.BlockSpec((CONV_W, DN_CONV_DIM), fixed)],
        out_specs=([pl.BlockSpec((nsq, tm, w), tok) for w in widths]
                   + [pl.BlockSpec((nsq, 8, DN_CONV_DIM), per_b)]),
        out_shape=out_shape + [jax.ShapeDtypeStruct((bsz, 8, DN_CONV_DIM), F32)],
        scratch_shapes=[pltpu.VMEM((nsq, tm + 8, DN_CONV_DIM), F32)],
        compiler_params=_params("parallel", "arbitrary"),
        name="inproj",
    )(x, sc, sh, norm_w, w_cat, b_cat, cprev8, conv_w)


def _bmm(a, b):
    return lax.dot_general(a.astype(BF16), b.astype(BF16), (((2,), (1,)), ((0,), (0,))),
                           preferred_element_type=F32)


def _bmm_nt(a, b):
    return lax.dot_general(a.astype(BF16), b.astype(BF16), (((2,), (2,)), ((0,), (0,))),
                           preferred_element_type=F32)


def _bd(x):
    x = x.astype(BF16)
    lo = lax.broadcasted_iota(jnp.int32, x.shape, 2) < x.shape[2] // 2
    zero = jnp.zeros_like(x)
    return jnp.concatenate([jnp.where(lo, x, zero), jnp.where(lo, zero, x)], axis=1)


def _pmm(a, b):
    return _bmm(a, _bd(b))


def _half_sums(x, lo):
    s_lo = jnp.sum(jnp.where(lo, x, 0.0), axis=-1, keepdims=True)
    s_hi = jnp.sum(jnp.where(lo, 0.0, x), axis=-1, keepdims=True)
    return jnp.where(lo, s_lo, s_hi)


def _deltanet_pair_kernel(u_ref, z_ref, ba_ref, s0_ref, alog_ref, dtb_ref, nw_ref,
                          o_ref, sfin_ref, s_scr, *, nb):
    n = pl.program_id(1)
    hd = DN_HEAD_DIM
    pw = 2 * hd
    npair = DN_QK_HEADS

    @pl.when(n == 0)
    def _():
        for i in range(nb):
            for j in range(npair):
                s_scr[i * npair + j] = jnp.concatenate([s0_ref[i, 2 * j], s0_ref[i, 2 * j + 1]], axis=1)

    row = lax.broadcasted_iota(jnp.int32, (CHUNK, pw), 0)
    col = lax.broadcasted_iota(jnp.int32, (CHUNK, pw), 1) % hd
    lo = lax.broadcasted_iota(jnp.int32, (CHUNK, pw), 1) < hd
    incl = row >= col
    strict = row > col
    r1 = lax.broadcasted_iota(jnp.int32, (CHUNK, CHUNK), 0)
    c1 = lax.broadcasted_iota(jnp.int32, (CHUNK, CHUNK), 1)
    lower01 = jnp.where(r1 >= c1, 1.0, 0.0).astype(BF16)
    eye = jnp.where(row == col, 1.0, 0.0).astype(F32)
    merge_masks = [(row // 2 == col // 2) & strict]
    blk = 2
    while blk < CHUNK:
        merge_masks.append((row // (2 * blk) == col // (2 * blk)) & (row // blk != col // blk) & strict)
        blk *= 2

    qk_lhs, qk_rhs, q_items, k_items, v_items, z_items = [], [], [], [], [], []
    zt_items, grow_items = [], []
    for i in range(nb):
        ba = ba_ref[i]
        beta_all = jax.nn.sigmoid(ba)
        sp = ba + dtb_ref[...]
        sp = jnp.maximum(sp, 0.0) + jnp.log1p(jnp.exp(-jnp.abs(sp)))
        g_all = -jnp.exp(alog_ref[...]) * sp
        gc_all = _dot_exact_lhs(lower01, g_all)
        gc_t = gc_all.T
        beta_t = beta_all.T

        for c in range(DN_QK_DIM // pw):
            qc = u_ref[i, :, c * pw:(c + 1) * pw]
            kc = u_ref[i, :, DN_QK_DIM + c * pw:DN_QK_DIM + (c + 1) * pw]
            qr = pltpu.roll(qc, hd, axis=1)
            kr = pltpu.roll(kc, hd, axis=1)
            for half in range(2):
                sel = lo if half == 0 else jnp.logical_not(lo)
                q_items.append(jnp.where(sel, qc, qr))
                k_items.append(jnp.where(sel, kc, kr))
                km = jnp.where(sel, kc, 0.0)
                qk_lhs.append(jnp.concatenate([qc, kc], axis=0))
                qk_rhs.append(jnp.concatenate([km, km], axis=0))
        for j in range(npair):
            a, b = 2 * j, 2 * j + 1
            v_items.append(u_ref[i, :, 2 * DN_QK_DIM + j * pw:2 * DN_QK_DIM + (j + 1) * pw])
            z_items.append(z_ref[i, :, j * pw:(j + 1) * pw])
            ra = jnp.concatenate([beta_t[a:a + 1, :], gc_t[8 + a:9 + a, :]], axis=1)
            rb = jnp.concatenate([beta_t[b:b + 1, :], gc_t[8 + b:9 + b, :]], axis=1)
            zt_items.append(jnp.concatenate([jnp.broadcast_to(ra, (CHUNK, pw)),
                                             jnp.broadcast_to(rb, (CHUNK, pw))], axis=0).T)
            grow_items.append(jnp.broadcast_to(
                jnp.concatenate([gc_t[8 + a:9 + a, :], gc_t[8 + b:9 + b, :]], axis=1), (CHUNK, pw)))

    q = jnp.stack(q_items)
    k = jnp.stack(k_items)
    v = jnp.stack(v_items)
    cols = jnp.stack(zt_items)
    beta, gcol = cols[:, :CHUNK], cols[:, CHUNK:]
    grow = jnp.stack(grow_items)
    glast = gcol[:, CHUNK - 1:CHUNK, :]

    qkk = _bmm_nt(jnp.stack(qk_lhs), jnp.stack(qk_rhs))
    qk, kk = qkk[:, :CHUNK], qkk[:, CHUNK:]

    decay = jnp.exp(jnp.where(incl, gcol - grow, NEG))
    eg = jnp.exp(gcol)

    m = jnp.where(strict, (kk * beta) * decay, 0.0)
    t = eye - jnp.where(merge_masks[0], m, 0.0)
    for mask in merge_masks[1:]:
        t = t - _pmm(t, _pmm(jnp.where(mask, m, 0.0), t))
    t0 = t.astype(BF16)
    t0_bd = _bd(t0)
    m_hi, m_lo = _split2(m)
    mt0 = _bmm(jnp.concatenate([m_hi, m_lo], axis=2), jnp.concatenate([t0_bd, t0_bd], axis=1))
    resid = eye - t0.astype(F32) - mt0
    t1 = _pmm(t0, resid).astype(BF16)

    s = s_scr[...]
    ks = _pmm(jnp.concatenate([k * (beta * eg), q * eg], axis=1), s)
    rhs_bd = _bd(v * beta - ks[:, :CHUNK])
    vnew = _bmm(jnp.concatenate([t0, t1], axis=2), jnp.concatenate([rhs_bd, rhs_bd], axis=1))
    o = ks[:, CHUNK:] + _pmm(qk * decay, vnew)
    kd = k * jnp.exp(glast - gcol)
    kv = _bmm(jnp.swapaxes(kd, 1, 2), vnew)
    s_scr[...] = s * jnp.exp(glast) + jnp.where(lo, kv[:, :hd], kv[:, hd:])

    o = o * lax.rsqrt(_half_sums(o * o, lo) * (1.0 / hd) + EPS) * nw_ref[...] * _silu(jnp.stack(z_items))
    for i in range(nb):
        for j in range(npair):
            o_ref[i, :, j * pw:(j + 1) * pw] = o[i * npair + j].astype(o_ref.dtype)

    @pl.when(n == pl.num_programs(1) - 1)
    def _():
        for i in range(nb):
            for j in range(npair):
                sp2 = s_scr[i * npair + j]
                sfin_ref[i, 2 * j] = sp2[:, :hd]
                sfin_ref[i, 2 * j + 1] = sp2[:, hd:]


def _deltanet(qkv, z, ba, s0, alog_l, dtb_l, dn_norm_w):
    bsz, t_len, _ = qkv.shape
    nb = DN_SEQS_PER_STEP
    tok = lambda b, n: (b, n, 0)
    per_b4 = lambda b, n: (b, 0, 0, 0)
    fixed = lambda b, n: (0, 0)
    return pl.pallas_call(
        functools.partial(_deltanet_pair_kernel, nb=nb),
        grid=(bsz // nb, t_len // CHUNK),
        in_specs=[pl.BlockSpec((nb, CHUNK, DN_CONV_DIM), tok),
                  pl.BlockSpec((nb, CHUNK, DN_V_DIM), tok),
                  pl.BlockSpec((nb, CHUNK, GATE_LANES), tok),
                  pl.BlockSpec((nb, DN_V_HEADS, DN_HEAD_DIM, DN_HEAD_DIM), per_b4),
                  pl.BlockSpec((1, GATE_LANES), fixed),
                  pl.BlockSpec((1, GATE_LANES), fixed),
                  pl.BlockSpec((1, 2 * DN_HEAD_DIM), fixed)],
        out_specs=[pl.BlockSpec((nb, CHUNK, DN_V_DIM), tok),
                   pl.BlockSpec((nb, DN_V_HEADS, DN_HEAD_DIM, DN_HEAD_DIM), per_b4)],
        out_shape=[jax.ShapeDtypeStruct((bsz, t_len, DN_V_DIM), BF16),
                   jax.ShapeDtypeStruct((bsz, DN_V_HEADS, DN_HEAD_DIM, DN_HEAD_DIM), F32)],
        scratch_shapes=[pltpu.VMEM((nb * DN_QK_HEADS, DN_HEAD_DIM, 2 * DN_HEAD_DIM), F32)],
        compiler_params=_params("parallel", "arbitrary"),
        name="deltanet",
    )(qkv, z, ba, s0, alog_l, dtb_l, dn_norm_w)


def _bias_kernel(bucket_ref, table_ref, o_ref):
    bucket = bucket_ref[...]
    for h in range(SWA_HEADS):
        acc = jnp.zeros(bucket.shape, F32)
        for b in range(NUM_BUCKETS):
            acc = jnp.where(bucket == b, table_ref[b, h], acc)
        o_ref[h] = acc


def _rel_bias(bucket, table):
    return pl.pallas_call(
        _bias_kernel,
        in_specs=[pl.BlockSpec(memory_space=pltpu.VMEM),
                  pl.BlockSpec(memory_space=pltpu.SMEM)],
        out_specs=pl.BlockSpec(memory_space=pltpu.VMEM),
        out_shape=jax.ShapeDtypeStruct((SWA_HEADS, CHUNK, BAND), F32),
        name="rel_bias",
    )(bucket, table)


def _swa_kernel(q_ref, *refs, hist, cps, n_units):
    k_refs, v_refs = refs[:n_units], refs[n_units:2 * n_units]
    bias_ref, sink_ref, o_ref = refs[2 * n_units:]
    n = pl.program_id(1)
    hd = SWA_HEAD_DIM
    kb = jnp.concatenate([r[0] for r in k_refs], axis=0)
    vb = jnp.concatenate([r[0] for r in v_refs], axis=0)
    row0 = (n * cps + hist) * CHUNK - WINDOW
    key = lax.broadcasted_iota(jnp.int32, (1, 1, BAND), 2)
    q_items, k_items, v_items, valid = [], [], [], []
    for c in range(cps):
        for kv in range(SWA_KV_HEADS):
            q_items.append(jnp.concatenate(
                [q_ref[0, c * CHUNK:(c + 1) * CHUNK, (kv * SWA_GROUP + g) * hd:(kv * SWA_GROUP + g + 1) * hd]
                 for g in range(SWA_GROUP)], axis=0))
            k_items.append(kb[c * CHUNK:c * CHUNK + BAND, kv * hd:(kv + 1) * hd])
            v_items.append(vb[c * CHUNK:c * CHUNK + BAND, kv * hd:(kv + 1) * hd])
            valid.append(row0 + c * CHUNK + key >= 0)
    s = _bmm_nt(jnp.stack(q_items), jnp.stack(k_items)) * (hd ** -0.5) + bias_ref[...]
    s = jnp.where(jnp.concatenate(valid, axis=0), s, NEG)
    sink = sink_ref[...]
    mx = jnp.maximum(jnp.max(s, axis=-1, keepdims=True), sink)
    p = jnp.exp(s - mx).astype(BF16)
    den = _bmm(p, jnp.ones((len(v_items), BAND, hd), BF16)) + jnp.exp(sink - mx)
    o = _bmm(p, jnp.stack(v_items)) / den
    for c in range(cps):
        for kv in range(SWA_KV_HEADS):
            for g in range(SWA_GROUP):
                h = kv * SWA_GROUP + g
                o_ref[0, c * CHUNK:(c + 1) * CHUNK, h * hd:(h + 1) * hd] = (
                    o[c * SWA_KV_HEADS + kv, g * CHUNK:(g + 1) * CHUNK].astype(o_ref.dtype))


def _swa(q, k_all, v_all, bias, sink_rows, hist):
    bsz, t_len, _ = q.shape
    cps = min(SWA_CHUNKS_PER_STEP, t_len // CHUNK)
    unit = WINDOW if cps * CHUNK % WINDOW == 0 else CHUNK
    assert (hist * CHUNK - WINDOW) % unit == 0 and (cps * CHUNK) % unit == 0
    n_units = (WINDOW + cps * CHUNK) // unit
    q_units = cps * CHUNK // unit
    unit0 = (hist * CHUNK - WINDOW) // unit
    tok = lambda b, n: (b, n, 0)

    def band(j):
        return lambda b, n: (b, jnp.maximum(n * q_units + unit0 + j, 0), 0)

    kv_specs = [pl.BlockSpec((1, unit, SWA_KV_DIM), band(j)) for j in range(n_units)]
    fixed3 = lambda b, n: (0, 0, 0)
    return pl.pallas_call(
        functools.partial(_swa_kernel, hist=hist, cps=cps, n_units=n_units),
        grid=(bsz, t_len // (cps * CHUNK)),
        in_specs=[pl.BlockSpec((1, cps * CHUNK, SWA_Q_DIM), tok)] + kv_specs + kv_specs
                 + [pl.BlockSpec((cps * SWA_KV_HEADS, SWA_GROUP * CHUNK, BAND), fixed3),
                    pl.BlockSpec((cps * SWA_KV_HEADS, SWA_GROUP * CHUNK, 1), fixed3)],
        out_specs=pl.BlockSpec((1, cps * CHUNK, SWA_Q_DIM), tok),
        out_shape=jax.ShapeDtypeStruct((bsz, t_len, SWA_Q_DIM), BF16),
        compiler_params=_params("parallel", "arbitrary"),
        name="swa",
    )(q, *([k_all] * n_units), *([v_all] * n_units),
      jnp.tile(bias, (cps, 1, 1)), jnp.tile(sink_rows, (cps, 1, 1)))


ROUTE_EXPERT, ROUTE_RANK, ROUTE_GATE = 0, TOP_K, 2 * TOP_K


def _outproj_kernel(odn_ref, oswa_ref, x_ref, g1_ref, sc_ref, sh_ref, wo1_ref, wo2_ref, bo_ref,
                    nw_ref, wr_ref, br_ref, cnt0_ref, *rest):
    x1_ref, h2_ref, route_ref, cnt_ref, cnt_scr, tri_scr = rest[-6:]
    first = (pl.program_id(0) == 0) & (pl.program_id(1) == 0)

    @pl.when(first)
    def _():
        cnt_scr[...] = cnt0_ref[...]
        r_i = lax.broadcasted_iota(jnp.int32, tri_scr.shape, 0)
        c_i = lax.broadcasted_iota(jnp.int32, tri_scr.shape, 1)
        tri_scr[...] = jnp.where(r_i > c_i, 1.0, 0.0).astype(BF16)

    nsq, tm, _ = x_ref.shape
    rows = nsq * tm
    mix = (_dot(odn_ref[...].reshape(rows, DN_V_DIM).astype(BF16), wo1_ref[...])
           + _dot(oswa_ref[...].reshape(rows, SWA_Q_DIM).astype(BF16), wo2_ref[...]))
    x1 = x_ref[...] + g1_ref[...] * (mix + bo_ref[...]).reshape(nsq, tm, D_MODEL)
    x1_ref[...] = x1
    h2 = _rms(x1) * nw_ref[...]
    h2 = (h2 * (1.0 + sc_ref[...]) + sh_ref[...]).reshape(rows, D_MODEL)
    h2_ref[...] = _pack_bf16_pairs(h2)

    logits = _dot_x3(h2, wr_ref[...]) + br_ref[...]
    lane = lax.broadcasted_iota(jnp.int32, logits.shape, 1)
    lane_f = lane.astype(F32)
    vals, idxs = [], []
    for _ in range(TOP_K):
        m = jnp.max(logits, axis=-1, keepdims=True)
        i = jnp.min(jnp.where(logits == m, lane_f, float(LANES)), axis=-1, keepdims=True)
        vals.append(m)
        idxs.append(i)
        logits = jnp.where(lane_f == i, -jnp.inf, logits)
    es = [jnp.exp(v - vals[0]) for v in vals]
    den = es[0] + es[1] + es[2] + es[3]
    onehot = jnp.zeros(lane.shape, F32)
    for kk in range(TOP_K):
        onehot = jnp.where(lane_f == idxs[kk], 1.0, onehot)
    before = _dot(tri_scr[...], onehot.astype(BF16)) + cnt_scr[...]
    cnt_scr[...] = cnt_scr[...] + jnp.sum(onehot, axis=0, keepdims=True)

    route = jnp.zeros(lane.shape, F32)
    for kk in range(TOP_K):
        rank = jnp.sum(jnp.where(lane_f == idxs[kk], before, 0.0), axis=-1, keepdims=True)
        route = jnp.where(lane == ROUTE_EXPERT + kk, idxs[kk], route)
        route = jnp.where(lane == ROUTE_RANK + kk, rank, route)
        route = jnp.where(lane == ROUTE_GATE + kk, es[kk] / den, route)
    route_ref[...] = route.reshape(nsq, tm, LANES)
    cnt_ref[...] = cnt_scr[...]


def _seqs_per_tile(t_len, tm, bsz):
    nsq = max(1, TOKEN_TILE // t_len) if tm == t_len else 1
    while bsz % nsq:
        nsq -= 1
    return nsq


def _outproj(odn, oswa, x, g1, sc, sh, wo1, wo2, bo, norm_w, wr, br, cnt0, tm, b0, bsz, h2_rows, h2_row0, h2_prev):
    t_len = x.shape[1]
    steps = t_len // tm
    nsq = _seqs_per_tile(t_len, tm, bsz)
    rows = nsq * tm
    assert h2_row0 % rows == 0 and b0 % nsq == 0
    blk0 = h2_row0 // rows
    sb0 = b0 // nsq
    tok_out = lambda b, i: (b, i, 0)
    tok = lambda b, i: (sb0 + b, i, 0)
    per_b = lambda b, i: (sb0 + b, 0, 0)
    fixed = lambda b, i: (0, 0)
    in_specs = [pl.BlockSpec((nsq, tm, DN_V_DIM), tok),
                pl.BlockSpec((nsq, tm, SWA_Q_DIM), tok),
                pl.BlockSpec((nsq, tm, D_MODEL), tok),
                pl.BlockSpec((nsq, 1, D_MODEL), per_b),
                pl.BlockSpec((nsq, 1, D_MODEL), per_b),
                pl.BlockSpec((nsq, 1, D_MODEL), per_b),
                pl.BlockSpec((DN_V_DIM, D_MODEL), fixed),
                pl.BlockSpec((SWA_Q_DIM, D_MODEL), fixed),
                pl.BlockSpec((1, D_MODEL), fixed),
                pl.BlockSpec((1, D_MODEL), fixed),
                pl.BlockSpec((D_MODEL, LANES), fixed),
                pl.BlockSpec((1, LANES), fixed),
                pl.BlockSpec((1, LANES), fixed)]
    args = [odn, oswa, x, g1, sc, sh, wo1, wo2, bo, norm_w, wr, br, cnt0]
    aliases = {}
    if h2_prev is not None:
        in_specs.append(pl.BlockSpec(memory_space=pl.ANY))
        aliases = {len(args): 1}
        args.append(h2_prev)
    return pl.pallas_call(
        _outproj_kernel,
        grid=(bsz // nsq, steps),
        in_specs=in_specs,
        out_specs=[pl.BlockSpec((nsq, tm, D_MODEL), tok_out),
                   pl.BlockSpec((rows, D_MODEL // 2), lambda b, i: (blk0 + b * steps + i, 0)),
                   pl.BlockSpec((nsq, tm, LANES), tok_out),
                   pl.BlockSpec((1, LANES), fixed)],
        out_shape=[jax.ShapeDtypeStruct((bsz, t_len, D_MODEL), F32),
                   jax.ShapeDtypeStruct((h2_rows, D_MODEL // 2), jnp.uint32),
                   jax.ShapeDtypeStruct((bsz, t_len, LANES), F32),
                   jax.ShapeDtypeStruct((1, LANES), F32)],
        scratch_shapes=[pltpu.VMEM((1, LANES), F32), pltpu.VMEM((rows, rows), BF16)],
        input_output_aliases=aliases,
        compiler_params=_params("arbitrary", "arbitrary"),
        name="outproj_router",
    )(*args)


def _moe_kernel(be_ref, nv_ref, xs_ref, w1_ref, b1_ref, w2_ref, b2_ref, y_ref):
    del be_ref
    nv = nv_ref[pl.program_id(0)]

    def ffn(n_rows):
        rows = lax.broadcasted_iota(jnp.int32, (n_rows, 1), 0)
        xb = jnp.where(rows < nv, _unpack_bf16_pairs(xs_ref[0:n_rows, :]), 0.0).astype(BF16)
        xf = xb.astype(F32)
        glu = _dot(xf, w1_ref[0, :, :D_FF]) + b1_ref[0, :, :D_FF]
        lin = _dot(xf, w1_ref[0, :, D_FF:]) + b1_ref[0, :, D_FF:]
        glu = jnp.minimum(glu, SWIGLU_LIMIT)
        lin = jnp.clip(lin, -SWIGLU_LIMIT, SWIGLU_LIMIT)
        act = glu * jax.nn.sigmoid(SWIGLU_ALPHA * glu) * (lin + 1.0)
        y = _dot(act.astype(BF16).astype(F32), w2_ref[0]) + b2_ref[0]
        y_ref[0:n_rows, :] = _pack_bf16_pairs(y)

    half = MOE_ROWS // 2

    @pl.when(nv == 0)
    def _():
        y_ref[...] = jnp.zeros(y_ref.shape, y_ref.dtype)

    @pl.when((nv > 0) & (nv <= half))
    def _():
        ffn(half)
        y_ref[half:, :] = jnp.zeros((MOE_ROWS - half, y_ref.shape[1]), y_ref.dtype)

    @pl.when(nv > half)
    def _():
        ffn(MOE_ROWS)


def _moe_experts(blk_expert, blk_valid, xs, w1, b1, w2, b2):
    n_rows = xs.shape[0]
    n_blocks = n_rows // MOE_ROWS
    half = D_MODEL // 2
    grid_spec = pltpu.PrefetchScalarGridSpec(
        num_scalar_prefetch=2,
        grid=(n_blocks,),
        in_specs=[pl.BlockSpec((MOE_ROWS, half), lambda i, be, nv: (i, 0)),
                  pl.BlockSpec((1, D_MODEL, 2 * D_FF), lambda i, be, nv: (be[i], 0, 0)),
                  pl.BlockSpec((1, 1, 2 * D_FF), lambda i, be, nv: (be[i], 0, 0)),
                  pl.BlockSpec((1, D_FF, D_MODEL), lambda i, be, nv: (be[i], 0, 0)),
                  pl.BlockSpec((1, 1, D_MODEL), lambda i, be, nv: (be[i], 0, 0))],
        out_specs=pl.BlockSpec((MOE_ROWS, half), lambda i, be, nv: (i, 0)),
    )
    return pl.pallas_call(
        _moe_kernel,
        grid_spec=grid_spec,
        out_shape=jax.ShapeDtypeStruct((n_rows, half), jnp.uint32),
        compiler_params=_params("arbitrary"),
        name="moe_experts",
    )(blk_expert, blk_valid, xs, w1, b1, w2, b2)


def _sc_mesh():
    return plsc.VectorSubcoreMesh(core_axis_name="core", subcore_axis_name="subcore")


def _gather_rows(x, idx):
    m = idx.shape[0]
    w = x.shape[1]

    @pl.kernel(out_type=jax.ShapeDtypeStruct((m, w), x.dtype), mesh=_sc_mesh())
    def gather_kernel(x_hbm, i_hbm, o_hbm):
        def body(i_vmem, o_vmem):
            pltpu.sync_copy(x_hbm.at[i_vmem.at[0]], o_vmem)

        pltpu.emit_pipeline(
            body,
            grid=(m // SC_WINDOW,),
            in_specs=[pl.BlockSpec((1, SC_WINDOW), lambda i: (i, 0))],
            out_specs=[pl.BlockSpec((SC_WINDOW, w), lambda i: (i, 0))],
            core_axis_name=("core", "subcore"),
            dimension_semantics=(pltpu.PARALLEL,),
        )(i_hbm, o_hbm)

    return gather_kernel(x, idx.reshape(m // SC_WINDOW, SC_WINDOW))


def _scatter_rows(x, idx, n_out):
    n, w = x.shape
    kk = idx.shape[1]
    idx3 = jnp.transpose(idx.reshape(n // SC_WINDOW, SC_WINDOW, kk), (0, 2, 1))

    @pl.kernel(out_type=jax.ShapeDtypeStruct((n_out, w), x.dtype), mesh=_sc_mesh())
    def scatter_kernel(x_hbm, i_hbm, o_hbm):
        def body(x_vmem, i_vmem):
            for k in range(kk):
                pltpu.sync_copy(x_vmem, o_hbm.at[i_vmem.at[0, k]])

        pltpu.emit_pipeline(
            body,
            grid=(n // SC_WINDOW,),
            in_specs=[pl.BlockSpec((SC_WINDOW, w), lambda i: (i, 0)),
                      pl.BlockSpec((1, kk, SC_WINDOW), lambda i: (i, 0, 0))],
            out_specs=[],
            core_axis_name=("core", "subcore"),
            dimension_semantics=(pltpu.PARALLEL,),
        )(x_hbm, i_hbm)

    return scatter_kernel(x, idx3)


def _final_kernel(x1_ref, yg_ref, tg_ref, g2_ref, nw_ref, *rest):
    y_ref = rest[-1]
    nsq, tm, _ = x1_ref.shape
    tg = tg_ref[...].reshape(nsq * tm, LANES)
    moe = _unpack_bf16_pairs(yg_ref[0]) * tg[:, ROUTE_GATE:ROUTE_GATE + 1]
    for kk in range(1, TOP_K):
        moe = moe + _unpack_bf16_pairs(yg_ref[kk]) * tg[:, ROUTE_GATE + kk:ROUTE_GATE + kk + 1]
    x2 = x1_ref[...] + g2_ref[...] * moe.reshape(nsq, tm, D_MODEL)
    y_ref[...] = _rms(x2) * nw_ref[...]


def _final(x1, yg, tg, g2, norm_w, tm, tok_offset, b0, bsz_total, prev=None):
    bsz, t_len, _ = x1.shape
    tok = lambda b, i: (b, i, 0)
    steps = t_len // tm
    nsq = _seqs_per_tile(t_len, tm, bsz)
    rows = nsq * tm
    assert tok_offset % rows == 0 and b0 % nsq == 0
    blk0 = tok_offset // rows
    sb0 = b0 // nsq
    in_specs = [pl.BlockSpec((nsq, tm, D_MODEL), tok),
                pl.BlockSpec((TOP_K, rows, D_MODEL // 2), lambda b, i: (0, blk0 + b * steps + i, 0)),
                pl.BlockSpec((nsq, tm, LANES), tok),
                pl.BlockSpec((nsq, 1, D_MODEL), lambda b, i: (sb0 + b, 0, 0)),
                pl.BlockSpec((1, D_MODEL), lambda b, i: (0, 0))]
    args = [x1, yg, tg, g2, norm_w]
    aliases = {}
    if prev is not None:
        in_specs.append(pl.BlockSpec(memory_space=pl.ANY))
        aliases = {len(args): 0}
        args.append(prev)
    return pl.pallas_call(
        _final_kernel,
        grid=(bsz // nsq, steps),
        in_specs=in_specs,
        out_specs=pl.BlockSpec((nsq, tm, D_MODEL), lambda b, i: (sb0 + b, i, 0)),
        out_shape=jax.ShapeDtypeStruct((bsz_total, t_len, D_MODEL), F32),
        input_output_aliases=aliases,
        compiler_params=_params("parallel", "arbitrary"),
        name="combine_final",
    )(*args)


def _t5_bucket(rel):
    half = NUM_BUCKETS // 2
    max_exact = half // 2
    ret = jnp.where(rel > 0, half, 0)
    n = jnp.abs(rel)
    nf = jnp.maximum(n, 1).astype(jnp.float32)
    large = max_exact + (jnp.log(nf / max_exact) / math.log(MAX_DISTANCE / max_exact)
                         * (half - max_exact)).astype(jnp.int32)
    large = jnp.minimum(large, half - 1)
    return ret + jnp.where(n < max_exact, n, large)


def _route(top_i, rank, sizes):
    n_tok = top_i.shape[0]
    n_asg = n_tok * TOP_K
    padded = (sizes + MOE_ROWS - 1) // MOE_ROWS * MOE_ROWS
    pend = jnp.cumsum(padded)
    pstart = pend - padded
    n_blocks = -(-n_asg // MOE_ROWS) + N_EXPERTS
    blk_row0 = jnp.arange(n_blocks, dtype=jnp.int32) * MOE_ROWS
    blk_expert = jnp.minimum(jnp.sum(pend[None, :] <= blk_row0[:, None], axis=1), N_EXPERTS - 1).astype(jnp.int32)
    blk_valid = jnp.clip(pstart[blk_expert] + sizes[blk_expert] - blk_row0, 0, MOE_ROWS).astype(jnp.int32)
    onehot = top_i[:, :, None] == jnp.arange(N_EXPERTS, dtype=jnp.int32)
    pos = jnp.sum(jnp.where(onehot, pstart, 0), axis=-1) + rank
    return pos, blk_expert, blk_valid, n_blocks * MOE_ROWS


def kernel(x_prompt, x_sample, c_prompt, c_sample, state_conv, state_delta, cache_swa_k, cache_swa_v,
           w_ada, b_ada, norm_mix, w_in, b_in, conv_w, a_log, dt_bias, dn_norm_w, sinks, rel_bias,
           w_out, b_out, norm_ffn, w_router, b_router, w1, b1, w2, b2, norm_final):
    depth = w_ada.shape[0]
    assert depth == 1, "the final norm is fused into the layer's combine step"
    bp, tp, _ = x_prompt.shape
    bs, ts, _ = x_sample.shape
    groups = [dict(x=x_prompt, c=c_prompt, hist=0), dict(x=x_sample, c=c_sample, hist=WIN_CHUNKS)]

    q_rel = jnp.arange(CHUNK)
    k_rel = jnp.arange(BAND) - WIN_CHUNKS * CHUNK
    bucket = _t5_bucket(k_rel[None, :] - q_rel[:, None]).astype(jnp.int32)
    bias = _rel_bias(bucket, rel_bias)

    nf_w = norm_final.reshape(1, D_MODEL)
    outs = {g: dict(conv=[], delta=[], k=[], v=[]) for g in range(2)}
    xs_cur = [x_prompt, x_sample]

    for l in range(depth):
        o1 = DN_CONV_DIM + DN_V_DIM
        o2 = o1 + 2 * DN_V_HEADS
        wl, bl = w_in[l], b_in[l]
        w_cat = jnp.concatenate(
            [wl[:, :o1], wl[:, o2:], wl[:, o1:o2], jnp.zeros((D_MODEL, GATE_LANES - 2 * DN_V_HEADS), F32)],
            axis=1).astype(BF16)
        b_cat = jnp.concatenate(
            [bl[:o1], bl[o2:], bl[o1:o2], jnp.zeros((GATE_LANES - 2 * DN_V_HEADS,), F32)]).reshape(1, PROJ_DIM)
        pad8 = jnp.zeros((DN_V_HEADS,), F32)
        padr = jnp.zeros((GATE_LANES - 2 * DN_V_HEADS,), F32)
        alog_l = jnp.concatenate([pad8, a_log[l], padr]).reshape(1, GATE_LANES)
        dtb_l = jnp.concatenate([pad8, dt_bias[l], padr]).reshape(1, GATE_LANES)
        wo1 = w_out[l][:DN_V_DIM].astype(BF16)
        wo2 = w_out[l][DN_V_DIM:].astype(BF16)
        wr = jnp.concatenate([w_router[l], jnp.zeros((D_MODEL, LANES - N_EXPERTS), F32)], axis=1)
        br = jnp.concatenate([b_router[l], jnp.full((LANES - N_EXPERTS,), NEG, F32)]).reshape(1, LANES)
        b1l = b1[l].reshape(N_EXPERTS, 1, 2 * D_FF)
        b2l = b2[l].reshape(N_EXPERTS, 1, D_MODEL)

        mod = _modulation(jnp.concatenate([c_prompt, c_sample], axis=0), w_ada[l], b_ada[l])
        att = []
        for gi, grp in enumerate(groups):
            x = xs_cur[gi]
            bsz, t_len, _ = x.shape
            tm = min(TOKEN_TILE, t_len)
            m = mod[:bp] if gi == 0 else mod[bp:]
            sh1, sc1, g1, sh2, sc2, g2 = [a.reshape(bsz, 1, D_MODEL) for a in jnp.split(m, 6, axis=-1)]
            if gi == 0:
                cprev = jnp.zeros((bsz, CONV_W - 1, DN_CONV_DIM), F32)
                s0 = jnp.zeros((bsz, DN_V_HEADS, DN_HEAD_DIM, DN_HEAD_DIM), F32)
            else:
                cprev = state_conv[l]
                s0 = state_delta[l]
            cprev8 = jnp.concatenate([jnp.zeros((bsz, 8 - (CONV_W - 1), DN_CONV_DIM), F32), cprev], axis=1)
            qkv, z, sq, sk, sv, ba, utail = _inproj(x, sc1, sh1, norm_mix[l].reshape(1, D_MODEL), w_cat, b_cat,
                                                    cprev8, conv_w[l], tm)
            if gi == 0:
                k_all, v_all = sk, sv
            else:
                k_all = jnp.concatenate([cache_swa_k[l].reshape(bsz, -1, SWA_KV_DIM), sk], axis=1)
                v_all = jnp.concatenate([cache_swa_v[l].reshape(bsz, -1, SWA_KV_DIM), sv], axis=1)
            swa_len = cache_swa_k.shape[2]
            o_dn, s_new = _deltanet(qkv, z, ba, s0, alog_l, dtb_l,
                                    jnp.tile(dn_norm_w[l], 2).reshape(1, 2 * DN_HEAD_DIM))
            sink_rows = jnp.repeat(sinks[l], CHUNK).reshape(SWA_KV_HEADS, SWA_GROUP * CHUNK, 1)
            o_swa = _swa(sq, k_all, v_all, bias.reshape(SWA_KV_HEADS, SWA_GROUP * CHUNK, BAND), sink_rows,
                         grp["hist"])
            conv_new = utail[:, 8 - (CONV_W - 1):]
            k_state = k_all[:, k_all.shape[1] - swa_len:].reshape(bsz, swa_len, SWA_KV_HEADS, SWA_HEAD_DIM)
            v_state = v_all[:, v_all.shape[1] - swa_len:].reshape(bsz, swa_len, SWA_KV_HEADS, SWA_HEAD_DIM)
            outs[gi]["conv"].append(conv_new)
            outs[gi]["delta"].append(s_new)
            outs[gi]["k"].append(k_state)
            outs[gi]["v"].append(v_state)
            att.append(dict(x=x, o_dn=o_dn, o_swa=o_swa, g1=g1, sc2=sc2, sh2=sh2, g2=g2, tm=tm,
                            bsz=bsz, t_len=t_len))

        n_first = bp * (MOE_PART_SPLIT[0]) // sum(MOE_PART_SPLIT)
        parts = ([[(1, 0, bs), (0, 0, n_first)], [(0, n_first, bp - n_first)]] if 0 < n_first < bp
                 else [[(1, 0, bs), (0, 0, bp)]])
        work = []
        for part in parts:
            cnt = jnp.zeros((1, LANES), F32)
            n_part = sum(nseq * att[gi]["t_len"] for gi, _, nseq in part)
            h2_all = jnp.zeros((n_part, D_MODEL // 2), jnp.uint32) if len(part) > 1 else None
            segs, row0 = [], 0
            for gi, b0, nseq in part:
                a = att[gi]
                x1, h2_all, route, cnt = _outproj(a["o_dn"], a["o_swa"], a["x"], a["g1"], a["sc2"], a["sh2"],
                                                  wo1, wo2, b_out[l].reshape(1, D_MODEL),
                                                  norm_ffn[l].reshape(1, D_MODEL), wr, br, cnt, a["tm"], b0, nseq,
                                                  n_part, row0, h2_all)
                segs.append(dict(gi=gi, b0=b0, nseq=nseq, x1=x1, tg=route))
                row0 += nseq * a["t_len"]
            route_all = jnp.concatenate(
                [s["tg"].reshape(-1, LANES)[:, :ROUTE_GATE] for s in segs], axis=0).astype(jnp.int32)
            ti_all = route_all[:, ROUTE_EXPERT:ROUTE_EXPERT + TOP_K]
            rk_all = route_all[:, ROUTE_RANK:ROUTE_RANK + TOP_K]
            sizes = cnt[0, :N_EXPERTS].astype(jnp.int32)
            pos, blk_expert, blk_valid, n_rows = _route(ti_all, rk_all, sizes)
            xs = _scatter_rows(h2_all, pos, n_rows)
            work.append(dict(segs=segs, pos=pos, blk_expert=blk_expert, blk_valid=blk_valid, xs=xs,
                             n_tok=h2_all.shape[0]))
        for wk in work:
            wk["y"] = _moe_experts(wk["blk_expert"], wk["blk_valid"], wk["xs"], w1[l], b1l, w2[l], b2l)
        for wk in work:
            wk["yg"] = _gather_rows(wk["y"], wk["pos"].T.reshape(-1)).reshape(TOP_K, wk["n_tok"], D_MODEL // 2)
        y_out = [None, None]
        for wk in work:
            off = 0
            for s in wk["segs"]:
                a = att[s["gi"]]
                y_out[s["gi"]] = _final(s["x1"], wk["yg"], s["tg"], a["g2"], nf_w, min(FINAL_TILE, a["t_len"]),
                                        off, s["b0"], a["bsz"], prev=y_out[s["gi"]])
                off += s["nseq"] * a["t_len"]
        xs_cur = y_out

    res = [xs_cur[0], xs_cur[1]]
    for gi in range(2):
        for name in ("conv", "delta", "k", "v"):
            res.append(jnp.stack(outs[gi][name]))
    return tuple(res)
```

```python
import functools
import math

import jax
import jax.numpy as jnp
from jax import lax
from jax.experimental import pallas as pl
from jax.experimental.pallas import tpu as pltpu
from jax.experimental.pallas import tpu_sc as plsc

F32 = jnp.float32
BF16 = jnp.bfloat16

D_MODEL = 1024
CHUNK = 64
EPS = 1e-6
DN_QK_HEADS = 4
DN_V_HEADS = 8
DN_HEAD_DIM = 64
DN_QK_DIM = DN_QK_HEADS * DN_HEAD_DIM
DN_V_DIM = DN_V_HEADS * DN_HEAD_DIM
DN_CONV_DIM = 2 * DN_QK_DIM + DN_V_DIM
CONV_W = 4
SWA_HEADS = 8
SWA_KV_HEADS = 2
SWA_HEAD_DIM = 64
SWA_GROUP = SWA_HEADS // SWA_KV_HEADS
SWA_Q_DIM = SWA_HEADS * SWA_HEAD_DIM
SWA_KV_DIM = SWA_KV_HEADS * SWA_HEAD_DIM
WINDOW = 128
WIN_CHUNKS = WINDOW // CHUNK
BAND = (WIN_CHUNKS + 1) * CHUNK
NUM_BUCKETS = 32
MAX_DISTANCE = 128
N_EXPERTS = 32
TOP_K = 4
D_FF = 1024
SWIGLU_ALPHA = 1.702
SWIGLU_LIMIT = 7.0

LANES = 128
GATE_LANES = LANES
PROJ_DIM = DN_CONV_DIM + DN_V_DIM + SWA_Q_DIM + 2 * SWA_KV_DIM + GATE_LANES
MOE_ROWS = 1024
FINAL_TILE = 1024
MOE_PART_SPLIT = (1, 1)
TOKEN_TILE = 512
DN_SEQS_PER_STEP = 8
SWA_CHUNKS_PER_STEP = 16
SC_WINDOW = 64
V7X_VMEM_BYTES = 64 * 1024 * 1024
VMEM_LIMIT = V7X_VMEM_BYTES - 8 * 1024 * 1024
NEG = -1e30


def _params(*sem):
    return pltpu.CompilerParams(dimension_semantics=sem, vmem_limit_bytes=VMEM_LIMIT)


def _split2(a):
    hi = a.astype(BF16)
    lo = (a - hi.astype(F32)).astype(BF16)
    return hi, lo


def _dot(a, b):
    return jnp.dot(a, b, preferred_element_type=F32)


def _dot_x3(a, b):
    a1, a2 = _split2(a)
    b1, b2 = _split2(b)
    return _dot(a1, b1) + (_dot(a1, b2) + _dot(a2, b1))


def _dot_exact_lhs(l01, g):
    g1 = g.astype(BF16)
    r = g - g1.astype(F32)
    g2 = r.astype(BF16)
    g3 = (r - g2.astype(F32)).astype(BF16)
    return _dot(l01, g1) + (_dot(l01, g2) + _dot(l01, g3))


def _pack_bf16_pairs(x):
    w = x.shape[1] // 2
    bits = lax.bitcast_convert_type(x.astype(BF16).astype(F32), jnp.uint32)
    return (bits[:, w:] & jnp.uint32(0xFFFF0000)) | (bits[:, :w] >> 16)


def _unpack_bf16_pairs(p):
    lo = lax.bitcast_convert_type(p << 16, F32)
    hi = lax.bitcast_convert_type(p & jnp.uint32(0xFFFF0000), F32)
    return jnp.concatenate([lo, hi], axis=1)


def _silu(x):
    return x * jax.nn.sigmoid(x)


def _rms(x):
    return x * lax.rsqrt(jnp.mean(x * x, axis=-1, keepdims=True) + EPS)


def _mod_kernel(c_ref, w_ref, b_ref, o_ref):
    o_ref[...] = _dot_x3(_silu(c_ref[...]), w_ref[...]) + b_ref[...]


def _modulation(c, w_ada, b_ada):
    nb = c.shape[0]
    n_out = w_ada.shape[1]
    tn = 1024
    return pl.pallas_call(
        _mod_kernel,
        grid=(n_out // tn,),
        in_specs=[pl.BlockSpec((nb, D_MODEL), lambda j: (0, 0)),
                  pl.BlockSpec((D_MODEL, tn), lambda j: (0, j)),
                  pl.BlockSpec((1, tn), lambda j: (0, j))],
        out_specs=pl.BlockSpec((nb, tn), lambda j: (0, j)),
        out_shape=jax.ShapeDtypeStruct((nb, n_out), F32),
        compiler_params=_params("arbitrary"),
        name="modulation",
    )(c, w_ada, b_ada.reshape(1, n_out))


def _inproj_kernel(x_ref, sc_ref, sh_ref, nw_ref, w_ref, b_ref,
                   cprev_ref, cw_ref, qkv_ref, z_ref, sq_ref, sk_ref, sv_ref, ba_ref, utail_ref, xc_scr):
    nsq, tm, _ = x_ref.shape
    rows = nsq * tm

    @pl.when(pl.program_id(1) == 0)
    def _():
        xc_scr[:, 0:8, :] = cprev_ref[...]

    h = _rms(x_ref[...]) * nw_ref[...]
    h = (h * (1.0 + sc_ref[...]) + sh_ref[...]).reshape(rows, D_MODEL)
    p = _dot(h.astype(BF16), w_ref[...]) + b_ref[...]
    o = DN_CONV_DIM
    for ref in (z_ref, sq_ref, sk_ref, sv_ref, ba_ref):
        w = ref.shape[-1]
        ref[...] = p[:, o:o + w].astype(ref.dtype).reshape(nsq, tm, w)
        o += w

    xc_scr[:, 8:8 + tm, :] = p[:, :DN_CONV_DIM].reshape(nsq, tm, DN_CONV_DIM)
    conv = xc_scr[:, 5:5 + tm, :] * cw_ref[0:1, :]
    for j in range(1, CONV_W):
        conv = conv + xc_scr[:, 5 + j:5 + j + tm, :] * cw_ref[j:j + 1, :]
    tail = xc_scr[:, tm:tm + 8, :]
    xc_scr[:, 0:8, :] = tail
    utail_ref[...] = tail
    cu = _silu(conv)

    pw = 2 * DN_HEAD_DIM
    lo = lax.broadcasted_iota(jnp.int32, (nsq, tm, pw), 2) < DN_HEAD_DIM
    for c in range(DN_QK_DIM // pw):
        qc = cu[:, :, c * pw:(c + 1) * pw]
        kc = cu[:, :, DN_QK_DIM + c * pw:DN_QK_DIM + (c + 1) * pw]
        qkv_ref[:, :, c * pw:(c + 1) * pw] = (
            qc * lax.rsqrt(_half_sums(qc * qc, lo) + EPS) * (DN_HEAD_DIM ** -0.5))
        qkv_ref[:, :, DN_QK_DIM + c * pw:DN_QK_DIM + (c + 1) * pw] = kc * lax.rsqrt(_half_sums(kc * kc, lo) + EPS)
    qkv_ref[:, :, 2 * DN_QK_DIM:] = cu[:, :, 2 * DN_QK_DIM:]


def _inproj(x, sc, sh, norm_w, w_cat, b_cat, cprev8, conv_w, tm):
    bsz, t_len, _ = x.shape
    widths = (DN_CONV_DIM, DN_V_DIM, SWA_Q_DIM, SWA_KV_DIM, SWA_KV_DIM, GATE_LANES)
    nsq = _seqs_per_tile(t_len, tm, bsz)
    tok = lambda b, i: (b, i, 0)
    per_b = lambda b, i: (b, 0, 0)
    fixed = lambda b, i: (0, 0)
    out_shape = [jax.ShapeDtypeStruct((bsz, t_len, w), BF16 if i == 2 else F32) for i, w in enumerate(widths)]
    return pl.pallas_call(
        _inproj_kernel,
        grid=(bsz // nsq, t_len // tm),
        in_specs=[pl.BlockSpec((nsq, tm, D_MODEL), tok),
                  pl.BlockSpec((nsq, 1, D_MODEL), per_b),
                  pl.BlockSpec((nsq, 1, D_MODEL), per_b),
                  pl.BlockSpec((1, D_MODEL), fixed),
                  pl.BlockSpec((D_MODEL, PROJ_DIM), fixed),
                  pl.BlockSpec((1, PROJ_DIM), fixed),
                  pl.BlockSpec((nsq, 8, DN_CONV_DIM), per_b),
                  pl.BlockSpec((CONV_W, DN_CONV_DIM), fixed)],
        out_specs=([pl.BlockSpec((nsq, tm, w), tok) for w in widths]
                   + [pl.BlockSpec((nsq, 8, DN_CONV_DIM), per_b)]),
        out_shape=out_shape + [jax.ShapeDtypeStruct((bsz, 8, DN_CONV_DIM), F32)],
        scratch_shapes=[pltpu.VMEM((nsq, tm + 8, DN_CONV_DIM), F32)],
        compiler_params=_params("parallel", "arbitrary"),
        name="inproj",
    )(x, sc, sh, norm_w, w_cat, b_cat, cprev8, conv_w)


def _bmm(a, b):
    return lax.dot_general(a.astype(BF16), b.astype(BF16), (((2,), (1,)), ((0,), (0,))),
                           preferred_element_type=F32)


def _bmm_nt(a, b):
    return lax.dot_general(a.astype(BF16), b.astype(BF16), (((2,), (2,)), ((0,), (0,))),
                           preferred_element_type=F32)


def _bd(x):
    x = x.astype(BF16)
    lo = lax.broadcasted_iota(jnp.int32, x.shape, 2) < x.shape[2] // 2
    zero = jnp.zeros_like(x)
    return jnp.concatenate([jnp.where(lo, x, zero), jnp.where(lo, zero, x)], axis=1)


def _pmm(a, b):
    return _bmm(a, _bd(b))


def _half_sums(x, lo):
    s_lo = jnp.sum(jnp.where(lo, x, 0.0), axis=-1, keepdims=True)
    s_hi = jnp.sum(jnp.where(lo, 0.0, x), axis=-1, keepdims=True)
    return jnp.where(lo, s_lo, s_hi)


def _deltanet_pair_kernel(u_ref, z_ref, ba_ref, s0_ref, alog_ref, dtb_ref, nw_ref,
                          o_ref, sfin_ref, s_scr, *, nb):
    n = pl.program_id(1)
    hd = DN_HEAD_DIM
    pw = 2 * hd
    npair = DN_QK_HEADS

    @pl.when(n == 0)
    def _():
        for i in range(nb):
            for j in range(npair):
                s_scr[i * npair + j] = jnp.concatenate([s0_ref[i, 2 * j], s0_ref[i, 2 * j + 1]], axis=1)

    row = lax.broadcasted_iota(jnp.int32, (CHUNK, pw), 0)
    col = lax.broadcasted_iota(jnp.int32, (CHUNK, pw), 1) % hd
    lo = lax.broadcasted_iota(jnp.int32, (CHUNK, pw), 1) < hd
    incl = row >= col
    strict = row > col
    r1 = lax.broadcasted_iota(jnp.int32, (CHUNK, CHUNK), 0)
    c1 = lax.broadcasted_iota(jnp.int32, (CHUNK, CHUNK), 1)
    lower01 = jnp.where(r1 >= c1, 1.0, 0.0).astype(BF16)
    eye = jnp.where(row == col, 1.0, 0.0).astype(F32)
    merge_masks = [(row // 2 == col // 2) & strict]
    blk = 2
    while blk < CHUNK:
        merge_masks.append((row // (2 * blk) == col // (2 * blk)) & (row // blk != col // blk) & strict)
        blk *= 2

    qk_lhs, qk_rhs, q_items, k_items, v_items, z_items = [], [], [], [], [], []
    zt_items, grow_items = [], []
    for i in range(nb):
        ba = ba_ref[i]
        beta_all = jax.nn.sigmoid(ba)
        sp = ba + dtb_ref[...]
        sp = jnp.maximum(sp, 0.0) + jnp.log1p(jnp.exp(-jnp.abs(sp)))
        g_all = -jnp.exp(alog_ref[...]) * sp
        gc_all = _dot_exact_lhs(lower01, g_all)
        gc_t = gc_all.T
        beta_t = beta_all.T

        for c in range(DN_QK_DIM // pw):
            qc = u_ref[i, :, c * pw:(c + 1) * pw]
            kc = u_ref[i, :, DN_QK_DIM + c * pw:DN_QK_DIM + (c + 1) * pw]
            qr = pltpu.roll(qc, hd, axis=1)
            kr = pltpu.roll(kc, hd, axis=1)
            for half in range(2):
                sel = lo if half == 0 else jnp.logical_not(lo)
                q_items.append(jnp.where(sel, qc, qr))
                k_items.append(jnp.where(sel, kc, kr))
                km = jnp.where(sel, kc, 0.0)
                qk_lhs.append(jnp.concatenate([qc, kc], axis=0))
                qk_rhs.append(jnp.concatenate([km, km], axis=0))
        for j in range(npair):
            a, b = 2 * j, 2 * j + 1
            v_items.append(u_ref[i, :, 2 * DN_QK_DIM + j * pw:2 * DN_QK_DIM + (j + 1) * pw])
            z_items.append(z_ref[i, :, j * pw:(j + 1) * pw])
            ra = jnp.concatenate([beta_t[a:a + 1, :], gc_t[8 + a:9 + a, :]], axis=1)
            rb = jnp.concatenate([beta_t[b:b + 1, :], gc_t[8 + b:9 + b, :]], axis=1)
            zt_items.append(jnp.concatenate([jnp.broadcast_to(ra, (CHUNK, pw)),
                                             jnp.broadcast_to(rb, (CHUNK, pw))], axis=0).T)
            grow_items.append(jnp.broadcast_to(
                jnp.concatenate([gc_t[8 + a:9 + a, :], gc_t[8 + b:9 + b, :]], axis=1), (CHUNK, pw)))

    q = jnp.stack(q_items)
    k = jnp.stack(k_items)
    v = jnp.stack(v_items)
    cols = jnp.stack(zt_items)
    beta, gcol = cols[:, :CHUNK], cols[:, CHUNK:]
    grow = jnp.stack(grow_items)
    glast = gcol[:, CHUNK - 1:CHUNK, :]

    qkk = _bmm_nt(jnp.stack(qk_lhs), jnp.stack(qk_rhs))
    qk, kk = qkk[:, :CHUNK], qkk[:, CHUNK:]

    decay = jnp.exp(jnp.where(incl, gcol - grow, NEG))
    eg = jnp.exp(gcol)

    m = jnp.where(strict, (kk * beta) * decay, 0.0)
    t = eye - jnp.where(merge_masks[0], m, 0.0)
    for mask in merge_masks[1:]:
        t = t - _pmm(t, _pmm(jnp.where(mask, m, 0.0), t))
    t0 = t.astype(BF16)
    t0_bd = _bd(t0)
    m_hi, m_lo = _split2(m)
    mt0 = _bmm(jnp.concatenate([m_hi, m_lo], axis=2), jnp.concatenate([t0_bd, t0_bd], axis=1))
    resid = eye - t0.astype(F32) - mt0
    t1 = _pmm(t0, resid).astype(BF16)

    s = s_scr[...]
    ks = _pmm(jnp.concatenate([k * (beta * eg), q * eg], axis=1), s)
    rhs_bd = _bd(v * beta - ks[:, :CHUNK])
    vnew = _bmm(jnp.concatenate([t0, t1], axis=2), jnp.concatenate([rhs_bd, rhs_bd], axis=1))
    o = ks[:, CHUNK:] + _pmm(qk * decay, vnew)
    kd = k * jnp.exp(glast - gcol)
    kv = _bmm(jnp.swapaxes(kd, 1, 2), vnew)
    s_scr[...] = s * jnp.exp(glast) + jnp.where(lo, kv[:, :hd], kv[:, hd:])

    o = o * lax.rsqrt(_half_sums(o * o, lo) * (1.0 / hd) + EPS) * nw_ref[...] * _silu(jnp.stack(z_items))
    for i in range(nb):
        for j in range(npair):
            o_ref[i, :, j * pw:(j + 1) * pw] = o[i * npair + j].astype(o_ref.dtype)

    @pl.when(n == pl.num_programs(1) - 1)
    def _():
        for i in range(nb):
            for j in range(npair):
                sp2 = s_scr[i * npair + j]
                sfin_ref[i, 2 * j] = sp2[:, :hd]
                sfin_ref[i, 2 * j + 1] = sp2[:, hd:]


def _deltanet(qkv, z, ba, s0, alog_l, dtb_l, dn_norm_w):
    bsz, t_len, _ = qkv.shape
    nb = DN_SEQS_PER_STEP
    tok = lambda b, n: (b, n, 0)
    per_b4 = lambda b, n: (b, 0, 0, 0)
    fixed = lambda b, n: (0, 0)
    return pl.pallas_call(
        functools.partial(_deltanet_pair_kernel, nb=nb),
        grid=(bsz // nb, t_len // CHUNK),
        in_specs=[pl.BlockSpec((nb, CHUNK, DN_CONV_DIM), tok),
                  pl.BlockSpec((nb, CHUNK, DN_V_DIM), tok),
                  pl.BlockSpec((nb, CHUNK, GATE_LANES), tok),
                  pl.BlockSpec((nb, DN_V_HEADS, DN_HEAD_DIM, DN_HEAD_DIM), per_b4),
                  pl.BlockSpec((1, GATE_LANES), fixed),
                  pl.BlockSpec((1, GATE_LANES), fixed),
                  pl.BlockSpec((1, 2 * DN_HEAD_DIM), fixed)],
        out_specs=[pl.BlockSpec((nb, CHUNK, DN_V_DIM), tok),
                   pl.BlockSpec((nb, DN_V_HEADS, DN_HEAD_DIM, DN_HEAD_DIM), per_b4)],
        out_shape=[jax.ShapeDtypeStruct((bsz, t_len, DN_V_DIM), BF16),
                   jax.ShapeDtypeStruct((bsz, DN_V_HEADS, DN_HEAD_DIM, DN_HEAD_DIM), F32)],
        scratch_shapes=[pltpu.VMEM((nb * DN_QK_HEADS, DN_HEAD_DIM, 2 * DN_HEAD_DIM), F32)],
        compiler_params=_params("parallel", "arbitrary"),
        name="deltanet",
    )(qkv, z, ba, s0, alog_l, dtb_l, dn_norm_w)


def _bias_kernel(bucket_ref, table_ref, o_ref):
    bucket = bucket_ref[...]
    for h in range(SWA_HEADS):
        acc = jnp.zeros(bucket.shape, F32)
        for b in range(NUM_BUCKETS):
            acc = jnp.where(bucket == b, table_ref[b, h], acc)
        o_ref[h] = acc


def _rel_bias(bucket, table):
    return pl.pallas_call(
        _bias_kernel,
        in_specs=[pl.BlockSpec(memory_space=pltpu.VMEM),
                  pl.BlockSpec(memory_space=pltpu.SMEM)],
        out_specs=pl.BlockSpec(memory_space=pltpu.VMEM),
        out_shape=jax.ShapeDtypeStruct((SWA_HEADS, CHUNK, BAND), F32),
        name="rel_bias",
    )(bucket, table)


def _swa_kernel(q_ref, *refs, hist, cps, n_units):
    k_refs, v_refs = refs[:n_units], refs[n_units:2 * n_units]
    bias_ref, sink_ref, o_ref = refs[2 * n_units:]
    n = pl.program_id(1)
    hd = SWA_HEAD_DIM
    kb = jnp.concatenate([r[0] for r in k_refs], axis=0)
    vb = jnp.concatenate([r[0] for r in v_refs], axis=0)
    row0 = (n * cps + hist) * CHUNK - WINDOW
    key = lax.broadcasted_iota(jnp.int32, (1, 1, BAND), 2)
    q_items, k_items, v_items, valid = [], [], [], []
    for c in range(cps):
        for kv in range(SWA_KV_HEADS):
            q_items.append(jnp.concatenate(
                [q_ref[0, c * CHUNK:(c + 1) * CHUNK, (kv * SWA_GROUP + g) * hd:(kv * SWA_GROUP + g + 1) * hd]
                 for g in range(SWA_GROUP)], axis=0))
            k_items.append(kb[c * CHUNK:c * CHUNK + BAND, kv * hd:(kv + 1) * hd])
            v_items.append(vb[c * CHUNK:c * CHUNK + BAND, kv * hd:(kv + 1) * hd])
            valid.append(row0 + c * CHUNK + key >= 0)
    s = _bmm_nt(jnp.stack(q_items), jnp.stack(k_items)) * (hd ** -0.5) + bias_ref[...]
    s = jnp.where(jnp.concatenate(valid, axis=0), s, NEG)
    sink = sink_ref[...]
    mx = jnp.maximum(jnp.max(s, axis=-1, keepdims=True), sink)
    p = jnp.exp(s - mx).astype(BF16)
    den = _bmm(p, jnp.ones((len(v_items), BAND, hd), BF16)) + jnp.exp(sink - mx)
    o = _bmm(p, jnp.stack(v_items)) / den
    for c in range(cps):
        for kv in range(SWA_KV_HEADS):
            for g in range(SWA_GROUP):
                h = kv * SWA_GROUP + g
                o_ref[0, c * CHUNK:(c + 1) * CHUNK, h * hd:(h + 1) * hd] = (
                    o[c * SWA_KV_HEADS + kv, g * CHUNK:(g + 1) * CHUNK].astype(o_ref.dtype))


def _swa(q, k_all, v_all, bias, sink_rows, hist):
    bsz, t_len, _ = q.shape
    cps = min(SWA_CHUNKS_PER_STEP, t_len // CHUNK)
    unit = WINDOW if cps * CHUNK % WINDOW == 0 else CHUNK
    assert (hist * CHUNK - WINDOW) % unit == 0 and (cps * CHUNK) % unit == 0
    n_units = (WINDOW + cps * CHUNK) // unit
    q_units = cps * CHUNK // unit
    unit0 = (hist * CHUNK - WINDOW) // unit
    tok = lambda b, n: (b, n, 0)

    def band(j):
        return lambda b, n: (b, jnp.maximum(n * q_units + unit0 + j, 0), 0)

    kv_specs = [pl.BlockSpec((1, unit, SWA_KV_DIM), band(j)) for j in range(n_units)]
    fixed3 = lambda b, n: (0, 0, 0)
    return pl.pallas_call(
        functools.partial(_swa_kernel, hist=hist, cps=cps, n_units=n_units),
        grid=(bsz, t_len // (cps * CHUNK)),
        in_specs=[pl.BlockSpec((1, cps * CHUNK, SWA_Q_DIM), tok)] + kv_specs + kv_specs
                 + [pl.BlockSpec((cps * SWA_KV_HEADS, SWA_GROUP * CHUNK, BAND), fixed3),
                    pl.BlockSpec((cps * SWA_KV_HEADS, SWA_GROUP * CHUNK, 1), fixed3)],
        out_specs=pl.BlockSpec((1, cps * CHUNK, SWA_Q_DIM), tok),
        out_shape=jax.ShapeDtypeStruct((bsz, t_len, SWA_Q_DIM), BF16),
        compiler_params=_params("parallel", "arbitrary"),
        name="swa",
    )(q, *([k_all] * n_units), *([v_all] * n_units),
      jnp.tile(bias, (cps, 1, 1)), jnp.tile(sink_rows, (cps, 1, 1)))


ROUTE_EXPERT, ROUTE_RANK, ROUTE_GATE = 0, TOP_K, 2 * TOP_K


def _outproj_kernel(odn_ref, oswa_ref, x_ref, g1_ref, sc_ref, sh_ref, wo1_ref, wo2_ref, bo_ref,
                    nw_ref, wr_ref, br_ref, cnt0_ref, *rest):
    x1_ref, h2_ref, route_ref, cnt_ref, cnt_scr, tri_scr = rest[-6:]
    first = (pl.program_id(0) == 0) & (pl.program_id(1) == 0)

    @pl.when(first)
    def _():
        cnt_scr[...] = cnt0_ref[...]
        r_i = lax.broadcasted_iota(jnp.int32, tri_scr.shape, 0)
        c_i = lax.broadcasted_iota(jnp.int32, tri_scr.shape, 1)
        tri_scr[...] = jnp.where(r_i > c_i, 1.0, 0.0).astype(BF16)

    nsq, tm, _ = x_ref.shape
    rows = nsq * tm
    mix = (_dot(odn_ref[...].reshape(rows, DN_V_DIM).astype(BF16), wo1_ref[...])
           + _dot(oswa_ref[...].reshape(rows, SWA_Q_DIM).astype(BF16), wo2_ref[...]))
    x1 = x_ref[...] + g1_ref[...] * (mix + bo_ref[...]).reshape(nsq, tm, D_MODEL)
    x1_ref[...] = x1
    h2 = _rms(x1) * nw_ref[...]
    h2 = (h2 * (1.0 + sc_ref[...]) + sh_ref[...]).reshape(rows, D_MODEL)
    h2_ref[...] = _pack_bf16_pairs(h2)

    logits = _dot_x3(h2, wr_ref[...]) + br_ref[...]
    lane = lax.broadcasted_iota(jnp.int32, logits.shape, 1)
    lane_f = lane.astype(F32)
    vals, idxs = [], []
    for _ in range(TOP_K):
        m = jnp.max(logits, axis=-1, keepdims=True)
        i = jnp.min(jnp.where(logits == m, lane_f, float(LANES)), axis=-1, keepdims=True)
        vals.append(m)
        idxs.append(i)
        logits = jnp.where(lane_f == i, -jnp.inf, logits)
    es = [jnp.exp(v - vals[0]) for v in vals]
    den = es[0] + es[1] + es[2] + es[3]
    onehot = jnp.zeros(lane.shape, F32)
    for kk in range(TOP_K):
        onehot = jnp.where(lane_f == idxs[kk], 1.0, onehot)
    before = _dot(tri_scr[...], onehot.astype(BF16)) + cnt_scr[...]
    cnt_scr[...] = cnt_scr[...] + jnp.sum(onehot, axis=0, keepdims=True)

    route = jnp.zeros(lane.shape, F32)
    for kk in range(TOP_K):
        rank = jnp.sum(jnp.where(lane_f == idxs[kk], before, 0.0), axis=-1, keepdims=True)
        route = jnp.where(lane == ROUTE_EXPERT + kk, idxs[kk], route)
        route = jnp.where(lane == ROUTE_RANK + kk, rank, route)
        route = jnp.where(lane == ROUTE_GATE + kk, es[kk] / den, route)
    route_ref[...] = route.reshape(nsq, tm, LANES)
    cnt_ref[...] = cnt_scr[...]


def _seqs_per_tile(t_len, tm, bsz):
    nsq = max(1, TOKEN_TILE // t_len) if tm == t_len else 1
    while bsz % nsq:
        nsq -= 1
    return nsq


def _outproj(odn, oswa, x, g1, sc, sh, wo1, wo2, bo, norm_w, wr, br, cnt0, tm, b0, bsz, h2_rows, h2_row0, h2_prev):
    t_len = x.shape[1]
    steps = t_len // tm
    nsq = _seqs_per_tile(t_len, tm, bsz)
    rows = nsq * tm
    assert h2_row0 % rows == 0 and b0 % nsq == 0
    blk0 = h2_row0 // rows
    sb0 = b0 // nsq
    tok_out = lambda b, i: (b, i, 0)
    tok = lambda b, i: (sb0 + b, i, 0)
    per_b = lambda b, i: (sb0 + b, 0, 0)
    fixed = lambda b, i: (0, 0)
    in_specs = [pl.BlockSpec((nsq, tm, DN_V_DIM), tok),
                pl.BlockSpec((nsq, tm, SWA_Q_DIM), tok),
                pl.BlockSpec((nsq, tm, D_MODEL), tok),
                pl.BlockSpec((nsq, 1, D_MODEL), per_b),
                pl.BlockSpec((nsq, 1, D_MODEL), per_b),
                pl.BlockSpec((nsq, 1, D_MODEL), per_b),
                pl.BlockSpec((DN_V_DIM, D_MODEL), fixed),
                pl.BlockSpec((SWA_Q_DIM, D_MODEL), fixed),
                pl.BlockSpec((1, D_MODEL), fixed),
                pl.BlockSpec((1, D_MODEL), fixed),
                pl.BlockSpec((D_MODEL, LANES), fixed),
                pl.BlockSpec((1, LANES), fixed),
                pl.BlockSpec((1, LANES), fixed)]
    args = [odn, oswa, x, g1, sc, sh, wo1, wo2, bo, norm_w, wr, br, cnt0]
    aliases = {}
    if h2_prev is not None:
        in_specs.append(pl.BlockSpec(memory_space=pl.ANY))
        aliases = {len(args): 1}
        args.append(h2_prev)
    return pl.pallas_call(
        _outproj_kernel,
        grid=(bsz // nsq, steps),
        in_specs=in_specs,
        out_specs=[pl.BlockSpec((nsq, tm, D_MODEL), tok_out),
                   pl.BlockSpec((rows, D_MODEL // 2), lambda b, i: (blk0 + b * steps + i, 0)),
                   pl.BlockSpec((nsq, tm, LANES), tok_out),
                   pl.BlockSpec((1, LANES), fixed)],
        out_shape=[jax.ShapeDtypeStruct((bsz, t_len, D_MODEL), F32),
                   jax.ShapeDtypeStruct((h2_rows, D_MODEL // 2), jnp.uint32),
                   jax.ShapeDtypeStruct((bsz, t_len, LANES), F32),
                   jax.ShapeDtypeStruct((1, LANES), F32)],
        scratch_shapes=[pltpu.VMEM((1, LANES), F32), pltpu.VMEM((rows, rows), BF16)],
        input_output_aliases=aliases,
        compiler_params=_params("arbitrary", "arbitrary"),
        name="outproj_router",
    )(*args)


def _moe_kernel(be_ref, nv_ref, xs_ref, w1_ref, b1_ref, w2_ref, b2_ref, y_ref):
    del be_ref
    nv = nv_ref[pl.program_id(0)]

    def ffn(n_rows):
        rows = lax.broadcasted_iota(jnp.int32, (n_rows, 1), 0)
        xb = jnp.where(rows < nv, _unpack_bf16_pairs(xs_ref[0:n_rows, :]), 0.0).astype(BF16)
        xf = xb.astype(F32)
        glu = _dot(xf, w1_ref[0, :, :D_FF]) + b1_ref[0, :, :D_FF]
        lin = _dot(xf, w1_ref[0, :, D_FF:]) + b1_ref[0, :, D_FF:]
        glu = jnp.minimum(glu, SWIGLU_LIMIT)
        lin = jnp.clip(lin, -SWIGLU_LIMIT, SWIGLU_LIMIT)
        act = glu * jax.nn.sigmoid(SWIGLU_ALPHA * glu) * (lin + 1.0)
        y = _dot(act.astype(BF16).astype(F32), w2_ref[0]) + b2_ref[0]
        y_ref[0:n_rows, :] = _pack_bf16_pairs(y)

    half = MOE_ROWS // 2

    @pl.when(nv == 0)
    def _():
        y_ref[...] = jnp.zeros(y_ref.shape, y_ref.dtype)

    @pl.when((nv > 0) & (nv <= half))
    def _():
        ffn(half)
        y_ref[half:, :] = jnp.zeros((MOE_ROWS - half, y_ref.shape[1]), y_ref.dtype)

    @pl.when(nv > half)
    def _():
        ffn(MOE_ROWS)


def _moe_experts(blk_expert, blk_valid, xs, w1, b1, w2, b2):
    n_rows = xs.shape[0]
    n_blocks = n_rows // MOE_ROWS
    half = D_MODEL // 2
    grid_spec = pltpu.PrefetchScalarGridSpec(
        num_scalar_prefetch=2,
        grid=(n_blocks,),
        in_specs=[pl.BlockSpec((MOE_ROWS, half), lambda i, be, nv: (i, 0)),
                  pl.BlockSpec((1, D_MODEL, 2 * D_FF), lambda i, be, nv: (be[i], 0, 0)),
                  pl.BlockSpec((1, 1, 2 * D_FF), lambda i, be, nv: (be[i], 0, 0)),
                  pl.BlockSpec((1, D_FF, D_MODEL), lambda i, be, nv: (be[i], 0, 0)),
                  pl.BlockSpec((1, 1, D_MODEL), lambda i, be, nv: (be[i], 0, 0))],
        out_specs=pl.BlockSpec((MOE_ROWS, half), lambda i, be, nv: (i, 0)),
    )
    return pl.pallas_call(
        _moe_kernel,
        grid_spec=grid_spec,
        out_shape=jax.ShapeDtypeStruct((n_rows, half), jnp.uint32),
        compiler_params=_params("arbitrary"),
        name="moe_experts",
    )(blk_expert, blk_valid, xs, w1, b1, w2, b2)


def _sc_mesh():
    return plsc.VectorSubcoreMesh(core_axis_name="core", subcore_axis_name="subcore")


def _gather_rows(x, idx):
    m = idx.shape[0]
    w = x.shape[1]

    @pl.kernel(out_type=jax.ShapeDtypeStruct((m, w), x.dtype), mesh=_sc_mesh())
    def gather_kernel(x_hbm, i_hbm, o_hbm):
        def body(i_vmem, o_vmem):
            pltpu.sync_copy(x_hbm.at[i_vmem.at[0]], o_vmem)

        pltpu.emit_pipeline(
            body,
            grid=(m // SC_WINDOW,),
            in_specs=[pl.BlockSpec((1, SC_WINDOW), lambda i: (i, 0))],
            out_specs=[pl.BlockSpec((SC_WINDOW, w), lambda i: (i, 0))],
            core_axis_name=("core", "subcore"),
            dimension_semantics=(pltpu.PARALLEL,),
        )(i_hbm, o_hbm)

    return gather_kernel(x, idx.reshape(m // SC_WINDOW, SC_WINDOW))


def _scatter_rows(x, idx, n_out):
    n, w = x.shape
    kk = idx.shape[0]
    idx3 = jnp.transpose(idx.reshape(kk, n // SC_WINDOW, SC_WINDOW), (1, 0, 2))

    @pl.kernel(out_type=jax.ShapeDtypeStruct((n_out, w), x.dtype), mesh=_sc_mesh())
    def scatter_kernel(x_hbm, i_hbm, o_hbm):
        def body(x_vmem, i_vmem):
            for k in range(kk):
                pltpu.sync_copy(x_vmem, o_hbm.at[i_vmem.at[0, k]])

        pltpu.emit_pipeline(
            body,
            grid=(n // SC_WINDOW,),
            in_specs=[pl.BlockSpec((SC_WINDOW, w), lambda i: (i, 0)),
                      pl.BlockSpec((1, kk, SC_WINDOW), lambda i: (i, 0, 0))],
            out_specs=[],
            core_axis_name=("core", "subcore"),
            dimension_semantics=(pltpu.PARALLEL,),
        )(x_hbm, i_hbm)

    return scatter_kernel(x, idx3)


def _final_kernel(x1_ref, yg_ref, tg_ref, g2_ref, nw_ref, *rest):
    y_ref = rest[-1]
    nsq, tm, _ = x1_ref.shape
    tg = tg_ref[...].reshape(nsq * tm, LANES)
    moe = _unpack_bf16_pairs(yg_ref[0]) * tg[:, ROUTE_GATE:ROUTE_GATE + 1]
    for kk in range(1, TOP_K):
        moe = moe + _unpack_bf16_pairs(yg_ref[kk]) * tg[:, ROUTE_GATE + kk:ROUTE_GATE + kk + 1]
    x2 = x1_ref[...] + g2_ref[...] * moe.reshape(nsq, tm, D_MODEL)
    y_ref[...] = _rms(x2) * nw_ref[...]


def _final(x1, yg, tg, g2, norm_w, tm, tok_offset, b0, bsz_total, prev=None):
    bsz, t_len, _ = x1.shape
    tok = lambda b, i: (b, i, 0)
    steps = t_len // tm
    nsq = _seqs_per_tile(t_len, tm, bsz)
    rows = nsq * tm
    assert tok_offset % rows == 0 and b0 % nsq == 0
    blk0 = tok_offset // rows
    sb0 = b0 // nsq
    in_specs = [pl.BlockSpec((nsq, tm, D_MODEL), tok),
                pl.BlockSpec((TOP_K, rows, D_MODEL // 2), lambda b, i: (0, blk0 + b * steps + i, 0)),
                pl.BlockSpec((nsq, tm, LANES), tok),
                pl.BlockSpec((nsq, 1, D_MODEL), lambda b, i: (sb0 + b, 0, 0)),
                pl.BlockSpec((1, D_MODEL), lambda b, i: (0, 0))]
    args = [x1, yg, tg, g2, norm_w]
    aliases = {}
    if prev is not None:
        in_specs.append(pl.BlockSpec(memory_space=pl.ANY))
        aliases = {len(args): 0}
        args.append(prev)
    return pl.pallas_call(
        _final_kernel,
        grid=(bsz // nsq, steps),
        in_specs=in_specs,
        out_specs=pl.BlockSpec((nsq, tm, D_MODEL), lambda b, i: (sb0 + b, i, 0)),
        out_shape=jax.ShapeDtypeStruct((bsz_total, t_len, D_MODEL), F32),
        input_output_aliases=aliases,
        compiler_params=_params("parallel", "arbitrary"),
        name="combine_final",
    )(*args)


def _t5_bucket(rel):
    half = NUM_BUCKETS // 2
    max_exact = half // 2
    ret = jnp.where(rel > 0, half, 0)
    n = jnp.abs(rel)
    nf = jnp.maximum(n, 1).astype(jnp.float32)
    large = max_exact + (jnp.log(nf / max_exact) / math.log(MAX_DISTANCE / max_exact)
                         * (half - max_exact)).astype(jnp.int32)
    large = jnp.minimum(large, half - 1)
    return ret + jnp.where(n < max_exact, n, large)


def _route(top_i, rank, sizes):
    n_tok = top_i.shape[1]
    n_asg = n_tok * TOP_K
    padded = (sizes + MOE_ROWS - 1) // MOE_ROWS * MOE_ROWS
    pend = jnp.cumsum(padded)
    pstart = pend - padded
    n_blocks = -(-n_asg // MOE_ROWS) + N_EXPERTS
    blk_row0 = jnp.arange(n_blocks, dtype=jnp.int32) * MOE_ROWS
    blk_expert = jnp.minimum(jnp.sum(pend[None, :] <= blk_row0[:, None], axis=1), N_EXPERTS - 1).astype(jnp.int32)
    blk_valid = jnp.clip(pstart[blk_expert] + sizes[blk_expert] - blk_row0, 0, MOE_ROWS).astype(jnp.int32)
    experts = jnp.arange(N_EXPERTS, dtype=jnp.int32)[:, None, None]
    pos = jnp.sum(jnp.where(top_i[None] == experts, pstart[:, None, None], 0), axis=0) + rank
    return pos, blk_expert, blk_valid, n_blocks * MOE_ROWS


def kernel(x_prompt, x_sample, c_prompt, c_sample, state_conv, state_delta, cache_swa_k, cache_swa_v,
           w_ada, b_ada, norm_mix, w_in, b_in, conv_w, a_log, dt_bias, dn_norm_w, sinks, rel_bias,
           w_out, b_out, norm_ffn, w_router, b_router, w1, b1, w2, b2, norm_final):
    depth = w_ada.shape[0]
    assert depth == 1, "the final norm is fused into the layer's combine step"
    bp, tp, _ = x_prompt.shape
    bs, ts, _ = x_sample.shape
    groups = [dict(x=x_prompt, c=c_prompt, hist=0), dict(x=x_sample, c=c_sample, hist=WIN_CHUNKS)]

    q_rel = jnp.arange(CHUNK)
    k_rel = jnp.arange(BAND) - WIN_CHUNKS * CHUNK
    bucket = _t5_bucket(k_rel[None, :] - q_rel[:, None]).astype(jnp.int32)
    bias = _rel_bias(bucket, rel_bias)

    nf_w = norm_final.reshape(1, D_MODEL)
    outs = {g: dict(conv=[], delta=[], k=[], v=[]) for g in range(2)}
    xs_cur = [x_prompt, x_sample]

    for l in range(depth):
        o1 = DN_CONV_DIM + DN_V_DIM
        o2 = o1 + 2 * DN_V_HEADS
        wl, bl = w_in[l], b_in[l]
        w_cat = jnp.concatenate(
            [wl[:, :o1], wl[:, o2:], wl[:, o1:o2], jnp.zeros((D_MODEL, GATE_LANES - 2 * DN_V_HEADS), F32)],
            axis=1).astype(BF16)
        b_cat = jnp.concatenate(
            [bl[:o1], bl[o2:], bl[o1:o2], jnp.zeros((GATE_LANES - 2 * DN_V_HEADS,), F32)]).reshape(1, PROJ_DIM)
        pad8 = jnp.zeros((DN_V_HEADS,), F32)
        padr = jnp.zeros((GATE_LANES - 2 * DN_V_HEADS,), F32)
        alog_l = jnp.concatenate([pad8, a_log[l], padr]).reshape(1, GATE_LANES)
        dtb_l = jnp.concatenate([pad8, dt_bias[l], padr]).reshape(1, GATE_LANES)
        wo1 = w_out[l][:DN_V_DIM].astype(BF16)
        wo2 = w_out[l][DN_V_DIM:].astype(BF16)
        wr = jnp.concatenate([w_router[l], jnp.zeros((D_MODEL, LANES - N_EXPERTS), F32)], axis=1)
        br = jnp.concatenate([b_router[l], jnp.full((LANES - N_EXPERTS,), NEG, F32)]).reshape(1, LANES)
        b1l = b1[l].reshape(N_EXPERTS, 1, 2 * D_FF)
        b2l = b2[l].reshape(N_EXPERTS, 1, D_MODEL)

        mod = _modulation(jnp.concatenate([c_prompt, c_sample], axis=0), w_ada[l], b_ada[l])
        att = []
        for gi, grp in enumerate(groups):
            x = xs_cur[gi]
            bsz, t_len, _ = x.shape
            tm = min(TOKEN_TILE, t_len)
            m = mod[:bp] if gi == 0 else mod[bp:]
            sh1, sc1, g1, sh2, sc2, g2 = [a.reshape(bsz, 1, D_MODEL) for a in jnp.split(m, 6, axis=-1)]
            if gi == 0:
                cprev = jnp.zeros((bsz, CONV_W - 1, DN_CONV_DIM), F32)
                s0 = jnp.zeros((bsz, DN_V_HEADS, DN_HEAD_DIM, DN_HEAD_DIM), F32)
            else:
                cprev = state_conv[l]
                s0 = state_delta[l]
            cprev8 = jnp.concatenate([jnp.zeros((bsz, 8 - (CONV_W - 1), DN_CONV_DIM), F32), cprev], axis=1)
            qkv, z, sq, sk, sv, ba, utail = _inproj(x, sc1, sh1, norm_mix[l].reshape(1, D_MODEL), w_cat, b_cat,
                                                    cprev8, conv_w[l], tm)
            if gi == 0:
                k_all, v_all = sk, sv
            else:
                k_all = jnp.concatenate([cache_swa_k[l].reshape(bsz, -1, SWA_KV_DIM), sk], axis=1)
                v_all = jnp.concatenate([cache_swa_v[l].reshape(bsz, -1, SWA_KV_DIM), sv], axis=1)
            swa_len = cache_swa_k.shape[2]
            o_dn, s_new = _deltanet(qkv, z, ba, s0, alog_l, dtb_l,
                                    jnp.tile(dn_norm_w[l], 2).reshape(1, 2 * DN_HEAD_DIM))
            sink_rows = jnp.repeat(sinks[l], CHUNK).reshape(SWA_KV_HEADS, SWA_GROUP * CHUNK, 1)
            o_swa = _swa(sq, k_all, v_all, bias.reshape(SWA_KV_HEADS, SWA_GROUP * CHUNK, BAND), sink_rows,
                         grp["hist"])
            conv_new = utail[:, 8 - (CONV_W - 1):]
            k_state = k_all[:, k_all.shape[1] - swa_len:].reshape(bsz, swa_len, SWA_KV_HEADS, SWA_HEAD_DIM)
            v_state = v_all[:, v_all.shape[1] - swa_len:].reshape(bsz, swa_len, SWA_KV_HEADS, SWA_HEAD_DIM)
            outs[gi]["conv"].append(conv_new)
            outs[gi]["delta"].append(s_new)
            outs[gi]["k"].append(k_state)
            outs[gi]["v"].append(v_state)
            att.append(dict(x=x, o_dn=o_dn, o_swa=o_swa, g1=g1, sc2=sc2, sh2=sh2, g2=g2, tm=tm,
                            bsz=bsz, t_len=t_len))

        n_first = bp * (MOE_PART_SPLIT[0]) // sum(MOE_PART_SPLIT)
        parts = ([[(1, 0, bs), (0, 0, n_first)], [(0, n_first, bp - n_first)]] if 0 < n_first < bp
                 else [[(1, 0, bs), (0, 0, bp)]])
        work = []
        for part in parts:
            cnt = jnp.zeros((1, LANES), F32)
            n_part = sum(nseq * att[gi]["t_len"] for gi, _, nseq in part)
            h2_all = jnp.zeros((n_part, D_MODEL // 2), jnp.uint32) if len(part) > 1 else None
            segs, row0 = [], 0
            for gi, b0, nseq in part:
                a = att[gi]
                x1, h2_all, route, cnt = _outproj(a["o_dn"], a["o_swa"], a["x"], a["g1"], a["sc2"], a["sh2"],
                                                  wo1, wo2, b_out[l].reshape(1, D_MODEL),
                                                  norm_ffn[l].reshape(1, D_MODEL), wr, br, cnt, a["tm"], b0, nseq,
                                                  n_part, row0, h2_all)
                segs.append(dict(gi=gi, b0=b0, nseq=nseq, x1=x1, tg=route))
                row0 += nseq * a["t_len"]
            route_all = jnp.concatenate(
                [s["tg"].reshape(-1, LANES)[:, :ROUTE_GATE].T for s in segs], axis=1).astype(jnp.int32)
            ti_all = route_all[ROUTE_EXPERT:ROUTE_EXPERT + TOP_K]
            rk_all = route_all[ROUTE_RANK:ROUTE_RANK + TOP_K]
            sizes = cnt[0, :N_EXPERTS].astype(jnp.int32)
            pos, blk_expert, blk_valid, n_rows = _route(ti_all, rk_all, sizes)
            xs = _scatter_rows(h2_all, pos, n_rows)
            work.append(dict(segs=segs, pos=pos, blk_expert=blk_expert, blk_valid=blk_valid, xs=xs,
                             n_tok=h2_all.shape[0]))
        for wk in work:
            wk["y"] = _moe_experts(wk["blk_expert"], wk["blk_valid"], wk["xs"], w1[l], b1l, w2[l], b2l)
        for wk in work:
            wk["yg"] = _gather_rows(wk["y"], wk["pos"].reshape(-1)).reshape(TOP_K, wk["n_tok"], D_MODEL // 2)
        y_out = [None, None]
        for wk in work:
            off = 0
            for s in wk["segs"]:
                a = att[s["gi"]]
                y_out[s["gi"]] = _final(s["x1"], wk["yg"], s["tg"], a["g2"], nf_w, min(FINAL_TILE, a["t_len"]),
                                        off, s["b0"], a["bsz"], prev=y_out[s["gi"]])
                off += s["nseq"] * a["t_len"]
        xs_cur = y_out

    res = [xs_cur[0], xs_cur[1]]
    for gi in range(2):
        for name in ("conv", "delta", "k", "v"):
            res.append(jnp.stack(outs[gi][name]))
    return tuple(res)
```

```python
import functools
import math

import jax
import jax.numpy as jnp
from jax import lax
from jax.experimental import pallas as pl
from jax.experimental.pallas import tpu as pltpu
from jax.experimental.pallas import tpu_sc as plsc

F32 = jnp.float32
BF16 = jnp.bfloat16

D_MODEL = 1024
CHUNK = 64
EPS = 1e-6
DN_QK_HEADS = 4
DN_V_HEADS = 8
DN_HEAD_DIM = 64
DN_QK_DIM = DN_QK_HEADS * DN_HEAD_DIM
DN_V_DIM = DN_V_HEADS * DN_HEAD_DIM
DN_CONV_DIM = 2 * DN_QK_DIM + DN_V_DIM
CONV_W = 4
SWA_HEADS = 8
SWA_KV_HEADS = 2
SWA_HEAD_DIM = 64
SWA_GROUP = SWA_HEADS // SWA_KV_HEADS
SWA_Q_DIM = SWA_HEADS * SWA_HEAD_DIM
SWA_KV_DIM = SWA_KV_HEADS * SWA_HEAD_DIM
WINDOW = 128
WIN_CHUNKS = WINDOW // CHUNK
BAND = (WIN_CHUNKS + 1) * CHUNK
NUM_BUCKETS = 32
MAX_DISTANCE = 128
N_EXPERTS = 32
TOP_K = 4
D_FF = 1024
SWIGLU_ALPHA = 1.702
SWIGLU_LIMIT = 7.0

LANES = 128
GATE_LANES = LANES
PROJ_DIM = DN_CONV_DIM + DN_V_DIM + SWA_Q_DIM + 2 * SWA_KV_DIM + GATE_LANES
MOE_ROWS = 1024
FINAL_TILE = 1024
MOE_PART_SPLIT = (1, 1)
TOKEN_TILE = 512
DN_SEQS_PER_STEP = 8
SWA_CHUNKS_PER_STEP = 16
SC_WINDOW = 64
V7X_VMEM_BYTES = 64 * 1024 * 1024
VMEM_LIMIT = V7X_VMEM_BYTES - 8 * 1024 * 1024
NEG = -1e30


def _params(*sem):
    return pltpu.CompilerParams(dimension_semantics=sem, vmem_limit_bytes=VMEM_LIMIT)


def _split2(a):
    hi = a.astype(BF16)
    lo = (a - hi.astype(F32)).astype(BF16)
    return hi, lo


def _dot(a, b):
    return jnp.dot(a, b, preferred_element_type=F32)


def _dot_x3(a, b):
    a1, a2 = _split2(a)
    b1, b2 = _split2(b)
    return _dot(a1, b1) + (_dot(a1, b2) + _dot(a2, b1))


def _dot_exact_lhs(l01, g):
    g1 = g.astype(BF16)
    r = g - g1.astype(F32)
    g2 = r.astype(BF16)
    g3 = (r - g2.astype(F32)).astype(BF16)
    return _dot(l01, g1) + (_dot(l01, g2) + _dot(l01, g3))


def _pack_bf16_pairs(x):
    w = x.shape[1] // 2
    bits = lax.bitcast_convert_type(x.astype(BF16).astype(F32), jnp.uint32)
    return (bits[:, w:] & jnp.uint32(0xFFFF0000)) | (bits[:, :w] >> 16)


def _unpack_bf16_pairs(p):
    lo = lax.bitcast_convert_type(p << 16, F32)
    hi = lax.bitcast_convert_type(p & jnp.uint32(0xFFFF0000), F32)
    return jnp.concatenate([lo, hi], axis=1)


def _silu(x):
    return x * jax.nn.sigmoid(x)


def _rms(x):
    return x * lax.rsqrt(jnp.mean(x * x, axis=-1, keepdims=True) + EPS)


def _mod_kernel(c_ref, w_ref, b_ref, o_ref):
    o_ref[...] = _dot_x3(_silu(c_ref[...]), w_ref[...]) + b_ref[...]


def _modulation(c, w_ada, b_ada):
    nb = c.shape[0]
    n_out = w_ada.shape[1]
    tn = 1024
    return pl.pallas_call(
        _mod_kernel,
        grid=(n_out // tn,),
        in_specs=[pl.BlockSpec((nb, D_MODEL), lambda j: (0, 0)),
                  pl.BlockSpec((D_MODEL, tn), lambda j: (0, j)),
                  pl.BlockSpec((1, tn), lambda j: (0, j))],
        out_specs=pl.BlockSpec((nb, tn), lambda j: (0, j)),
        out_shape=jax.ShapeDtypeStruct((nb, n_out), F32),
        compiler_params=_params("arbitrary"),
        name="modulation",
    )(c, w_ada, b_ada.reshape(1, n_out))


def _inproj_kernel(x_ref, sc_ref, sh_ref, nw_ref, w_ref, b_ref,
                   cprev_ref, cw_ref, qkv_ref, z_ref, sq_ref, sk_ref, sv_ref, ba_ref, utail_ref, xc_scr):
    nsq, tm, _ = x_ref.shape
    rows = nsq * tm

    @pl.when(pl.program_id(1) == 0)
    def _():
        xc_scr[:, 0:8, :] = cprev_ref[...]

    h = _rms(x_ref[...]) * nw_ref[...]
    h = (h * (1.0 + sc_ref[...]) + sh_ref[...]).reshape(rows, D_MODEL)
    p = _dot(h.astype(BF16), w_ref[...]) + b_ref[...]
    o = DN_CONV_DIM
    for ref in (z_ref, sq_ref, sk_ref, sv_ref, ba_ref):
        w = ref.shape[-1]
        ref[...] = p[:, o:o + w].astype(ref.dtype).reshape(nsq, tm, w)
        o += w

    xc_scr[:, 8:8 + tm, :] = p[:, :DN_CONV_DIM].reshape(nsq, tm, DN_CONV_DIM)
    conv = xc_scr[:, 5:5 + tm, :] * cw_ref[0:1, :]
    for j in range(1, CONV_W):
        conv = conv + xc_scr[:, 5 + j:5 + j + tm, :] * cw_ref[j:j + 1, :]
    tail = xc_scr[:, tm:tm + 8, :]
    xc_scr[:, 0:8, :] = tail
    utail_ref[...] = tail
    cu = _silu(conv)

    pw = 2 * DN_HEAD_DIM
    lo = lax.broadcasted_iota(jnp.int32, (nsq, tm, pw), 2) < DN_HEAD_DIM
    for c in range(DN_QK_DIM // pw):
        qc = cu[:, :, c * pw:(c + 1) * pw]
        kc = cu[:, :, DN_QK_DIM + c * pw:DN_QK_DIM + (c + 1) * pw]
        qkv_ref[:, :, c * pw:(c + 1) * pw] = (
            qc * lax.rsqrt(_half_sums(qc * qc, lo) + EPS) * (DN_HEAD_DIM ** -0.5))
        qkv_ref[:, :, DN_QK_DIM + c * pw:DN_QK_DIM + (c + 1) * pw] = kc * lax.rsqrt(_half_sums(kc * kc, lo) + EPS)
    qkv_ref[:, :, 2 * DN_QK_DIM:] = cu[:, :, 2 * DN_QK_DIM:]


def _inproj(x, sc, sh, norm_w, w_cat, b_cat, cprev8, conv_w, tm):
    bsz, t_len, _ = x.shape
    widths = (DN_CONV_DIM, DN_V_DIM, SWA_Q_DIM, SWA_KV_DIM, SWA_KV_DIM, GATE_LANES)
    nsq = _seqs_per_tile(t_len, tm, bsz)
    tok = lambda b, i: (b, i, 0)
    per_b = lambda b, i: (b, 0, 0)
    fixed = lambda b, i: (0, 0)
    out_shape = [jax.ShapeDtypeStruct((bsz, t_len, w), BF16 if i == 2 else F32) for i, w in enumerate(widths)]
    return pl.pallas_call(
        _inproj_kernel,
        grid=(bsz // nsq, t_len // tm),
        in_specs=[pl.BlockSpec((nsq, tm, D_MODEL), tok),
                  pl.BlockSpec((nsq, 1, D_MODEL), per_b),
                  pl.BlockSpec((nsq, 1, D_MODEL), per_b),
                  pl.BlockSpec((1, D_MODEL), fixed),
                  pl.BlockSpec((D_MODEL, PROJ_DIM), fixed),
                  pl.BlockSpec((1, PROJ_DIM), fixed),
                  pl.BlockSpec((nsq, 8, DN_CONV_DIM), per_b),
                  pl.BlockSpec((CONV_W, DN_CONV_DIM), fixed)],
        out_specs=([pl.BlockSpec((nsq, tm, w), tok) for w in widths]
                   + [pl.BlockSpec((nsq, 8, DN_CONV_DIM), per_b)]),
        out_shape=out_shape + [jax.ShapeDtypeStruct((bsz, 8, DN_CONV_DIM), F32)],
        scratch_shapes=[pltpu.VMEM((nsq, tm + 8, DN_CONV_DIM), F32)],
        compiler_params=_params("parallel", "arbitrary"),
        name="inproj",
    )(x, sc, sh, norm_w, w_cat, b_cat, cprev8, conv_w)


def _bmm(a, b):
    return lax.dot_general(a.astype(BF16), b.astype(BF16), (((2,), (1,)), ((0,), (0,))),
                           preferred_element_type=F32)


def _bmm_nt(a, b):
    return lax.dot_general(a.astype(BF16), b.astype(BF16), (((2,), (2,)), ((0,), (0,))),
                           preferred_element_type=F32)


def _bd(x):
    x = x.astype(BF16)
    lo = lax.broadcasted_iota(jnp.int32, x.shape, 2) < x.shape[2] // 2
    zero = jnp.zeros_like(x)
    return jnp.concatenate([jnp.where(lo, x, zero), jnp.where(lo, zero, x)], axis=1)


def _pmm(a, b):
    return _bmm(a, _bd(b))


def _half_sums(x, lo):
    s_lo = jnp.sum(jnp.where(lo, x, 0.0), axis=-1, keepdims=True)
    s_hi = jnp.sum(jnp.where(lo, 0.0, x), axis=-1, keepdims=True)
    return jnp.where(lo, s_lo, s_hi)


def _deltanet_pair_kernel(u_ref, z_ref, ba_ref, s0_ref, alog_ref, dtb_ref, nw_ref,
                          o_ref, sfin_ref, s_scr, *, nb):
    n = pl.program_id(1)
    hd = DN_HEAD_DIM
    pw = 2 * hd
    npair = DN_QK_HEADS

    @pl.when(n == 0)
    def _():
        for i in range(nb):
            for j in range(npair):
                s_scr[i * npair + j] = jnp.concatenate([s0_ref[i, 2 * j], s0_ref[i, 2 * j + 1]], axis=1)

    row = lax.broadcasted_iota(jnp.int32, (CHUNK, pw), 0)
    col = lax.broadcasted_iota(jnp.int32, (CHUNK, pw), 1) % hd
    lo = lax.broadcasted_iota(jnp.int32, (CHUNK, pw), 1) < hd
    incl = row >= col
    strict = row > col
    r1 = lax.broadcasted_iota(jnp.int32, (CHUNK, CHUNK), 0)
    c1 = lax.broadcasted_iota(jnp.int32, (CHUNK, CHUNK), 1)
    lower01 = jnp.where(r1 >= c1, 1.0, 0.0).astype(BF16)
    eye = jnp.where(row == col, 1.0, 0.0).astype(F32)
    merge_masks = [(row // 2 == col // 2) & strict]
    blk = 2
    while blk < CHUNK:
        merge_masks.append((row // (2 * blk) == col // (2 * blk)) & (row // blk != col // blk) & strict)
        blk *= 2

    qk_lhs, qk_rhs, q_items, k_items, v_items, z_items = [], [], [], [], [], []
    zt_items, grow_items = [], []
    for i in range(nb):
        ba = ba_ref[i]
        beta_all = jax.nn.sigmoid(ba)
        sp = ba + dtb_ref[...]
        sp = jnp.maximum(sp, 0.0) + jnp.log1p(jnp.exp(-jnp.abs(sp)))
        g_all = -jnp.exp(alog_ref[...]) * sp
        gc_all = _dot_exact_lhs(lower01, g_all)
        gc_t = gc_all.T
        beta_t = beta_all.T

        for c in range(DN_QK_DIM // pw):
            qc = u_ref[i, :, c * pw:(c + 1) * pw]
            kc = u_ref[i, :, DN_QK_DIM + c * pw:DN_QK_DIM + (c + 1) * pw]
            qr = pltpu.roll(qc, hd, axis=1)
            kr = pltpu.roll(kc, hd, axis=1)
            for half in range(2):
                sel = lo if half == 0 else jnp.logical_not(lo)
                q_items.append(jnp.where(sel, qc, qr))
                k_items.append(jnp.where(sel, kc, kr))
                km = jnp.where(sel, kc, 0.0)
                qk_lhs.append(jnp.concatenate([qc, kc], axis=0))
                qk_rhs.append(jnp.concatenate([km, km], axis=0))
        for j in range(npair):
            a, b = 2 * j, 2 * j + 1
            v_items.append(u_ref[i, :, 2 * DN_QK_DIM + j * pw:2 * DN_QK_DIM + (j + 1) * pw])
            z_items.append(z_ref[i, :, j * pw:(j + 1) * pw])
            ra = jnp.concatenate([beta_t[a:a + 1, :], gc_t[8 + a:9 + a, :]], axis=1)
            rb = jnp.concatenate([beta_t[b:b + 1, :], gc_t[8 + b:9 + b, :]], axis=1)
            zt_items.append(jnp.concatenate([jnp.broadcast_to(ra, (CHUNK, pw)),
                                             jnp.broadcast_to(rb, (CHUNK, pw))], axis=0).T)
            grow_items.append(jnp.broadcast_to(
                jnp.concatenate([gc_t[8 + a:9 + a, :], gc_t[8 + b:9 + b, :]], axis=1), (CHUNK, pw)))

    q = jnp.stack(q_items)
    k = jnp.stack(k_items)
    v = jnp.stack(v_items)
    cols = jnp.stack(zt_items)
    beta, gcol = cols[:, :CHUNK], cols[:, CHUNK:]
    grow = jnp.stack(grow_items)
    glast = gcol[:, CHUNK - 1:CHUNK, :]

    qkk = _bmm_nt(jnp.stack(qk_lhs), jnp.stack(qk_rhs))
    qk, kk = qkk[:, :CHUNK], qkk[:, CHUNK:]

    decay = jnp.exp(jnp.where(incl, gcol - grow, NEG))
    eg = jnp.exp(gcol)

    m = jnp.where(strict, (kk * beta) * decay, 0.0)
    t = eye - jnp.where(merge_masks[0], m, 0.0)
    for mask in merge_masks[1:]:
        t = t - _pmm(t, _pmm(jnp.where(mask, m, 0.0), t))
    t0 = t.astype(BF16)
    t0_bd = _bd(t0)
    m_hi, m_lo = _split2(m)
    mt0 = _bmm(jnp.concatenate([m_hi, m_lo], axis=2), jnp.concatenate([t0_bd, t0_bd], axis=1))
    resid = eye - t0.astype(F32) - mt0
    t1 = _pmm(t0, resid).astype(BF16)

    s = s_scr[...]
    ks = _pmm(jnp.concatenate([k * (beta * eg), q * eg], axis=1), s)
    rhs_bd = _bd(v * beta - ks[:, :CHUNK])
    vnew = _bmm(jnp.concatenate([t0, t1], axis=2), jnp.concatenate([rhs_bd, rhs_bd], axis=1))
    o = ks[:, CHUNK:] + _pmm(qk * decay, vnew)
    kd = k * jnp.exp(glast - gcol)
    kv = _bmm(jnp.swapaxes(kd, 1, 2), vnew)
    s_scr[...] = s * jnp.exp(glast) + jnp.where(lo, kv[:, :hd], kv[:, hd:])

    o = o * lax.rsqrt(_half_sums(o * o, lo) * (1.0 / hd) + EPS) * nw_ref[...] * _silu(jnp.stack(z_items))
    for i in range(nb):
        for j in range(npair):
            o_ref[i, :, j * pw:(j + 1) * pw] = o[i * npair + j].astype(o_ref.dtype)

    @pl.when(n == pl.num_programs(1) - 1)
    def _():
        for i in range(nb):
            for j in range(npair):
                sp2 = s_scr[i * npair + j]
                sfin_ref[i, 2 * j] = sp2[:, :hd]
                sfin_ref[i, 2 * j + 1] = sp2[:, hd:]


def _deltanet(qkv, z, ba, s0, alog_l, dtb_l, dn_norm_w):
    bsz, t_len, _ = qkv.shape
    nb = DN_SEQS_PER_STEP
    tok = lambda b, n: (b, n, 0)
    per_b4 = lambda b, n: (b, 0, 0, 0)
    fixed = lambda b, n: (0, 0)
    return pl.pallas_call(
        functools.partial(_deltanet_pair_kernel, nb=nb),
        grid=(bsz // nb, t_len // CHUNK),
        in_specs=[pl.BlockSpec((nb, CHUNK, DN_CONV_DIM), tok),
                  pl.BlockSpec((nb, CHUNK, DN_V_DIM), tok),
                  pl.BlockSpec((nb, CHUNK, GATE_LANES), tok),
                  pl.BlockSpec((nb, DN_V_HEADS, DN_HEAD_DIM, DN_HEAD_DIM), per_b4),
                  pl.BlockSpec((1, GATE_LANES), fixed),
                  pl.BlockSpec((1, GATE_LANES), fixed),
                  pl.BlockSpec((1, 2 * DN_HEAD_DIM), fixed)],
        out_specs=[pl.BlockSpec((nb, CHUNK, DN_V_DIM), tok),
                   pl.BlockSpec((nb, DN_V_HEADS, DN_HEAD_DIM, DN_HEAD_DIM), per_b4)],
        out_shape=[jax.ShapeDtypeStruct((bsz, t_len, DN_V_DIM), BF16),
                   jax.ShapeDtypeStruct((bsz, DN_V_HEADS, DN_HEAD_DIM, DN_HEAD_DIM), F32)],
        scratch_shapes=[pltpu.VMEM((nb * DN_QK_HEADS, DN_HEAD_DIM, 2 * DN_HEAD_DIM), F32)],
        compiler_params=_params("parallel", "arbitrary"),
        name="deltanet",
    )(qkv, z, ba, s0, alog_l, dtb_l, dn_norm_w)


def _bias_kernel(bucket_ref, table_ref, o_ref):
    bucket = bucket_ref[...]
    for h in range(SWA_HEADS):
        acc = jnp.zeros(bucket.shape, F32)
        for b in range(NUM_BUCKETS):
            acc = jnp.where(bucket == b, table_ref[b, h], acc)
        o_ref[h] = acc


def _rel_bias(bucket, table):
    return pl.pallas_call(
        _bias_kernel,
        in_specs=[pl.BlockSpec(memory_space=pltpu.VMEM),
                  pl.BlockSpec(memory_space=pltpu.SMEM)],
        out_specs=pl.BlockSpec(memory_space=pltpu.VMEM),
        out_shape=jax.ShapeDtypeStruct((SWA_HEADS, CHUNK, BAND), F32),
        name="rel_bias",
    )(bucket, table)


def _swa_kernel(q_ref, *refs, hist, cps, n_units):
    k_refs, v_refs = refs[:n_units], refs[n_units:2 * n_units]
    bias_ref, sink_ref, o_ref = refs[2 * n_units:]
    n = pl.program_id(1)
    hd = SWA_HEAD_DIM
    kb = jnp.concatenate([r[0] for r in k_refs], axis=0)
    vb = jnp.concatenate([r[0] for r in v_refs], axis=0)
    row0 = (n * cps + hist) * CHUNK - WINDOW
    key = lax.broadcasted_iota(jnp.int32, (1, 1, BAND), 2)
    q_items, k_items, v_items, valid = [], [], [], []
    for c in range(cps):
        for kv in range(SWA_KV_HEADS):
            q_items.append(jnp.concatenate(
                [q_ref[0, c * CHUNK:(c + 1) * CHUNK, (kv * SWA_GROUP + g) * hd:(kv * SWA_GROUP + g + 1) * hd]
                 for g in range(SWA_GROUP)], axis=0))
            k_items.append(kb[c * CHUNK:c * CHUNK + BAND, kv * hd:(kv + 1) * hd])
            v_items.append(vb[c * CHUNK:c * CHUNK + BAND, kv * hd:(kv + 1) * hd])
            valid.append(row0 + c * CHUNK + key >= 0)
    s = _bmm_nt(jnp.stack(q_items), jnp.stack(k_items)) * (hd ** -0.5) + bias_ref[...]
    s = jnp.where(jnp.concatenate(valid, axis=0), s, NEG)
    sink = sink_ref[...]
    mx = jnp.maximum(jnp.max(s, axis=-1, keepdims=True), sink)
    p = jnp.exp(s - mx).astype(BF16)
    den = _bmm(p, jnp.ones((len(v_items), BAND, hd), BF16)) + jnp.exp(sink - mx)
    o = _bmm(p, jnp.stack(v_items)) / den
    for c in range(cps):
        for kv in range(SWA_KV_HEADS):
            for g in range(SWA_GROUP):
                h = kv * SWA_GROUP + g
                o_ref[0, c * CHUNK:(c + 1) * CHUNK, h * hd:(h + 1) * hd] = (
                    o[c * SWA_KV_HEADS + kv, g * CHUNK:(g + 1) * CHUNK].astype(o_ref.dtype))


def _swa(q, k_all, v_all, bias, sink_rows, hist):
    bsz, t_len, _ = q.shape
    cps = min(SWA_CHUNKS_PER_STEP, t_len // CHUNK)
    unit = WINDOW if cps * CHUNK % WINDOW == 0 else CHUNK
    assert (hist * CHUNK - WINDOW) % unit == 0 and (cps * CHUNK) % unit == 0
    n_units = (WINDOW + cps * CHUNK) // unit
    q_units = cps * CHUNK // unit
    unit0 = (hist * CHUNK - WINDOW) // unit
    tok = lambda b, n: (b, n, 0)

    def band(j):
        return lambda b, n: (b, jnp.maximum(n * q_units + unit0 + j, 0), 0)

    kv_specs = [pl.BlockSpec((1, unit, SWA_KV_DIM), band(j)) for j in range(n_units)]
    fixed3 = lambda b, n: (0, 0, 0)
    return pl.pallas_call(
        functools.partial(_swa_kernel, hist=hist, cps=cps, n_units=n_units),
        grid=(bsz, t_len // (cps * CHUNK)),
        in_specs=[pl.BlockSpec((1, cps * CHUNK, SWA_Q_DIM), tok)] + kv_specs + kv_specs
                 + [pl.BlockSpec((cps * SWA_KV_HEADS, SWA_GROUP * CHUNK, BAND), fixed3),
                    pl.BlockSpec((cps * SWA_KV_HEADS, SWA_GROUP * CHUNK, 1), fixed3)],
        out_specs=pl.BlockSpec((1, cps * CHUNK, SWA_Q_DIM), tok),
        out_shape=jax.ShapeDtypeStruct((bsz, t_len, SWA_Q_DIM), BF16),
        compiler_params=_params("parallel", "arbitrary"),
        name="swa",
    )(q, *([k_all] * n_units), *([v_all] * n_units),
      jnp.tile(bias, (cps, 1, 1)), jnp.tile(sink_rows, (cps, 1, 1)))


ROUTE_EXPERT, ROUTE_RANK, ROUTE_GATE = 0, TOP_K, 2 * TOP_K


def _outproj_kernel(odn_ref, oswa_ref, x_ref, g1_ref, sc_ref, sh_ref, wo1_ref, wo2_ref, bo_ref,
                    nw_ref, wr_ref, br_ref, cnt0_ref, *rest):
    x1_ref, h2_ref, route_ref, cnt_ref, cnt_scr, tri_scr = rest[-6:]
    first = (pl.program_id(0) == 0) & (pl.program_id(1) == 0)

    @pl.when(first)
    def _():
        cnt_scr[...] = cnt0_ref[...]
        r_i = lax.broadcasted_iota(jnp.int32, tri_scr.shape, 0)
        c_i = lax.broadcasted_iota(jnp.int32, tri_scr.shape, 1)
        tri_scr[...] = jnp.where(r_i > c_i, 1.0, 0.0).astype(BF16)

    nsq, tm, _ = x_ref.shape
    rows = nsq * tm
    mix = (_dot(odn_ref[...].reshape(rows, DN_V_DIM).astype(BF16), wo1_ref[...])
           + _dot(oswa_ref[...].reshape(rows, SWA_Q_DIM).astype(BF16), wo2_ref[...]))
    x1 = x_ref[...] + g1_ref[...] * (mix + bo_ref[...]).reshape(nsq, tm, D_MODEL)
    x1_ref[...] = x1
    h2 = _rms(x1) * nw_ref[...]
    h2 = (h2 * (1.0 + sc_ref[...]) + sh_ref[...]).reshape(rows, D_MODEL)
    h2_ref[...] = _pack_bf16_pairs(h2)

    logits = _dot_x3(h2, wr_ref[...]) + br_ref[...]
    lane = lax.broadcasted_iota(jnp.int32, logits.shape, 1)
    lane_f = lane.astype(F32)
    vals, idxs = [], []
    for _ in range(TOP_K):
        m = jnp.max(logits, axis=-1, keepdims=True)
        i = jnp.min(jnp.where(logits == m, lane_f, float(LANES)), axis=-1, keepdims=True)
        vals.append(m)
        idxs.append(i)
        logits = jnp.where(lane_f == i, -jnp.inf, logits)
    es = [jnp.exp(v - vals[0]) for v in vals]
    den = es[0] + es[1] + es[2] + es[3]
    onehot = jnp.zeros(lane.shape, F32)
    for kk in range(TOP_K):
        onehot = jnp.where(lane_f == idxs[kk], 1.0, onehot)
    before = _dot(tri_scr[...], onehot.astype(BF16)) + cnt_scr[...]
    cnt_scr[...] = cnt_scr[...] + jnp.sum(onehot, axis=0, keepdims=True)

    route = jnp.zeros(lane.shape, F32)
    for kk in range(TOP_K):
        rank = jnp.sum(jnp.where(lane_f == idxs[kk], before, 0.0), axis=-1, keepdims=True)
        route = jnp.where(lane == ROUTE_EXPERT + kk, idxs[kk], route)
        route = jnp.where(lane == ROUTE_RANK + kk, rank, route)
        route = jnp.where(lane == ROUTE_GATE + kk, es[kk] / den, route)
    route_ref[...] = route.reshape(nsq, tm, LANES)
    cnt_ref[...] = cnt_scr[...]


def _seqs_per_tile(t_len, tm, bsz):
    nsq = max(1, TOKEN_TILE // t_len) if tm == t_len else 1
    while bsz % nsq:
        nsq -= 1
    return nsq


def _outproj(odn, oswa, x, g1, sc, sh, wo1, wo2, bo, norm_w, wr, br, cnt0, tm, b0, bsz, h2_rows, h2_row0, h2_prev):
    t_len = x.shape[1]
    steps = t_len // tm
    nsq = _seqs_per_tile(t_len, tm, bsz)
    rows = nsq * tm
    assert h2_row0 % rows == 0 and b0 % nsq == 0
    blk0 = h2_row0 // rows
    sb0 = b0 // nsq
    tok_out = lambda b, i: (b, i, 0)
    tok = lambda b, i: (sb0 + b, i, 0)
    per_b = lambda b, i: (sb0 + b, 0, 0)
    fixed = lambda b, i: (0, 0)
    in_specs = [pl.BlockSpec((nsq, tm, DN_V_DIM), tok),
                pl.BlockSpec((nsq, tm, SWA_Q_DIM), tok),
                pl.BlockSpec((nsq, tm, D_MODEL), tok),
                pl.BlockSpec((nsq, 1, D_MODEL), per_b),
                pl.BlockSpec((nsq, 1, D_MODEL), per_b),
                pl.BlockSpec((nsq, 1, D_MODEL), per_b),
                pl.BlockSpec((DN_V_DIM, D_MODEL), fixed),
                pl.BlockSpec((SWA_Q_DIM, D_MODEL), fixed),
                pl.BlockSpec((1, D_MODEL), fixed),
                pl.BlockSpec((1, D_MODEL), fixed),
                pl.BlockSpec((D_MODEL, LANES), fixed),
                pl.BlockSpec((1, LANES), fixed),
                pl.BlockSpec((1, LANES), fixed)]
    args = [odn, oswa, x, g1, sc, sh, wo1, wo2, bo, norm_w, wr, br, cnt0]
    aliases = {}
    if h2_prev is not None:
        in_specs.append(pl.BlockSpec(memory_space=pl.ANY))
        aliases = {len(args): 1}
        args.append(h2_prev)
    return pl.pallas_call(
        _outproj_kernel,
        grid=(bsz // nsq, steps),
        in_specs=in_specs,
        out_specs=[pl.BlockSpec((nsq, tm, D_MODEL), tok_out),
                   pl.BlockSpec((rows, D_MODEL // 2), lambda b, i: (blk0 + b * steps + i, 0)),
                   pl.BlockSpec((nsq, tm, LANES), tok_out),
                   pl.BlockSpec((1, LANES), fixed)],
        out_shape=[jax.ShapeDtypeStruct((bsz, t_len, D_MODEL), F32),
                   jax.ShapeDtypeStruct((h2_rows, D_MODEL // 2), jnp.uint32),
                   jax.ShapeDtypeStruct((bsz, t_len, LANES), F32),
                   jax.ShapeDtypeStruct((1, LANES), F32)],
        scratch_shapes=[pltpu.VMEM((1, LANES), F32), pltpu.VMEM((rows, rows), BF16)],
        input_output_aliases=aliases,
        compiler_params=_params("arbitrary", "arbitrary"),
        name="outproj_router",
    )(*args)


def _moe_kernel(be_ref, nv_ref, xs_ref, w1_ref, b1_ref, w2_ref, b2_ref, y_ref):
    del be_ref
    nv = nv_ref[pl.program_id(0)]

    def ffn(n_rows):
        rows = lax.broadcasted_iota(jnp.int32, (n_rows, 1), 0)
        xb = jnp.where(rows < nv, _unpack_bf16_pairs(xs_ref[0:n_rows, :]), 0.0).astype(BF16)
        xf = xb.astype(F32)
        glu = _dot(xf, w1_ref[0, :, :D_FF]) + b1_ref[0, :, :D_FF]
        lin = _dot(xf, w1_ref[0, :, D_FF:]) + b1_ref[0, :, D_FF:]
        glu = jnp.minimum(glu, SWIGLU_LIMIT)
        lin = jnp.clip(lin, -SWIGLU_LIMIT, SWIGLU_LIMIT)
        act = glu * jax.nn.sigmoid(SWIGLU_ALPHA * glu) * (lin + 1.0)
        y = _dot(act.astype(BF16).astype(F32), w2_ref[0]) + b2_ref[0]
        y_ref[0:n_rows, :] = _pack_bf16_pairs(y)

    half = MOE_ROWS // 2

    @pl.when(nv == 0)
    def _():
        y_ref[...] = jnp.zeros(y_ref.shape, y_ref.dtype)

    @pl.when((nv > 0) & (nv <= half))
    def _():
        ffn(half)
        y_ref[half:, :] = jnp.zeros((MOE_ROWS - half, y_ref.shape[1]), y_ref.dtype)

    @pl.when(nv > half)
    def _():
        ffn(MOE_ROWS)


def _moe_experts(blk_expert, blk_valid, xs, w1, b1, w2, b2):
    n_rows = xs.shape[0]
    n_blocks = n_rows // MOE_ROWS
    half = D_MODEL // 2
    grid_spec = pltpu.PrefetchScalarGridSpec(
        num_scalar_prefetch=2,
        grid=(n_blocks,),
        in_specs=[pl.BlockSpec((MOE_ROWS, half), lambda i, be, nv: (i, 0)),
                  pl.BlockSpec((1, D_MODEL, 2 * D_FF), lambda i, be, nv: (be[i], 0, 0)),
                  pl.BlockSpec((1, 1, 2 * D_FF), lambda i, be, nv: (be[i], 0, 0)),
                  pl.BlockSpec((1, D_FF, D_MODEL), lambda i, be, nv: (be[i], 0, 0)),
                  pl.BlockSpec((1, 1, D_MODEL), lambda i, be, nv: (be[i], 0, 0))],
        out_specs=pl.BlockSpec((MOE_ROWS, half), lambda i, be, nv: (i, 0)),
    )
    return pl.pallas_call(
        _moe_kernel,
        grid_spec=grid_spec,
        out_shape=jax.ShapeDtypeStruct((n_rows, half), jnp.uint32),
        compiler_params=_params("arbitrary"),
        name="moe_experts",
    )(blk_expert, blk_valid, xs, w1, b1, w2, b2)


def _sc_mesh():
    return plsc.VectorSubcoreMesh(core_axis_name="core", subcore_axis_name="subcore")


def _gather_rows(x, idx):
    m = idx.shape[0]
    w = x.shape[1]

    @pl.kernel(out_type=jax.ShapeDtypeStruct((m, w), x.dtype), mesh=_sc_mesh())
    def gather_kernel(x_hbm, i_hbm, o_hbm):
        def body(i_vmem, o_vmem):
            pltpu.sync_copy(x_hbm.at[i_vmem.at[0]], o_vmem)

        pltpu.emit_pipeline(
            body,
            grid=(m // SC_WINDOW,),
            in_specs=[pl.BlockSpec((1, SC_WINDOW), lambda i: (i, 0))],
            out_specs=[pl.BlockSpec((SC_WINDOW, w), lambda i: (i, 0))],
            core_axis_name=("core", "subcore"),
            dimension_semantics=(pltpu.PARALLEL,),
        )(i_hbm, o_hbm)

    return gather_kernel(x, idx.reshape(m // SC_WINDOW, SC_WINDOW))


def _scatter_rows(x, idx, n_out):
    n, w = x.shape
    kk = idx.shape[0]
    idx3 = jnp.transpose(idx.reshape(kk, n // SC_WINDOW, SC_WINDOW), (1, 0, 2))

    @pl.kernel(out_type=jax.ShapeDtypeStruct((n_out, w), x.dtype), mesh=_sc_mesh())
    def scatter_kernel(x_hbm, i_hbm, o_hbm):
        def body(x_vmem, i_vmem):
            for k in range(kk):
                pltpu.sync_copy(x_vmem, o_hbm.at[i_vmem.at[0, k]])

        pltpu.emit_pipeline(
            body,
            grid=(n // SC_WINDOW,),
            in_specs=[pl.BlockSpec((SC_WINDOW, w), lambda i: (i, 0)),
                      pl.BlockSpec((1, kk, SC_WINDOW), lambda i: (i, 0, 0))],
            out_specs=[],
            core_axis_name=("core", "subcore"),
            dimension_semantics=(pltpu.PARALLEL,),
        )(x_hbm, i_hbm)

    return scatter_kernel(x, idx3)


def _final_kernel(x1_ref, yg_ref, tg_ref, g2_ref, nw_ref, *rest):
    y_ref = rest[-1]
    nsq, tm, _ = x1_ref.shape
    tg = tg_ref[...].reshape(nsq * tm, LANES)
    moe = _unpack_bf16_pairs(yg_ref[0]) * tg[:, ROUTE_GATE:ROUTE_GATE + 1]
    for kk in range(1, TOP_K):
        moe = moe + _unpack_bf16_pairs(yg_ref[kk]) * tg[:, ROUTE_GATE + kk:ROUTE_GATE + kk + 1]
    x2 = x1_ref[...] + g2_ref[...] * moe.reshape(nsq, tm, D_MODEL)
    y_ref[...] = _rms(x2) * nw_ref[...]


def _final(x1, yg, tg, g2, norm_w, tm, tok_offset, b0, bsz_total, prev=None):
    bsz, t_len, _ = x1.shape
    tok = lambda b, i: (b, i, 0)
    steps = t_len // tm
    nsq = _seqs_per_tile(t_len, tm, bsz)
    rows = nsq * tm
    assert tok_offset % rows == 0 and b0 % nsq == 0
    blk0 = tok_offset // rows
    sb0 = b0 // nsq
    in_specs = [pl.BlockSpec((nsq, tm, D_MODEL), tok, pipeline_mode=pl.Buffered(3)),
                pl.BlockSpec((TOP_K, rows, D_MODEL // 2), lambda b, i: (0, blk0 + b * steps + i, 0),
                             pipeline_mode=pl.Buffered(3)),
                pl.BlockSpec((nsq, tm, LANES), tok),
                pl.BlockSpec((nsq, 1, D_MODEL), lambda b, i: (sb0 + b, 0, 0)),
                pl.BlockSpec((1, D_MODEL), lambda b, i: (0, 0))]
    out_spec = pl.BlockSpec((nsq, tm, D_MODEL), lambda b, i: (sb0 + b, i, 0))
    args = [x1, yg, tg, g2, norm_w]
    aliases = {}
    if prev is not None:
        aliases = {len(args): 0}
        args.append(prev)

    def outer(x1_hbm, yg_hbm, tg_hbm, g2_hbm, nw_hbm, *rest):
        pltpu.emit_pipeline(_final_kernel, grid=(bsz // nsq, steps), in_specs=in_specs, out_specs=[out_spec])(
            x1_hbm, yg_hbm, tg_hbm, g2_hbm, nw_hbm, rest[-1])

    return pl.pallas_call(
        outer,
        in_specs=[pl.BlockSpec(memory_space=pl.ANY)] * len(args),
        out_specs=pl.BlockSpec(memory_space=pl.ANY),
        out_shape=jax.ShapeDtypeStruct((bsz_total, t_len, D_MODEL), F32),
        input_output_aliases=aliases,
        compiler_params=pltpu.CompilerParams(vmem_limit_bytes=VMEM_LIMIT),
        name="combine_final",
    )(*args)


def _t5_bucket(rel):
    half = NUM_BUCKETS // 2
    max_exact = half // 2
    ret = jnp.where(rel > 0, half, 0)
    n = jnp.abs(rel)
    nf = jnp.maximum(n, 1).astype(jnp.float32)
    large = max_exact + (jnp.log(nf / max_exact) / math.log(MAX_DISTANCE / max_exact)
                         * (half - max_exact)).astype(jnp.int32)
    large = jnp.minimum(large, half - 1)
    return ret + jnp.where(n < max_exact, n, large)


def _route(top_i, rank, sizes):
    n_tok = top_i.shape[1]
    n_asg = n_tok * TOP_K
    padded = (sizes + MOE_ROWS - 1) // MOE_ROWS * MOE_ROWS
    pend = jnp.cumsum(padded)
    pstart = pend - padded
    n_blocks = -(-n_asg // MOE_ROWS) + N_EXPERTS
    blk_row0 = jnp.arange(n_blocks, dtype=jnp.int32) * MOE_ROWS
    blk_expert = jnp.minimum(jnp.sum(pend[None, :] <= blk_row0[:, None], axis=1), N_EXPERTS - 1).astype(jnp.int32)
    blk_valid = jnp.clip(pstart[blk_expert] + sizes[blk_expert] - blk_row0, 0, MOE_ROWS).astype(jnp.int32)
    experts = jnp.arange(N_EXPERTS, dtype=jnp.int32)[:, None, None]
    pos = jnp.sum(jnp.where(top_i[None] == experts, pstart[:, None, None], 0), axis=0) + rank
    return pos, blk_expert, blk_valid, n_blocks * MOE_ROWS


def kernel(x_prompt, x_sample, c_prompt, c_sample, state_conv, state_delta, cache_swa_k, cache_swa_v,
           w_ada, b_ada, norm_mix, w_in, b_in, conv_w, a_log, dt_bias, dn_norm_w, sinks, rel_bias,
           w_out, b_out, norm_ffn, w_router, b_router, w1, b1, w2, b2, norm_final):
    depth = w_ada.shape[0]
    assert depth == 1, "the final norm is fused into the layer's combine step"
    bp, tp, _ = x_prompt.shape
    bs, ts, _ = x_sample.shape
    groups = [dict(x=x_prompt, c=c_prompt, hist=0), dict(x=x_sample, c=c_sample, hist=WIN_CHUNKS)]

    q_rel = jnp.arange(CHUNK)
    k_rel = jnp.arange(BAND) - WIN_CHUNKS * CHUNK
    bucket = _t5_bucket(k_rel[None, :] - q_rel[:, None]).astype(jnp.int32)
    bias = _rel_bias(bucket, rel_bias)

    nf_w = norm_final.reshape(1, D_MODEL)
    outs = {g: dict(conv=[], delta=[], k=[], v=[]) for g in range(2)}
    xs_cur = [x_prompt, x_sample]

    for l in range(depth):
        o1 = DN_CONV_DIM + DN_V_DIM
        o2 = o1 + 2 * DN_V_HEADS
        wl, bl = w_in[l], b_in[l]
        w_cat = jnp.concatenate(
            [wl[:, :o1], wl[:, o2:], wl[:, o1:o2], jnp.zeros((D_MODEL, GATE_LANES - 2 * DN_V_HEADS), F32)],
            axis=1).astype(BF16)
        b_cat = jnp.concatenate(
            [bl[:o1], bl[o2:], bl[o1:o2], jnp.zeros((GATE_LANES - 2 * DN_V_HEADS,), F32)]).reshape(1, PROJ_DIM)
        pad8 = jnp.zeros((DN_V_HEADS,), F32)
        padr = jnp.zeros((GATE_LANES - 2 * DN_V_HEADS,), F32)
        alog_l = jnp.concatenate([pad8, a_log[l], padr]).reshape(1, GATE_LANES)
        dtb_l = jnp.concatenate([pad8, dt_bias[l], padr]).reshape(1, GATE_LANES)
        wo1 = w_out[l][:DN_V_DIM].astype(BF16)
        wo2 = w_out[l][DN_V_DIM:].astype(BF16)
        wr = jnp.concatenate([w_router[l], jnp.zeros((D_MODEL, LANES - N_EXPERTS), F32)], axis=1)
        br = jnp.concatenate([b_router[l], jnp.full((LANES - N_EXPERTS,), NEG, F32)]).reshape(1, LANES)
        b1l = b1[l].reshape(N_EXPERTS, 1, 2 * D_FF)
        b2l = b2[l].reshape(N_EXPERTS, 1, D_MODEL)

        mod = _modulation(jnp.concatenate([c_prompt, c_sample], axis=0), w_ada[l], b_ada[l])
        att = []
        for gi, grp in enumerate(groups):
            x = xs_cur[gi]
            bsz, t_len, _ = x.shape
            tm = min(TOKEN_TILE, t_len)
            m = mod[:bp] if gi == 0 else mod[bp:]
            sh1, sc1, g1, sh2, sc2, g2 = [a.reshape(bsz, 1, D_MODEL) for a in jnp.split(m, 6, axis=-1)]
            if gi == 0:
                cprev = jnp.zeros((bsz, CONV_W - 1, DN_CONV_DIM), F32)
                s0 = jnp.zeros((bsz, DN_V_HEADS, DN_HEAD_DIM, DN_HEAD_DIM), F32)
            else:
                cprev = state_conv[l]
                s0 = state_delta[l]
            cprev8 = jnp.concatenate([jnp.zeros((bsz, 8 - (CONV_W - 1), DN_CONV_DIM), F32), cprev], axis=1)
            qkv, z, sq, sk, sv, ba, utail = _inproj(x, sc1, sh1, norm_mix[l].reshape(1, D_MODEL), w_cat, b_cat,
                                                    cprev8, conv_w[l], tm)
            if gi == 0:
                k_all, v_all = sk, sv
            else:
                k_all = jnp.concatenate([cache_swa_k[l].reshape(bsz, -1, SWA_KV_DIM), sk], axis=1)
                v_all = jnp.concatenate([cache_swa_v[l].reshape(bsz, -1, SWA_KV_DIM), sv], axis=1)
            swa_len = cache_swa_k.shape[2]
            o_dn, s_new = _deltanet(qkv, z, ba, s0, alog_l, dtb_l,
                                    jnp.tile(dn_norm_w[l], 2).reshape(1, 2 * DN_HEAD_DIM))
            sink_rows = jnp.repeat(sinks[l], CHUNK).reshape(SWA_KV_HEADS, SWA_GROUP * CHUNK, 1)
            o_swa = _swa(sq, k_all, v_all, bias.reshape(SWA_KV_HEADS, SWA_GROUP * CHUNK, BAND), sink_rows,
                         grp["hist"])
            conv_new = utail[:, 8 - (CONV_W - 1):]
            k_state = k_all[:, k_all.shape[1] - swa_len:].reshape(bsz, swa_len, SWA_KV_HEADS, SWA_HEAD_DIM)
            v_state = v_all[:, v_all.shape[1] - swa_len:].reshape(bsz, swa_len, SWA_KV_HEADS, SWA_HEAD_DIM)
            outs[gi]["conv"].append(conv_new)
            outs[gi]["delta"].append(s_new)
            outs[gi]["k"].append(k_state)
            outs[gi]["v"].append(v_state)
            att.append(dict(x=x, o_dn=o_dn, o_swa=o_swa, g1=g1, sc2=sc2, sh2=sh2, g2=g2, tm=tm,
                            bsz=bsz, t_len=t_len))

        n_first = bp * (MOE_PART_SPLIT[0]) // sum(MOE_PART_SPLIT)
        parts = ([[(1, 0, bs), (0, 0, n_first)], [(0, n_first, bp - n_first)]] if 0 < n_first < bp
                 else [[(1, 0, bs), (0, 0, bp)]])
        work = []
        for part in parts:
            cnt = jnp.zeros((1, LANES), F32)
            n_part = sum(nseq * att[gi]["t_len"] for gi, _, nseq in part)
            h2_all = jnp.zeros((n_part, D_MODEL // 2), jnp.uint32) if len(part) > 1 else None
            segs, row0 = [], 0
            for gi, b0, nseq in part:
                a = att[gi]
                x1, h2_all, route, cnt = _outproj(a["o_dn"], a["o_swa"], a["x"], a["g1"], a["sc2"], a["sh2"],
                                                  wo1, wo2, b_out[l].reshape(1, D_MODEL),
                                                  norm_ffn[l].reshape(1, D_MODEL), wr, br, cnt, a["tm"], b0, nseq,
                                                  n_part, row0, h2_all)
                segs.append(dict(gi=gi, b0=b0, nseq=nseq, x1=x1, tg=route))
                row0 += nseq * a["t_len"]
            route_all = jnp.concatenate(
                [s["tg"].reshape(-1, LANES)[:, :ROUTE_GATE].T for s in segs], axis=1).astype(jnp.int32)
            ti_all = route_all[ROUTE_EXPERT:ROUTE_EXPERT + TOP_K]
            rk_all = route_all[ROUTE_RANK:ROUTE_RANK + TOP_K]
            sizes = cnt[0, :N_EXPERTS].astype(jnp.int32)
            pos, blk_expert, blk_valid, n_rows = _route(ti_all, rk_all, sizes)
            xs = _scatter_rows(h2_all, pos, n_rows)
            work.append(dict(segs=segs, pos=pos, blk_expert=blk_expert, blk_valid=blk_valid, xs=xs,
                             n_tok=h2_all.shape[0]))
        for wk in work:
            wk["y"] = _moe_experts(wk["blk_expert"], wk["blk_valid"], wk["xs"], w1[l], b1l, w2[l], b2l)
        for wk in work:
            wk["yg"] = _gather_rows(wk["y"], wk["pos"].reshape(-1)).reshape(TOP_K, wk["n_tok"], D_MODEL // 2)
        y_out = [None, None]
        for wk in work:
            off = 0
            for s in wk["segs"]:
                a = att[s["gi"]]
                y_out[s["gi"]] = _final(s["x1"], wk["yg"], s["tg"], a["g2"], nf_w, min(FINAL_TILE, a["t_len"]),
                                        off, s["b0"], a["bsz"], prev=y_out[s["gi"]])
                off += s["nseq"] * a["t_len"]
        xs_cur = y_out

    res = [xs_cur[0], xs_cur[1]]
    for gi in range(2):
        for name in ("conv", "delta", "k", "v"):
            res.append(jnp.stack(outs[gi][name]))
    return tuple(res)
```
